```python
import math
import jax, jax.numpy as jnp
from jax import lax
import numpy as np

D_MODEL = 1024
BATCH = 2
SEQ = 16384
DEPTH = 4
DEC_BATCH = 16
DEC_SEQ = 2048
PAST_LEN = 128

F_GROUPS = 4
F_GROUP_DIM = 128
F_WIDTH = F_GROUPS * F_GROUP_DIM
SG_CHUNK = 128
SG_GROUPS = 4
SG_GROUP_DIM = 64
SG_WIDTH = SG_GROUPS * SG_GROUP_DIM
MLA_HEADS = 8
MLA_Q_RANK = 256
MLA_KV_RANK = 192
MLA_NOPE = 64
MLA_ROPE = 64
MLA_V = 64
MLA_QK_DIM = MLA_NOPE + MLA_ROPE
ROPE_BASE = 10000.0
ATTN_Q_BLOCK = 128
DIL_PAIRS = ((128, 1), (512, 4), (2048, 16))
DIL_GROUPS = 3
DIL_HEADS = 4
DIL_HEAD_DIM = 32
DIL_Q_WIDTH = DIL_GROUPS * DIL_HEADS * DIL_HEAD_DIM
DIL_KV_WIDTH = DIL_HEADS * DIL_HEAD_DIM
REL_BUCKETS = 32
REL_MAX_DIST = 1024
MIX_WIDTH = F_WIDTH + 2 * SG_WIDTH + MLA_Q_RANK + MLA_KV_RANK + MLA_ROPE + DIL_Q_WIDTH + 2 * DIL_KV_WIDTH
N_BRANCH = 4
MOE_GROUPS = 4
MOE_EXPERTS_PER_GROUP = 4
MOE_EXPERTS = MOE_GROUPS * MOE_EXPERTS_PER_GROUP
MOE_TOPK = 2
MOE_FF = 512
DN_ALPHA = (2 * DEPTH) ** 0.25
DN_BETA = (8 * DEPTH) ** -0.25
LN_EPS = 1e-5
RMS_EPS = 1e-6

kernel_name = "hybrid_bidir_encoder_gated_parallel"

F32 = jnp.float32


def layer_norm(x, g, b):
    xf = x.astype(F32)
    mu = jnp.mean(xf, -1, keepdims=True)
    var = jnp.mean(jnp.square(xf - mu), -1, keepdims=True)
    return ((xf - mu) * lax.rsqrt(var + LN_EPS) * g.astype(F32) + b.astype(F32)).astype(x.dtype)


def rms_norm(x, g):
    xf = x.astype(F32)
    return (xf * lax.rsqrt(jnp.mean(xf * xf, -1, keepdims=True) + RMS_EPS) * g.astype(F32)).astype(x.dtype)


def apply_rope(x):
    S, R = x.shape[1], x.shape[-1]
    half = R // 2
    inv = ROPE_BASE ** (-jnp.arange(half, dtype=F32) / half)
    ang = jnp.arange(S, dtype=F32)[:, None] * inv[None, :]
    cos = jnp.cos(ang)[None, :, None, :]
    sin = jnp.sin(ang)[None, :, None, :]
    x1 = x[..., :half].astype(F32)
    x2 = x[..., half:].astype(F32)
    return jnp.concatenate([x1 * cos - x2 * sin, x1 * sin + x2 * cos], -1).astype(x.dtype)


def t5_bucket(rel):
    nb = REL_BUCKETS // 2
    max_exact = nb // 2
    ret = jnp.where(rel > 0, nb, 0)
    n = jnp.abs(rel)
    nf = jnp.maximum(n, 1).astype(F32)
    large = max_exact + (jnp.log(nf / max_exact) / math.log(REL_MAX_DIST / max_exact) * (nb - max_exact)).astype(jnp.int32)
    large = jnp.minimum(large, nb - 1)
    return ret + jnp.where(n < max_exact, n, large)


def fourier_mix(a):
    B, S, _ = a.shape
    af = a.astype(F32).reshape(B, S, F_GROUPS, F_GROUP_DIM)
    z = jnp.fft.fft2(af, axes=(1, 3), norm="ortho")
    return jnp.real(z).reshape(B, S, F_WIDTH).astype(a.dtype)


def spatial_gating(uv, ln_g, ln_b, w_s, b_s):
    B, S, _ = uv.shape
    u, v = uv[..., :SG_WIDTH], uv[..., SG_WIDTH:]
    v = layer_norm(v, ln_g, ln_b)
    vc = v.reshape(B, S // SG_CHUNK, SG_CHUNK, SG_GROUPS, SG_GROUP_DIM)
    mixed = jnp.einsum('gpq,bnqgc->bnpgc', w_s, vc) + b_s.T[None, None, :, :, None]
    return u * mixed.reshape(B, S, SG_WIDTH)


def blocked_attention(q, k, v, scale):
    B, S, H, Dk = q.shape
    nb = S // ATTN_Q_BLOCK
    qb = q.reshape(B, nb, ATTN_Q_BLOCK, H, Dk).swapaxes(0, 1)

    def one_block(qi):
        s = jnp.einsum('bqhd,bkhd->bhqk', qi, k, preferred_element_type=F32) * scale
        p = jax.nn.softmax(s, axis=-1)
        return jnp.einsum('bhqk,bkhd->bqhd', p.astype(v.dtype), v)

    o = lax.map(one_block, qb)
    return o.swapaxes(0, 1).reshape(B, S, H, v.shape[-1])


def mla(cq, ckv, kr, q_norm_g, kv_norm_g, w_uq, w_ukv):
    B, S, _ = cq.shape
    cq = rms_norm(cq, q_norm_g)
    ckv = rms_norm(ckv, kv_norm_g)
    q = (cq @ w_uq).reshape(B, S, MLA_HEADS, MLA_QK_DIM)
    q = jnp.concatenate([q[..., :MLA_NOPE], apply_rope(q[..., MLA_NOPE:])], -1)
    kv = (ckv @ w_ukv).reshape(B, S, MLA_HEADS, MLA_NOPE + MLA_V)
    k_nope, v = kv[..., :MLA_NOPE], kv[..., MLA_NOPE:]
    k_rope = apply_rope(kr[:, :, None, :])
    k = jnp.concatenate([k_nope, jnp.broadcast_to(k_rope, (B, S, MLA_HEADS, MLA_ROPE))], -1)
    o = blocked_attention(q, k, v, MLA_QK_DIM ** -0.5)
    return o.reshape(B, S, MLA_HEADS * MLA_V)


def dilated_attention(q, k, v, rel_bias):
    B, S = q.shape[0], q.shape[1]
    nb = S // ATTN_Q_BLOCK
    scale = DIL_HEAD_DIM ** -0.5
    offs, biases = [], []
    for g, (w, d) in enumerate(DIL_PAIRS):
        side = w // (2 * d)
        off = d * jnp.arange(-side, side + 1, dtype=jnp.int32)
        bias = rel_bias[t5_bucket(off)][:, g * DIL_HEADS:(g + 1) * DIL_HEADS].astype(F32).T
        offs.append(off)
        biases.append(bias)
    qb = q.reshape(B, nb, ATTN_Q_BLOCK, DIL_GROUPS, DIL_HEADS, DIL_HEAD_DIM).swapaxes(0, 1)

    def one_block(args):
        i, qi = args
        pos = i * ATTN_Q_BLOCK + jnp.arange(ATTN_Q_BLOCK, dtype=jnp.int32)
        outs, lses = [], []
        for g in range(DIL_GROUPS):
            kpos = pos[:, None] + offs[g][None, :]
            valid = (kpos >= 0) & (kpos < S)
            kidx = jnp.clip(kpos, 0, S - 1)
            kg = jnp.take(k, kidx, axis=1)
            vg = jnp.take(v, kidx, axis=1)
            s = jnp.einsum('bqhd,bqnhd->bhqn', qi[:, :, g], kg, preferred_element_type=F32) * scale
            s = jnp.where(valid[None, None], s + biases[g][None, :, None, :], -jnp.inf)
            m = jnp.max(s, -1, keepdims=True)
            e = jnp.exp(s - m)
            den = jnp.sum(e, -1, keepdims=True)
            outs.append(jnp.einsum('bhqn,bqnhd->bqhd', (e / den).astype(v.dtype), vg))
            lses.append((m + jnp.log(den))[..., 0])
        wts = jax.nn.softmax(jnp.stack(lses, -1), -1).transpose(0, 2, 1, 3)
        out = outs[0] * wts[..., 0:1].astype(outs[0].dtype)
        for g in range(1, DIL_GROUPS):
            out = out + outs[g] * wts[..., g:g + 1].astype(outs[g].dtype)
        return out

    o = lax.map(one_block, (jnp.arange(nb, dtype=jnp.int32), qb))
    return o.swapaxes(0, 1).reshape(B, S, DIL_HEADS * DIL_HEAD_DIM)


def hier_moe(x, w_rg, b_rg, w_re, b_re, w_gate, w_up, w_down):
    B, S, Dm = x.shape
    t = x.reshape(-1, Dm)
    T = t.shape[0]
    gl = (t @ w_rg).astype(F32) + b_rg.astype(F32)
    g_top = jnp.argmax(gl, -1)
    p_group = jnp.max(jax.nn.softmax(gl, -1), -1, keepdims=True)
    el = ((t @ w_re).astype(F32) + b_re.astype(F32)).reshape(T, MOE_GROUPS, MOE_EXPERTS_PER_GROUP)
    el_sel = el[jnp.arange(T), g_top]
    top_v, top_i = lax.top_k(el_sel, MOE_TOPK)
    p_exp = jax.nn.softmax(top_v, -1) * p_group
    expert_id = g_top[:, None] * MOE_EXPERTS_PER_GROUP + top_i
    combine = jnp.sum(jax.nn.one_hot(expert_id, MOE_EXPERTS, dtype=F32) * p_exp[..., None], axis=1)
    y = jnp.zeros_like(t)
    for g in range(MOE_GROUPS):
        sl = slice(g * MOE_EXPERTS_PER_GROUP, (g + 1) * MOE_EXPERTS_PER_GROUP)
        h = jax.nn.silu(jnp.einsum('td,edf->tef', t, w_gate[sl])) * jnp.einsum('td,edf->tef', t, w_up[sl])
        h = h * combine[:, sl, None].astype(h.dtype)
        y = y + jnp.einsum('tef,efd->td', h, w_down[sl])
    return y.reshape(B, S, Dm)


def encoder_layer(x, rel_bias, w_in, sg_ln_g, sg_ln_b, sg_w, sg_b, mla_q_norm, mla_kv_norm, mla_w_uq, mla_w_ukv,
                  w_branch_a, w_branch_b, w_branch_c, w_branch_d, w_gate, b_gate, w_o, ln1_g, ln1_b,
                  moe_w_rg, moe_b_rg, moe_w_re, moe_b_re, moe_w_gate, moe_w_up, moe_w_down, ln2_g, ln2_b):
    B, S, _ = x.shape
    z = x @ w_in
    o = 0
    za = z[..., o:o + F_WIDTH]; o += F_WIDTH
    zb = z[..., o:o + 2 * SG_WIDTH]; o += 2 * SG_WIDTH
    zcq = z[..., o:o + MLA_Q_RANK]; o += MLA_Q_RANK
    zckv = z[..., o:o + MLA_KV_RANK]; o += MLA_KV_RANK
    zkr = z[..., o:o + MLA_ROPE]; o += MLA_ROPE
    zdq = z[..., o:o + DIL_Q_WIDTH].reshape(B, S, DIL_GROUPS, DIL_HEADS, DIL_HEAD_DIM); o += DIL_Q_WIDTH
    zdk = z[..., o:o + DIL_KV_WIDTH].reshape(B, S, DIL_HEADS, DIL_HEAD_DIM); o += DIL_KV_WIDTH
    zdv = z[..., o:o + DIL_KV_WIDTH].reshape(B, S, DIL_HEADS, DIL_HEAD_DIM)

    ya = fourier_mix(za) @ w_branch_a
    yb = spatial_gating(jax.nn.gelu(zb), sg_ln_g, sg_ln_b, sg_w, sg_b) @ w_branch_b
    yc = mla(zcq, zckv, zkr, mla_q_norm, mla_kv_norm, mla_w_uq, mla_w_ukv) @ w_branch_c
    yd = dilated_attention(zdq, zdk, zdv, rel_bias) @ w_branch_d

    gates = jax.nn.sigmoid((x @ w_gate + b_gate).astype(F32)).astype(x.dtype)
    D = D_MODEL
    merged = (gates[..., 0:D] * ya + gates[..., D:2 * D] * yb
              + gates[..., 2 * D:3 * D] * yc + gates[..., 3 * D:4 * D] * yd)
    x = layer_norm(DN_ALPHA * x + merged @ w_o, ln1_g, ln1_b)
    x = layer_norm(DN_ALPHA * x + hier_moe(x, moe_w_rg, moe_b_rg, moe_w_re, moe_b_re, moe_w_gate, moe_w_up, moe_w_down),
                   ln2_g, ln2_b)
    return x


def setup_inputs(seed: int = 0) -> dict:
    key = jax.random.key(seed)
    ks = iter(jax.random.split(key, 48))

    def nrm(shape, scale):
        return jax.random.normal(next(ks), shape, F32) * scale

    def gain(shape):
        return 1.0 + 0.05 * jax.random.normal(next(ks), shape, F32)

    L = DEPTH
    D = D_MODEL
    return {
        "x_prompt": nrm((BATCH, SEQ, D), 1.0),
        "x_sample": nrm((DEC_BATCH, DEC_SEQ, D), 1.0),
        "ln_in_g": gain((D,)),
        "ln_in_b": nrm((D,), 0.02),
        "rel_bias": nrm((REL_BUCKETS, DIL_GROUPS * DIL_HEADS), 0.5),
        "w_in": nrm((L, D, MIX_WIDTH), D ** -0.5),
        "sg_ln_g": gain((L, SG_WIDTH)),
        "sg_ln_b": nrm((L, SG_WIDTH), 0.02),
        "sg_w": nrm((L, SG_GROUPS, SG_CHUNK, SG_CHUNK), 0.5 * SG_CHUNK ** -0.5),
        "sg_b": gain((L, SG_GROUPS, SG_CHUNK)),
        "mla_q_norm": gain((L, MLA_Q_RANK)),
        "mla_kv_norm": gain((L, MLA_KV_RANK)),
        "mla_w_uq": nrm((L, MLA_Q_RANK, MLA_HEADS * MLA_QK_DIM), MLA_Q_RANK ** -0.5),
        "mla_w_ukv": nrm((L, MLA_KV_RANK, MLA_HEADS * (MLA_NOPE + MLA_V)), MLA_KV_RANK ** -0.5),
        "w_branch_a": nrm((L, F_WIDTH, D), DN_BETA * F_WIDTH ** -0.5),
        "w_branch_b": nrm((L, SG_WIDTH, D), DN_BETA * SG_WIDTH ** -0.5),
        "w_branch_c": nrm((L, MLA_HEADS * MLA_V, D), DN_BETA * (MLA_HEADS * MLA_V) ** -0.5),
        "w_branch_d": nrm((L, DIL_HEADS * DIL_HEAD_DIM, D), DN_BETA * (DIL_HEADS * DIL_HEAD_DIM) ** -0.5),
        "w_gate": nrm((L, D, N_BRANCH * D), D ** -0.5),
        "b_gate": nrm((L, N_BRANCH * D), 0.02),
        "w_o": nrm((L, D, D), DN_BETA * D ** -0.5),
        "ln1_g": gain((L, D)),
        "ln1_b": nrm((L, D), 0.02),
        "moe_w_rg": nrm((L, D, MOE_GROUPS), D ** -0.5),
        "moe_b_rg": nrm((L, MOE_GROUPS), 0.01),
        "moe_w_re": nrm((L, D, MOE_EXPERTS), D ** -0.5),
        "moe_b_re": nrm((L, MOE_EXPERTS), 0.01),
        "moe_w_gate": nrm((L, MOE_EXPERTS, D, MOE_FF), D ** -0.5),
        "moe_w_up": nrm((L, MOE_EXPERTS, D, MOE_FF), D ** -0.5),
        "moe_w_down": nrm((L, MOE_EXPERTS, MOE_FF, D), DN_BETA * MOE_FF ** -0.5),
        "ln2_g": gain((L, D)),
        "ln2_b": nrm((L, D), 0.02),
    }


def reference(x_prompt, x_sample, ln_in_g, ln_in_b, rel_bias, w_in, sg_ln_g, sg_ln_b, sg_w, sg_b,
              mla_q_norm, mla_kv_norm, mla_w_uq, mla_w_ukv, w_branch_a, w_branch_b, w_branch_c, w_branch_d,
              w_gate, b_gate, w_o, ln1_g, ln1_b, moe_w_rg, moe_b_rg, moe_w_re, moe_b_re,
              moe_w_gate, moe_w_up, moe_w_down, ln2_g, ln2_b):
    def trunk(x):
        h = layer_norm(x, ln_in_g, ln_in_b)
        for l in range(DEPTH):
            h = encoder_layer(h, rel_bias, w_in[l], sg_ln_g[l], sg_ln_b[l], sg_w[l], sg_b[l],
                              mla_q_norm[l], mla_kv_norm[l], mla_w_uq[l], mla_w_ukv[l],
                              w_branch_a[l], w_branch_b[l], w_branch_c[l], w_branch_d[l],
                              w_gate[l], b_gate[l], w_o[l], ln1_g[l], ln1_b[l],
                              moe_w_rg[l], moe_b_rg[l], moe_w_re[l], moe_b_re[l],
                              moe_w_gate[l], moe_w_up[l], moe_w_down[l], ln2_g[l], ln2_b[l])
        return h

    y_prompt = trunk(x_prompt)
    y_sample = trunk(x_sample)
    return (y_prompt, y_sample)
```

```python
import functools
import math

import numpy as np
import jax
import jax.numpy as jnp
from jax import lax
from jax.experimental import pallas as pl
from jax.experimental.pallas import tpu as pltpu

F32 = jnp.float32
BF16 = jnp.bfloat16

D_MODEL = 1024
DEPTH = 4
F_GROUPS = 4
F_GROUP_DIM = 128
F_WIDTH = F_GROUPS * F_GROUP_DIM
SG_CHUNK = 128
SG_GROUPS = 4
SG_GROUP_DIM = 64
SG_WIDTH = SG_GROUPS * SG_GROUP_DIM
MLA_HEADS = 8
MLA_Q_RANK = 256
MLA_KV_RANK = 192
MLA_NOPE = 64
MLA_ROPE = 64
MLA_V = 64
MLA_QK_DIM = MLA_NOPE + MLA_ROPE
ROPE_BASE = 10000.0
DIL_PAIRS = ((128, 1), (512, 4), (2048, 16))
DIL_GROUPS = 3
DIL_HEADS = 4
DIL_HEAD_DIM = 32
DIL_Q_WIDTH = DIL_GROUPS * DIL_HEADS * DIL_HEAD_DIM
DIL_KV_WIDTH = DIL_HEADS * DIL_HEAD_DIM
DIL_SIDE = 64
REL_BUCKETS = 32
REL_MAX_DIST = 1024
MIX_WIDTH = F_WIDTH + 2 * SG_WIDTH + MLA_Q_RANK + MLA_KV_RANK + MLA_ROPE + DIL_Q_WIDTH + 2 * DIL_KV_WIDTH
MOE_GROUPS = 4
MOE_EXPERTS_PER_GROUP = 4
MOE_EXPERTS = MOE_GROUPS * MOE_EXPERTS_PER_GROUP
MOE_FF = 512
DN_ALPHA = (2 * DEPTH) ** 0.25
LN_EPS = 1e-5
RMS_EPS = 1e-6

OFF_A = 0
OFF_B = OFF_A + F_WIDTH
OFF_CQ = OFF_B + 2 * SG_WIDTH
OFF_CKV = OFF_CQ + MLA_Q_RANK
OFF_DQ = OFF_CKV + MLA_KV_RANK + MLA_ROPE
OFF_DK = OFF_DQ + DIL_Q_WIDTH
OFF_DV = OFF_DK + DIL_KV_WIDTH

LANES = 128
VMEM_LIMIT_BYTES = 56 * 1024 * 1024
TOKEN_TILE = 512
MOE_TOKEN_TILE = 1024
ATTN_Q_TILE = 512
ATTN_KV_TILE = 512
DIL_BLOCK = 128
NEG_BIG = -1e30


def _cparams(*sem):
    return pltpu.CompilerParams(dimension_semantics=sem, vmem_limit_bytes=VMEM_LIMIT_BYTES)


def _full(shape):
    n = len(shape)
    return pl.BlockSpec(shape, lambda *_: (0,) * n)


def _layer_norm(x, g, b):
    mu = jnp.mean(x, -1, keepdims=True)
    xc = x - mu
    var = jnp.mean(xc * xc, -1, keepdims=True)
    return xc * lax.rsqrt(var + LN_EPS) * g + b


def _dot(a, b):
    return jnp.dot(a, b, preferred_element_type=F32)


def _ln_kernel(x_ref, g_ref, b_ref, o_ref):
    o_ref[...] = _layer_norm(x_ref[...], g_ref[...], b_ref[...])


def _input_layer_norm(x, g, b):
    T = x.shape[0]
    tm = TOKEN_TILE
    return pl.pallas_call(
        _ln_kernel,
        grid=(T // tm,),
        in_specs=[pl.BlockSpec((tm, D_MODEL), lambda i: (i, 0)), _full((1, D_MODEL)), _full((1, D_MODEL))],
        out_specs=pl.BlockSpec((tm, D_MODEL), lambda i: (i, 0)),
        out_shape=jax.ShapeDtypeStruct((T, D_MODEL), F32),
        compiler_params=_cparams("parallel"),
        name="input_layer_norm",
    )(x, g.reshape(1, -1), b.reshape(1, -1))


def _rope_lanes(t, c, sa, sb):
    return t * c + pltpu.roll(t, 96, 1) * sa + pltpu.roll(t, 32, 1) * sb


def _in_proj_kernel(x_ref, w_in_ref, sg_g_ref, sg_b_ref, sg_w_ref, sg_bias_ref, qn_ref, kvn_ref,
                    wuq_ref, wk_ref, wv_ref, rc_ref, rsa_ref, rsb_ref,
                    za_ref, sg_ref, q_ref, k_ref, v_ref, dq_ref, dk_ref, dv_ref):
    tm = x_ref.shape[0]
    z = _dot(x_ref[...].astype(BF16), w_in_ref[...])

    za_ref[...] = z[:, OFF_A:OFF_A + F_WIDTH]

    zb = jax.nn.gelu(z[:, OFF_B:OFF_B + 2 * SG_WIDTH])
    u = zb[:, :SG_WIDTH]
    vn = _layer_norm(zb[:, SG_WIDTH:], sg_g_ref[...], sg_b_ref[...]).astype(BF16)
    lane = lax.broadcasted_iota(jnp.int32, (SG_CHUNK, LANES), 1)
    low_half = lane < SG_GROUP_DIM
    for ci in range(tm // SG_CHUNK):
        rows = slice(ci * SG_CHUNK, (ci + 1) * SG_CHUNK)
        for j in range(SG_WIDTH // LANES):
            cols = slice(j * LANES, (j + 1) * LANES)
            vblk = vn[rows, cols]
            mixed = jnp.where(low_half, _dot(sg_w_ref[2 * j], vblk), _dot(sg_w_ref[2 * j + 1], vblk))
            sg_ref[rows, cols] = (u[rows, cols] * (mixed + sg_bias_ref[:, cols])).astype(sg_ref.dtype)

    rc, rsa, rsb = rc_ref[...], rsa_ref[...], rsb_ref[...]

    cq = z[:, OFF_CQ:OFF_CQ + MLA_Q_RANK]
    cq = cq * lax.rsqrt(jnp.mean(cq * cq, -1, keepdims=True) + RMS_EPS) * qn_ref[...]
    q = _dot(cq.astype(BF16), wuq_ref[...]) * (MLA_QK_DIM ** -0.5)
    ra = pltpu.roll(q, MLA_HEADS * LANES - 32, 1)
    rb = pltpu.roll(q, 32, 1)
    for h in range(MLA_HEADS):
        cols = slice(h * LANES, (h + 1) * LANES)
        q_ref[:, cols] = (q[:, cols] * rc + ra[:, cols] * rsa + rb[:, cols] * rsb).astype(q_ref.dtype)

    slab = z[:, OFF_CKV:OFF_CKV + 2 * LANES]
    lane2 = lax.broadcasted_iota(jnp.int32, (tm, 2 * LANES), 1)
    ckv_sq = jnp.where(lane2 < MLA_KV_RANK, slab * slab, 0.0)
    ms = jnp.sum(ckv_sq, -1, keepdims=True) * (1.0 / MLA_KV_RANK)
    ckv = (slab * lax.rsqrt(ms + RMS_EPS) * kvn_ref[...]).astype(BF16)
    k_nope = _dot(ckv, wk_ref[...])
    v_ref[...] = _dot(ckv, wv_ref[...]).astype(v_ref.dtype)
    kr_slab = _rope_lanes(slab[:, LANES:], rc, rsa, rsb)
    kr_slab = jnp.where(lax.broadcasted_iota(jnp.int32, (tm, LANES), 1) >= MLA_NOPE, kr_slab, 0.0)
    for h in range(MLA_HEADS):
        cols = slice(h * LANES, (h + 1) * LANES)
        k_ref[:, cols] = (k_nope[:, cols] + kr_slab).astype(k_ref.dtype)

    dq_ref[...] = (z[:, OFF_DQ:OFF_DQ + DIL_Q_WIDTH] * (DIL_HEAD_DIM ** -0.5)).astype(dq_ref.dtype)
    dk_ref[...] = z[:, OFF_DK:OFF_DK + DIL_KV_WIDTH].astype(dk_ref.dtype)
    dv_ref[...] = z[:, OFF_DV:OFF_DV + DIL_KV_WIDTH].astype(dv_ref.dtype)


def _in_proj(x, S, lw, rope):
    T = x.shape[0]
    tm = TOKEN_TILE
    nblk = S // tm
    tok = lambda w: pl.BlockSpec((tm, w), lambda i: (i, 0))
    pos = pl.BlockSpec((tm, LANES), lambda i: (i % nblk, 0))
    outs = [(F_WIDTH, F32), (SG_WIDTH, BF16), (MLA_HEADS * LANES, BF16), (MLA_HEADS * LANES, BF16),
            (MLA_HEADS * MLA_V, BF16), (DIL_Q_WIDTH, BF16), (DIL_KV_WIDTH, BF16), (DIL_KV_WIDTH, BF16)]
    return pl.pallas_call(
        _in_proj_kernel,
        grid=(T // tm,),
        in_specs=[tok(D_MODEL), _full((D_MODEL, MIX_WIDTH)), _full((1, SG_WIDTH)), _full((1, SG_WIDTH)),
                  _full((SG_GROUPS, SG_CHUNK, SG_CHUNK)), _full((SG_CHUNK, SG_WIDTH)),
                  _full((1, MLA_Q_RANK)), _full((1, 2 * LANES)),
                  _full((MLA_Q_RANK, MLA_HEADS * LANES)), _full((2 * LANES, MLA_HEADS * LANES)),
                  _full((2 * LANES, MLA_HEADS * MLA_V)), pos, pos, pos],
        out_specs=[tok(w) for w, _ in outs],
        out_shape=[jax.ShapeDtypeStruct((T, w), dt) for w, dt in outs],
        compiler_params=_cparams("parallel"),
        name="in_proj",
    )(x, lw["w_in"], lw["sg_ln_g"], lw["sg_ln_b"], lw["sg_w"], lw["sg_bias"], lw["q_norm"], lw["kv_norm"],
      lw["w_uq"], lw["w_k"], lw["w_v"], *rope)


def _fourier1_kernel(x_ref, cs_ref, m1_ref, ct_ref, st_ref, gr_ref, gi_ref):
    n1 = x_ref.shape[0]
    per = x_ref.shape[1] // F_WIDTH
    cs = cs_ref[...]
    m1 = m1_ref[...]
    for j in range(per):
        ct = ct_ref[:, j * LANES:(j + 1) * LANES]
        st = st_ref[:, j * LANES:(j + 1) * LANES]
        for g in range(F_GROUPS):
            cols = slice(j * F_WIDTH + g * LANES, j * F_WIDTH + (g + 1) * LANES)
            ab = _dot(x_ref[:, cols].astype(BF16), cs)
            stacked = jnp.concatenate([ab[:, :LANES], ab[:, LANES:]], axis=0).astype(BF16)
            g2 = _dot(m1, stacked)
            gr, gi = g2[:n1], g2[n1:]
            gr_ref[:, cols] = gr * ct - gi * st
            gi_ref[:, cols] = gr * st + gi * ct


def _fourier2_kernel(gr_ref, gi_ref, w2_ref, o_ref):
    kk = gr_ref.shape[0]
    for j in range(kk):
        stacked = jnp.concatenate([gr_ref[j], gi_ref[j]], axis=0).astype(BF16)
        o_ref[:, j * F_WIDTH:(j + 1) * F_WIDTH] = _dot(w2_ref[...], stacked).astype(o_ref.dtype)


def _fourier_mix(za, B, S, consts):
    n1 = S // LANES
    n2 = LANES
    per = 2
    kk = 8
    cs, m1, ct, st, w2 = consts
    x = za.reshape(B, n1, n2 * F_WIDTH)
    blk1 = pl.BlockSpec((None, n1, per * F_WIDTH), lambda b, j: (b, 0, j))
    twid = pl.BlockSpec((n1, per * LANES), lambda b, j: (0, j))
    gr, gi = pl.pallas_call(
        _fourier1_kernel,
        grid=(B, n2 // per),
        in_specs=[blk1, _full((LANES, 2 * LANES)), _full((2 * n1, 2 * n1)), twid, twid],
        out_specs=[blk1, blk1],
        out_shape=[jax.ShapeDtypeStruct(x.shape, F32)] * 2,
        compiler_params=_cparams("parallel", "parallel"),
        name="fourier_stage1",
    )(x, cs, m1, ct, st)
    gr = gr.reshape(B, n1, n2, F_WIDTH)
    gi = gi.reshape(B, n1, n2, F_WIDTH)
    blk2 = pl.BlockSpec((None, kk, n2, F_WIDTH), lambda b, j: (b, j, 0, 0))
    y = pl.pallas_call(
        _fourier2_kernel,
        grid=(B, n1 // kk),
        in_specs=[blk2, blk2, _full((n2, 2 * n2))],
        out_specs=pl.BlockSpec((None, n2, kk * F_WIDTH), lambda b, j: (b, 0, j)),
        out_shape=jax.ShapeDtypeStruct((B, n2, n1 * F_WIDTH), BF16),
        compiler_params=_cparams("parallel", "parallel"),
        name="fourier_stage2",
    )(gr, gi, w2)
    return y.reshape(B * S, F_WIDTH)


def _fourier_consts(S):
    n1 = S // LANES
    n2 = LANES
    c = np.arange(F_GROUP_DIM)
    ang_c = 2.0 * np.pi * np.outer(c, c) / F_GROUP_DIM
    norm = 1.0 / math.sqrt(S * F_GROUP_DIM)
    cs = np.concatenate([np.cos(ang_c), np.sin(ang_c)], axis=1) * norm
    a1 = np.arange(n1)
    ang1 = 2.0 * np.pi * np.outer(a1, a1) / n1
    c1, s1 = np.cos(ang1), np.sin(ang1)
    m1 = np.block([[c1, -s1], [s1, c1]])
    a2 = np.arange(n2)
    ang_t = 2.0 * np.pi * np.outer(a1, a2) / S
    ct = np.repeat(np.cos(ang_t), LANES, axis=1)
    st = np.repeat(np.sin(ang_t), LANES, axis=1)
    ang2 = 2.0 * np.pi * np.outer(a2, a2) / n2
    w2 = np.concatenate([np.cos(ang2), -np.sin(ang2)], axis=1)
    return (jnp.asarray(cs, BF16), jnp.asarray(m1, BF16), jnp.asarray(ct, F32), jnp.asarray(st, F32),
            jnp.asarray(w2, BF16))


def _flash_kernel(q_ref, k_ref, v_ref, o_ref, m_ref, l_ref, acc_ref):
    ki = pl.program_id(2)

    @pl.when(ki == 0)
    def _():
        m_ref[...] = jnp.full(m_ref.shape, NEG_BIG, F32)
        l_ref[...] = jnp.zeros(l_ref.shape, F32)
        acc_ref[...] = jnp.zeros(acc_ref.shape, F32)

    for h in range(MLA_HEADS):
        qh = q_ref[:, h * LANES:(h + 1) * LANES]
        kh = k_ref[:, h * LANES:(h + 1) * LANES]
        s = lax.dot_general(qh, kh, (((1,), (1,)), ((), ())), preferred_element_type=F32)
        m_prev = m_ref[h]
        m_new = jnp.maximum(m_prev, jnp.max(s, axis=1, keepdims=True))
        alpha = jnp.exp(m_prev - m_new)
        p = jnp.exp(s - m_new)
        l_ref[h] = alpha * l_ref[h] + jnp.sum(p, axis=1, keepdims=True)
        m_ref[h] = m_new
        vh = v_ref[:, h * MLA_V:(h + 1) * MLA_V]
        acc_ref[h] = alpha * acc_ref[h] + _dot(p.astype(BF16), vh)

    @pl.when(ki == pl.num_programs(2) - 1)
    def _():
        for h in range(MLA_HEADS):
            o_ref[:, h * MLA_V:(h + 1) * MLA_V] = (acc_ref[h] / l_ref[h]).astype(o_ref.dtype)


def _latent_attention(q, k, v, B, S):
    tq = min(ATTN_Q_TILE, S)
    tk = min(ATTN_KV_TILE, S)
    q = q.reshape(B, S, MLA_HEADS * LANES)
    k = k.reshape(B, S, MLA_HEADS * LANES)
    v = v.reshape(B, S, MLA_HEADS * MLA_V)
    out = pl.pallas_call(
        _flash_kernel,
        grid=(B, S // tq, S // tk),
        in_specs=[pl.BlockSpec((None, tq, MLA_HEADS * LANES), lambda b, i, j: (b, i, 0)),
                  pl.BlockSpec((None, tk, MLA_HEADS * LANES), lambda b, i, j: (b, j, 0)),
                  pl.BlockSpec((None, tk, MLA_HEADS * MLA_V), lambda b, i, j: (b, j, 0))],
        out_specs=pl.BlockSpec((None, tq, MLA_HEADS * MLA_V), lambda b, i, j: (b, i, 0)),
        out_shape=jax.ShapeDtypeStruct((B, S, MLA_HEADS * MLA_V), BF16),
        scratch_shapes=[pltpu.VMEM((MLA_HEADS, tq, 1), F32), pltpu.VMEM((MLA_HEADS, tq, 1), F32),
                        pltpu.VMEM((MLA_HEADS, tq, MLA_V), F32)],
        compiler_params=_cparams("parallel", "parallel", "arbitrary"),
        name="latent_attention",
    )(q, k, v)
    return out.reshape(B * S, MLA_HEADS * MLA_V)


def _dilated_kernel(q_ref, kp_ref, kc_ref, kn_ref, vp_ref, vc_ref, vn_ref, bias_ref, o_ref, lse_ref):
    i = pl.program_id(2)
    n = pl.num_programs(2)
    q = q_ref[...]
    k3 = jnp.concatenate([kp_ref[...], kc_ref[...], kn_ref[...]], axis=0)
    v3 = jnp.concatenate([vp_ref[...], vc_ref[...], vn_ref[...]], axis=0)
    col = lax.broadcasted_iota(jnp.int32, (DIL_BLOCK, 3 * DIL_BLOCK), 1)
    in_seq = jnp.logical_and(jnp.logical_or(i > 0, col >= DIL_BLOCK),
                             jnp.logical_or(i < n - 1, col < 2 * DIL_BLOCK))
    lane = lax.broadcasted_iota(jnp.int32, (1, LANES), 1)
    out = jnp.zeros((DIL_BLOCK, LANES), F32)
    lse = jnp.zeros((DIL_BLOCK, LANES), F32)
    for h in range(DIL_HEADS):
        head = jnp.logical_and(lane >= h * DIL_HEAD_DIM, lane < (h + 1) * DIL_HEAD_DIM)
        qh = jnp.where(head, q, jnp.zeros_like(q))
        s = lax.dot_general(qh, k3, (((1,), (1,)), ((), ())), preferred_element_type=F32)
        s = jnp.where(in_seq, s + bias_ref[h], NEG_BIG)
        m = jnp.max(s, axis=1, keepdims=True)
        e = jnp.exp(s - m)
        den = jnp.sum(e, axis=1, keepdims=True)
        pv = _dot((e / den).astype(BF16), v3)
        out = jnp.where(head, pv, out)
        lse = jnp.where(head, m + jnp.log(den), lse)
    o_ref[...] = out
    lse_ref[...] = lse


def _dilated_group(dq, dk, dv, bias, g, d, B, S):
    rows = S // d
    nblk = rows // DIL_BLOCK
    q = dq.reshape(B, rows, d * DIL_Q_WIDTH)
    k = dk.reshape(B, rows, d * DIL_KV_WIDTH)
    v = dv.reshape(B, rows, d * DIL_KV_WIDTH)
    blk = lambda f: pl.BlockSpec((None, DIL_BLOCK, LANES), f)
    prev = lambda b, c, i: (b, jnp.maximum(i - 1, 0), c)
    cur = lambda b, c, i: (b, i, c)
    nxt = lambda b, c, i: (b, jnp.minimum(i + 1, nblk - 1), c)
    o, lse = pl.pallas_call(
        _dilated_kernel,
        grid=(B, d, nblk),
        in_specs=[blk(lambda b, c, i: (b, i, c * DIL_GROUPS + g)),
                  blk(prev), blk(cur), blk(nxt), blk(prev), blk(cur), blk(nxt),
                  _full((DIL_HEADS, DIL_BLOCK, 3 * DIL_BLOCK))],
        out_specs=[blk(cur), blk(cur)],
        out_shape=[jax.ShapeDtypeStruct((B, rows, d * LANES), F32)] * 2,
        compiler_params=_cparams("parallel", "parallel", "parallel"),
        name="dilated_attention_g%d" % g,
    )(q, k, k, k, v, v, v, bias)
    return o.reshape(B * S, LANES), lse.reshape(B * S, LANES)


def _t5_bucket(rel):
    nb = REL_BUCKETS // 2
    max_exact = nb // 2
    ret = jnp.where(rel > 0, nb, 0)
    n = jnp.abs(rel)
    nf = jnp.maximum(n, 1).astype(F32)
    large = max_exact + (jnp.log(nf / max_exact) / math.log(REL_MAX_DIST / max_exact) * (nb - max_exact)).astype(jnp.int32)
    large = jnp.minimum(large, nb - 1)
    return ret + jnp.where(n < max_exact, n, large)


def _dilated_bias_tables(rel_bias):
    qi = np.arange(DIL_BLOCK)[:, None]
    kj = np.arange(3 * DIL_BLOCK)[None, :]
    rel = kj - DIL_BLOCK - qi
    in_band = np.abs(rel) <= DIL_SIDE
    idx = np.clip(rel + DIL_SIDE, 0, 2 * DIL_SIDE)
    tables = []
    for g, (_, d) in enumerate(DIL_PAIRS):
        off = d * jnp.arange(-DIL_SIDE, DIL_SIDE + 1, dtype=jnp.int32)
        b = rel_bias[_t5_bucket(off)][:, g * DIL_HEADS:(g + 1) * DIL_HEADS].astype(F32).T
        tables.append(jnp.where(in_band[None], b[:, idx], NEG_BIG))
    return tables


def _merge_kernel(x_ref, fa_ref, sg_ref, oc_ref, od_ref, lse_ref, wa_ref, wb_ref, wc_ref, wd_ref,
                  wg_ref, bg_ref, wo_ref, g_ref, b_ref, o_ref):
    x = x_ref[...]
    xb = x.astype(BF16)
    l0, l1, l2 = lse_ref[0], lse_ref[1], lse_ref[2]
    mx = jnp.maximum(jnp.maximum(l0, l1), l2)
    e0, e1, e2 = jnp.exp(l0 - mx), jnp.exp(l1 - mx), jnp.exp(l2 - mx)
    den = e0 + e1 + e2
    od = od_ref[0] * (e0 / den) + od_ref[1] * (e1 / den) + od_ref[2] * (e2 / den)
    branches = ((fa_ref[...], wa_ref), (sg_ref[...], wb_ref), (oc_ref[...], wc_ref), (od.astype(BF16), wd_ref))
    merged = None
    for i, (act, w_ref) in enumerate(branches):
        cols = slice(i * D_MODEL, (i + 1) * D_MODEL)
        gate = jax.nn.sigmoid(_dot(xb, wg_ref[:, cols]) + bg_ref[:, cols])
        term = gate * _dot(act, w_ref[...])
        merged = term if merged is None else merged + term
    y = DN_ALPHA * x + _dot(merged.astype(BF16), wo_ref[...])
    o_ref[...] = _layer_norm(y, g_ref[...], b_ref[...])


def _merge(x, fa, sg, oc, od, lse, lw):
    T = x.shape[0]
    tm = TOKEN_TILE
    tok = lambda w: pl.BlockSpec((tm, w), lambda i: (i, 0))
    grp = pl.BlockSpec((DIL_GROUPS, tm, LANES), lambda i: (0, i, 0))
    return pl.pallas_call(
        _merge_kernel,
        grid=(T // tm,),
        in_specs=[tok(D_MODEL), tok(F_WIDTH), tok(SG_WIDTH), tok(MLA_HEADS * MLA_V), grp, grp,
                  _full((F_WIDTH, D_MODEL)), _full((SG_WIDTH, D_MODEL)), _full((MLA_HEADS * MLA_V, D_MODEL)),
                  _full((DIL_KV_WIDTH, D_MODEL)), _full((D_MODEL, 4 * D_MODEL)), _full((1, 4 * D_MODEL)),
                  _full((D_MODEL, D_MODEL)), _full((1, D_MODEL)), _full((1, D_MODEL))],
        out_specs=tok(D_MODEL),
        out_shape=jax.ShapeDtypeStruct((T, D_MODEL), F32),
        compiler_params=_cparams("parallel"),
        name="merge",
    )(x, fa, sg, oc, od, lse, lw["w_a"], lw["w_b"], lw["w_c"], lw["w_d"], lw["w_gate"], lw["b_gate"],
      lw["w_o"], lw["ln1_g"], lw["ln1_b"])


def _routing_weights(logits):
    tm = logits.shape[0]
    lane = lax.broadcasted_iota(jnp.int32, (tm, LANES), 1)
    is_g = lane < MOE_GROUPS
    gl = jnp.where(is_g, logits, NEG_BIG)
    gmax = jnp.max(gl, -1, keepdims=True)
    g_top = jnp.min(jnp.where(gl == gmax, lane, LANES), -1, keepdims=True)
    p_group = 1.0 / jnp.sum(jnp.where(is_g, jnp.exp(gl - gmax), 0.0), -1, keepdims=True)
    lo = MOE_GROUPS + g_top * MOE_EXPERTS_PER_GROUP
    in_grp = jnp.logical_and(lane >= lo, lane < lo + MOE_EXPERTS_PER_GROUP)
    el = jnp.where(in_grp, logits, NEG_BIG)
    v1 = jnp.max(el, -1, keepdims=True)
    i1 = jnp.min(jnp.where(el == v1, lane, LANES), -1, keepdims=True)
    el2 = jnp.where(lane == i1, NEG_BIG, el)
    v2 = jnp.max(el2, -1, keepdims=True)
    i2 = jnp.min(jnp.where(el2 == v2, lane, LANES), -1, keepdims=True)
    e2 = jnp.exp(v2 - v1)
    p1 = p_group / (1.0 + e2)
    p2 = p_group * e2 / (1.0 + e2)
    return jnp.where(lane == i1, p1, 0.0) + jnp.where(lane == i2, p2, 0.0)


def _moe_kernel(x_ref, wr_ref, br_ref, wg_ref, wu_ref, wd_ref, g_ref, b_ref, o_ref, comb_ref, acc_ref):
    e = pl.program_id(1)

    @pl.when(e == 0)
    def _():
        logits = jnp.dot(x_ref[...], wr_ref[...], preferred_element_type=F32,
                         precision=lax.Precision.HIGHEST) + br_ref[...]
        comb_ref[...] = _routing_weights(logits)
        acc_ref[...] = jnp.zeros(acc_ref.shape, F32)

    xb = x_ref[...].astype(BF16)
    lane = lax.broadcasted_iota(jnp.int32, comb_ref.shape, 1)
    w = jnp.sum(jnp.where(lane == MOE_GROUPS + e, comb_ref[...], 0.0), -1, keepdims=True)
    h = jax.nn.silu(_dot(xb, wg_ref[...])) * _dot(xb, wu_ref[...]) * w
    acc_ref[...] += _dot(h.astype(BF16), wd_ref[...])

    @pl.when(e == pl.num_programs(1) - 1)
    def _():
        o_ref[...] = _layer_norm(DN_ALPHA * x_ref[...] + acc_ref[...], g_ref[...], b_ref[...])


def _moe(x, lw):
    T = x.shape[0]
    tm = MOE_TOKEN_TILE
    return pl.pallas_call(
        _moe_kernel,
        grid=(T // tm, MOE_EXPERTS),
        in_specs=[pl.BlockSpec((tm, D_MODEL), lambda i, e: (i, 0)),
                  _full((D_MODEL, LANES)), _full((1, LANES)),
                  pl.BlockSpec((None, D_MODEL, MOE_FF), lambda i, e: (e, 0, 0)),
                  pl.BlockSpec((None, D_MODEL, MOE_FF), lambda i, e: (e, 0, 0)),
                  pl.BlockSpec((None, MOE_FF, D_MODEL), lambda i, e: (e, 0, 0)),
                  _full((1, D_MODEL)), _full((1, D_MODEL))],
        out_specs=pl.BlockSpec((tm, D_MODEL), lambda i, e: (i, 0)),
        out_shape=jax.ShapeDtypeStruct((T, D_MODEL), F32),
        scratch_shapes=[pltpu.VMEM((tm, LANES), F32), pltpu.VMEM((tm, D_MODEL), F32)],
        compiler_params=_cparams("parallel", "arbitrary"),
        name="moe",
    )(x, lw["w_router"], lw["b_router"], lw["moe_w_gate"], lw["moe_w_up"], lw["moe_w_down"],
      lw["ln2_g"], lw["ln2_b"])


def _rope_tables(S):
    half = MLA_ROPE // 2
    inv = ROPE_BASE ** (-jnp.arange(half, dtype=F32) / half)
    ang = jnp.arange(S, dtype=F32)[:, None] * inv[None, :]
    cos, sin = jnp.cos(ang), jnp.sin(ang)
    one = jnp.ones((S, MLA_NOPE), F32)
    zero = jnp.zeros((S, MLA_NOPE), F32)
    zh = jnp.zeros((S, half), F32)
    rc = jnp.concatenate([one, cos, cos], axis=1)
    rsa = jnp.concatenate([zero, -sin, zh], axis=1)
    rsb = jnp.concatenate([zero, zh, sin], axis=1)
    return rc, rsa, rsb


def _prep_layer(l, p):
    row = lambda a: a.reshape(1, -1).astype(F32)
    w_ukv = p["mla_w_ukv"][l].reshape(MLA_KV_RANK, MLA_HEADS, MLA_NOPE + MLA_V)
    pad_rows = ((0, 2 * LANES - MLA_KV_RANK), (0, 0))
    w_k = jnp.pad(w_ukv[:, :, :MLA_NOPE], ((0, 0), (0, 0), (0, LANES - MLA_NOPE))).reshape(MLA_KV_RANK, -1)
    w_v = w_ukv[:, :, MLA_NOPE:].reshape(MLA_KV_RANK, -1)
    w_router = jnp.concatenate([p["moe_w_rg"][l], p["moe_w_re"][l]], axis=1)
    b_router = jnp.concatenate([p["moe_b_rg"][l], p["moe_b_re"][l]])
    npad = LANES - MOE_GROUPS - MOE_EXPERTS
    return {
        "w_in": p["w_in"][l].astype(BF16),
        "sg_ln_g": row(p["sg_ln_g"][l]), "sg_ln_b": row(p["sg_ln_b"][l]),
        "sg_w": p["sg_w"][l].astype(BF16),
        "sg_bias": jnp.repeat(p["sg_b"][l].T, SG_GROUP_DIM, axis=1).astype(F32),
        "q_norm": row(p["mla_q_norm"][l]),
        "kv_norm": jnp.pad(row(p["mla_kv_norm"][l]), ((0, 0), (0, 2 * LANES - MLA_KV_RANK))),
        "w_uq": p["mla_w_uq"][l].astype(BF16),
        "w_k": jnp.pad(w_k, pad_rows).astype(BF16),
        "w_v": jnp.pad(w_v, pad_rows).astype(BF16),
        "w_a": p["w_branch_a"][l].astype(BF16), "w_b": p["w_branch_b"][l].astype(BF16),
        "w_c": p["w_branch_c"][l].astype(BF16), "w_d": p["w_branch_d"][l].astype(BF16),
        "w_gate": p["w_gate"][l].astype(BF16), "b_gate": row(p["b_gate"][l]),
        "w_o": p["w_o"][l].astype(BF16),
        "ln1_g": row(p["ln1_g"][l]), "ln1_b": row(p["ln1_b"][l]),
        "w_router": jnp.pad(w_router, ((0, 0), (0, npad))).astype(F32),
        "b_router": jnp.pad(b_router, (0, npad)).reshape(1, -1).astype(F32),
        "moe_w_gate": p["moe_w_gate"][l].astype(BF16), "moe_w_up": p["moe_w_up"][l].astype(BF16),
        "moe_w_down": p["moe_w_down"][l].astype(BF16),
        "ln2_g": row(p["ln2_g"][l]), "ln2_b": row(p["ln2_b"][l]),
    }


def _trunk(x, p, layers, bias_tables):
    B, S, _ = x.shape
    rope = _rope_tables(S)
    fconsts = _fourier_consts(S)
    h = _input_layer_norm(x.reshape(B * S, D_MODEL), p["ln_in_g"], p["ln_in_b"])
    for lw in layers:
        za, sg, q, k, v, dq, dk, dv = _in_proj(h, S, lw, rope)
        fa = _fourier_mix(za, B, S, fconsts)
        oc = _latent_attention(q, k, v, B, S)
        ods, lses = [], []
        for g, (_, d) in enumerate(DIL_PAIRS):
            o, lse = _dilated_group(dq, dk, dv, bias_tables[g], g, d, B, S)
            ods.append(o)
            lses.append(lse)
        h = _merge(h, fa, sg, oc, jnp.stack(ods), jnp.stack(lses), lw)
        h = _moe(h, lw)
    return h.reshape(B, S, D_MODEL)


def kernel(x_prompt, x_sample, ln_in_g, ln_in_b, rel_bias, w_in, sg_ln_g, sg_ln_b, sg_w, sg_b, mla_q_norm, mla_kv_norm, mla_w_uq, mla_w_ukv, w_branch_a, w_branch_b, w_branch_c, w_branch_d, w_gate, b_gate, w_o, ln1_g, ln1_b, moe_w_rg, moe_b_rg, moe_w_re, moe_b_re, moe_w_gate, moe_w_up, moe_w_down, ln2_g, ln2_b):
    p = dict(ln_in_g=ln_in_g, ln_in_b=ln_in_b, w_in=w_in, sg_ln_g=sg_ln_g, sg_ln_b=sg_ln_b, sg_w=sg_w, sg_b=sg_b,
             mla_q_norm=mla_q_norm, mla_kv_norm=mla_kv_norm, mla_w_uq=mla_w_uq, mla_w_ukv=mla_w_ukv,
             w_branch_a=w_branch_a, w_branch_b=w_branch_b, w_branch_c=w_branch_c, w_branch_d=w_branch_d,
             w_gate=w_gate, b_gate=b_gate, w_o=w_o, ln1_g=ln1_g, ln1_b=ln1_b,
             moe_w_rg=moe_w_rg, moe_b_rg=moe_b_rg, moe_w_re=moe_w_re, moe_b_re=moe_b_re,
             moe_w_gate=moe_w_gate, moe_w_up=moe_w_up, moe_w_down=moe_w_down, ln2_g=ln2_g, ln2_b=ln2_b)
    layers = [_prep_layer(l, p) for l in range(w_in.shape[0])]
    bias_tables = _dilated_bias_tables(rel_bias)
    return _trunk(x_prompt, p, layers, bias_tables), _trunk(x_sample, p, layers, bias_tables)
```

```python
import functools
import math

import numpy as np
import jax
import jax.numpy as jnp
from jax import lax
from jax.experimental import pallas as pl
from jax.experimental.pallas import tpu as pltpu

F32 = jnp.float32
BF16 = jnp.bfloat16

D_MODEL = 1024
DEPTH = 4
F_GROUPS = 4
F_GROUP_DIM = 128
F_WIDTH = F_GROUPS * F_GROUP_DIM
SG_CHUNK = 128
SG_GROUPS = 4
SG_GROUP_DIM = 64
SG_WIDTH = SG_GROUPS * SG_GROUP_DIM
MLA_HEADS = 8
MLA_Q_RANK = 256
MLA_KV_RANK = 192
MLA_NOPE = 64
MLA_ROPE = 64
MLA_V = 64
MLA_QK_DIM = MLA_NOPE + MLA_ROPE
ROPE_BASE = 10000.0
DIL_PAIRS = ((128, 1), (512, 4), (2048, 16))
DIL_GROUPS = 3
DIL_HEADS = 4
DIL_HEAD_DIM = 32
DIL_Q_WIDTH = DIL_GROUPS * DIL_HEADS * DIL_HEAD_DIM
DIL_KV_WIDTH = DIL_HEADS * DIL_HEAD_DIM
DIL_SIDE = 64
REL_BUCKETS = 32
REL_MAX_DIST = 1024
MIX_WIDTH = F_WIDTH + 2 * SG_WIDTH + MLA_Q_RANK + MLA_KV_RANK + MLA_ROPE + DIL_Q_WIDTH + 2 * DIL_KV_WIDTH
MOE_GROUPS = 4
MOE_EXPERTS_PER_GROUP = 4
MOE_EXPERTS = MOE_GROUPS * MOE_EXPERTS_PER_GROUP
MOE_FF = 512
DN_ALPHA = (2 * DEPTH) ** 0.25
LN_EPS = 1e-5
RMS_EPS = 1e-6

OFF_A = 0
OFF_B = OFF_A + F_WIDTH
OFF_CQ = OFF_B + 2 * SG_WIDTH
OFF_CKV = OFF_CQ + MLA_Q_RANK
OFF_DQ = OFF_CKV + MLA_KV_RANK + MLA_ROPE
OFF_DK = OFF_DQ + DIL_Q_WIDTH
OFF_DV = OFF_DK + DIL_KV_WIDTH

LANES = 128
SUBLANES = 8
VMEM_LIMIT_BYTES = 56 * 1024 * 1024
TOKEN_TILE = 512
MOE_TOKEN_TILE = 1024
ATTN_Q_BLOCK = 512
ATTN_Q_SUB = 256
ATTN_KV_BLOCK = 2048
ATTN_CHUNK = 128
ATTN_CHUNKS_PER_ITER = 2
ATTN_PIPELINE_DEPTH = 3
DIL_BLOCK = 128
NEG_BIG = -1e30
LOG2_E = math.log2(math.e)


def _cparams(*sem):
    return pltpu.CompilerParams(dimension_semantics=sem, vmem_limit_bytes=VMEM_LIMIT_BYTES)


def _full(shape):
    n = len(shape)
    return pl.BlockSpec(shape, lambda *_: (0,) * n)


def _layer_norm(x, g, b):
    mu = jnp.mean(x, -1, keepdims=True)
    xc = x - mu
    var = jnp.mean(xc * xc, -1, keepdims=True)
    return xc * lax.rsqrt(var + LN_EPS) * g + b


def _dot(a, b):
    return jnp.dot(a, b, preferred_element_type=F32)


def _ln_kernel(x_ref, g_ref, b_ref, o_ref):
    o_ref[...] = _layer_norm(x_ref[...], g_ref[...], b_ref[...])


def _input_layer_norm(x, g, b):
    T = x.shape[0]
    tm = TOKEN_TILE
    return pl.pallas_call(
        _ln_kernel,
        grid=(T // tm,),
        in_specs=[pl.BlockSpec((tm, D_MODEL), lambda i: (i, 0)), _full((1, D_MODEL)), _full((1, D_MODEL))],
        out_specs=pl.BlockSpec((tm, D_MODEL), lambda i: (i, 0)),
        out_shape=jax.ShapeDtypeStruct((T, D_MODEL), F32),
        compiler_params=_cparams("parallel"),
        name="input_layer_norm",
    )(x, g.reshape(1, -1), b.reshape(1, -1))


def _rope_lanes(t, c, sa, sb):
    return t * c + pltpu.roll(t, 96, 1) * sa + pltpu.roll(t, 32, 1) * sb


def _in_proj_kernel(x_ref, w_in_ref, sg_g_ref, sg_b_ref, sg_w_ref, sg_bias_ref, qn_ref, kvn_ref,
                    wuq_ref, wk_ref, wv_ref, rc_ref, rsa_ref, rsb_ref,
                    za_ref, sg_ref, q_ref, k_ref, vt_ref,
                    dq0_ref, dk0_ref, dv0_ref, dq1_ref, dk1_ref, dv1_ref, dq2_ref, dk2_ref, dv2_ref, dil_ref):
    tm = x_ref.shape[0]
    z = _dot(x_ref[...].astype(BF16), w_in_ref[...])

    za_ref[...] = z[:, OFF_A:OFF_A + F_WIDTH]

    zb = jax.nn.gelu(z[:, OFF_B:OFF_B + 2 * SG_WIDTH])
    u = zb[:, :SG_WIDTH]
    vn = _layer_norm(zb[:, SG_WIDTH:], sg_g_ref[...], sg_b_ref[...]).astype(BF16)
    lane = lax.broadcasted_iota(jnp.int32, (SG_CHUNK, LANES), 1)
    low_half = lane < SG_GROUP_DIM
    for ci in range(tm // SG_CHUNK):
        rows = slice(ci * SG_CHUNK, (ci + 1) * SG_CHUNK)
        for j in range(SG_WIDTH // LANES):
            cols = slice(j * LANES, (j + 1) * LANES)
            vblk = vn[rows, cols]
            mixed = jnp.where(low_half, _dot(sg_w_ref[2 * j], vblk), _dot(sg_w_ref[2 * j + 1], vblk))
            sg_ref[rows, cols] = (u[rows, cols] * (mixed + sg_bias_ref[:, cols])).astype(sg_ref.dtype)

    rc, rsa, rsb = rc_ref[...], rsa_ref[...], rsb_ref[...]

    cq = z[:, OFF_CQ:OFF_CQ + MLA_Q_RANK]
    cq = cq * lax.rsqrt(jnp.mean(cq * cq, -1, keepdims=True) + RMS_EPS) * qn_ref[...]
    q = _dot(cq.astype(BF16), wuq_ref[...]) * (MLA_QK_DIM ** -0.5 * LOG2_E)
    ra = pltpu.roll(q, MLA_HEADS * LANES - 32, 1)
    rb = pltpu.roll(q, 32, 1)
    for h in range(MLA_HEADS):
        cols = slice(h * LANES, (h + 1) * LANES)
        q_ref[:, cols] = (q[:, cols] * rc + ra[:, cols] * rsa + rb[:, cols] * rsb).astype(q_ref.dtype)

    slab = z[:, OFF_CKV:OFF_CKV + 2 * LANES]
    lane2 = lax.broadcasted_iota(jnp.int32, (tm, 2 * LANES), 1)
    ckv_sq = jnp.where(lane2 < MLA_KV_RANK, slab * slab, 0.0)
    ms = jnp.sum(ckv_sq, -1, keepdims=True) * (1.0 / MLA_KV_RANK)
    ckv = (slab * lax.rsqrt(ms + RMS_EPS) * kvn_ref[...]).astype(BF16)
    k_nope = _dot(ckv, wk_ref[...])
    v_t = _dot(ckv, wv_ref[...]).T
    for j in range(tm // ATTN_CHUNK):
        vt_ref[j] = v_t[:, j * ATTN_CHUNK:(j + 1) * ATTN_CHUNK].astype(vt_ref.dtype)
    kr_slab = _rope_lanes(slab[:, LANES:], rc, rsa, rsb)
    kr_slab = jnp.where(lax.broadcasted_iota(jnp.int32, (tm, LANES), 1) >= MLA_NOPE, kr_slab, 0.0)
    for h in range(MLA_HEADS):
        cols = slice(h * LANES, (h + 1) * LANES)
        k_ref[:, cols] = (k_nope[:, cols] + kr_slab).astype(k_ref.dtype)

    for slab in range(DIL_GROUPS + 2):
        t = z[:, OFF_DQ + slab * LANES:OFF_DQ + (slab + 1) * LANES]
        dil_ref[slab] = t * (DIL_HEAD_DIM ** -0.5) if slab < DIL_GROUPS else t
    dq0_ref[...] = dil_ref[0].astype(dq0_ref.dtype)
    dk0_ref[...] = dil_ref[DIL_GROUPS].astype(dk0_ref.dtype)
    dv0_ref[...] = dil_ref[DIL_GROUPS + 1].astype(dv0_ref.dtype)
    for g, (qr, kr, vr) in ((1, (dq1_ref, dk1_ref, dv1_ref)), (2, (dq2_ref, dk2_ref, dv2_ref))):
        d = DIL_PAIRS[g][1]
        for c in range(d):
            rows = pl.ds(c, tm // d, stride=d)
            qr[c] = dil_ref[g, rows, :].astype(qr.dtype)
            kr[c] = dil_ref[DIL_GROUPS, rows, :].astype(kr.dtype)
            vr[c] = dil_ref[DIL_GROUPS + 1, rows, :].astype(vr.dtype)


def _in_proj(x, B, S, lw, rope):
    T = x.shape[0]
    tm = TOKEN_TILE
    nblk = S // tm
    tok = lambda w: pl.BlockSpec((tm, w), lambda i: (i, 0))
    pos = pl.BlockSpec((tm, LANES), lambda i: (i % nblk, 0))
    outs = [(F_WIDTH, F32), (SG_WIDTH, BF16), (MLA_HEADS * LANES, BF16), (MLA_HEADS * LANES, BF16)]
    out_specs = [tok(w) for w, _ in outs]
    out_shape = [jax.ShapeDtypeStruct((T, w), dt) for w, dt in outs]
    out_specs.append(pl.BlockSpec((tm // ATTN_CHUNK, MLA_HEADS * MLA_V, ATTN_CHUNK), lambda i: (i, 0, 0)))
    out_shape.append(jax.ShapeDtypeStruct((T // ATTN_CHUNK, MLA_HEADS * MLA_V, ATTN_CHUNK), BF16))
    for _, d in DIL_PAIRS:
        for _ in range(3):
            if d == 1:
                out_specs.append(tok(LANES))
                out_shape.append(jax.ShapeDtypeStruct((T, LANES), BF16))
            else:
                out_specs.append(pl.BlockSpec((None, d, tm // d, LANES), lambda i: (i // nblk, 0, i % nblk, 0)))
                out_shape.append(jax.ShapeDtypeStruct((B, d, S // d, LANES), BF16))
    return pl.pallas_call(
        _in_proj_kernel,
        grid=(T // tm,),
        in_specs=[tok(D_MODEL), _full((D_MODEL, MIX_WIDTH)), _full((1, SG_WIDTH)), _full((1, SG_WIDTH)),
                  _full((SG_GROUPS, SG_CHUNK, SG_CHUNK)), _full((SG_CHUNK, SG_WIDTH)),
                  _full((1, MLA_Q_RANK)), _full((1, 2 * LANES)),
                  _full((MLA_Q_RANK, MLA_HEADS * LANES)), _full((2 * LANES, MLA_HEADS * LANES)),
                  _full((2 * LANES, MLA_HEADS * MLA_V)), pos, pos, pos],
        out_specs=out_specs,
        out_shape=out_shape,
        scratch_shapes=[pltpu.VMEM((DIL_GROUPS + 2, tm, LANES), F32)],
        compiler_params=_cparams("parallel"),
        name="in_proj",
    )(x, lw["w_in"], lw["sg_ln_g"], lw["sg_ln_b"], lw["sg_w"], lw["sg_bias"], lw["q_norm"], lw["kv_norm"],
      lw["w_uq"], lw["w_k"], lw["w_v"], *rope)


FOURIER_ROWS = SUBLANES


def _fourier1_kernel(x_ref, cs_ref, k1_ref, ct_ref, st_ref, gr_ref, gi_ref):
    n1, r, _ = x_ref.shape
    rows = n1 * r
    x = x_ref[...].reshape(rows, F_WIDTH)
    ct, st = ct_ref[...], st_ref[...]
    outs_r, outs_i = [], []
    for g in range(F_GROUPS):
        ab = _dot(x[:, g * LANES:(g + 1) * LANES].astype(BF16), cs_ref[...])
        stacked = jnp.concatenate([ab[:, :LANES], ab[:, LANES:]], axis=0).astype(BF16)
        g2 = _dot(k1_ref[...], stacked)
        gr, gi = g2[:rows], g2[rows:]
        outs_r.append(gr * ct - gi * st)
        outs_i.append(gr * st + gi * ct)
    gr_ref[...] = jnp.concatenate(outs_r, axis=1).reshape(n1, r, F_WIDTH)
    gi_ref[...] = jnp.concatenate(outs_i, axis=1).reshape(n1, r, F_WIDTH)


def _fourier2_kernel(gr_ref, gi_ref, k2_ref, o_ref):
    r, n2, _ = gr_ref.shape
    stacked = jnp.concatenate([gr_ref[...].reshape(r * n2, F_WIDTH), gi_ref[...].reshape(r * n2, F_WIDTH)],
                              axis=0).astype(BF16)
    o_ref[...] = _dot(k2_ref[...], stacked).reshape(n2, r, F_WIDTH)


def _fourier_mix(za, B, S, consts):
    n1 = S // LANES
    n2 = LANES
    r = FOURIER_ROWS
    cs, k1, ct, st, k2 = consts
    x = za.reshape(B, n1, n2, F_WIDTH)
    blk1 = pl.BlockSpec((None, n1, r, F_WIDTH), lambda b, j: (b, 0, j, 0))
    twid = pl.BlockSpec((None, n1 * r, LANES), lambda b, j: (j, 0, 0))
    gr, gi = pl.pallas_call(
        _fourier1_kernel,
        grid=(B, n2 // r),
        in_specs=[blk1, _full((LANES, 2 * LANES)), _full((2 * n1 * r, 2 * n1 * r)), twid, twid],
        out_specs=[blk1, blk1],
        out_shape=[jax.ShapeDtypeStruct(x.shape, F32)] * 2,
        compiler_params=_cparams("parallel", "parallel"),
        name="fourier_stage1",
    )(x, cs, k1, ct, st)
    y = pl.pallas_call(
        _fourier2_kernel,
        grid=(B, n1 // r),
        in_specs=[pl.BlockSpec((None, r, n2, F_WIDTH), lambda b, j: (b, j, 0, 0))] * 2
                 + [_full((n2 * r, 2 * n2 * r))],
        out_specs=pl.BlockSpec((None, n2, r, F_WIDTH), lambda b, j: (b, 0, j, 0)),
        out_shape=jax.ShapeDtypeStruct((B, n2, n1, F_WIDTH), F32),
        compiler_params=_cparams("parallel", "parallel"),
        name="fourier_stage2",
    )(gr, gi, k2)
    return y.reshape(B * S, F_WIDTH)


def _fourier_consts(S):
    n1 = S // LANES
    n2 = LANES
    c = np.arange(F_GROUP_DIM)
    ang_c = 2.0 * np.pi * np.outer(c, c) / F_GROUP_DIM
    norm = 1.0 / math.sqrt(S * F_GROUP_DIM)
    cs = np.concatenate([np.cos(ang_c), np.sin(ang_c)], axis=1) * norm
    r = FOURIER_ROWS
    eye = np.eye(r)
    a1 = np.arange(n1)
    ang1 = 2.0 * np.pi * np.outer(a1, a1) / n1
    c1, s1 = np.kron(np.cos(ang1), eye), np.kron(np.sin(ang1), eye)
    k1 = np.block([[c1, -s1], [s1, c1]])
    a2 = np.arange(n2)
    ang_t = 2.0 * np.pi * np.outer(a1, a2) / S

    def twiddle(t):
        t = t.reshape(n1, n2 // r, r).transpose(1, 0, 2).reshape(n2 // r, n1 * r)
        return np.broadcast_to(t[:, :, None], (n2 // r, n1 * r, LANES))

    ang2 = 2.0 * np.pi * np.outer(a2, a2) / n2
    spread = lambda w: np.einsum("ks,ab->kabs", w, eye).reshape(n2 * r, r * n2)
    k2 = np.concatenate([spread(np.cos(ang2)), spread(-np.sin(ang2))], axis=1)
    return (jnp.asarray(cs, BF16), jnp.asarray(k1, BF16), jnp.asarray(twiddle(np.cos(ang_t)), F32),
            jnp.asarray(twiddle(np.sin(ang_t)), F32), jnp.asarray(k2, BF16))


def _flash_kernel(q_ref, k_ref, vt_ref, o_ref, m_ref, l_ref, acc_ref):
    ki = pl.program_id(2)
    nsub = q_ref.shape[0] // ATTN_Q_SUB
    nchunk = k_ref.shape[0] // ATTN_CHUNK

    @pl.when(ki == 0)
    def _():
        m_ref[...] = jnp.full(m_ref.shape, NEG_BIG, F32)
        l_ref[...] = jnp.zeros(l_ref.shape, F32)
        acc_ref[...] = jnp.zeros(acc_ref.shape, F32)

    per_chunk = MLA_HEADS * nsub
    n_tiles = ATTN_CHUNKS_PER_ITER * per_chunk

    def tile_index(it, t):
        cc, rem = divmod(t, per_chunk)
        h, qs = divmod(rem, nsub)
        return it * ATTN_CHUNKS_PER_ITER + cc, h, qs

    def scores(it, t):
        c, h, qs = tile_index(it, t)
        r0 = pl.multiple_of(c * ATTN_CHUNK, ATTN_CHUNK)
        kc = k_ref[pl.ds(r0, ATTN_CHUNK), h * LANES:(h + 1) * LANES]
        qh = q_ref[qs * ATTN_Q_SUB:(qs + 1) * ATTN_Q_SUB, h * LANES:(h + 1) * LANES]
        return lax.dot_general(kc, qh, (((1,), (1,)), ((), ())), preferred_element_type=F32)

    def accumulate(it, t, s):
        c, h, qs = tile_index(it, t)
        rows = slice(h * MLA_V, (h + 1) * MLA_V)
        m_prev = m_ref[qs, h]
        m_new = jnp.maximum(m_prev, jnp.max(s, axis=0, keepdims=True))
        alpha = jnp.exp2(m_prev - m_new)
        p = jnp.exp2(s - m_new[0:1, :])
        l_ref[qs, h] = alpha * l_ref[qs, h] + jnp.sum(p, axis=0, keepdims=True)
        m_ref[qs, h] = m_new
        vt = vt_ref[c, rows, :]
        acc_ref[qs, rows, :] = alpha[0:1, :] * acc_ref[qs, rows, :] + _dot(vt, p.astype(BF16))

    def iteration(it, carry):
        pending = {}
        for t in range(n_tiles + ATTN_PIPELINE_DEPTH):
            if t < n_tiles:
                pending[t] = scores(it, t)
            if t >= ATTN_PIPELINE_DEPTH:
                accumulate(it, t - ATTN_PIPELINE_DEPTH, pending.pop(t - ATTN_PIPELINE_DEPTH))
        return carry

    lax.fori_loop(0, nchunk // ATTN_CHUNKS_PER_ITER, iteration, 0)

    @pl.when(ki == pl.num_programs(2) - 1)
    def _():
        for qs in range(nsub):
            out_t = jnp.concatenate(
                [acc_ref[qs, h * MLA_V:(h + 1) * MLA_V, :] * (1.0 / l_ref[qs, h, 0:1, :])
                 for h in range(MLA_HEADS)], axis=0)
            o_ref[qs * ATTN_Q_SUB:(qs + 1) * ATTN_Q_SUB, :] = out_t.T.astype(o_ref.dtype)


def _latent_attention(q, k, vt, B, S):
    tq = min(ATTN_Q_BLOCK, S)
    tk = min(ATTN_KV_BLOCK, S)
    nsub = tq // ATTN_Q_SUB
    q = q.reshape(B, S, MLA_HEADS * LANES)
    k = k.reshape(B, S, MLA_HEADS * LANES)
    vt = vt.reshape(B, S // ATTN_CHUNK, MLA_HEADS * MLA_V, ATTN_CHUNK)
    out = pl.pallas_call(
        _flash_kernel,
        grid=(B, S // tq, S // tk),
        in_specs=[pl.BlockSpec((None, tq, MLA_HEADS * LANES), lambda b, i, j: (b, i, 0)),
                  pl.BlockSpec((None, tk, MLA_HEADS * LANES), lambda b, i, j: (b, j, 0)),
                  pl.BlockSpec((None, tk // ATTN_CHUNK, MLA_HEADS * MLA_V, ATTN_CHUNK),
                               lambda b, i, j: (b, j, 0, 0))],
        out_specs=pl.BlockSpec((None, tq, MLA_HEADS * MLA_V), lambda b, i, j: (b, i, 0)),
        out_shape=jax.ShapeDtypeStruct((B, S, MLA_HEADS * MLA_V), BF16),
        scratch_shapes=[pltpu.VMEM((nsub, MLA_HEADS, SUBLANES, ATTN_Q_SUB), F32),
                        pltpu.VMEM((nsub, MLA_HEADS, SUBLANES, ATTN_Q_SUB), F32),
                        pltpu.VMEM((nsub, MLA_HEADS * MLA_V, ATTN_Q_SUB), F32)],
        compiler_params=_cparams("parallel", "parallel", "arbitrary"),
        name="latent_attention",
    )(q, k, vt)
    return out.reshape(B * S, MLA_HEADS * MLA_V)


def _dilated_kernel(q_ref, kp_ref, kc_ref, kn_ref, vp_ref, vc_ref, vn_ref, bias_ref, o_ref, lse_ref):
    i = pl.program_id(2)
    n = pl.num_programs(2)
    q = q_ref[...]
    k3 = jnp.concatenate([kp_ref[...], kc_ref[...], kn_ref[...]], axis=0)
    v3 = jnp.concatenate([vp_ref[...], vc_ref[...], vn_ref[...]], axis=0)
    v3_t = v3.astype(F32).T.astype(BF16)
    row = lax.broadcasted_iota(jnp.int32, (3 * DIL_BLOCK, DIL_BLOCK), 0)
    in_seq = jnp.logical_and(jnp.logical_or(i > 0, row >= DIL_BLOCK),
                             jnp.logical_or(i < n - 1, row < 2 * DIL_BLOCK))
    lane = lax.broadcasted_iota(jnp.int32, (1, LANES), 1)
    dim = lax.broadcasted_iota(jnp.int32, (LANES, 1), 0)
    out_t = jnp.zeros((LANES, DIL_BLOCK), F32)
    lse_t = jnp.zeros((LANES, DIL_BLOCK), F32)
    for h in range(DIL_HEADS):
        head_lanes = jnp.logical_and(lane >= h * DIL_HEAD_DIM, lane < (h + 1) * DIL_HEAD_DIM)
        head_rows = jnp.logical_and(dim >= h * DIL_HEAD_DIM, dim < (h + 1) * DIL_HEAD_DIM)
        qh = jnp.where(head_lanes, q, jnp.zeros_like(q))
        s = lax.dot_general(k3, qh, (((1,), (1,)), ((), ())), preferred_element_type=F32)
        s = jnp.where(in_seq, s + bias_ref[h], NEG_BIG)
        m = jnp.max(s, axis=0, keepdims=True)
        e = jnp.exp(s - m)
        den = jnp.sum(e, axis=0, keepdims=True)
        pv_t = _dot(v3_t, (e * (1.0 / den)).astype(BF16))
        out_t = jnp.where(head_rows, pv_t, out_t)
        lse_t = jnp.where(head_rows, m + jnp.log(den), lse_t)
    o_ref[...] = out_t.T
    lse_ref[...] = lse_t.T


def _dilated_group(q, k, v, bias_t, g, d, B, S):
    rows = S // d
    nblk = rows // DIL_BLOCK
    shape = (B, d, rows, LANES)
    q, k, v = q.reshape(shape), k.reshape(shape), v.reshape(shape)
    blk = lambda f: pl.BlockSpec((None, None, DIL_BLOCK, LANES), f)
    prev = lambda b, c, i: (b, c, jnp.maximum(i - 1, 0), 0)
    cur = lambda b, c, i: (b, c, i, 0)
    nxt = lambda b, c, i: (b, c, jnp.minimum(i + 1, nblk - 1), 0)
    return pl.pallas_call(
        _dilated_kernel,
        grid=(B, d, nblk),
        in_specs=[blk(cur), blk(prev), blk(cur), blk(nxt), blk(prev), blk(cur), blk(nxt),
                  _full((DIL_HEADS, 3 * DIL_BLOCK, DIL_BLOCK))],
        out_specs=[blk(cur), blk(cur)],
        out_shape=[jax.ShapeDtypeStruct(shape, F32)] * 2,
        compiler_params=_cparams("parallel", "parallel", "parallel"),
        name="dilated_attention_g%d" % g,
    )(q, k, k, k, v, v, v, bias_t)


def _t5_bucket(rel):
    nb = REL_BUCKETS // 2
    max_exact = nb // 2
    ret = jnp.where(rel > 0, nb, 0)
    n = jnp.abs(rel)
    nf = jnp.maximum(n, 1).astype(F32)
    large = max_exact + (jnp.log(nf / max_exact) / math.log(REL_MAX_DIST / max_exact) * (nb - max_exact)).astype(jnp.int32)
    large = jnp.minimum(large, nb - 1)
    return ret + jnp.where(n < max_exact, n, large)


def _dilated_bias_tables(rel_bias):
    qi = np.arange(DIL_BLOCK)[None, :]
    kj = np.arange(3 * DIL_BLOCK)[:, None]
    rel = kj - DIL_BLOCK - qi
    in_band = np.abs(rel) <= DIL_SIDE
    idx = np.clip(rel + DIL_SIDE, 0, 2 * DIL_SIDE)
    tables = []
    for g, (_, d) in enumerate(DIL_PAIRS):
        off = d * jnp.arange(-DIL_SIDE, DIL_SIDE + 1, dtype=jnp.int32)
        b = rel_bias[_t5_bucket(off)][:, g * DIL_HEADS:(g + 1) * DIL_HEADS].astype(F32).T
        tables.append(jnp.where(in_band[None], b[:, idx], NEG_BIG))
    return tables


def _merge_kernel(x_ref, fa_ref, sg_ref, oc_ref, o0_ref, l0_ref, o1_ref, l1_ref, o2_ref, l2_ref,
                  wa_ref, wb_ref, wc_ref, wd_ref, wg_ref, bg_ref, wo_ref, g_ref, b_ref, o_ref, nat_ref):
    x = x_ref[...]
    xb = x.astype(BF16)
    tm = x.shape[0]
    for slot, (src, g) in enumerate(((o1_ref, 1), (l1_ref, 1), (o2_ref, 2), (l2_ref, 2))):
        d = DIL_PAIRS[g][1]
        for c in range(d):
            nat_ref[slot, pl.ds(c, tm // d, stride=d), :] = src[c]
    l0, l1, l2 = l0_ref[...], nat_ref[1], nat_ref[3]
    mx = jnp.maximum(jnp.maximum(l0, l1), l2)
    e0, e1, e2 = jnp.exp(l0 - mx), jnp.exp(l1 - mx), jnp.exp(l2 - mx)
    den = e0 + e1 + e2
    od = o0_ref[...] * (e0 / den) + nat_ref[0] * (e1 / den) + nat_ref[2] * (e2 / den)
    branches = ((fa_ref[...].astype(BF16), wa_ref), (sg_ref[...], wb_ref), (oc_ref[...], wc_ref),
                (od.astype(BF16), wd_ref))
    merged = None
    for i, (act, w_ref) in enumerate(branches):
        cols = slice(i * D_MODEL, (i + 1) * D_MODEL)
        gate = jax.nn.sigmoid(_dot(xb, wg_ref[:, cols]) + bg_ref[:, cols])
        term = gate * _dot(act, w_ref[...])
        merged = term if merged is None else merged + term
    y = DN_ALPHA * x + _dot(merged.astype(BF16), wo_ref[...])
    o_ref[...] = _layer_norm(y, g_ref[...], b_ref[...])


def _merge(x, S, fa, sg, oc, dil, lw):
    T = x.shape[0]
    tm = TOKEN_TILE
    nblk = S // tm
    tok = lambda w: pl.BlockSpec((tm, w), lambda i: (i, 0))
    dil_specs, dil_args = [], []
    for (o, lse), (_, d) in zip(dil, DIL_PAIRS):
        for a in (o, lse):
            if d == 1:
                dil_specs.append(tok(LANES))
                dil_args.append(a.reshape(T, LANES))
            else:
                dil_specs.append(pl.BlockSpec((None, d, tm // d, LANES), lambda i: (i // nblk, 0, i % nblk, 0)))
                dil_args.append(a)
    return pl.pallas_call(
        _merge_kernel,
        grid=(T // tm,),
        in_specs=[tok(D_MODEL), tok(F_WIDTH), tok(SG_WIDTH), tok(MLA_HEADS * MLA_V)] + dil_specs + [
                  _full((F_WIDTH, D_MODEL)), _full((SG_WIDTH, D_MODEL)), _full((MLA_HEADS * MLA_V, D_MODEL)),
                  _full((DIL_KV_WIDTH, D_MODEL)), _full((D_MODEL, 4 * D_MODEL)), _full((1, 4 * D_MODEL)),
                  _full((D_MODEL, D_MODEL)), _full((1, D_MODEL)), _full((1, D_MODEL))],
        out_specs=tok(D_MODEL),
        out_shape=jax.ShapeDtypeStruct((T, D_MODEL), F32),
        scratch_shapes=[pltpu.VMEM((4, tm, LANES), F32)],
        compiler_params=_cparams("parallel"),
        name="merge",
    )(x, fa, sg, oc, *dil_args, lw["w_a"], lw["w_b"], lw["w_c"], lw["w_d"], lw["w_gate"], lw["b_gate"],
      lw["w_o"], lw["ln1_g"], lw["ln1_b"])


def _routing_weights(logits):
    tm = logits.shape[0]
    lane = lax.broadcasted_iota(jnp.int32, (tm, LANES), 1)
    is_g = lane < MOE_GROUPS
    gl = jnp.where(is_g, logits, NEG_BIG)
    gmax = jnp.max(gl, -1, keepdims=True)
    g_top = jnp.min(jnp.where(gl == gmax, lane, LANES), -1, keepdims=True)
    p_group = 1.0 / jnp.sum(jnp.where(is_g, jnp.exp(gl - gmax), 0.0), -1, keepdims=True)
    lo = MOE_GROUPS + g_top * MOE_EXPERTS_PER_GROUP
    in_grp = jnp.logical_and(lane >= lo, lane < lo + MOE_EXPERTS_PER_GROUP)
    el = jnp.where(in_grp, logits, NEG_BIG)
    v1 = jnp.max(el, -1, keepdims=True)
    i1 = jnp.min(jnp.where(el == v1, lane, LANES), -1, keepdims=True)
    el2 = jnp.where(lane == i1, NEG_BIG, el)
    v2 = jnp.max(el2, -1, keepdims=True)
    i2 = jnp.min(jnp.where(el2 == v2, lane, LANES), -1, keepdims=True)
    e2 = jnp.exp(v2 - v1)
    p1 = p_group / (1.0 + e2)
    p2 = p_group * e2 / (1.0 + e2)
    return jnp.where(lane == i1, p1, 0.0) + jnp.where(lane == i2, p2, 0.0)


def _moe_kernel(x_ref, wr_ref, br_ref, wg_ref, wu_ref, wd_ref, g_ref, b_ref, o_ref, comb_ref, acc_ref):
    e = pl.program_id(1)

    @pl.when(e == 0)
    def _():
        logits = jnp.dot(x_ref[...], wr_ref[...], preferred_element_type=F32,
                         precision=lax.Precision.HIGHEST) + br_ref[...]
        comb_ref[...] = _routing_weights(logits)
        acc_ref[...] = jnp.zeros(acc_ref.shape, F32)

    xb = x_ref[...].astype(BF16)
    lane = lax.broadcasted_iota(jnp.int32, comb_ref.shape, 1)
    w = jnp.sum(jnp.where(lane == MOE_GROUPS + e, comb_ref[...], 0.0), -1, keepdims=True)
    h = jax.nn.silu(_dot(xb, wg_ref[...])) * _dot(xb, wu_ref[...]) * w
    acc_ref[...] += _dot(h.astype(BF16), wd_ref[...])

    @pl.when(e == pl.num_programs(1) - 1)
    def _():
        o_ref[...] = _layer_norm(DN_ALPHA * x_ref[...] + acc_ref[...], g_ref[...], b_ref[...])


def _moe(x, lw):
    T = x.shape[0]
    tm = MOE_TOKEN_TILE
    return pl.pallas_call(
        _moe_kernel,
        grid=(T // tm, MOE_EXPERTS),
        in_specs=[pl.BlockSpec((tm, D_MODEL), lambda i, e: (i, 0)),
                  _full((D_MODEL, LANES)), _full((1, LANES)),
                  pl.BlockSpec((None, D_MODEL, MOE_FF), lambda i, e: (e, 0, 0)),
                  pl.BlockSpec((None, D_MODEL, MOE_FF), lambda i, e: (e, 0, 0)),
                  pl.BlockSpec((None, MOE_FF, D_MODEL), lambda i, e: (e, 0, 0)),
                  _full((1, D_MODEL)), _full((1, D_MODEL))],
        out_specs=pl.BlockSpec((tm, D_MODEL), lambda i, e: (i, 0)),
        out_shape=jax.ShapeDtypeStruct((T, D_MODEL), F32),
        scratch_shapes=[pltpu.VMEM((tm, LANES), F32), pltpu.VMEM((tm, D_MODEL), F32)],
        compiler_params=_cparams("parallel", "arbitrary"),
        name="moe",
    )(x, lw["w_router"], lw["b_router"], lw["moe_w_gate"], lw["moe_w_up"], lw["moe_w_down"],
      lw["ln2_g"], lw["ln2_b"])


def _rope_tables(S):
    half = MLA_ROPE // 2
    inv = ROPE_BASE ** (-jnp.arange(half, dtype=F32) / half)
    ang = jnp.arange(S, dtype=F32)[:, None] * inv[None, :]
    cos, sin = jnp.cos(ang), jnp.sin(ang)
    one = jnp.ones((S, MLA_NOPE), F32)
    zero = jnp.zeros((S, MLA_NOPE), F32)
    zh = jnp.zeros((S, half), F32)
    rc = jnp.concatenate([one, cos, cos], axis=1)
    rsa = jnp.concatenate([zero, -sin, zh], axis=1)
    rsb = jnp.concatenate([zero, zh, sin], axis=1)
    return rc, rsa, rsb


def _prep_layer(l, p):
    row = lambda a: a.reshape(1, -1).astype(F32)
    w_ukv = p["mla_w_ukv"][l].reshape(MLA_KV_RANK, MLA_HEADS, MLA_NOPE + MLA_V)
    pad_rows = ((0, 2 * LANES - MLA_KV_RANK), (0, 0))
    w_k = jnp.pad(w_ukv[:, :, :MLA_NOPE], ((0, 0), (0, 0), (0, LANES - MLA_NOPE))).reshape(MLA_KV_RANK, -1)
    w_v = w_ukv[:, :, MLA_NOPE:].reshape(MLA_KV_RANK, -1)
    w_router = jnp.concatenate([p["moe_w_rg"][l], p["moe_w_re"][l]], axis=1)
    b_router = jnp.concatenate([p["moe_b_rg"][l], p["moe_b_re"][l]])
    npad = LANES - MOE_GROUPS - MOE_EXPERTS
    return {
        "w_in": p["w_in"][l].astype(BF16),
        "sg_ln_g": row(p["sg_ln_g"][l]), "sg_ln_b": row(p["sg_ln_b"][l]),
        "sg_w": p["sg_w"][l].astype(BF16),
        "sg_bias": jnp.repeat(p["sg_b"][l].T, SG_GROUP_DIM, axis=1).astype(F32),
        "q_norm": row(p["mla_q_norm"][l]),
        "kv_norm": jnp.pad(row(p["mla_kv_norm"][l]), ((0, 0), (0, 2 * LANES - MLA_KV_RANK))),
        "w_uq": p["mla_w_uq"][l].astype(BF16),
        "w_k": jnp.pad(w_k, pad_rows).astype(BF16),
        "w_v": jnp.pad(w_v, pad_rows).astype(BF16),
        "w_a": p["w_branch_a"][l].astype(BF16), "w_b": p["w_branch_b"][l].astype(BF16),
        "w_c": p["w_branch_c"][l].astype(BF16), "w_d": p["w_branch_d"][l].astype(BF16),
        "w_gate": p["w_gate"][l].astype(BF16), "b_gate": row(p["b_gate"][l]),
        "w_o": p["w_o"][l].astype(BF16),
        "ln1_g": row(p["ln1_g"][l]), "ln1_b": row(p["ln1_b"][l]),
        "w_router": jnp.pad(w_router, ((0, 0), (0, npad))).astype(F32),
        "b_router": jnp.pad(b_router, (0, npad)).reshape(1, -1).astype(F32),
        "moe_w_gate": p["moe_w_gate"][l].astype(BF16), "moe_w_up": p["moe_w_up"][l].astype(BF16),
        "moe_w_down": p["moe_w_down"][l].astype(BF16),
        "ln2_g": row(p["ln2_g"][l]), "ln2_b": row(p["ln2_b"][l]),
    }


def _trunk(x, p, layers, bias_tables):
    B, S, _ = x.shape
    rope = _rope_tables(S)
    fconsts = _fourier_consts(S)
    h = _input_layer_norm(x.reshape(B * S, D_MODEL), p["ln_in_g"], p["ln_in_b"])
    for lw in layers:
        za, sg, q, k, vt, *dil_in = _in_proj(h, B, S, lw, rope)
        fa = _fourier_mix(za, B, S, fconsts)
        oc = _latent_attention(q, k, vt, B, S)
        dil = [_dilated_group(*dil_in[3 * g:3 * g + 3], bias_tables[g], g, d, B, S)
               for g, (_, d) in enumerate(DIL_PAIRS)]
        h = _merge(h, S, fa, sg, oc, dil, lw)
        h = _moe(h, lw)
    return h.reshape(B, S, D_MODEL)


def kernel(x_prompt, x_sample, ln_in_g, ln_in_b, rel_bias, w_in, sg_ln_g, sg_ln_b, sg_w, sg_b, mla_q_norm, mla_kv_norm, mla_w_uq, mla_w_ukv, w_branch_a, w_branch_b, w_branch_c, w_branch_d, w_gate, b_gate, w_o, ln1_g, ln1_b, moe_w_rg, moe_b_rg, moe_w_re, moe_b_re, moe_w_gate, moe_w_up, moe_w_down, ln2_g, ln2_b):
    p = dict(ln_in_g=ln_in_g, ln_in_b=ln_in_b, w_in=w_in, sg_ln_g=sg_ln_g, sg_ln_b=sg_ln_b, sg_w=sg_w, sg_b=sg_b,
             mla_q_norm=mla_q_norm, mla_kv_norm=mla_kv_norm, mla_w_uq=mla_w_uq, mla_w_ukv=mla_w_ukv,
             w_branch_a=w_branch_a, w_branch_b=w_branch_b, w_branch_c=w_branch_c, w_branch_d=w_branch_d,
             w_gate=w_gate, b_gate=b_gate, w_o=w_o, ln1_g=ln1_g, ln1_b=ln1_b,
             moe_w_rg=moe_w_rg, moe_b_rg=moe_b_rg, moe_w_re=moe_w_re, moe_b_re=moe_b_re,
             moe_w_gate=moe_w_gate, moe_w_up=moe_w_up, moe_w_down=moe_w_down, ln2_g=ln2_g, ln2_b=ln2_b)
    layers = [_prep_layer(l, p) for l in range(w_in.shape[0])]
    bias_tables = _dilated_bias_tables(rel_bias)
    return _trunk(x_prompt, p, layers, bias_tables), _trunk(x_sample, p, layers, bias_tables)
```

```python
import functools
import math

import numpy as np
import jax
import jax.numpy as jnp
from jax import lax
from jax.experimental import pallas as pl
from jax.experimental.pallas import tpu as pltpu

F32 = jnp.float32
BF16 = jnp.bfloat16

D_MODEL = 1024
DEPTH = 4
F_GROUPS = 4
F_GROUP_DIM = 128
F_WIDTH = F_GROUPS * F_GROUP_DIM
SG_CHUNK = 128
SG_GROUPS = 4
SG_GROUP_DIM = 64
SG_WIDTH = SG_GROUPS * SG_GROUP_DIM
MLA_HEADS = 8
MLA_Q_RANK = 256
MLA_KV_RANK = 192
MLA_NOPE = 64
MLA_ROPE = 64
MLA_V = 64
MLA_QK_DIM = MLA_NOPE + MLA_ROPE
ROPE_BASE = 10000.0
DIL_PAIRS = ((128, 1), (512, 4), (2048, 16))
DIL_GROUPS = 3
DIL_HEADS = 4
DIL_HEAD_DIM = 32
DIL_Q_WIDTH = DIL_GROUPS * DIL_HEADS * DIL_HEAD_DIM
DIL_KV_WIDTH = DIL_HEADS * DIL_HEAD_DIM
DIL_SIDE = 64
REL_BUCKETS = 32
REL_MAX_DIST = 1024
MIX_WIDTH = F_WIDTH + 2 * SG_WIDTH + MLA_Q_RANK + MLA_KV_RANK + MLA_ROPE + DIL_Q_WIDTH + 2 * DIL_KV_WIDTH
MOE_GROUPS = 4
MOE_EXPERTS_PER_GROUP = 4
MOE_EXPERTS = MOE_GROUPS * MOE_EXPERTS_PER_GROUP
MOE_FF = 512
DN_ALPHA = (2 * DEPTH) ** 0.25
LN_EPS = 1e-5
RMS_EPS = 1e-6

OFF_A = 0
OFF_B = OFF_A + F_WIDTH
OFF_CQ = OFF_B + 2 * SG_WIDTH
OFF_CKV = OFF_CQ + MLA_Q_RANK
OFF_DQ = OFF_CKV + MLA_KV_RANK + MLA_ROPE
OFF_DK = OFF_DQ + DIL_Q_WIDTH
OFF_DV = OFF_DK + DIL_KV_WIDTH

LANES = 128
SUBLANES = 8
VMEM_LIMIT_BYTES = 56 * 1024 * 1024
TOKEN_TILE = 512
MOE_TOKEN_TILE = 1024
ATTN_Q_BLOCK = 512
ATTN_Q_SUB = 256
ATTN_KV_BLOCK = 2048
ATTN_CHUNK = 256
ATTN_CHUNKS_PER_ITER = 2
ATTN_PIPELINE_DEPTH = 6
ATTN_V_ROWS = 80
DIL_BLOCK = 128
NEG_BIG = -1e30
LOG2_E = math.log2(math.e)


def _cparams(*sem):
    return pltpu.CompilerParams(dimension_semantics=sem, vmem_limit_bytes=VMEM_LIMIT_BYTES)


def _full(shape):
    n = len(shape)
    return pl.BlockSpec(shape, lambda *_: (0,) * n)


def _layer_norm(x, g, b):
    mu = jnp.mean(x, -1, keepdims=True)
    xc = x - mu
    var = jnp.mean(xc * xc, -1, keepdims=True)
    return xc * lax.rsqrt(var + LN_EPS) * g + b


def _dot(a, b):
    return jnp.dot(a, b, preferred_element_type=F32)


def _ln_kernel(x_ref, g_ref, b_ref, o_ref):
    o_ref[...] = _layer_norm(x_ref[...], g_ref[...], b_ref[...])


def _input_layer_norm(x, g, b):
    T = x.shape[0]
    tm = TOKEN_TILE
    return pl.pallas_call(
        _ln_kernel,
        grid=(T // tm,),
        in_specs=[pl.BlockSpec((tm, D_MODEL), lambda i: (i, 0)), _full((1, D_MODEL)), _full((1, D_MODEL))],
        out_specs=pl.BlockSpec((tm, D_MODEL), lambda i: (i, 0)),
        out_shape=jax.ShapeDtypeStruct((T, D_MODEL), F32),
        compiler_params=_cparams("parallel"),
        name="input_layer_norm",
    )(x, g.reshape(1, -1), b.reshape(1, -1))


def _rope_lanes(t, c, sa, sb):
    return t * c + pltpu.roll(t, 96, 1) * sa + pltpu.roll(t, 32, 1) * sb


def _in_proj_kernel(x_ref, w_in_ref, sg_g_ref, sg_b_ref, sg_w_ref, sg_bias_ref, qn_ref, kvn_ref,
                    wuq_ref, wk_ref, wv_ref, rc_ref, rsa_ref, rsb_ref,
                    za_ref, sg_ref, qt_ref, k_ref, vt_ref,
                    dq0_ref, dk0_ref, dv0_ref, dq1_ref, dk1_ref, dv1_ref, dq2_ref, dk2_ref, dv2_ref, dil_ref):
    tm = x_ref.shape[0]
    z = _dot(x_ref[...].astype(BF16), w_in_ref[...])

    za_ref[...] = z[:, OFF_A:OFF_A + F_WIDTH]

    zb = jax.nn.gelu(z[:, OFF_B:OFF_B + 2 * SG_WIDTH])
    u = zb[:, :SG_WIDTH]
    vn = _layer_norm(zb[:, SG_WIDTH:], sg_g_ref[...], sg_b_ref[...]).astype(BF16)
    lane = lax.broadcasted_iota(jnp.int32, (SG_CHUNK, LANES), 1)
    low_half = lane < SG_GROUP_DIM
    for ci in range(tm // SG_CHUNK):
        rows = slice(ci * SG_CHUNK, (ci + 1) * SG_CHUNK)
        for j in range(SG_WIDTH // LANES):
            cols = slice(j * LANES, (j + 1) * LANES)
            vblk = vn[rows, cols]
            mixed = jnp.where(low_half, _dot(sg_w_ref[2 * j], vblk), _dot(sg_w_ref[2 * j + 1], vblk))
            sg_ref[rows, cols] = (u[rows, cols] * (mixed + sg_bias_ref[:, cols])).astype(sg_ref.dtype)

    rc, rsa, rsb = rc_ref[...], rsa_ref[...], rsb_ref[...]

    cq = z[:, OFF_CQ:OFF_CQ + MLA_Q_RANK]
    cq = cq * lax.rsqrt(jnp.mean(cq * cq, -1, keepdims=True) + RMS_EPS) * qn_ref[...]
    q = _dot(cq.astype(BF16), wuq_ref[...]) * (MLA_QK_DIM ** -0.5 * LOG2_E)
    ra = pltpu.roll(q, MLA_HEADS * LANES - 32, 1)
    rb = pltpu.roll(q, 32, 1)
    q_t = jnp.concatenate(
        [q[:, h * LANES:(h + 1) * LANES] * rc + ra[:, h * LANES:(h + 1) * LANES] * rsa
         + rb[:, h * LANES:(h + 1) * LANES] * rsb for h in range(MLA_HEADS)], axis=1).T
    for j in range(tm // ATTN_Q_SUB):
        qt_ref[j] = q_t[:, j * ATTN_Q_SUB:(j + 1) * ATTN_Q_SUB].astype(qt_ref.dtype)

    slab = z[:, OFF_CKV:OFF_CKV + 2 * LANES]
    lane2 = lax.broadcasted_iota(jnp.int32, (tm, 2 * LANES), 1)
    ckv_sq = jnp.where(lane2 < MLA_KV_RANK, slab * slab, 0.0)
    ms = jnp.sum(ckv_sq, -1, keepdims=True) * (1.0 / MLA_KV_RANK)
    ckv = (slab * lax.rsqrt(ms + RMS_EPS) * kvn_ref[...]).astype(BF16)
    k_nope = _dot(ckv, wk_ref[...])
    v_t = _dot(ckv, wv_ref[...]).T.astype(vt_ref.dtype)
    ones = jnp.ones((ATTN_V_ROWS - MLA_V, ATTN_CHUNK), vt_ref.dtype)
    for j in range(tm // ATTN_CHUNK):
        cols = slice(j * ATTN_CHUNK, (j + 1) * ATTN_CHUNK)
        for h in range(MLA_HEADS):
            vt_ref[j, h * ATTN_V_ROWS:h * ATTN_V_ROWS + MLA_V, :] = v_t[h * MLA_V:(h + 1) * MLA_V, cols]
            vt_ref[j, h * ATTN_V_ROWS + MLA_V:(h + 1) * ATTN_V_ROWS, :] = ones
    kr_slab = _rope_lanes(slab[:, LANES:], rc, rsa, rsb)
    kr_slab = jnp.where(lax.broadcasted_iota(jnp.int32, (tm, LANES), 1) >= MLA_NOPE, kr_slab, 0.0)
    for h in range(MLA_HEADS):
        cols = slice(h * LANES, (h + 1) * LANES)
        k_ref[:, cols] = (k_nope[:, cols] + kr_slab).astype(k_ref.dtype)

    for slab in range(DIL_GROUPS + 2):
        t = z[:, OFF_DQ + slab * LANES:OFF_DQ + (slab + 1) * LANES]
        dil_ref[slab] = t * (DIL_HEAD_DIM ** -0.5) if slab < DIL_GROUPS else t
    dq0_ref[...] = dil_ref[0].astype(dq0_ref.dtype)
    dk0_ref[...] = dil_ref[DIL_GROUPS].astype(dk0_ref.dtype)
    dv0_ref[...] = dil_ref[DIL_GROUPS + 1].astype(dv0_ref.dtype)
    for g, (qr, kr, vr) in ((1, (dq1_ref, dk1_ref, dv1_ref)), (2, (dq2_ref, dk2_ref, dv2_ref))):
        d = DIL_PAIRS[g][1]
        for c in range(d):
            rows = pl.ds(c, tm // d, stride=d)
            qr[c] = dil_ref[g, rows, :].astype(qr.dtype)
            kr[c] = dil_ref[DIL_GROUPS, rows, :].astype(kr.dtype)
            vr[c] = dil_ref[DIL_GROUPS + 1, rows, :].astype(vr.dtype)


def _in_proj(x, B, S, lw, rope):
    T = x.shape[0]
    tm = TOKEN_TILE
    nblk = S // tm
    tok = lambda w: pl.BlockSpec((tm, w), lambda i: (i, 0))
    pos = pl.BlockSpec((tm, LANES), lambda i: (i % nblk, 0))
    outs = [(F_WIDTH, F32), (SG_WIDTH, BF16)]
    out_specs = [tok(w) for w, _ in outs]
    out_shape = [jax.ShapeDtypeStruct((T, w), dt) for w, dt in outs]
    out_specs.append(pl.BlockSpec((tm // ATTN_Q_SUB, MLA_HEADS * LANES, ATTN_Q_SUB), lambda i: (i, 0, 0)))
    out_shape.append(jax.ShapeDtypeStruct((T // ATTN_Q_SUB, MLA_HEADS * LANES, ATTN_Q_SUB), BF16))
    out_specs.append(tok(MLA_HEADS * LANES))
    out_shape.append(jax.ShapeDtypeStruct((T, MLA_HEADS * LANES), BF16))
    out_specs.append(pl.BlockSpec((tm // ATTN_CHUNK, MLA_HEADS * ATTN_V_ROWS, ATTN_CHUNK), lambda i: (i, 0, 0)))
    out_shape.append(jax.ShapeDtypeStruct((T // ATTN_CHUNK, MLA_HEADS * ATTN_V_ROWS, ATTN_CHUNK), BF16))
    for _, d in DIL_PAIRS:
        for _ in range(3):
            if d == 1:
                out_specs.append(tok(LANES))
                out_shape.append(jax.ShapeDtypeStruct((T, LANES), BF16))
            else:
                out_specs.append(pl.BlockSpec((None, d, tm // d, LANES), lambda i: (i // nblk, 0, i % nblk, 0)))
                out_shape.append(jax.ShapeDtypeStruct((B, d, S // d, LANES), BF16))
    return pl.pallas_call(
        _in_proj_kernel,
        grid=(T // tm,),
        in_specs=[tok(D_MODEL), _full((D_MODEL, MIX_WIDTH)), _full((1, SG_WIDTH)), _full((1, SG_WIDTH)),
                  _full((SG_GROUPS, SG_CHUNK, SG_CHUNK)), _full((SG_CHUNK, SG_WIDTH)),
                  _full((1, MLA_Q_RANK)), _full((1, 2 * LANES)),
                  _full((MLA_Q_RANK, MLA_HEADS * LANES)), _full((2 * LANES, MLA_HEADS * LANES)),
                  _full((2 * LANES, MLA_HEADS * MLA_V)), pos, pos, pos],
        out_specs=out_specs,
        out_shape=out_shape,
        scratch_shapes=[pltpu.VMEM((DIL_GROUPS + 2, tm, LANES), F32)],
        compiler_params=_cparams("parallel"),
        name="in_proj",
    )(x, lw["w_in"], lw["sg_ln_g"], lw["sg_ln_b"], lw["sg_w"], lw["sg_bias"], lw["q_norm"], lw["kv_norm"],
      lw["w_uq"], lw["w_k"], lw["w_v"], *rope)


FOURIER_ROWS = SUBLANES


def _fourier1_kernel(x_ref, cs_ref, k1_ref, ct_ref, st_ref, gr_ref, gi_ref):
    n1, r, _ = x_ref.shape
    rows = n1 * r
    x = x_ref[...].reshape(rows, F_WIDTH)
    ct, st = ct_ref[...], st_ref[...]
    outs_r, outs_i = [], []
    for g in range(F_GROUPS):
        ab = _dot(x[:, g * LANES:(g + 1) * LANES].astype(BF16), cs_ref[...])
        stacked = jnp.concatenate([ab[:, :LANES], ab[:, LANES:]], axis=0).astype(BF16)
        g2 = _dot(k1_ref[...], stacked)
        gr, gi = g2[:rows], g2[rows:]
        outs_r.append(gr * ct - gi * st)
        outs_i.append(gr * st + gi * ct)
    gr_ref[...] = jnp.concatenate(outs_r, axis=1).reshape(n1, r, F_WIDTH)
    gi_ref[...] = jnp.concatenate(outs_i, axis=1).reshape(n1, r, F_WIDTH)


def _fourier2_kernel(gr_ref, gi_ref, k2_ref, o_ref):
    r, n2, _ = gr_ref.shape
    stacked = jnp.concatenate([gr_ref[...].reshape(r * n2, F_WIDTH), gi_ref[...].reshape(r * n2, F_WIDTH)],
                              axis=0).astype(BF16)
    o_ref[...] = _dot(k2_ref[...], stacked).reshape(n2, r, F_WIDTH)


def _fourier_mix(za, B, S, consts):
    n1 = S // LANES
    n2 = LANES
    r = FOURIER_ROWS
    cs, k1, ct, st, k2 = consts
    x = za.reshape(B, n1, n2, F_WIDTH)
    blk1 = pl.BlockSpec((None, n1, r, F_WIDTH), lambda b, j: (b, 0, j, 0))
    twid = pl.BlockSpec((None, n1 * r, LANES), lambda b, j: (j, 0, 0))
    gr, gi = pl.pallas_call(
        _fourier1_kernel,
        grid=(B, n2 // r),
        in_specs=[blk1, _full((LANES, 2 * LANES)), _full((2 * n1 * r, 2 * n1 * r)), twid, twid],
        out_specs=[blk1, blk1],
        out_shape=[jax.ShapeDtypeStruct(x.shape, F32)] * 2,
        compiler_params=_cparams("parallel", "parallel"),
        name="fourier_stage1",
    )(x, cs, k1, ct, st)
    y = pl.pallas_call(
        _fourier2_kernel,
        grid=(B, n1 // r),
        in_specs=[pl.BlockSpec((None, r, n2, F_WIDTH), lambda b, j: (b, j, 0, 0))] * 2
                 + [_full((n2 * r, 2 * n2 * r))],
        out_specs=pl.BlockSpec((None, n2, r, F_WIDTH), lambda b, j: (b, 0, j, 0)),
        out_shape=jax.ShapeDtypeStruct((B, n2, n1, F_WIDTH), F32),
        compiler_params=_cparams("parallel", "parallel"),
        name="fourier_stage2",
    )(gr, gi, k2)
    return y.reshape(B * S, F_WIDTH)


def _fourier_consts(S):
    n1 = S // LANES
    n2 = LANES
    c = np.arange(F_GROUP_DIM)
    ang_c = 2.0 * np.pi * np.outer(c, c) / F_GROUP_DIM
    norm = 1.0 / math.sqrt(S * F_GROUP_DIM)
    cs = np.concatenate([np.cos(ang_c), np.sin(ang_c)], axis=1) * norm
    r = FOURIER_ROWS
    eye = np.eye(r)
    a1 = np.arange(n1)
    ang1 = 2.0 * np.pi * np.outer(a1, a1) / n1
    c1, s1 = np.kron(np.cos(ang1), eye), np.kron(np.sin(ang1), eye)
    k1 = np.block([[c1, -s1], [s1, c1]])
    a2 = np.arange(n2)
    ang_t = 2.0 * np.pi * np.outer(a1, a2) / S

    def twiddle(t):
        t = t.reshape(n1, n2 // r, r).transpose(1, 0, 2).reshape(n2 // r, n1 * r)
        return np.broadcast_to(t[:, :, None], (n2 // r, n1 * r, LANES))

    ang2 = 2.0 * np.pi * np.outer(a2, a2) / n2
    spread = lambda w: np.einsum("ks,ab->kabs", w, eye).reshape(n2 * r, r * n2)
    k2 = np.concatenate([spread(np.cos(ang2)), spread(-np.sin(ang2))], axis=1)
    return (jnp.asarray(cs, BF16), jnp.asarray(k1, BF16), jnp.asarray(twiddle(np.cos(ang_t)), F32),
            jnp.asarray(twiddle(np.sin(ang_t)), F32), jnp.asarray(k2, BF16))


def _flash_kernel(qt_ref, k_ref, vt_ref, o_ref, m_ref, l_ref, acc_ref):
    ki = pl.program_id(2)
    nsub = qt_ref.shape[0]
    nchunk = k_ref.shape[0] // ATTN_CHUNK

    @pl.when(ki == 0)
    def _():
        m_ref[...] = jnp.full(m_ref.shape, NEG_BIG, F32)
        l_ref[...] = jnp.zeros(l_ref.shape, F32)
        acc_ref[...] = jnp.zeros(acc_ref.shape, F32)

    per_chunk = MLA_HEADS * nsub
    n_tiles = ATTN_CHUNKS_PER_ITER * per_chunk

    def tile_index(it, t):
        cc, rem = divmod(t, per_chunk)
        h, qs = divmod(rem, nsub)
        return it * ATTN_CHUNKS_PER_ITER + cc, h, qs

    def scores(it, t):
        c, h, qs = tile_index(it, t)
        r0 = pl.multiple_of(c * ATTN_CHUNK, ATTN_CHUNK)
        kc = k_ref[pl.ds(r0, ATTN_CHUNK), h * LANES:(h + 1) * LANES]
        return _dot(kc, qt_ref[qs, h * LANES:(h + 1) * LANES, :])

    def accumulate(it, t, s):
        c, h, qs = tile_index(it, t)
        rows = slice(h * MLA_V, (h + 1) * MLA_V)
        m_prev = m_ref[qs, h]
        m_new = jnp.maximum(m_prev, jnp.max(s, axis=0, keepdims=True))
        alpha = jnp.exp2(m_prev - m_new)
        p = jnp.exp2(s - m_new[0:1, :]).astype(BF16)
        m_ref[qs, h] = m_new
        pv = _dot(vt_ref[c, h * ATTN_V_ROWS:(h + 1) * ATTN_V_ROWS, :], p)
        l_ref[qs, h] = alpha * l_ref[qs, h] + pv[MLA_V:MLA_V + SUBLANES, :]
        acc_ref[qs, rows, :] = alpha[0:1, :] * acc_ref[qs, rows, :] + pv[:MLA_V, :]

    def iteration(it, carry):
        pending = {}
        for t in range(n_tiles + ATTN_PIPELINE_DEPTH):
            if t < n_tiles:
                pending[t] = scores(it, t)
            if t >= ATTN_PIPELINE_DEPTH:
                accumulate(it, t - ATTN_PIPELINE_DEPTH, pending.pop(t - ATTN_PIPELINE_DEPTH))
        return carry

    lax.fori_loop(0, nchunk // ATTN_CHUNKS_PER_ITER, iteration, 0)

    @pl.when(ki == pl.num_programs(2) - 1)
    def _():
        for qs in range(nsub):
            out_t = jnp.concatenate(
                [acc_ref[qs, h * MLA_V:(h + 1) * MLA_V, :] * (1.0 / l_ref[qs, h, 0:1, :])
                 for h in range(MLA_HEADS)], axis=0)
            o_ref[qs * ATTN_Q_SUB:(qs + 1) * ATTN_Q_SUB, :] = out_t.T.astype(o_ref.dtype)


def _latent_attention(qt, k, vt, B, S):
    tq = min(ATTN_Q_BLOCK, S)
    tk = min(ATTN_KV_BLOCK, S)
    nsub = tq // ATTN_Q_SUB
    qt = qt.reshape(B, S // ATTN_Q_SUB, MLA_HEADS * LANES, ATTN_Q_SUB)
    k = k.reshape(B, S, MLA_HEADS * LANES)
    vt = vt.reshape(B, S // ATTN_CHUNK, MLA_HEADS * ATTN_V_ROWS, ATTN_CHUNK)
    out = pl.pallas_call(
        _flash_kernel,
        grid=(B, S // tq, S // tk),
        in_specs=[pl.BlockSpec((None, nsub, MLA_HEADS * LANES, ATTN_Q_SUB), lambda b, i, j: (b, i, 0, 0)),
                  pl.BlockSpec((None, tk, MLA_HEADS * LANES), lambda b, i, j: (b, j, 0)),
                  pl.BlockSpec((None, tk // ATTN_CHUNK, MLA_HEADS * ATTN_V_ROWS, ATTN_CHUNK),
                               lambda b, i, j: (b, j, 0, 0))],
        out_specs=pl.BlockSpec((None, tq, MLA_HEADS * MLA_V), lambda b, i, j: (b, i, 0)),
        out_shape=jax.ShapeDtypeStruct((B, S, MLA_HEADS * MLA_V), BF16),
        scratch_shapes=[pltpu.VMEM((nsub, MLA_HEADS, SUBLANES, ATTN_Q_SUB), F32),
                        pltpu.VMEM((nsub, MLA_HEADS, SUBLANES, ATTN_Q_SUB), F32),
                        pltpu.VMEM((nsub, MLA_HEADS * MLA_V, ATTN_Q_SUB), F32)],
        compiler_params=_cparams("parallel", "parallel", "arbitrary"),
        name="latent_attention",
    )(qt, k, vt)
    return out.reshape(B * S, MLA_HEADS * MLA_V)


def _dilated_kernel(q_ref, kp_ref, kc_ref, kn_ref, vp_ref, vc_ref, vn_ref, bias_ref, o_ref, lse_ref):
    i = pl.program_id(2)
    n = pl.num_programs(2)
    q = q_ref[...]
    k3 = jnp.concatenate([kp_ref[...], kc_ref[...], kn_ref[...]], axis=0)
    v3 = jnp.concatenate([vp_ref[...], vc_ref[...], vn_ref[...]], axis=0)
    v3_t = v3.astype(F32).T.astype(BF16)
    row = lax.broadcasted_iota(jnp.int32, (3 * DIL_BLOCK, DIL_BLOCK), 0)
    in_seq = jnp.logical_and(jnp.logical_or(i > 0, row >= DIL_BLOCK),
                             jnp.logical_or(i < n - 1, row < 2 * DIL_BLOCK))
    lane = lax.broadcasted_iota(jnp.int32, (1, LANES), 1)
    dim = lax.broadcasted_iota(jnp.int32, (LANES, 1), 0)
    out_t = jnp.zeros((LANES, DIL_BLOCK), F32)
    lse_t = jnp.zeros((LANES, DIL_BLOCK), F32)
    for h in range(DIL_HEADS):
        head_lanes = jnp.logical_and(lane >= h * DIL_HEAD_DIM, lane < (h + 1) * DIL_HEAD_DIM)
        head_rows = jnp.logical_and(dim >= h * DIL_HEAD_DIM, dim < (h + 1) * DIL_HEAD_DIM)
        qh = jnp.where(head_lanes, q, jnp.zeros_like(q))
        s = lax.dot_general(k3, qh, (((1,), (1,)), ((), ())), preferred_element_type=F32)
        s = jnp.where(in_seq, s + bias_ref[h], NEG_BIG)
        m = jnp.max(s, axis=0, keepdims=True)
        e = jnp.exp(s - m)
        den = jnp.sum(e, axis=0, keepdims=True)
        pv_t = _dot(v3_t, (e * (1.0 / den)).astype(BF16))
        out_t = jnp.where(head_rows, pv_t, out_t)
        lse_t = jnp.where(head_rows, m + jnp.log(den), lse_t)
    o_ref[...] = out_t.T
    lse_ref[...] = lse_t.T


def _dilated_group(q, k, v, bias_t, g, d, B, S):
    rows = S // d
    nblk = rows // DIL_BLOCK
    shape = (B, d, rows, LANES)
    q, k, v = q.reshape(shape), k.reshape(shape), v.reshape(shape)
    blk = lambda f: pl.BlockSpec((None, None, DIL_BLOCK, LANES), f)
    prev = lambda b, c, i: (b, c, jnp.maximum(i - 1, 0), 0)
    cur = lambda b, c, i: (b, c, i, 0)
    nxt = lambda b, c, i: (b, c, jnp.minimum(i + 1, nblk - 1), 0)
    return pl.pallas_call(
        _dilated_kernel,
        grid=(B, d, nblk),
        in_specs=[blk(cur), blk(prev), blk(cur), blk(nxt), blk(prev), blk(cur), blk(nxt),
                  _full((DIL_HEADS, 3 * DIL_BLOCK, DIL_BLOCK))],
        out_specs=[blk(cur), blk(cur)],
        out_shape=[jax.ShapeDtypeStruct(shape, F32)] * 2,
        compiler_params=_cparams("parallel", "parallel", "parallel"),
        name="dilated_attention_g%d" % g,
    )(q, k, k, k, v, v, v, bias_t)


def _t5_bucket(rel):
    nb = REL_BUCKETS // 2
    max_exact = nb // 2
    ret = jnp.where(rel > 0, nb, 0)
    n = jnp.abs(rel)
    nf = jnp.maximum(n, 1).astype(F32)
    large = max_exact + (jnp.log(nf / max_exact) / math.log(REL_MAX_DIST / max_exact) * (nb - max_exact)).astype(jnp.int32)
    large = jnp.minimum(large, nb - 1)
    return ret + jnp.where(n < max_exact, n, large)


def _dilated_bias_tables(rel_bias):
    qi = np.arange(DIL_BLOCK)[None, :]
    kj = np.arange(3 * DIL_BLOCK)[:, None]
    rel = kj - DIL_BLOCK - qi
    in_band = np.abs(rel) <= DIL_SIDE
    idx = np.clip(rel + DIL_SIDE, 0, 2 * DIL_SIDE)
    tables = []
    for g, (_, d) in enumerate(DIL_PAIRS):
        off = d * jnp.arange(-DIL_SIDE, DIL_SIDE + 1, dtype=jnp.int32)
        b = rel_bias[_t5_bucket(off)][:, g * DIL_HEADS:(g + 1) * DIL_HEADS].astype(F32).T
        tables.append(jnp.where(in_band[None], b[:, idx], NEG_BIG))
    return tables


def _merge_kernel(x_ref, fa_ref, sg_ref, oc_ref, o0_ref, l0_ref, o1_ref, l1_ref, o2_ref, l2_ref,
                  wa_ref, wb_ref, wc_ref, wd_ref, wg_ref, bg_ref, wo_ref, g_ref, b_ref, o_ref, nat_ref):
    x = x_ref[...]
    xb = x.astype(BF16)
    tm = x.shape[0]
    for slot, (src, g) in enumerate(((o1_ref, 1), (l1_ref, 1), (o2_ref, 2), (l2_ref, 2))):
        d = DIL_PAIRS[g][1]
        for c in range(d):
            nat_ref[slot, pl.ds(c, tm // d, stride=d), :] = src[c]
    l0, l1, l2 = l0_ref[...], nat_ref[1], nat_ref[3]
    mx = jnp.maximum(jnp.maximum(l0, l1), l2)
    e0, e1, e2 = jnp.exp(l0 - mx), jnp.exp(l1 - mx), jnp.exp(l2 - mx)
    den = e0 + e1 + e2
    od = o0_ref[...] * (e0 / den) + nat_ref[0] * (e1 / den) + nat_ref[2] * (e2 / den)
    branches = ((fa_ref[...].astype(BF16), wa_ref), (sg_ref[...], wb_ref), (oc_ref[...], wc_ref),
                (od.astype(BF16), wd_ref))
    merged = None
    for i, (act, w_ref) in enumerate(branches):
        cols = slice(i * D_MODEL, (i + 1) * D_MODEL)
        gate = jax.nn.sigmoid(_dot(xb, wg_ref[:, cols]) + bg_ref[:, cols])
        term = gate * _dot(act, w_ref[...])
        merged = term if merged is None else merged + term
    y = DN_ALPHA * x + _dot(merged.astype(BF16), wo_ref[...])
    o_ref[...] = _layer_norm(y, g_ref[...], b_ref[...])


def _merge(x, S, fa, sg, oc, dil, lw):
    T = x.shape[0]
    tm = TOKEN_TILE
    nblk = S // tm
    tok = lambda w: pl.BlockSpec((tm, w), lambda i: (i, 0))
    dil_specs, dil_args = [], []
    for (o, lse), (_, d) in zip(dil, DIL_PAIRS):
        for a in (o, lse):
            if d == 1:
                dil_specs.append(tok(LANES))
                dil_args.append(a.reshape(T, LANES))
            else:
                dil_specs.append(pl.BlockSpec((None, d, tm // d, LANES), lambda i: (i // nblk, 0, i % nblk, 0)))
                dil_args.append(a)
    return pl.pallas_call(
        _merge_kernel,
        grid=(T // tm,),
        in_specs=[tok(D_MODEL), tok(F_WIDTH), tok(SG_WIDTH), tok(MLA_HEADS * MLA_V)] + dil_specs + [
                  _full((F_WIDTH, D_MODEL)), _full((SG_WIDTH, D_MODEL)), _full((MLA_HEADS * MLA_V, D_MODEL)),
                  _full((DIL_KV_WIDTH, D_MODEL)), _full((D_MODEL, 4 * D_MODEL)), _full((1, 4 * D_MODEL)),
                  _full((D_MODEL, D_MODEL)), _full((1, D_MODEL)), _full((1, D_MODEL))],
        out_specs=tok(D_MODEL),
        out_shape=jax.ShapeDtypeStruct((T, D_MODEL), F32),
        scratch_shapes=[pltpu.VMEM((4, tm, LANES), F32)],
        compiler_params=_cparams("parallel"),
        name="merge",
    )(x, fa, sg, oc, *dil_args, lw["w_a"], lw["w_b"], lw["w_c"], lw["w_d"], lw["w_gate"], lw["b_gate"],
      lw["w_o"], lw["ln1_g"], lw["ln1_b"])


def _routing_weights(logits):
    tm = logits.shape[0]
    lane = lax.broadcasted_iota(jnp.int32, (tm, LANES), 1)
    is_g = lane < MOE_GROUPS
    gl = jnp.where(is_g, logits, NEG_BIG)
    gmax = jnp.max(gl, -1, keepdims=True)
    g_top = jnp.min(jnp.where(gl == gmax, lane, LANES), -1, keepdims=True)
    p_group = 1.0 / jnp.sum(jnp.where(is_g, jnp.exp(gl - gmax), 0.0), -1, keepdims=True)
    lo = MOE_GROUPS + g_top * MOE_EXPERTS_PER_GROUP
    in_grp = jnp.logical_and(lane >= lo, lane < lo + MOE_EXPERTS_PER_GROUP)
    el = jnp.where(in_grp, logits, NEG_BIG)
    v1 = jnp.max(el, -1, keepdims=True)
    i1 = jnp.min(jnp.where(el == v1, lane, LANES), -1, keepdims=True)
    el2 = jnp.where(lane == i1, NEG_BIG, el)
    v2 = jnp.max(el2, -1, keepdims=True)
    i2 = jnp.min(jnp.where(el2 == v2, lane, LANES), -1, keepdims=True)
    e2 = jnp.exp(v2 - v1)
    p1 = p_group / (1.0 + e2)
    p2 = p_group * e2 / (1.0 + e2)
    return jnp.where(lane == i1, p1, 0.0) + jnp.where(lane == i2, p2, 0.0)


def _moe_kernel(x_ref, wr_ref, br_ref, wg_ref, wu_ref, wd_ref, g_ref, b_ref, o_ref, comb_ref, acc_ref):
    e = pl.program_id(1)

    @pl.when(e == 0)
    def _():
        logits = jnp.dot(x_ref[...], wr_ref[...], preferred_element_type=F32,
                         precision=lax.Precision.HIGHEST) + br_ref[...]
        comb_ref[...] = _routing_weights(logits)
        acc_ref[...] = jnp.zeros(acc_ref.shape, F32)

    xb = x_ref[...].astype(BF16)
    lane = lax.broadcasted_iota(jnp.int32, comb_ref.shape, 1)
    w = jnp.sum(jnp.where(lane == MOE_GROUPS + e, comb_ref[...], 0.0), -1, keepdims=True)
    h = jax.nn.silu(_dot(xb, wg_ref[...])) * _dot(xb, wu_ref[...]) * w
    acc_ref[...] += _dot(h.astype(BF16), wd_ref[...])

    @pl.when(e == pl.num_programs(1) - 1)
    def _():
        o_ref[...] = _layer_norm(DN_ALPHA * x_ref[...] + acc_ref[...], g_ref[...], b_ref[...])


def _moe(x, lw):
    T = x.shape[0]
    tm = MOE_TOKEN_TILE
    return pl.pallas_call(
        _moe_kernel,
        grid=(T // tm, MOE_EXPERTS),
        in_specs=[pl.BlockSpec((tm, D_MODEL), lambda i, e: (i, 0)),
                  _full((D_MODEL, LANES)), _full((1, LANES)),
                  pl.BlockSpec((None, D_MODEL, MOE_FF), lambda i, e: (e, 0, 0)),
                  pl.BlockSpec((None, D_MODEL, MOE_FF), lambda i, e: (e, 0, 0)),
                  pl.BlockSpec((None, MOE_FF, D_MODEL), lambda i, e: (e, 0, 0)),
                  _full((1, D_MODEL)), _full((1, D_MODEL))],
        out_specs=pl.BlockSpec((tm, D_MODEL), lambda i, e: (i, 0)),
        out_shape=jax.ShapeDtypeStruct((T, D_MODEL), F32),
        scratch_shapes=[pltpu.VMEM((tm, LANES), F32), pltpu.VMEM((tm, D_MODEL), F32)],
        compiler_params=_cparams("parallel", "arbitrary"),
        name="moe",
    )(x, lw["w_router"], lw["b_router"], lw["moe_w_gate"], lw["moe_w_up"], lw["moe_w_down"],
      lw["ln2_g"], lw["ln2_b"])


def _rope_tables(S):
    half = MLA_ROPE // 2
    inv = ROPE_BASE ** (-jnp.arange(half, dtype=F32) / half)
    ang = jnp.arange(S, dtype=F32)[:, None] * inv[None, :]
    cos, sin = jnp.cos(ang), jnp.sin(ang)
    one = jnp.ones((S, MLA_NOPE), F32)
    zero = jnp.zeros((S, MLA_NOPE), F32)
    zh = jnp.zeros((S, half), F32)
    rc = jnp.concatenate([one, cos, cos], axis=1)
    rsa = jnp.concatenate([zero, -sin, zh], axis=1)
    rsb = jnp.concatenate([zero, zh, sin], axis=1)
    return rc, rsa, rsb


def _prep_layer(l, p):
    row = lambda a: a.reshape(1, -1).astype(F32)
    w_ukv = p["mla_w_ukv"][l].reshape(MLA_KV_RANK, MLA_HEADS, MLA_NOPE + MLA_V)
    pad_rows = ((0, 2 * LANES - MLA_KV_RANK), (0, 0))
    w_k = jnp.pad(w_ukv[:, :, :MLA_NOPE], ((0, 0), (0, 0), (0, LANES - MLA_NOPE))).reshape(MLA_KV_RANK, -1)
    w_v = w_ukv[:, :, MLA_NOPE:].reshape(MLA_KV_RANK, -1)
    w_router = jnp.concatenate([p["moe_w_rg"][l], p["moe_w_re"][l]], axis=1)
    b_router = jnp.concatenate([p["moe_b_rg"][l], p["moe_b_re"][l]])
    npad = LANES - MOE_GROUPS - MOE_EXPERTS
    return {
        "w_in": p["w_in"][l].astype(BF16),
        "sg_ln_g": row(p["sg_ln_g"][l]), "sg_ln_b": row(p["sg_ln_b"][l]),
        "sg_w": p["sg_w"][l].astype(BF16),
        "sg_bias": jnp.repeat(p["sg_b"][l].T, SG_GROUP_DIM, axis=1).astype(F32),
        "q_norm": row(p["mla_q_norm"][l]),
        "kv_norm": jnp.pad(row(p["mla_kv_norm"][l]), ((0, 0), (0, 2 * LANES - MLA_KV_RANK))),
        "w_uq": p["mla_w_uq"][l].astype(BF16),
        "w_k": jnp.pad(w_k, pad_rows).astype(BF16),
        "w_v": jnp.pad(w_v, pad_rows).astype(BF16),
        "w_a": p["w_branch_a"][l].astype(BF16), "w_b": p["w_branch_b"][l].astype(BF16),
        "w_c": p["w_branch_c"][l].astype(BF16), "w_d": p["w_branch_d"][l].astype(BF16),
        "w_gate": p["w_gate"][l].astype(BF16), "b_gate": row(p["b_gate"][l]),
        "w_o": p["w_o"][l].astype(BF16),
        "ln1_g": row(p["ln1_g"][l]), "ln1_b": row(p["ln1_b"][l]),
        "w_router": jnp.pad(w_router, ((0, 0), (0, npad))).astype(F32),
        "b_router": jnp.pad(b_router, (0, npad)).reshape(1, -1).astype(F32),
        "moe_w_gate": p["moe_w_gate"][l].astype(BF16), "moe_w_up": p["moe_w_up"][l].astype(BF16),
        "moe_w_down": p["moe_w_down"][l].astype(BF16),
        "ln2_g": row(p["ln2_g"][l]), "ln2_b": row(p["ln2_b"][l]),
    }


def _trunk(x, p, layers, bias_tables):
    B, S, _ = x.shape
    rope = _rope_tables(S)
    fconsts = _fourier_consts(S)
    h = _input_layer_norm(x.reshape(B * S, D_MODEL), p["ln_in_g"], p["ln_in_b"])
    for lw in layers:
        za, sg, q, k, vt, *dil_in = _in_proj(h, B, S, lw, rope)
        fa = _fourier_mix(za, B, S, fconsts)
        oc = _latent_attention(q, k, vt, B, S)
        dil = [_dilated_group(*dil_in[3 * g:3 * g + 3], bias_tables[g], g, d, B, S)
               for g, (_, d) in enumerate(DIL_PAIRS)]
        h = _merge(h, S, fa, sg, oc, dil, lw)
        h = _moe(h, lw)
    return h.reshape(B, S, D_MODEL)


def kernel(x_prompt, x_sample, ln_in_g, ln_in_b, rel_bias, w_in, sg_ln_g, sg_ln_b, sg_w, sg_b, mla_q_norm, mla_kv_norm, mla_w_uq, mla_w_ukv, w_branch_a, w_branch_b, w_branch_c, w_branch_d, w_gate, b_gate, w_o, ln1_g, ln1_b, moe_w_rg, moe_b_rg, moe_w_re, moe_b_re, moe_w_gate, moe_w_up, moe_w_down, ln2_g, ln2_b):
    p = dict(ln_in_g=ln_in_g, ln_in_b=ln_in_b, w_in=w_in, sg_ln_g=sg_ln_g, sg_ln_b=sg_ln_b, sg_w=sg_w, sg_b=sg_b,
             mla_q_norm=mla_q_norm, mla_kv_norm=mla_kv_norm, mla_w_uq=mla_w_uq, mla_w_ukv=mla_w_ukv,
             w_branch_a=w_branch_a, w_branch_b=w_branch_b, w_branch_c=w_branch_c, w_branch_d=w_branch_d,
             w_gate=w_gate, b_gate=b_gate, w_o=w_o, ln1_g=ln1_g, ln1_b=ln1_b,
             moe_w_rg=moe_w_rg, moe_b_rg=moe_b_rg, moe_w_re=moe_w_re, moe_b_re=moe_b_re,
             moe_w_gate=moe_w_gate, moe_w_up=moe_w_up, moe_w_down=moe_w_down, ln2_g=ln2_g, ln2_b=ln2_b)
    layers = [_prep_layer(l, p) for l in range(w_in.shape[0])]
    bias_tables = _dilated_bias_tables(rel_bias)
    return _trunk(x_prompt, p, layers, bias_tables), _trunk(x_sample, p, layers, bias_tables)
```

```python
import functools
import math

import numpy as np
import jax
import jax.numpy as jnp
from jax import lax
from jax.experimental import pallas as pl
from jax.experimental.pallas import tpu as pltpu

F32 = jnp.float32
BF16 = jnp.bfloat16

D_MODEL = 1024
DEPTH = 4
F_GROUPS = 4
F_GROUP_DIM = 128
F_WIDTH = F_GROUPS * F_GROUP_DIM
SG_CHUNK = 128
SG_GROUPS = 4
SG_GROUP_DIM = 64
SG_WIDTH = SG_GROUPS * SG_GROUP_DIM
MLA_HEADS = 8
MLA_Q_RANK = 256
MLA_KV_RANK = 192
MLA_NOPE = 64
MLA_ROPE = 64
MLA_V = 64
MLA_QK_DIM = MLA_NOPE + MLA_ROPE
ROPE_BASE = 10000.0
DIL_PAIRS = ((128, 1), (512, 4), (2048, 16))
DIL_GROUPS = 3
DIL_HEADS = 4
DIL_HEAD_DIM = 32
DIL_Q_WIDTH = DIL_GROUPS * DIL_HEADS * DIL_HEAD_DIM
DIL_KV_WIDTH = DIL_HEADS * DIL_HEAD_DIM
DIL_SIDE = 64
REL_BUCKETS = 32
REL_MAX_DIST = 1024
MIX_WIDTH = F_WIDTH + 2 * SG_WIDTH + MLA_Q_RANK + MLA_KV_RANK + MLA_ROPE + DIL_Q_WIDTH + 2 * DIL_KV_WIDTH
MOE_GROUPS = 4
MOE_EXPERTS_PER_GROUP = 4
MOE_EXPERTS = MOE_GROUPS * MOE_EXPERTS_PER_GROUP
MOE_FF = 512
DN_ALPHA = (2 * DEPTH) ** 0.25
LN_EPS = 1e-5
RMS_EPS = 1e-6

OFF_A = 0
OFF_B = OFF_A + F_WIDTH
OFF_CQ = OFF_B + 2 * SG_WIDTH
OFF_CKV = OFF_CQ + MLA_Q_RANK
OFF_DQ = OFF_CKV + MLA_KV_RANK + MLA_ROPE
OFF_DK = OFF_DQ + DIL_Q_WIDTH
OFF_DV = OFF_DK + DIL_KV_WIDTH

LANES = 128
SUBLANES = 8
VMEM_LIMIT_BYTES = 56 * 1024 * 1024
TOKEN_TILE = 512
ATTN_Q_BLOCK = 512
ATTN_Q_SUB = 256
ATTN_KV_BLOCK = 2048
ATTN_CHUNK = 256
ATTN_CHUNKS_PER_ITER = 2
ATTN_PIPELINE_DEPTH = 6
ATTN_V_ROWS = 80
DIL_BLOCK = 128
NEG_BIG = -1e30
LOG2_E = math.log2(math.e)


def _cparams(*sem):
    return pltpu.CompilerParams(dimension_semantics=sem, vmem_limit_bytes=VMEM_LIMIT_BYTES)


def _full(shape):
    n = len(shape)
    return pl.BlockSpec(shape, lambda *_: (0,) * n)


def _layer_norm(x, g, b):
    mu = jnp.mean(x, -1, keepdims=True)
    xc = x - mu
    var = jnp.mean(xc * xc, -1, keepdims=True)
    return xc * lax.rsqrt(var + LN_EPS) * g + b


def _dot(a, b):
    return jnp.dot(a, b, preferred_element_type=F32)


def _ln_kernel(x_ref, g_ref, b_ref, o_ref):
    o_ref[...] = _layer_norm(x_ref[...], g_ref[...], b_ref[...])


def _input_layer_norm(x, g, b):
    T = x.shape[0]
    tm = TOKEN_TILE
    return pl.pallas_call(
        _ln_kernel,
        grid=(T // tm,),
        in_specs=[pl.BlockSpec((tm, D_MODEL), lambda i: (i, 0)), _full((1, D_MODEL)), _full((1, D_MODEL))],
        out_specs=pl.BlockSpec((tm, D_MODEL), lambda i: (i, 0)),
        out_shape=jax.ShapeDtypeStruct((T, D_MODEL), F32),
        compiler_params=_cparams("parallel"),
        name="input_layer_norm",
    )(x, g.reshape(1, -1), b.reshape(1, -1))


def _rope_lanes(t, c, sa, sb):
    return t * c + pltpu.roll(t, 96, 1) * sa + pltpu.roll(t, 32, 1) * sb


def _in_proj_kernel(x_ref, w_in_ref, sg_g_ref, sg_b_ref, sg_w_ref, sg_bias_ref, qn_ref, kvn_ref,
                    wuq_ref, wk_ref, wv_ref, rc_ref, rsa_ref, rsb_ref,
                    za_ref, sg_ref, qt_ref, k_ref, vt_ref,
                    dq0_ref, dk0_ref, dv0_ref, dq1_ref, dk1_ref, dv1_ref, dq2_ref, dk2_ref, dv2_ref, dil_ref):
    tm = x_ref.shape[0]
    z = _dot(x_ref[...].astype(BF16), w_in_ref[...])

    za_ref[...] = z[:, OFF_A:OFF_A + F_WIDTH]

    zb = jax.nn.gelu(z[:, OFF_B:OFF_B + 2 * SG_WIDTH])
    u = zb[:, :SG_WIDTH]
    vn = _layer_norm(zb[:, SG_WIDTH:], sg_g_ref[...], sg_b_ref[...]).astype(BF16)
    lane = lax.broadcasted_iota(jnp.int32, (SG_CHUNK, LANES), 1)
    low_half = lane < SG_GROUP_DIM
    for ci in range(tm // SG_CHUNK):
        rows = slice(ci * SG_CHUNK, (ci + 1) * SG_CHUNK)
        for j in range(SG_WIDTH // LANES):
            cols = slice(j * LANES, (j + 1) * LANES)
            vblk = vn[rows, cols]
            mixed = jnp.where(low_half, _dot(sg_w_ref[2 * j], vblk), _dot(sg_w_ref[2 * j + 1], vblk))
            sg_ref[rows, cols] = (u[rows, cols] * (mixed + sg_bias_ref[:, cols])).astype(sg_ref.dtype)

    rc, rsa, rsb = rc_ref[...], rsa_ref[...], rsb_ref[...]

    cq = z[:, OFF_CQ:OFF_CQ + MLA_Q_RANK]
    cq = cq * lax.rsqrt(jnp.mean(cq * cq, -1, keepdims=True) + RMS_EPS) * qn_ref[...]
    q = _dot(cq.astype(BF16), wuq_ref[...]) * (MLA_QK_DIM ** -0.5 * LOG2_E)
    ra = pltpu.roll(q, MLA_HEADS * LANES - 32, 1)
    rb = pltpu.roll(q, 32, 1)
    q_t = jnp.concatenate(
        [q[:, h * LANES:(h + 1) * LANES] * rc + ra[:, h * LANES:(h + 1) * LANES] * rsa
         + rb[:, h * LANES:(h + 1) * LANES] * rsb for h in range(MLA_HEADS)], axis=1).T
    for j in range(tm // ATTN_Q_SUB):
        qt_ref[j] = q_t[:, j * ATTN_Q_SUB:(j + 1) * ATTN_Q_SUB].astype(qt_ref.dtype)

    slab = z[:, OFF_CKV:OFF_CKV + 2 * LANES]
    lane2 = lax.broadcasted_iota(jnp.int32, (tm, 2 * LANES), 1)
    ckv_sq = jnp.where(lane2 < MLA_KV_RANK, slab * slab, 0.0)
    ms = jnp.sum(ckv_sq, -1, keepdims=True) * (1.0 / MLA_KV_RANK)
    ckv = (slab * lax.rsqrt(ms + RMS_EPS) * kvn_ref[...]).astype(BF16)
    k_nope = _dot(ckv, wk_ref[...])
    v_t = _dot(ckv, wv_ref[...]).T.astype(vt_ref.dtype)
    ones = jnp.ones((ATTN_V_ROWS - MLA_V, ATTN_CHUNK), vt_ref.dtype)
    for j in range(tm // ATTN_CHUNK):
        cols = slice(j * ATTN_CHUNK, (j + 1) * ATTN_CHUNK)
        for h in range(MLA_HEADS):
            vt_ref[j, h * ATTN_V_ROWS:h * ATTN_V_ROWS + MLA_V, :] = v_t[h * MLA_V:(h + 1) * MLA_V, cols]
            vt_ref[j, h * ATTN_V_ROWS + MLA_V:(h + 1) * ATTN_V_ROWS, :] = ones
    kr_slab = _rope_lanes(slab[:, LANES:], rc, rsa, rsb)
    kr_slab = jnp.where(lax.broadcasted_iota(jnp.int32, (tm, LANES), 1) >= MLA_NOPE, kr_slab, 0.0)
    for h in range(MLA_HEADS):
        cols = slice(h * LANES, (h + 1) * LANES)
        k_ref[:, cols] = (k_nope[:, cols] + kr_slab).astype(k_ref.dtype)

    for slab in range(DIL_GROUPS + 2):
        t = z[:, OFF_DQ + slab * LANES:OFF_DQ + (slab + 1) * LANES]
        dil_ref[slab] = t * (DIL_HEAD_DIM ** -0.5) if slab < DIL_GROUPS else t
    dq0_ref[...] = dil_ref[0].astype(dq0_ref.dtype)
    dk0_ref[...] = dil_ref[DIL_GROUPS].astype(dk0_ref.dtype)
    dv0_ref[...] = dil_ref[DIL_GROUPS + 1].astype(dv0_ref.dtype)
    for g, (qr, kr, vr) in ((1, (dq1_ref, dk1_ref, dv1_ref)), (2, (dq2_ref, dk2_ref, dv2_ref))):
        d = DIL_PAIRS[g][1]
        for c in range(d):
            rows = pl.ds(c, tm // d, stride=d)
            qr[c] = dil_ref[g, rows, :].astype(qr.dtype)
            kr[c] = dil_ref[DIL_GROUPS, rows, :].astype(kr.dtype)
            vr[c] = dil_ref[DIL_GROUPS + 1, rows, :].astype(vr.dtype)


def _in_proj(x, B, S, lw, rope):
    T = x.shape[0]
    tm = TOKEN_TILE
    nblk = S // tm
    tok = lambda w: pl.BlockSpec((tm, w), lambda i: (i, 0))
    pos = pl.BlockSpec((tm, LANES), lambda i: (i % nblk, 0))
    outs = [(F_WIDTH, F32), (SG_WIDTH, BF16)]
    out_specs = [tok(w) for w, _ in outs]
    out_shape = [jax.ShapeDtypeStruct((T, w), dt) for w, dt in outs]
    out_specs.append(pl.BlockSpec((tm // ATTN_Q_SUB, MLA_HEADS * LANES, ATTN_Q_SUB), lambda i: (i, 0, 0)))
    out_shape.append(jax.ShapeDtypeStruct((T // ATTN_Q_SUB, MLA_HEADS * LANES, ATTN_Q_SUB), BF16))
    out_specs.append(tok(MLA_HEADS * LANES))
    out_shape.append(jax.ShapeDtypeStruct((T, MLA_HEADS * LANES), BF16))
    out_specs.append(pl.BlockSpec((tm // ATTN_CHUNK, MLA_HEADS * ATTN_V_ROWS, ATTN_CHUNK), lambda i: (i, 0, 0)))
    out_shape.append(jax.ShapeDtypeStruct((T // ATTN_CHUNK, MLA_HEADS * ATTN_V_ROWS, ATTN_CHUNK), BF16))
    for _, d in DIL_PAIRS:
        for _ in range(3):
            if d == 1:
                out_specs.append(tok(LANES))
                out_shape.append(jax.ShapeDtypeStruct((T, LANES), BF16))
            else:
                out_specs.append(pl.BlockSpec((None, d, tm // d, LANES), lambda i: (i // nblk, 0, i % nblk, 0)))
                out_shape.append(jax.ShapeDtypeStruct((B, d, S // d, LANES), BF16))
    return pl.pallas_call(
        _in_proj_kernel,
        grid=(T // tm,),
        in_specs=[tok(D_MODEL), _full((D_MODEL, MIX_WIDTH)), _full((1, SG_WIDTH)), _full((1, SG_WIDTH)),
                  _full((SG_GROUPS, SG_CHUNK, SG_CHUNK)), _full((SG_CHUNK, SG_WIDTH)),
                  _full((1, MLA_Q_RANK)), _full((1, 2 * LANES)),
                  _full((MLA_Q_RANK, MLA_HEADS * LANES)), _full((2 * LANES, MLA_HEADS * LANES)),
                  _full((2 * LANES, MLA_HEADS * MLA_V)), pos, pos, pos],
        out_specs=out_specs,
        out_shape=out_shape,
        scratch_shapes=[pltpu.VMEM((DIL_GROUPS + 2, tm, LANES), F32)],
        compiler_params=_cparams("parallel"),
        name="in_proj",
    )(x, lw["w_in"], lw["sg_ln_g"], lw["sg_ln_b"], lw["sg_w"], lw["sg_bias"], lw["q_norm"], lw["kv_norm"],
      lw["w_uq"], lw["w_k"], lw["w_v"], *rope)


FOURIER_ROWS = SUBLANES


def _fourier1_kernel(x_ref, cs_ref, k1_ref, ct_ref, st_ref, gr_ref, gi_ref):
    n1, r, _ = x_ref.shape
    rows = n1 * r
    x = x_ref[...].reshape(rows, F_WIDTH)
    ct, st = ct_ref[...], st_ref[...]
    outs_r, outs_i = [], []
    for g in range(F_GROUPS):
        ab = _dot(x[:, g * LANES:(g + 1) * LANES].astype(BF16), cs_ref[...])
        stacked = jnp.concatenate([ab[:, :LANES], ab[:, LANES:]], axis=0).astype(BF16)
        g2 = _dot(k1_ref[...], stacked)
        gr, gi = g2[:rows], g2[rows:]
        outs_r.append(gr * ct - gi * st)
        outs_i.append(gr * st + gi * ct)
    gr_ref[...] = jnp.concatenate(outs_r, axis=1).reshape(n1, r, F_WIDTH)
    gi_ref[...] = jnp.concatenate(outs_i, axis=1).reshape(n1, r, F_WIDTH)


def _fourier2_kernel(gr_ref, gi_ref, k2_ref, o_ref):
    r, n2, _ = gr_ref.shape
    stacked = jnp.concatenate([gr_ref[...].reshape(r * n2, F_WIDTH), gi_ref[...].reshape(r * n2, F_WIDTH)],
                              axis=0).astype(BF16)
    o_ref[...] = _dot(k2_ref[...], stacked).reshape(n2, r, F_WIDTH)


def _fourier_mix(za, B, S, consts):
    n1 = S // LANES
    n2 = LANES
    r = FOURIER_ROWS
    cs, k1, ct, st, k2 = consts
    x = za.reshape(B, n1, n2, F_WIDTH)
    blk1 = pl.BlockSpec((None, n1, r, F_WIDTH), lambda b, j: (b, 0, j, 0))
    twid = pl.BlockSpec((None, n1 * r, LANES), lambda b, j: (j, 0, 0))
    gr, gi = pl.pallas_call(
        _fourier1_kernel,
        grid=(B, n2 // r),
        in_specs=[blk1, _full((LANES, 2 * LANES)), _full((2 * n1 * r, 2 * n1 * r)), twid, twid],
        out_specs=[blk1, blk1],
        out_shape=[jax.ShapeDtypeStruct(x.shape, F32)] * 2,
        compiler_params=_cparams("parallel", "parallel"),
        name="fourier_stage1",
    )(x, cs, k1, ct, st)
    y = pl.pallas_call(
        _fourier2_kernel,
        grid=(B, n1 // r),
        in_specs=[pl.BlockSpec((None, r, n2, F_WIDTH), lambda b, j: (b, j, 0, 0))] * 2
                 + [_full((n2 * r, 2 * n2 * r))],
        out_specs=pl.BlockSpec((None, n2, r, F_WIDTH), lambda b, j: (b, 0, j, 0)),
        out_shape=jax.ShapeDtypeStruct((B, n2, n1, F_WIDTH), F32),
        compiler_params=_cparams("parallel", "parallel"),
        name="fourier_stage2",
    )(gr, gi, k2)
    return y.reshape(B * S, F_WIDTH)


def _fourier_consts(S):
    n1 = S // LANES
    n2 = LANES
    c = np.arange(F_GROUP_DIM)
    ang_c = 2.0 * np.pi * np.outer(c, c) / F_GROUP_DIM
    norm = 1.0 / math.sqrt(S * F_GROUP_DIM)
    cs = np.concatenate([np.cos(ang_c), np.sin(ang_c)], axis=1) * norm
    r = FOURIER_ROWS
    eye = np.eye(r)
    a1 = np.arange(n1)
    ang1 = 2.0 * np.pi * np.outer(a1, a1) / n1
    c1, s1 = np.kron(np.cos(ang1), eye), np.kron(np.sin(ang1), eye)
    k1 = np.block([[c1, -s1], [s1, c1]])
    a2 = np.arange(n2)
    ang_t = 2.0 * np.pi * np.outer(a1, a2) / S

    def twiddle(t):
        t = t.reshape(n1, n2 // r, r).transpose(1, 0, 2).reshape(n2 // r, n1 * r)
        return np.broadcast_to(t[:, :, None], (n2 // r, n1 * r, LANES))

    ang2 = 2.0 * np.pi * np.outer(a2, a2) / n2
    spread = lambda w: np.einsum("ks,ab->kabs", w, eye).reshape(n2 * r, r * n2)
    k2 = np.concatenate([spread(np.cos(ang2)), spread(-np.sin(ang2))], axis=1)
    return (jnp.asarray(cs, BF16), jnp.asarray(k1, BF16), jnp.asarray(twiddle(np.cos(ang_t)), F32),
            jnp.asarray(twiddle(np.sin(ang_t)), F32), jnp.asarray(k2, BF16))


def _flash_kernel(qt_ref, k_ref, vt_ref, o_ref, m_ref, l_ref, acc_ref):
    ki = pl.program_id(2)
    nsub = qt_ref.shape[0]
    nchunk = k_ref.shape[0] // ATTN_CHUNK

    @pl.when(ki == 0)
    def _():
        m_ref[...] = jnp.full(m_ref.shape, NEG_BIG, F32)
        l_ref[...] = jnp.zeros(l_ref.shape, F32)
        acc_ref[...] = jnp.zeros(acc_ref.shape, F32)

    per_chunk = MLA_HEADS * nsub
    n_tiles = ATTN_CHUNKS_PER_ITER * per_chunk

    def tile_index(it, t):
        cc, rem = divmod(t, per_chunk)
        h, qs = divmod(rem, nsub)
        return it * ATTN_CHUNKS_PER_ITER + cc, h, qs

    def scores(it, t):
        c, h, qs = tile_index(it, t)
        r0 = pl.multiple_of(c * ATTN_CHUNK, ATTN_CHUNK)
        kc = k_ref[pl.ds(r0, ATTN_CHUNK), h * LANES:(h + 1) * LANES]
        return _dot(kc, qt_ref[qs, h * LANES:(h + 1) * LANES, :])

    def accumulate(it, t, s):
        c, h, qs = tile_index(it, t)
        rows = slice(h * MLA_V, (h + 1) * MLA_V)
        m_prev = m_ref[qs, h]
        m_new = jnp.maximum(m_prev, jnp.max(s, axis=0, keepdims=True))
        alpha = jnp.exp2(m_prev - m_new)
        p = jnp.exp2(s - m_new[0:1, :]).astype(BF16)
        m_ref[qs, h] = m_new
        pv = _dot(vt_ref[c, h * ATTN_V_ROWS:(h + 1) * ATTN_V_ROWS, :], p)
        l_ref[qs, h] = alpha * l_ref[qs, h] + pv[MLA_V:MLA_V + SUBLANES, :]
        acc_ref[qs, rows, :] = alpha[0:1, :] * acc_ref[qs, rows, :] + pv[:MLA_V, :]

    def iteration(it, carry):
        pending = {}
        for t in range(n_tiles + ATTN_PIPELINE_DEPTH):
            if t < n_tiles:
                pending[t] = scores(it, t)
            if t >= ATTN_PIPELINE_DEPTH:
                accumulate(it, t - ATTN_PIPELINE_DEPTH, pending.pop(t - ATTN_PIPELINE_DEPTH))
        return carry

    lax.fori_loop(0, nchunk // ATTN_CHUNKS_PER_ITER, iteration, 0)

    @pl.when(ki == pl.num_programs(2) - 1)
    def _():
        for qs in range(nsub):
            out_t = jnp.concatenate(
                [acc_ref[qs, h * MLA_V:(h + 1) * MLA_V, :] * (1.0 / l_ref[qs, h, 0:1, :])
                 for h in range(MLA_HEADS)], axis=0)
            o_ref[qs * ATTN_Q_SUB:(qs + 1) * ATTN_Q_SUB, :] = out_t.T.astype(o_ref.dtype)


def _latent_attention(qt, k, vt, B, S):
    tq = min(ATTN_Q_BLOCK, S)
    tk = min(ATTN_KV_BLOCK, S)
    nsub = tq // ATTN_Q_SUB
    qt = qt.reshape(B, S // ATTN_Q_SUB, MLA_HEADS * LANES, ATTN_Q_SUB)
    k = k.reshape(B, S, MLA_HEADS * LANES)
    vt = vt.reshape(B, S // ATTN_CHUNK, MLA_HEADS * ATTN_V_ROWS, ATTN_CHUNK)
    out = pl.pallas_call(
        _flash_kernel,
        grid=(B, S // tq, S // tk),
        in_specs=[pl.BlockSpec((None, nsub, MLA_HEADS * LANES, ATTN_Q_SUB), lambda b, i, j: (b, i, 0, 0)),
                  pl.BlockSpec((None, tk, MLA_HEADS * LANES), lambda b, i, j: (b, j, 0)),
                  pl.BlockSpec((None, tk // ATTN_CHUNK, MLA_HEADS * ATTN_V_ROWS, ATTN_CHUNK),
                               lambda b, i, j: (b, j, 0, 0))],
        out_specs=pl.BlockSpec((None, tq, MLA_HEADS * MLA_V), lambda b, i, j: (b, i, 0)),
        out_shape=jax.ShapeDtypeStruct((B, S, MLA_HEADS * MLA_V), BF16),
        scratch_shapes=[pltpu.VMEM((nsub, MLA_HEADS, SUBLANES, ATTN_Q_SUB), F32),
                        pltpu.VMEM((nsub, MLA_HEADS, SUBLANES, ATTN_Q_SUB), F32),
                        pltpu.VMEM((nsub, MLA_HEADS * MLA_V, ATTN_Q_SUB), F32)],
        compiler_params=_cparams("parallel", "parallel", "arbitrary"),
        name="latent_attention",
    )(qt, k, vt)
    return out.reshape(B * S, MLA_HEADS * MLA_V)


def _dilated_kernel(q_ref, kp_ref, kc_ref, kn_ref, vp_ref, vc_ref, vn_ref, bias_ref, o_ref, lse_ref):
    i = pl.program_id(2)
    n = pl.num_programs(2)
    q = q_ref[...]
    k3 = jnp.concatenate([kp_ref[...], kc_ref[...], kn_ref[...]], axis=0)
    v3 = jnp.concatenate([vp_ref[...], vc_ref[...], vn_ref[...]], axis=0)
    v3_t = v3.astype(F32).T.astype(BF16)
    row = lax.broadcasted_iota(jnp.int32, (3 * DIL_BLOCK, DIL_BLOCK), 0)
    in_seq = jnp.logical_and(jnp.logical_or(i > 0, row >= DIL_BLOCK),
                             jnp.logical_or(i < n - 1, row < 2 * DIL_BLOCK))
    lane = lax.broadcasted_iota(jnp.int32, (1, LANES), 1)
    dim = lax.broadcasted_iota(jnp.int32, (LANES, 1), 0)
    out_t = jnp.zeros((LANES, DIL_BLOCK), F32)
    lse_t = jnp.zeros((LANES, DIL_BLOCK), F32)
    for h in range(DIL_HEADS):
        head_lanes = jnp.logical_and(lane >= h * DIL_HEAD_DIM, lane < (h + 1) * DIL_HEAD_DIM)
        head_rows = jnp.logical_and(dim >= h * DIL_HEAD_DIM, dim < (h + 1) * DIL_HEAD_DIM)
        qh = jnp.where(head_lanes, q, jnp.zeros_like(q))
        s = lax.dot_general(k3, qh, (((1,), (1,)), ((), ())), preferred_element_type=F32)
        s = jnp.where(in_seq, s + bias_ref[h], NEG_BIG)
        m = jnp.max(s, axis=0, keepdims=True)
        e = jnp.exp(s - m)
        den = jnp.sum(e, axis=0, keepdims=True)
        pv_t = _dot(v3_t, (e * (1.0 / den)).astype(BF16))
        out_t = jnp.where(head_rows, pv_t, out_t)
        lse_t = jnp.where(head_rows, m + jnp.log(den), lse_t)
    o_ref[...] = out_t.T
    lse_ref[...] = lse_t.T


def _dilated_group(q, k, v, bias_t, g, d, B, S):
    rows = S // d
    nblk = rows // DIL_BLOCK
    shape = (B, d, rows, LANES)
    q, k, v = q.reshape(shape), k.reshape(shape), v.reshape(shape)
    blk = lambda f: pl.BlockSpec((None, None, DIL_BLOCK, LANES), f)
    prev = lambda b, c, i: (b, c, jnp.maximum(i - 1, 0), 0)
    cur = lambda b, c, i: (b, c, i, 0)
    nxt = lambda b, c, i: (b, c, jnp.minimum(i + 1, nblk - 1), 0)
    return pl.pallas_call(
        _dilated_kernel,
        grid=(B, d, nblk),
        in_specs=[blk(cur), blk(prev), blk(cur), blk(nxt), blk(prev), blk(cur), blk(nxt),
                  _full((DIL_HEADS, 3 * DIL_BLOCK, DIL_BLOCK))],
        out_specs=[blk(cur), blk(cur)],
        out_shape=[jax.ShapeDtypeStruct(shape, F32)] * 2,
        compiler_params=_cparams("parallel", "parallel", "parallel"),
        name="dilated_attention_g%d" % g,
    )(q, k, k, k, v, v, v, bias_t)


def _t5_bucket(rel):
    nb = REL_BUCKETS // 2
    max_exact = nb // 2
    ret = jnp.where(rel > 0, nb, 0)
    n = jnp.abs(rel)
    nf = jnp.maximum(n, 1).astype(F32)
    large = max_exact + (jnp.log(nf / max_exact) / math.log(REL_MAX_DIST / max_exact) * (nb - max_exact)).astype(jnp.int32)
    large = jnp.minimum(large, nb - 1)
    return ret + jnp.where(n < max_exact, n, large)


def _dilated_bias_tables(rel_bias):
    qi = np.arange(DIL_BLOCK)[None, :]
    kj = np.arange(3 * DIL_BLOCK)[:, None]
    rel = kj - DIL_BLOCK - qi
    in_band = np.abs(rel) <= DIL_SIDE
    idx = np.clip(rel + DIL_SIDE, 0, 2 * DIL_SIDE)
    tables = []
    for g, (_, d) in enumerate(DIL_PAIRS):
        off = d * jnp.arange(-DIL_SIDE, DIL_SIDE + 1, dtype=jnp.int32)
        b = rel_bias[_t5_bucket(off)][:, g * DIL_HEADS:(g + 1) * DIL_HEADS].astype(F32).T
        tables.append(jnp.where(in_band[None], b[:, idx], NEG_BIG))
    return tables


def _merge_kernel(x_ref, fa_ref, sg_ref, oc_ref, o0_ref, l0_ref, o1_ref, l1_ref, o2_ref, l2_ref,
                  wa_ref, wb_ref, wc_ref, wd_ref, wg_ref, bg_ref, wo_ref, g_ref, b_ref, o_ref, nat_ref):
    x = x_ref[...]
    xb = x.astype(BF16)
    tm = x.shape[0]
    for slot, (src, g) in enumerate(((o1_ref, 1), (l1_ref, 1), (o2_ref, 2), (l2_ref, 2))):
        d = DIL_PAIRS[g][1]
        for c in range(d):
            nat_ref[slot, pl.ds(c, tm // d, stride=d), :] = src[c]
    l0, l1, l2 = l0_ref[...], nat_ref[1], nat_ref[3]
    mx = jnp.maximum(jnp.maximum(l0, l1), l2)
    e0, e1, e2 = jnp.exp(l0 - mx), jnp.exp(l1 - mx), jnp.exp(l2 - mx)
    den = e0 + e1 + e2
    od = o0_ref[...] * (e0 / den) + nat_ref[0] * (e1 / den) + nat_ref[2] * (e2 / den)
    branches = ((fa_ref[...].astype(BF16), wa_ref), (sg_ref[...], wb_ref), (oc_ref[...], wc_ref),
                (od.astype(BF16), wd_ref))
    merged = None
    for i, (act, w_ref) in enumerate(branches):
        cols = slice(i * D_MODEL, (i + 1) * D_MODEL)
        gate = jax.nn.sigmoid(_dot(xb, wg_ref[:, cols]) + bg_ref[:, cols])
        term = gate * _dot(act, w_ref[...])
        merged = term if merged is None else merged + term
    y = DN_ALPHA * x + _dot(merged.astype(BF16), wo_ref[...])
    o_ref[...] = _layer_norm(y, g_ref[...], b_ref[...])


def _merge(x, S, fa, sg, oc, dil, lw):
    T = x.shape[0]
    tm = TOKEN_TILE
    nblk = S // tm
    tok = lambda w: pl.BlockSpec((tm, w), lambda i: (i, 0))
    dil_specs, dil_args = [], []
    for (o, lse), (_, d) in zip(dil, DIL_PAIRS):
        for a in (o, lse):
            if d == 1:
                dil_specs.append(tok(LANES))
                dil_args.append(a.reshape(T, LANES))
            else:
                dil_specs.append(pl.BlockSpec((None, d, tm // d, LANES), lambda i: (i // nblk, 0, i % nblk, 0)))
                dil_args.append(a)
    return pl.pallas_call(
        _merge_kernel,
        grid=(T // tm,),
        in_specs=[tok(D_MODEL), tok(F_WIDTH), tok(SG_WIDTH), tok(MLA_HEADS * MLA_V)] + dil_specs + [
                  _full((F_WIDTH, D_MODEL)), _full((SG_WIDTH, D_MODEL)), _full((MLA_HEADS * MLA_V, D_MODEL)),
                  _full((DIL_KV_WIDTH, D_MODEL)), _full((D_MODEL, 4 * D_MODEL)), _full((1, 4 * D_MODEL)),
                  _full((D_MODEL, D_MODEL)), _full((1, D_MODEL)), _full((1, D_MODEL))],
        out_specs=tok(D_MODEL),
        out_shape=jax.ShapeDtypeStruct((T, D_MODEL), F32),
        scratch_shapes=[pltpu.VMEM((4, tm, LANES), F32)],
        compiler_params=_cparams("parallel"),
        name="merge",
    )(x, fa, sg, oc, *dil_args, lw["w_a"], lw["w_b"], lw["w_c"], lw["w_d"], lw["w_gate"], lw["b_gate"],
      lw["w_o"], lw["ln1_g"], lw["ln1_b"])


MOE_PAIRS = tuple((a, b) for a in range(MOE_EXPERTS_PER_GROUP) for b in range(a + 1, MOE_EXPERTS_PER_GROUP))
MOE_CLASSES = MOE_GROUPS * len(MOE_PAIRS)
MOE_ROW_TILE = 512
INFO_CLASS, INFO_RANK, INFO_P_LO, INFO_P_HI = 0, 1, 2, 3
MOE_ROW_WIDTH = D_MODEL + LANES


def _route_kernel(x_ref, wr_ref, br_ref, tri_ref, info_ref, counts_ref, run_ref):
    @pl.when(pl.program_id(0) == 0)
    def _():
        run_ref[...] = jnp.zeros(run_ref.shape, F32)

    logits = jnp.dot(x_ref[...], wr_ref[...], preferred_element_type=F32,
                     precision=lax.Precision.HIGHEST) + br_ref[...]
    tm = logits.shape[0]
    lane = lax.broadcasted_iota(jnp.int32, (tm, LANES), 1)
    is_g = lane < MOE_GROUPS
    gl = jnp.where(is_g, logits, NEG_BIG)
    gmax = jnp.max(gl, -1, keepdims=True)
    g_top = jnp.min(jnp.where(gl == gmax, lane, LANES), -1, keepdims=True)
    p_group = 1.0 / jnp.sum(jnp.where(is_g, jnp.exp(gl - gmax), 0.0), -1, keepdims=True)
    base = MOE_GROUPS + g_top * MOE_EXPERTS_PER_GROUP
    in_grp = jnp.logical_and(lane >= base, lane < base + MOE_EXPERTS_PER_GROUP)
    el = jnp.where(in_grp, logits, NEG_BIG)
    v1 = jnp.max(el, -1, keepdims=True)
    i1 = jnp.min(jnp.where(el == v1, lane, LANES), -1, keepdims=True)
    el2 = jnp.where(lane == i1, NEG_BIG, el)
    v2 = jnp.max(el2, -1, keepdims=True)
    i2 = jnp.min(jnp.where(el2 == v2, lane, LANES), -1, keepdims=True)
    e2 = jnp.exp(v2 - v1)
    p1 = p_group / (1.0 + e2)
    p2 = p_group * e2 / (1.0 + e2)
    a = jnp.minimum(i1, i2) - base
    b = jnp.maximum(i1, i2) - base
    pair = jnp.where(a == 0, 0, jnp.where(a == 1, 3, 5)) + (b - a - 1)
    cls = g_top * len(MOE_PAIRS) + pair
    first_is_lo = i1 < i2
    p_lo = jnp.where(first_is_lo, p1, p2)
    p_hi = jnp.where(first_is_lo, p2, p1)
    onehot = lane == cls
    before = _dot(tri_ref[...], onehot.astype(BF16)) + run_ref[...]
    rank = jnp.sum(jnp.where(onehot, before, 0.0), -1, keepdims=True)
    run_ref[...] += jnp.sum(onehot.astype(F32), axis=0, keepdims=True)
    counts_ref[...] = run_ref[...]
    info_ref[...] = jnp.where(lane == INFO_CLASS, cls.astype(F32),
                              jnp.where(lane == INFO_RANK, rank,
                                        jnp.where(lane == INFO_P_LO, p_lo,
                                                  jnp.where(lane == INFO_P_HI, p_hi, 0.0))))


def _row_copies(n, make_copy):
    def start(r, carry):
        make_copy(r).start()
        return carry

    def wait(r, carry):
        make_copy(r).wait()
        return carry

    lax.fori_loop(0, n, start, 0)
    lax.fori_loop(0, n, wait, 0)


def _dispatch_kernel(dest_ref, x_ref, info_ref, init_hbm, rows_hbm, row_ref, sem):
    del init_hbm
    tm = x_ref.shape[0]
    row_ref[:, :D_MODEL] = x_ref[...]
    row_ref[:, D_MODEL:] = info_ref[...]
    _row_copies(tm, lambda r: pltpu.make_async_copy(
        row_ref.at[pl.ds(r, 1)], rows_hbm.at[pl.ds(dest_ref[0, r], 1)], sem))


def _expert_kernel(ea_ref, eb_ref, nused_ref, rows_ref, wga_ref, wua_ref, wda_ref, wgb_ref, wub_ref, wdb_ref,
                   y_ref):
    del ea_ref, eb_ref
    i = pl.program_id(0)

    @pl.when(i < nused_ref[0])
    def _():
        x = rows_ref[:, :D_MODEL].astype(BF16)
        p_lo = rows_ref[:, D_MODEL + INFO_P_LO:D_MODEL + INFO_P_LO + 1]
        p_hi = rows_ref[:, D_MODEL + INFO_P_HI:D_MODEL + INFO_P_HI + 1]
        ha = jax.nn.silu(_dot(x, wga_ref[...])) * _dot(x, wua_ref[...]) * p_lo
        hb = jax.nn.silu(_dot(x, wgb_ref[...])) * _dot(x, wub_ref[...]) * p_hi
        y_ref[...] = _dot(ha.astype(BF16), wda_ref[...]) + _dot(hb.astype(BF16), wdb_ref[...])

    @pl.when(i >= nused_ref[0])
    def _():
        y_ref[...] = jnp.zeros(y_ref.shape, F32)


def _combine_kernel(dest_ref, x_ref, g_ref, b_ref, y_hbm, o_ref, y_ref, sem):
    tm = x_ref.shape[0]
    _row_copies(tm, lambda r: pltpu.make_async_copy(
        y_hbm.at[pl.ds(dest_ref[0, r], 1)], y_ref.at[pl.ds(r, 1)], sem))
    o_ref[...] = _layer_norm(DN_ALPHA * x_ref[...] + y_ref[...], g_ref[...], b_ref[...])


def _moe(x, lw):
    T = x.shape[0]
    tm = TOKEN_TILE
    rt = MOE_ROW_TILE
    n_row_tiles = T // rt + MOE_CLASSES
    n_rows = n_row_tiles * rt
    tok = lambda w: pl.BlockSpec((tm, w), lambda i: (i, 0))

    tri = jnp.asarray(np.tril(np.ones((tm, tm)), -1), BF16)
    info, counts = pl.pallas_call(
        _route_kernel,
        grid=(T // tm,),
        in_specs=[tok(D_MODEL), _full((D_MODEL, LANES)), _full((1, LANES)), _full((tm, tm))],
        out_specs=[tok(LANES), _full((1, LANES))],
        out_shape=[jax.ShapeDtypeStruct((T, LANES), F32), jax.ShapeDtypeStruct((1, LANES), F32)],
        scratch_shapes=[pltpu.VMEM((1, LANES), F32)],
        compiler_params=_cparams("arbitrary"),
        name="moe_route",
    )(x, lw["w_router"], lw["b_router"], tri)

    cls = info[:, INFO_CLASS].astype(jnp.int32)
    rank = info[:, INFO_RANK].astype(jnp.int32)
    cnt = counts[0, :MOE_CLASSES].astype(jnp.int32)
    padded = (cnt + rt - 1) // rt * rt
    ends = jnp.cumsum(padded)
    dest = ((ends - padded)[cls] + rank).reshape(T // tm, 1, tm)
    tile_cls = jnp.minimum(jnp.searchsorted(ends, jnp.arange(n_row_tiles, dtype=jnp.int32) * rt, side="right"),
                           MOE_CLASSES - 1)
    group, pair = np.divmod(np.arange(MOE_CLASSES), len(MOE_PAIRS))
    lo_hi = np.asarray(MOE_PAIRS)[pair]
    ea = jnp.asarray(group * MOE_EXPERTS_PER_GROUP + lo_hi[:, 0], jnp.int32)[tile_cls]
    eb = jnp.asarray(group * MOE_EXPERTS_PER_GROUP + lo_hi[:, 1], jnp.int32)[tile_cls]
    n_used = (ends[-1:] // rt).astype(jnp.int32)

    dest_spec = pl.BlockSpec((None, 1, tm), lambda i: (i, 0, 0), memory_space=pltpu.SMEM)
    hbm = pl.BlockSpec(memory_space=pl.ANY)
    rows = pl.pallas_call(
        _dispatch_kernel,
        grid=(T // tm,),
        in_specs=[dest_spec, tok(D_MODEL), tok(LANES), hbm],
        out_specs=hbm,
        out_shape=jax.ShapeDtypeStruct((n_rows, MOE_ROW_WIDTH), F32),
        scratch_shapes=[pltpu.VMEM((tm, MOE_ROW_WIDTH), F32), pltpu.SemaphoreType.DMA],
        input_output_aliases={3: 0},
        compiler_params=_cparams("arbitrary"),
        name="moe_dispatch",
    )(dest, x, info, jnp.zeros((n_rows, MOE_ROW_WIDTH), F32))

    w_up = lambda sel: pl.BlockSpec((None, D_MODEL, MOE_FF), lambda i, ea, eb, nu: (sel(ea, eb)[i], 0, 0))
    w_dn = lambda sel: pl.BlockSpec((None, MOE_FF, D_MODEL), lambda i, ea, eb, nu: (sel(ea, eb)[i], 0, 0))
    first = lambda ea, eb: ea
    second = lambda ea, eb: eb
    y = pl.pallas_call(
        _expert_kernel,
        grid_spec=pltpu.PrefetchScalarGridSpec(
            num_scalar_prefetch=3,
            grid=(n_row_tiles,),
            in_specs=[pl.BlockSpec((rt, MOE_ROW_WIDTH), lambda i, ea, eb, nu: (i, 0)),
                      w_up(first), w_up(first), w_dn(first), w_up(second), w_up(second), w_dn(second)],
            out_specs=pl.BlockSpec((rt, D_MODEL), lambda i, ea, eb, nu: (i, 0)),
        ),
        out_shape=jax.ShapeDtypeStruct((n_rows, D_MODEL), F32),
        compiler_params=_cparams("arbitrary"),
        name="moe_experts",
    )(ea, eb, n_used, rows, lw["moe_w_gate"], lw["moe_w_up"], lw["moe_w_down"],
      lw["moe_w_gate"], lw["moe_w_up"], lw["moe_w_down"])

    return pl.pallas_call(
        _combine_kernel,
        grid=(T // tm,),
        in_specs=[dest_spec, tok(D_MODEL), _full((1, D_MODEL)), _full((1, D_MODEL)), hbm],
        out_specs=tok(D_MODEL),
        out_shape=jax.ShapeDtypeStruct((T, D_MODEL), F32),
        scratch_shapes=[pltpu.VMEM((tm, D_MODEL), F32), pltpu.SemaphoreType.DMA],
        compiler_params=_cparams("arbitrary"),
        name="moe_combine",
    )(dest, x, lw["ln2_g"], lw["ln2_b"], y)


def _rope_tables(S):
    half = MLA_ROPE // 2
    inv = ROPE_BASE ** (-jnp.arange(half, dtype=F32) / half)
    ang = jnp.arange(S, dtype=F32)[:, None] * inv[None, :]
    cos, sin = jnp.cos(ang), jnp.sin(ang)
    one = jnp.ones((S, MLA_NOPE), F32)
    zero = jnp.zeros((S, MLA_NOPE), F32)
    zh = jnp.zeros((S, half), F32)
    rc = jnp.concatenate([one, cos, cos], axis=1)
    rsa = jnp.concatenate([zero, -sin, zh], axis=1)
    rsb = jnp.concatenate([zero, zh, sin], axis=1)
    return rc, rsa, rsb


def _prep_layer(l, p):
    row = lambda a: a.reshape(1, -1).astype(F32)
    w_ukv = p["mla_w_ukv"][l].reshape(MLA_KV_RANK, MLA_HEADS, MLA_NOPE + MLA_V)
    pad_rows = ((0, 2 * LANES - MLA_KV_RANK), (0, 0))
    w_k = jnp.pad(w_ukv[:, :, :MLA_NOPE], ((0, 0), (0, 0), (0, LANES - MLA_NOPE))).reshape(MLA_KV_RANK, -1)
    w_v = w_ukv[:, :, MLA_NOPE:].reshape(MLA_KV_RANK, -1)
    w_router = jnp.concatenate([p["moe_w_rg"][l], p["moe_w_re"][l]], axis=1)
    b_router = jnp.concatenate([p["moe_b_rg"][l], p["moe_b_re"][l]])
    npad = LANES - MOE_GROUPS - MOE_EXPERTS
    return {
        "w_in": p["w_in"][l].astype(BF16),
        "sg_ln_g": row(p["sg_ln_g"][l]), "sg_ln_b": row(p["sg_ln_b"][l]),
        "sg_w": p["sg_w"][l].astype(BF16),
        "sg_bias": jnp.repeat(p["sg_b"][l].T, SG_GROUP_DIM, axis=1).astype(F32),
        "q_norm": row(p["mla_q_norm"][l]),
        "kv_norm": jnp.pad(row(p["mla_kv_norm"][l]), ((0, 0), (0, 2 * LANES - MLA_KV_RANK))),
        "w_uq": p["mla_w_uq"][l].astype(BF16),
        "w_k": jnp.pad(w_k, pad_rows).astype(BF16),
        "w_v": jnp.pad(w_v, pad_rows).astype(BF16),
        "w_a": p["w_branch_a"][l].astype(BF16), "w_b": p["w_branch_b"][l].astype(BF16),
        "w_c": p["w_branch_c"][l].astype(BF16), "w_d": p["w_branch_d"][l].astype(BF16),
        "w_gate": p["w_gate"][l].astype(BF16), "b_gate": row(p["b_gate"][l]),
        "w_o": p["w_o"][l].astype(BF16),
        "ln1_g": row(p["ln1_g"][l]), "ln1_b": row(p["ln1_b"][l]),
        "w_router": jnp.pad(w_router, ((0, 0), (0, npad))).astype(F32),
        "b_router": jnp.pad(b_router, (0, npad)).reshape(1, -1).astype(F32),
        "moe_w_gate": p["moe_w_gate"][l].astype(BF16), "moe_w_up": p["moe_w_up"][l].astype(BF16),
        "moe_w_down": p["moe_w_down"][l].astype(BF16),
        "ln2_g": row(p["ln2_g"][l]), "ln2_b": row(p["ln2_b"][l]),
    }


def _trunk(x, p, layers, bias_tables):
    B, S, _ = x.shape
    rope = _rope_tables(S)
    fconsts = _fourier_consts(S)
    h = _input_layer_norm(x.reshape(B * S, D_MODEL), p["ln_in_g"], p["ln_in_b"])
    for lw in layers:
        za, sg, q, k, vt, *dil_in = _in_proj(h, B, S, lw, rope)
        fa = _fourier_mix(za, B, S, fconsts)
        oc = _latent_attention(q, k, vt, B, S)
        dil = [_dilated_group(*dil_in[3 * g:3 * g + 3], bias_tables[g], g, d, B, S)
               for g, (_, d) in enumerate(DIL_PAIRS)]
        h = _merge(h, S, fa, sg, oc, dil, lw)
        h = _moe(h, lw)
    return h.reshape(B, S, D_MODEL)


def kernel(x_prompt, x_sample, ln_in_g, ln_in_b, rel_bias, w_in, sg_ln_g, sg_ln_b, sg_w, sg_b, mla_q_norm, mla_kv_norm, mla_w_uq, mla_w_ukv, w_branch_a, w_branch_b, w_branch_c, w_branch_d, w_gate, b_gate, w_o, ln1_g, ln1_b, moe_w_rg, moe_b_rg, moe_w_re, moe_b_re, moe_w_gate, moe_w_up, moe_w_down, ln2_g, ln2_b):
    p = dict(ln_in_g=ln_in_g, ln_in_b=ln_in_b, w_in=w_in, sg_ln_g=sg_ln_g, sg_ln_b=sg_ln_b, sg_w=sg_w, sg_b=sg_b,
             mla_q_norm=mla_q_norm, mla_kv_norm=mla_kv_norm, mla_w_uq=mla_w_uq, mla_w_ukv=mla_w_ukv,
             w_branch_a=w_branch_a, w_branch_b=w_branch_b, w_branch_c=w_branch_c, w_branch_d=w_branch_d,
             w_gate=w_gate, b_gate=b_gate, w_o=w_o, ln1_g=ln1_g, ln1_b=ln1_b,
             moe_w_rg=moe_w_rg, moe_b_rg=moe_b_rg, moe_w_re=moe_w_re, moe_b_re=moe_b_re,
             moe_w_gate=moe_w_gate, moe_w_up=moe_w_up, moe_w_down=moe_w_down, ln2_g=ln2_g, ln2_b=ln2_b)
    layers = [_prep_layer(l, p) for l in range(w_in.shape[0])]
    bias_tables = _dilated_bias_tables(rel_bias)
    return _trunk(x_prompt, p, layers, bias_tables), _trunk(x_sample, p, layers, bias_tables)
```

```python
import functools
import math

import numpy as np
import jax
import jax.numpy as jnp
from jax import lax
from jax.experimental import pallas as pl
from jax.experimental.pallas import tpu as pltpu

F32 = jnp.float32
BF16 = jnp.bfloat16

D_MODEL = 1024
DEPTH = 4
F_GROUPS = 4
F_GROUP_DIM = 128
F_WIDTH = F_GROUPS * F_GROUP_DIM
SG_CHUNK = 128
SG_GROUPS = 4
SG_GROUP_DIM = 64
SG_WIDTH = SG_GROUPS * SG_GROUP_DIM
MLA_HEADS = 8
MLA_Q_RANK = 256
MLA_KV_RANK = 192
MLA_NOPE = 64
MLA_ROPE = 64
MLA_V = 64
MLA_QK_DIM = MLA_NOPE + MLA_ROPE
ROPE_BASE = 10000.0
DIL_PAIRS = ((128, 1), (512, 4), (2048, 16))
DIL_GROUPS = 3
DIL_HEADS = 4
DIL_HEAD_DIM = 32
DIL_Q_WIDTH = DIL_GROUPS * DIL_HEADS * DIL_HEAD_DIM
DIL_KV_WIDTH = DIL_HEADS * DIL_HEAD_DIM
DIL_SIDE = 64
REL_BUCKETS = 32
REL_MAX_DIST = 1024
MIX_WIDTH = F_WIDTH + 2 * SG_WIDTH + MLA_Q_RANK + MLA_KV_RANK + MLA_ROPE + DIL_Q_WIDTH + 2 * DIL_KV_WIDTH
MOE_GROUPS = 4
MOE_EXPERTS_PER_GROUP = 4
MOE_EXPERTS = MOE_GROUPS * MOE_EXPERTS_PER_GROUP
MOE_FF = 512
DN_ALPHA = (2 * DEPTH) ** 0.25
LN_EPS = 1e-5
RMS_EPS = 1e-6

OFF_A = 0
OFF_B = OFF_A + F_WIDTH
OFF_CQ = OFF_B + 2 * SG_WIDTH
OFF_CKV = OFF_CQ + MLA_Q_RANK
OFF_DQ = OFF_CKV + MLA_KV_RANK + MLA_ROPE
OFF_DK = OFF_DQ + DIL_Q_WIDTH
OFF_DV = OFF_DK + DIL_KV_WIDTH

LANES = 128
SUBLANES = 8
VMEM_LIMIT_BYTES = 56 * 1024 * 1024
TOKEN_TILE = 512
ATTN_Q_BLOCK = 512
ATTN_Q_SUB = 256
ATTN_KV_BLOCK = 2048
ATTN_CHUNK = 256
ATTN_CHUNKS_PER_ITER = 2
ATTN_PIPELINE_DEPTH = 6
ATTN_V_ROWS = 80
DIL_BLOCK = 128
NEG_BIG = -1e30
LOG2_E = math.log2(math.e)


def _cparams(*sem):
    return pltpu.CompilerParams(dimension_semantics=sem, vmem_limit_bytes=VMEM_LIMIT_BYTES)


def _full(shape):
    n = len(shape)
    return pl.BlockSpec(shape, lambda *_: (0,) * n)


def _layer_norm(x, g, b):
    mu = jnp.mean(x, -1, keepdims=True)
    xc = x - mu
    var = jnp.mean(xc * xc, -1, keepdims=True)
    return xc * lax.rsqrt(var + LN_EPS) * g + b


def _dot(a, b):
    return jnp.dot(a, b, preferred_element_type=F32)


def _ln_kernel(x_ref, g_ref, b_ref, o_ref):
    o_ref[...] = _layer_norm(x_ref[...], g_ref[...], b_ref[...])


def _input_layer_norm(x, g, b):
    T = x.shape[0]
    tm = TOKEN_TILE
    return pl.pallas_call(
        _ln_kernel,
        grid=(T // tm,),
        in_specs=[pl.BlockSpec((tm, D_MODEL), lambda i: (i, 0)), _full((1, D_MODEL)), _full((1, D_MODEL))],
        out_specs=pl.BlockSpec((tm, D_MODEL), lambda i: (i, 0)),
        out_shape=jax.ShapeDtypeStruct((T, D_MODEL), F32),
        compiler_params=_cparams("parallel"),
        name="input_layer_norm",
    )(x, g.reshape(1, -1), b.reshape(1, -1))


def _rope_lanes(t, c, sa, sb):
    return t * c + pltpu.roll(t, 96, 1) * sa + pltpu.roll(t, 32, 1) * sb


def _in_proj_kernel(x_ref, w_in_ref, sg_g_ref, sg_b_ref, sg_w_ref, sg_bias_ref, qn_ref, kvn_ref,
                    wuq_ref, wk_ref, wv_ref, rc_ref, rsa_ref, rsb_ref,
                    za_ref, sg_ref, qt_ref, k_ref, vt_ref,
                    dq0_ref, dk0_ref, dv0_ref, dq1_ref, dk1_ref, dv1_ref, dq2_ref, dk2_ref, dv2_ref, dil_ref):
    tm = x_ref.shape[0]
    z = _dot(x_ref[...].astype(BF16), w_in_ref[...])

    za_ref[...] = z[:, OFF_A:OFF_A + F_WIDTH]

    zb = jax.nn.gelu(z[:, OFF_B:OFF_B + 2 * SG_WIDTH])
    u = zb[:, :SG_WIDTH]
    vn = _layer_norm(zb[:, SG_WIDTH:], sg_g_ref[...], sg_b_ref[...]).astype(BF16)
    lane = lax.broadcasted_iota(jnp.int32, (SG_CHUNK, LANES), 1)
    low_half = lane < SG_GROUP_DIM
    for ci in range(tm // SG_CHUNK):
        rows = slice(ci * SG_CHUNK, (ci + 1) * SG_CHUNK)
        for j in range(SG_WIDTH // LANES):
            cols = slice(j * LANES, (j + 1) * LANES)
            vblk = vn[rows, cols]
            mixed = jnp.where(low_half, _dot(sg_w_ref[2 * j], vblk), _dot(sg_w_ref[2 * j + 1], vblk))
            sg_ref[rows, cols] = (u[rows, cols] * (mixed + sg_bias_ref[:, cols])).astype(sg_ref.dtype)

    rc, rsa, rsb = rc_ref[...], rsa_ref[...], rsb_ref[...]

    cq = z[:, OFF_CQ:OFF_CQ + MLA_Q_RANK]
    cq = cq * lax.rsqrt(jnp.mean(cq * cq, -1, keepdims=True) + RMS_EPS) * qn_ref[...]
    q = _dot(cq.astype(BF16), wuq_ref[...]) * (MLA_QK_DIM ** -0.5 * LOG2_E)
    ra = pltpu.roll(q, MLA_HEADS * LANES - 32, 1)
    rb = pltpu.roll(q, 32, 1)
    q_t = jnp.concatenate(
        [q[:, h * LANES:(h + 1) * LANES] * rc + ra[:, h * LANES:(h + 1) * LANES] * rsa
         + rb[:, h * LANES:(h + 1) * LANES] * rsb for h in range(MLA_HEADS)], axis=1).T
    for j in range(tm // ATTN_Q_SUB):
        qt_ref[j] = q_t[:, j * ATTN_Q_SUB:(j + 1) * ATTN_Q_SUB].astype(qt_ref.dtype)

    slab = z[:, OFF_CKV:OFF_CKV + 2 * LANES]
    lane2 = lax.broadcasted_iota(jnp.int32, (tm, 2 * LANES), 1)
    ckv_sq = jnp.where(lane2 < MLA_KV_RANK, slab * slab, 0.0)
    ms = jnp.sum(ckv_sq, -1, keepdims=True) * (1.0 / MLA_KV_RANK)
    ckv = (slab * lax.rsqrt(ms + RMS_EPS) * kvn_ref[...]).astype(BF16)
    k_nope = _dot(ckv, wk_ref[...])
    v_t = _dot(ckv, wv_ref[...]).T.astype(vt_ref.dtype)
    ones = jnp.ones((ATTN_V_ROWS - MLA_V, ATTN_CHUNK), vt_ref.dtype)
    for j in range(tm // ATTN_CHUNK):
        cols = slice(j * ATTN_CHUNK, (j + 1) * ATTN_CHUNK)
        for h in range(MLA_HEADS):
            vt_ref[j, h * ATTN_V_ROWS:h * ATTN_V_ROWS + MLA_V, :] = v_t[h * MLA_V:(h + 1) * MLA_V, cols]
            vt_ref[j, h * ATTN_V_ROWS + MLA_V:(h + 1) * ATTN_V_ROWS, :] = ones
    kr_slab = _rope_lanes(slab[:, LANES:], rc, rsa, rsb)
    kr_slab = jnp.where(lax.broadcasted_iota(jnp.int32, (tm, LANES), 1) >= MLA_NOPE, kr_slab, 0.0)
    for h in range(MLA_HEADS):
        cols = slice(h * LANES, (h + 1) * LANES)
        k_ref[:, cols] = (k_nope[:, cols] + kr_slab).astype(k_ref.dtype)

    for slab in range(DIL_GROUPS + 2):
        t = z[:, OFF_DQ + slab * LANES:OFF_DQ + (slab + 1) * LANES]
        dil_ref[slab] = t * (DIL_HEAD_DIM ** -0.5) if slab < DIL_GROUPS else t
    dq0_ref[...] = dil_ref[0].astype(dq0_ref.dtype)
    dk0_ref[...] = dil_ref[DIL_GROUPS].astype(dk0_ref.dtype)
    dv0_ref[...] = dil_ref[DIL_GROUPS + 1].astype(dv0_ref.dtype)
    for g, (qr, kr, vr) in ((1, (dq1_ref, dk1_ref, dv1_ref)), (2, (dq2_ref, dk2_ref, dv2_ref))):
        d = DIL_PAIRS[g][1]
        for c in range(d):
            rows = pl.ds(c, tm // d, stride=d)
            qr[c] = dil_ref[g, rows, :].astype(qr.dtype)
            kr[c] = dil_ref[DIL_GROUPS, rows, :].astype(kr.dtype)
            vr[c] = dil_ref[DIL_GROUPS + 1, rows, :].astype(vr.dtype)


def _in_proj(x, B, S, lw, rope):
    T = x.shape[0]
    tm = TOKEN_TILE
    nblk = S // tm
    tok = lambda w: pl.BlockSpec((tm, w), lambda i: (i, 0))
    pos = pl.BlockSpec((tm, LANES), lambda i: (i % nblk, 0))
    outs = [(F_WIDTH, F32), (SG_WIDTH, BF16)]
    out_specs = [tok(w) for w, _ in outs]
    out_shape = [jax.ShapeDtypeStruct((T, w), dt) for w, dt in outs]
    out_specs.append(pl.BlockSpec((tm // ATTN_Q_SUB, MLA_HEADS * LANES, ATTN_Q_SUB), lambda i: (i, 0, 0)))
    out_shape.append(jax.ShapeDtypeStruct((T // ATTN_Q_SUB, MLA_HEADS * LANES, ATTN_Q_SUB), BF16))
    out_specs.append(tok(MLA_HEADS * LANES))
    out_shape.append(jax.ShapeDtypeStruct((T, MLA_HEADS * LANES), BF16))
    out_specs.append(pl.BlockSpec((tm // ATTN_CHUNK, MLA_HEADS * ATTN_V_ROWS, ATTN_CHUNK), lambda i: (i, 0, 0)))
    out_shape.append(jax.ShapeDtypeStruct((T // ATTN_CHUNK, MLA_HEADS * ATTN_V_ROWS, ATTN_CHUNK), BF16))
    for _, d in DIL_PAIRS:
        for _ in range(3):
            if d == 1:
                out_specs.append(tok(LANES))
                out_shape.append(jax.ShapeDtypeStruct((T, LANES), BF16))
            else:
                out_specs.append(pl.BlockSpec((None, d, tm // d, LANES), lambda i: (i // nblk, 0, i % nblk, 0)))
                out_shape.append(jax.ShapeDtypeStruct((B, d, S // d, LANES), BF16))
    return pl.pallas_call(
        _in_proj_kernel,
        grid=(T // tm,),
        in_specs=[tok(D_MODEL), _full((D_MODEL, MIX_WIDTH)), _full((1, SG_WIDTH)), _full((1, SG_WIDTH)),
                  _full((SG_GROUPS, SG_CHUNK, SG_CHUNK)), _full((SG_CHUNK, SG_WIDTH)),
                  _full((1, MLA_Q_RANK)), _full((1, 2 * LANES)),
                  _full((MLA_Q_RANK, MLA_HEADS * LANES)), _full((2 * LANES, MLA_HEADS * LANES)),
                  _full((2 * LANES, MLA_HEADS * MLA_V)), pos, pos, pos],
        out_specs=out_specs,
        out_shape=out_shape,
        scratch_shapes=[pltpu.VMEM((DIL_GROUPS + 2, tm, LANES), F32)],
        compiler_params=_cparams("parallel"),
        name="in_proj",
    )(x, lw["w_in"], lw["sg_ln_g"], lw["sg_ln_b"], lw["sg_w"], lw["sg_bias"], lw["q_norm"], lw["kv_norm"],
      lw["w_uq"], lw["w_k"], lw["w_v"], *rope)


FOURIER_ROWS = SUBLANES


def _fourier1_kernel(x_ref, cs_ref, k1_ref, ct_ref, st_ref, gr_ref, gi_ref):
    n1, r, _ = x_ref.shape
    rows = n1 * r
    x = x_ref[...].reshape(rows, F_WIDTH)
    ct, st = ct_ref[...], st_ref[...]
    outs_r, outs_i = [], []
    for g in range(F_GROUPS):
        ab = _dot(x[:, g * LANES:(g + 1) * LANES].astype(BF16), cs_ref[...])
        stacked = jnp.concatenate([ab[:, :LANES], ab[:, LANES:]], axis=0).astype(BF16)
        g2 = _dot(k1_ref[...], stacked)
        gr, gi = g2[:rows], g2[rows:]
        outs_r.append(gr * ct - gi * st)
        outs_i.append(gr * st + gi * ct)
    gr_ref[...] = jnp.concatenate(outs_r, axis=1).reshape(n1, r, F_WIDTH)
    gi_ref[...] = jnp.concatenate(outs_i, axis=1).reshape(n1, r, F_WIDTH)


def _fourier2_kernel(gr_ref, gi_ref, k2_ref, o_ref):
    r, n2, _ = gr_ref.shape
    stacked = jnp.concatenate([gr_ref[...].reshape(r * n2, F_WIDTH), gi_ref[...].reshape(r * n2, F_WIDTH)],
                              axis=0).astype(BF16)
    o_ref[...] = _dot(k2_ref[...], stacked).reshape(n2, r, F_WIDTH)


def _fourier_mix(za, B, S, consts):
    n1 = S // LANES
    n2 = LANES
    r = FOURIER_ROWS
    cs, k1, ct, st, k2 = consts
    x = za.reshape(B, n1, n2, F_WIDTH)
    blk1 = pl.BlockSpec((None, n1, r, F_WIDTH), lambda b, j: (b, 0, j, 0))
    twid = pl.BlockSpec((None, n1 * r, LANES), lambda b, j: (j, 0, 0))
    gr, gi = pl.pallas_call(
        _fourier1_kernel,
        grid=(B, n2 // r),
        in_specs=[blk1, _full((LANES, 2 * LANES)), _full((2 * n1 * r, 2 * n1 * r)), twid, twid],
        out_specs=[blk1, blk1],
        out_shape=[jax.ShapeDtypeStruct(x.shape, F32)] * 2,
        compiler_params=_cparams("parallel", "parallel"),
        name="fourier_stage1",
    )(x, cs, k1, ct, st)
    y = pl.pallas_call(
        _fourier2_kernel,
        grid=(B, n1 // r),
        in_specs=[pl.BlockSpec((None, r, n2, F_WIDTH), lambda b, j: (b, j, 0, 0))] * 2
                 + [_full((n2 * r, 2 * n2 * r))],
        out_specs=pl.BlockSpec((None, n2, r, F_WIDTH), lambda b, j: (b, 0, j, 0)),
        out_shape=jax.ShapeDtypeStruct((B, n2, n1, F_WIDTH), F32),
        compiler_params=_cparams("parallel", "parallel"),
        name="fourier_stage2",
    )(gr, gi, k2)
    return y.reshape(B * S, F_WIDTH)


def _fourier_consts(S):
    n1 = S // LANES
    n2 = LANES
    c = np.arange(F_GROUP_DIM)
    ang_c = 2.0 * np.pi * np.outer(c, c) / F_GROUP_DIM
    norm = 1.0 / math.sqrt(S * F_GROUP_DIM)
    cs = np.concatenate([np.cos(ang_c), np.sin(ang_c)], axis=1) * norm
    r = FOURIER_ROWS
    eye = np.eye(r)
    a1 = np.arange(n1)
    ang1 = 2.0 * np.pi * np.outer(a1, a1) / n1
    c1, s1 = np.kron(np.cos(ang1), eye), np.kron(np.sin(ang1), eye)
    k1 = np.block([[c1, -s1], [s1, c1]])
    a2 = np.arange(n2)
    ang_t = 2.0 * np.pi * np.outer(a1, a2) / S

    def twiddle(t):
        t = t.reshape(n1, n2 // r, r).transpose(1, 0, 2).reshape(n2 // r, n1 * r)
        return np.broadcast_to(t[:, :, None], (n2 // r, n1 * r, LANES))

    ang2 = 2.0 * np.pi * np.outer(a2, a2) / n2
    spread = lambda w: np.einsum("ks,ab->kabs", w, eye).reshape(n2 * r, r * n2)
    k2 = np.concatenate([spread(np.cos(ang2)), spread(-np.sin(ang2))], axis=1)
    return (jnp.asarray(cs, BF16), jnp.asarray(k1, BF16), jnp.asarray(twiddle(np.cos(ang_t)), F32),
            jnp.asarray(twiddle(np.sin(ang_t)), F32), jnp.asarray(k2, BF16))


def _flash_kernel(qt_ref, k_ref, vt_ref, o_ref, m_ref, l_ref, acc_ref):
    ki = pl.program_id(2)
    nsub = qt_ref.shape[0]
    nchunk = k_ref.shape[0] // ATTN_CHUNK

    @pl.when(ki == 0)
    def _():
        m_ref[...] = jnp.full(m_ref.shape, NEG_BIG, F32)
        l_ref[...] = jnp.zeros(l_ref.shape, F32)
        acc_ref[...] = jnp.zeros(acc_ref.shape, F32)

    per_chunk = MLA_HEADS * nsub
    n_tiles = ATTN_CHUNKS_PER_ITER * per_chunk

    def tile_index(it, t):
        cc, rem = divmod(t, per_chunk)
        h, qs = divmod(rem, nsub)
        return it * ATTN_CHUNKS_PER_ITER + cc, h, qs

    def scores(it, t):
        c, h, qs = tile_index(it, t)
        r0 = pl.multiple_of(c * ATTN_CHUNK, ATTN_CHUNK)
        kc = k_ref[pl.ds(r0, ATTN_CHUNK), h * LANES:(h + 1) * LANES]
        return _dot(kc, qt_ref[qs, h * LANES:(h + 1) * LANES, :])

    def accumulate(it, t, s):
        c, h, qs = tile_index(it, t)
        rows = slice(h * MLA_V, (h + 1) * MLA_V)
        m_prev = m_ref[qs, h]
        m_new = jnp.maximum(m_prev, jnp.max(s, axis=0, keepdims=True))
        alpha = jnp.exp2(m_prev - m_new)
        p = jnp.exp2(s - m_new[0:1, :]).astype(BF16)
        m_ref[qs, h] = m_new
        pv = _dot(vt_ref[c, h * ATTN_V_ROWS:(h + 1) * ATTN_V_ROWS, :], p)
        l_ref[qs, h] = alpha * l_ref[qs, h] + pv[MLA_V:MLA_V + SUBLANES, :]
        acc_ref[qs, rows, :] = alpha[0:1, :] * acc_ref[qs, rows, :] + pv[:MLA_V, :]

    def iteration(it, carry):
        pending = {}
        for t in range(n_tiles + ATTN_PIPELINE_DEPTH):
            if t < n_tiles:
                pending[t] = scores(it, t)
            if t >= ATTN_PIPELINE_DEPTH:
                accumulate(it, t - ATTN_PIPELINE_DEPTH, pending.pop(t - ATTN_PIPELINE_DEPTH))
        return carry

    lax.fori_loop(0, nchunk // ATTN_CHUNKS_PER_ITER, iteration, 0)

    @pl.when(ki == pl.num_programs(2) - 1)
    def _():
        for qs in range(nsub):
            out_t = jnp.concatenate(
                [acc_ref[qs, h * MLA_V:(h + 1) * MLA_V, :] * (1.0 / l_ref[qs, h, 0:1, :])
                 for h in range(MLA_HEADS)], axis=0)
            o_ref[qs * ATTN_Q_SUB:(qs + 1) * ATTN_Q_SUB, :] = out_t.T.astype(o_ref.dtype)


def _latent_attention(qt, k, vt, B, S):
    tq = min(ATTN_Q_BLOCK, S)
    tk = min(ATTN_KV_BLOCK, S)
    nsub = tq // ATTN_Q_SUB
    qt = qt.reshape(B, S // ATTN_Q_SUB, MLA_HEADS * LANES, ATTN_Q_SUB)
    k = k.reshape(B, S, MLA_HEADS * LANES)
    vt = vt.reshape(B, S // ATTN_CHUNK, MLA_HEADS * ATTN_V_ROWS, ATTN_CHUNK)
    out = pl.pallas_call(
        _flash_kernel,
        grid=(B, S // tq, S // tk),
        in_specs=[pl.BlockSpec((None, nsub, MLA_HEADS * LANES, ATTN_Q_SUB), lambda b, i, j: (b, i, 0, 0)),
                  pl.BlockSpec((None, tk, MLA_HEADS * LANES), lambda b, i, j: (b, j, 0)),
                  pl.BlockSpec((None, tk // ATTN_CHUNK, MLA_HEADS * ATTN_V_ROWS, ATTN_CHUNK),
                               lambda b, i, j: (b, j, 0, 0))],
        out_specs=pl.BlockSpec((None, tq, MLA_HEADS * MLA_V), lambda b, i, j: (b, i, 0)),
        out_shape=jax.ShapeDtypeStruct((B, S, MLA_HEADS * MLA_V), BF16),
        scratch_shapes=[pltpu.VMEM((nsub, MLA_HEADS, SUBLANES, ATTN_Q_SUB), F32),
                        pltpu.VMEM((nsub, MLA_HEADS, SUBLANES, ATTN_Q_SUB), F32),
                        pltpu.VMEM((nsub, MLA_HEADS * MLA_V, ATTN_Q_SUB), F32)],
        compiler_params=_cparams("parallel", "parallel", "arbitrary"),
        name="latent_attention",
    )(qt, k, vt)
    return out.reshape(B * S, MLA_HEADS * MLA_V)


def _dilated_kernel(q_ref, kp_ref, kc_ref, kn_ref, vp_ref, vc_ref, vn_ref, bias_ref, o_ref, lse_ref):
    i = pl.program_id(2)
    n = pl.num_programs(2)
    q = q_ref[...]
    k3 = jnp.concatenate([kp_ref[...], kc_ref[...], kn_ref[...]], axis=0)
    v3 = jnp.concatenate([vp_ref[...], vc_ref[...], vn_ref[...]], axis=0)
    v3_t = v3.astype(F32).T.astype(BF16)
    row = lax.broadcasted_iota(jnp.int32, (3 * DIL_BLOCK, DIL_BLOCK), 0)
    in_seq = jnp.logical_and(jnp.logical_or(i > 0, row >= DIL_BLOCK),
                             jnp.logical_or(i < n - 1, row < 2 * DIL_BLOCK))
    lane = lax.broadcasted_iota(jnp.int32, (1, LANES), 1)
    dim = lax.broadcasted_iota(jnp.int32, (LANES, 1), 0)
    out_t = jnp.zeros((LANES, DIL_BLOCK), F32)
    lse_t = jnp.zeros((LANES, DIL_BLOCK), F32)
    for h in range(DIL_HEADS):
        head_lanes = jnp.logical_and(lane >= h * DIL_HEAD_DIM, lane < (h + 1) * DIL_HEAD_DIM)
        head_rows = jnp.logical_and(dim >= h * DIL_HEAD_DIM, dim < (h + 1) * DIL_HEAD_DIM)
        qh = jnp.where(head_lanes, q, jnp.zeros_like(q))
        s = lax.dot_general(k3, qh, (((1,), (1,)), ((), ())), preferred_element_type=F32)
        s = jnp.where(in_seq, s + bias_ref[h], NEG_BIG)
        m = jnp.max(s, axis=0, keepdims=True)
        e = jnp.exp(s - m)
        den = jnp.sum(e, axis=0, keepdims=True)
        pv_t = _dot(v3_t, (e * (1.0 / den)).astype(BF16))
        out_t = jnp.where(head_rows, pv_t, out_t)
        lse_t = jnp.where(head_rows, m + jnp.log(den), lse_t)
    o_ref[...] = out_t.T
    lse_ref[...] = lse_t.T


def _dilated_group(q, k, v, bias_t, g, d, B, S):
    rows = S // d
    nblk = rows // DIL_BLOCK
    shape = (B, d, rows, LANES)
    q, k, v = q.reshape(shape), k.reshape(shape), v.reshape(shape)
    blk = lambda f: pl.BlockSpec((None, None, DIL_BLOCK, LANES), f)
    prev = lambda b, c, i: (b, c, jnp.maximum(i - 1, 0), 0)
    cur = lambda b, c, i: (b, c, i, 0)
    nxt = lambda b, c, i: (b, c, jnp.minimum(i + 1, nblk - 1), 0)
    return pl.pallas_call(
        _dilated_kernel,
        grid=(B, d, nblk),
        in_specs=[blk(cur), blk(prev), blk(cur), blk(nxt), blk(prev), blk(cur), blk(nxt),
                  _full((DIL_HEADS, 3 * DIL_BLOCK, DIL_BLOCK))],
        out_specs=[blk(cur), blk(cur)],
        out_shape=[jax.ShapeDtypeStruct(shape, F32)] * 2,
        compiler_params=_cparams("parallel", "parallel", "parallel"),
        name="dilated_attention_g%d" % g,
    )(q, k, k, k, v, v, v, bias_t)


def _t5_bucket(rel):
    nb = REL_BUCKETS // 2
    max_exact = nb // 2
    ret = jnp.where(rel > 0, nb, 0)
    n = jnp.abs(rel)
    nf = jnp.maximum(n, 1).astype(F32)
    large = max_exact + (jnp.log(nf / max_exact) / math.log(REL_MAX_DIST / max_exact) * (nb - max_exact)).astype(jnp.int32)
    large = jnp.minimum(large, nb - 1)
    return ret + jnp.where(n < max_exact, n, large)


def _dilated_bias_tables(rel_bias):
    qi = np.arange(DIL_BLOCK)[None, :]
    kj = np.arange(3 * DIL_BLOCK)[:, None]
    rel = kj - DIL_BLOCK - qi
    in_band = np.abs(rel) <= DIL_SIDE
    idx = np.clip(rel + DIL_SIDE, 0, 2 * DIL_SIDE)
    tables = []
    for g, (_, d) in enumerate(DIL_PAIRS):
        off = d * jnp.arange(-DIL_SIDE, DIL_SIDE + 1, dtype=jnp.int32)
        b = rel_bias[_t5_bucket(off)][:, g * DIL_HEADS:(g + 1) * DIL_HEADS].astype(F32).T
        tables.append(jnp.where(in_band[None], b[:, idx], NEG_BIG))
    return tables


def _merge_kernel(x_ref, fa_ref, sg_ref, oc_ref, o0_ref, l0_ref, o1_ref, l1_ref, o2_ref, l2_ref,
                  wa_ref, wb_ref, wc_ref, wd_ref, wg_ref, bg_ref, wo_ref, g_ref, b_ref, o_ref, nat_ref):
    x = x_ref[...]
    xb = x.astype(BF16)
    tm = x.shape[0]
    for slot, (src, g) in enumerate(((o1_ref, 1), (l1_ref, 1), (o2_ref, 2), (l2_ref, 2))):
        d = DIL_PAIRS[g][1]
        for c in range(d):
            nat_ref[slot, pl.ds(c, tm // d, stride=d), :] = src[c]
    l0, l1, l2 = l0_ref[...], nat_ref[1], nat_ref[3]
    mx = jnp.maximum(jnp.maximum(l0, l1), l2)
    e0, e1, e2 = jnp.exp(l0 - mx), jnp.exp(l1 - mx), jnp.exp(l2 - mx)
    den = e0 + e1 + e2
    od = o0_ref[...] * (e0 / den) + nat_ref[0] * (e1 / den) + nat_ref[2] * (e2 / den)
    branches = ((fa_ref[...].astype(BF16), wa_ref), (sg_ref[...], wb_ref), (oc_ref[...], wc_ref),
                (od.astype(BF16), wd_ref))
    merged = None
    for i, (act, w_ref) in enumerate(branches):
        cols = slice(i * D_MODEL, (i + 1) * D_MODEL)
        gate = jax.nn.sigmoid(_dot(xb, wg_ref[:, cols]) + bg_ref[:, cols])
        term = gate * _dot(act, w_ref[...])
        merged = term if merged is None else merged + term
    y = DN_ALPHA * x + _dot(merged.astype(BF16), wo_ref[...])
    o_ref[...] = _layer_norm(y, g_ref[...], b_ref[...])


def _merge(x, S, fa, sg, oc, dil, lw):
    T = x.shape[0]
    tm = TOKEN_TILE
    nblk = S // tm
    tok = lambda w: pl.BlockSpec((tm, w), lambda i: (i, 0))
    dil_specs, dil_args = [], []
    for (o, lse), (_, d) in zip(dil, DIL_PAIRS):
        for a in (o, lse):
            if d == 1:
                dil_specs.append(tok(LANES))
                dil_args.append(a.reshape(T, LANES))
            else:
                dil_specs.append(pl.BlockSpec((None, d, tm // d, LANES), lambda i: (i // nblk, 0, i % nblk, 0)))
                dil_args.append(a)
    return pl.pallas_call(
        _merge_kernel,
        grid=(T // tm,),
        in_specs=[tok(D_MODEL), tok(F_WIDTH), tok(SG_WIDTH), tok(MLA_HEADS * MLA_V)] + dil_specs + [
                  _full((F_WIDTH, D_MODEL)), _full((SG_WIDTH, D_MODEL)), _full((MLA_HEADS * MLA_V, D_MODEL)),
                  _full((DIL_KV_WIDTH, D_MODEL)), _full((D_MODEL, 4 * D_MODEL)), _full((1, 4 * D_MODEL)),
                  _full((D_MODEL, D_MODEL)), _full((1, D_MODEL)), _full((1, D_MODEL))],
        out_specs=tok(D_MODEL),
        out_shape=jax.ShapeDtypeStruct((T, D_MODEL), F32),
        scratch_shapes=[pltpu.VMEM((4, tm, LANES), F32)],
        compiler_params=_cparams("parallel"),
        name="merge",
    )(x, fa, sg, oc, *dil_args, lw["w_a"], lw["w_b"], lw["w_c"], lw["w_d"], lw["w_gate"], lw["b_gate"],
      lw["w_o"], lw["ln1_g"], lw["ln1_b"])


MOE_PAIRS = tuple((a, b) for a in range(MOE_EXPERTS_PER_GROUP) for b in range(a + 1, MOE_EXPERTS_PER_GROUP))
MOE_CLASSES = MOE_GROUPS * len(MOE_PAIRS)
MOE_ROW_TILE = 512
INFO_CLASS, INFO_RANK, INFO_P_LO, INFO_P_HI = 0, 1, 2, 3
MOE_ROW_WIDTH = D_MODEL + LANES


def _route_kernel(x_ref, wr_ref, br_ref, tri_ref, info_ref, counts_ref, run_ref):
    @pl.when(pl.program_id(0) == 0)
    def _():
        run_ref[...] = jnp.zeros(run_ref.shape, F32)

    logits = jnp.dot(x_ref[...], wr_ref[...], preferred_element_type=F32,
                     precision=lax.Precision.HIGHEST) + br_ref[...]
    tm = logits.shape[0]
    lane = lax.broadcasted_iota(jnp.int32, (tm, LANES), 1)
    is_g = lane < MOE_GROUPS
    gl = jnp.where(is_g, logits, NEG_BIG)
    gmax = jnp.max(gl, -1, keepdims=True)
    g_top = jnp.min(jnp.where(gl == gmax, lane, LANES), -1, keepdims=True)
    p_group = 1.0 / jnp.sum(jnp.where(is_g, jnp.exp(gl - gmax), 0.0), -1, keepdims=True)
    base = MOE_GROUPS + g_top * MOE_EXPERTS_PER_GROUP
    in_grp = jnp.logical_and(lane >= base, lane < base + MOE_EXPERTS_PER_GROUP)
    el = jnp.where(in_grp, logits, NEG_BIG)
    v1 = jnp.max(el, -1, keepdims=True)
    i1 = jnp.min(jnp.where(el == v1, lane, LANES), -1, keepdims=True)
    el2 = jnp.where(lane == i1, NEG_BIG, el)
    v2 = jnp.max(el2, -1, keepdims=True)
    i2 = jnp.min(jnp.where(el2 == v2, lane, LANES), -1, keepdims=True)
    e2 = jnp.exp(v2 - v1)
    p1 = p_group / (1.0 + e2)
    p2 = p_group * e2 / (1.0 + e2)
    a = jnp.minimum(i1, i2) - base
    b = jnp.maximum(i1, i2) - base
    pair = jnp.where(a == 0, 0, jnp.where(a == 1, 3, 5)) + (b - a - 1)
    cls = g_top * len(MOE_PAIRS) + pair
    first_is_lo = i1 < i2
    p_lo = jnp.where(first_is_lo, p1, p2)
    p_hi = jnp.where(first_is_lo, p2, p1)
    onehot = lane == cls
    before = _dot(tri_ref[...], onehot.astype(BF16)) + run_ref[...]
    rank = jnp.sum(jnp.where(onehot, before, 0.0), -1, keepdims=True)
    run_ref[...] += jnp.sum(onehot.astype(F32), axis=0, keepdims=True)
    counts_ref[...] = run_ref[...]
    info_ref[...] = jnp.where(lane == INFO_CLASS, cls.astype(F32),
                              jnp.where(lane == INFO_RANK, rank,
                                        jnp.where(lane == INFO_P_LO, p_lo,
                                                  jnp.where(lane == INFO_P_HI, p_hi, 0.0))))


ROW_COPY_UNROLL = 8


def _row_copies(n, make_copy, whole_copy):
    def start(r, carry):
        make_copy(r).start()
        return carry

    lax.fori_loop(0, n, start, 0, unroll=ROW_COPY_UNROLL)
    whole_copy.wait()


def _dispatch_kernel(dest_ref, x_ref, info_ref, init_hbm, rows_hbm, row_ref, sem):
    del init_hbm
    tm = x_ref.shape[0]
    row_ref[:, :D_MODEL] = x_ref[...]
    row_ref[:, D_MODEL:] = info_ref[...]
    _row_copies(tm, lambda r: pltpu.make_async_copy(
        row_ref.at[pl.ds(r, 1)], rows_hbm.at[pl.ds(dest_ref[0, r], 1)], sem),
        pltpu.make_async_copy(row_ref, rows_hbm.at[pl.ds(0, tm)], sem))


def _expert_kernel(ea_ref, eb_ref, nused_ref, rows_ref, wga_ref, wua_ref, wda_ref, wgb_ref, wub_ref, wdb_ref,
                   y_ref):
    del ea_ref, eb_ref
    i = pl.program_id(0)

    @pl.when(i < nused_ref[0])
    def _():
        x = rows_ref[:, :D_MODEL].astype(BF16)
        p_lo = rows_ref[:, D_MODEL + INFO_P_LO:D_MODEL + INFO_P_LO + 1]
        p_hi = rows_ref[:, D_MODEL + INFO_P_HI:D_MODEL + INFO_P_HI + 1]
        ha = jax.nn.silu(_dot(x, wga_ref[...])) * _dot(x, wua_ref[...]) * p_lo
        hb = jax.nn.silu(_dot(x, wgb_ref[...])) * _dot(x, wub_ref[...]) * p_hi
        y_ref[...] = _dot(ha.astype(BF16), wda_ref[...]) + _dot(hb.astype(BF16), wdb_ref[...])

    @pl.when(i >= nused_ref[0])
    def _():
        y_ref[...] = jnp.zeros(y_ref.shape, F32)


def _combine_kernel(dest_ref, x_ref, g_ref, b_ref, y_hbm, o_ref, y_ref, sem):
    tm = x_ref.shape[0]
    _row_copies(tm, lambda r: pltpu.make_async_copy(
        y_hbm.at[pl.ds(dest_ref[0, r], 1)], y_ref.at[pl.ds(r, 1)], sem),
        pltpu.make_async_copy(y_hbm.at[pl.ds(0, tm)], y_ref, sem))
    o_ref[...] = _layer_norm(DN_ALPHA * x_ref[...] + y_ref[...], g_ref[...], b_ref[...])


def _moe(x, lw):
    T = x.shape[0]
    tm = TOKEN_TILE
    rt = MOE_ROW_TILE
    n_row_tiles = T // rt + MOE_CLASSES
    n_rows = n_row_tiles * rt
    tok = lambda w: pl.BlockSpec((tm, w), lambda i: (i, 0))

    tri = jnp.asarray(np.tril(np.ones((tm, tm)), -1), BF16)
    info, counts = pl.pallas_call(
        _route_kernel,
        grid=(T // tm,),
        in_specs=[tok(D_MODEL), _full((D_MODEL, LANES)), _full((1, LANES)), _full((tm, tm))],
        out_specs=[tok(LANES), _full((1, LANES))],
        out_shape=[jax.ShapeDtypeStruct((T, LANES), F32), jax.ShapeDtypeStruct((1, LANES), F32)],
        scratch_shapes=[pltpu.VMEM((1, LANES), F32)],
        compiler_params=_cparams("arbitrary"),
        name="moe_route",
    )(x, lw["w_router"], lw["b_router"], tri)

    cls = info[:, INFO_CLASS].astype(jnp.int32)
    rank = info[:, INFO_RANK].astype(jnp.int32)
    cnt = counts[0, :MOE_CLASSES].astype(jnp.int32)
    padded = (cnt + rt - 1) // rt * rt
    ends = jnp.cumsum(padded)
    dest = ((ends - padded)[cls] + rank).reshape(T // tm, 1, tm)
    tile_cls = jnp.minimum(jnp.searchsorted(ends, jnp.arange(n_row_tiles, dtype=jnp.int32) * rt, side="right"),
                           MOE_CLASSES - 1)
    group, pair = np.divmod(np.arange(MOE_CLASSES), len(MOE_PAIRS))
    lo_hi = np.asarray(MOE_PAIRS)[pair]
    ea = jnp.asarray(group * MOE_EXPERTS_PER_GROUP + lo_hi[:, 0], jnp.int32)[tile_cls]
    eb = jnp.asarray(group * MOE_EXPERTS_PER_GROUP + lo_hi[:, 1], jnp.int32)[tile_cls]
    n_used = (ends[-1:] // rt).astype(jnp.int32)

    dest_spec = pl.BlockSpec((None, 1, tm), lambda i: (i, 0, 0), memory_space=pltpu.SMEM)
    hbm = pl.BlockSpec(memory_space=pl.ANY)
    rows = pl.pallas_call(
        _dispatch_kernel,
        grid=(T // tm,),
        in_specs=[dest_spec, tok(D_MODEL), tok(LANES), hbm],
        out_specs=hbm,
        out_shape=jax.ShapeDtypeStruct((n_rows, MOE_ROW_WIDTH), F32),
        scratch_shapes=[pltpu.VMEM((tm, MOE_ROW_WIDTH), F32), pltpu.SemaphoreType.DMA],
        input_output_aliases={3: 0},
        compiler_params=_cparams("arbitrary"),
        name="moe_dispatch",
    )(dest, x, info, jnp.zeros((n_rows, MOE_ROW_WIDTH), F32))

    w_up = lambda sel: pl.BlockSpec((None, D_MODEL, MOE_FF), lambda i, ea, eb, nu: (sel(ea, eb)[i], 0, 0))
    w_dn = lambda sel: pl.BlockSpec((None, MOE_FF, D_MODEL), lambda i, ea, eb, nu: (sel(ea, eb)[i], 0, 0))
    first = lambda ea, eb: ea
    second = lambda ea, eb: eb
    y = pl.pallas_call(
        _expert_kernel,
        grid_spec=pltpu.PrefetchScalarGridSpec(
            num_scalar_prefetch=3,
            grid=(n_row_tiles,),
            in_specs=[pl.BlockSpec((rt, MOE_ROW_WIDTH), lambda i, ea, eb, nu: (i, 0)),
                      w_up(first), w_up(first), w_dn(first), w_up(second), w_up(second), w_dn(second)],
            out_specs=pl.BlockSpec((rt, D_MODEL), lambda i, ea, eb, nu: (i, 0)),
        ),
        out_shape=jax.ShapeDtypeStruct((n_rows, D_MODEL), F32),
        compiler_params=_cparams("arbitrary"),
        name="moe_experts",
    )(ea, eb, n_used, rows, lw["moe_w_gate"], lw["moe_w_up"], lw["moe_w_down"],
      lw["moe_w_gate"], lw["moe_w_up"], lw["moe_w_down"])

    return pl.pallas_call(
        _combine_kernel,
        grid=(T // tm,),
        in_specs=[dest_spec, tok(D_MODEL), _full((1, D_MODEL)), _full((1, D_MODEL)), hbm],
        out_specs=tok(D_MODEL),
        out_shape=jax.ShapeDtypeStruct((T, D_MODEL), F32),
        scratch_shapes=[pltpu.VMEM((tm, D_MODEL), F32), pltpu.SemaphoreType.DMA],
        compiler_params=_cparams("arbitrary"),
        name="moe_combine",
    )(dest, x, lw["ln2_g"], lw["ln2_b"], y)


def _rope_tables(S):
    half = MLA_ROPE // 2
    inv = ROPE_BASE ** (-jnp.arange(half, dtype=F32) / half)
    ang = jnp.arange(S, dtype=F32)[:, None] * inv[None, :]
    cos, sin = jnp.cos(ang), jnp.sin(ang)
    one = jnp.ones((S, MLA_NOPE), F32)
    zero = jnp.zeros((S, MLA_NOPE), F32)
    zh = jnp.zeros((S, half), F32)
    rc = jnp.concatenate([one, cos, cos], axis=1)
    rsa = jnp.concatenate([zero, -sin, zh], axis=1)
    rsb = jnp.concatenate([zero, zh, sin], axis=1)
    return rc, rsa, rsb


def _prep_layer(l, p):
    row = lambda a: a.reshape(1, -1).astype(F32)
    w_ukv = p["mla_w_ukv"][l].reshape(MLA_KV_RANK, MLA_HEADS, MLA_NOPE + MLA_V)
    pad_rows = ((0, 2 * LANES - MLA_KV_RANK), (0, 0))
    w_k = jnp.pad(w_ukv[:, :, :MLA_NOPE], ((0, 0), (0, 0), (0, LANES - MLA_NOPE))).reshape(MLA_KV_RANK, -1)
    w_v = w_ukv[:, :, MLA_NOPE:].reshape(MLA_KV_RANK, -1)
    w_router = jnp.concatenate([p["moe_w_rg"][l], p["moe_w_re"][l]], axis=1)
    b_router = jnp.concatenate([p["moe_b_rg"][l], p["moe_b_re"][l]])
    npad = LANES - MOE_GROUPS - MOE_EXPERTS
    return {
        "w_in": p["w_in"][l].astype(BF16),
        "sg_ln_g": row(p["sg_ln_g"][l]), "sg_ln_b": row(p["sg_ln_b"][l]),
        "sg_w": p["sg_w"][l].astype(BF16),
        "sg_bias": jnp.repeat(p["sg_b"][l].T, SG_GROUP_DIM, axis=1).astype(F32),
        "q_norm": row(p["mla_q_norm"][l]),
        "kv_norm": jnp.pad(row(p["mla_kv_norm"][l]), ((0, 0), (0, 2 * LANES - MLA_KV_RANK))),
        "w_uq": p["mla_w_uq"][l].astype(BF16),
        "w_k": jnp.pad(w_k, pad_rows).astype(BF16),
        "w_v": jnp.pad(w_v, pad_rows).astype(BF16),
        "w_a": p["w_branch_a"][l].astype(BF16), "w_b": p["w_branch_b"][l].astype(BF16),
        "w_c": p["w_branch_c"][l].astype(BF16), "w_d": p["w_branch_d"][l].astype(BF16),
        "w_gate": p["w_gate"][l].astype(BF16), "b_gate": row(p["b_gate"][l]),
        "w_o": p["w_o"][l].astype(BF16),
        "ln1_g": row(p["ln1_g"][l]), "ln1_b": row(p["ln1_b"][l]),
        "w_router": jnp.pad(w_router, ((0, 0), (0, npad))).astype(F32),
        "b_router": jnp.pad(b_router, (0, npad)).reshape(1, -1).astype(F32),
        "moe_w_gate": p["moe_w_gate"][l].astype(BF16), "moe_w_up": p["moe_w_up"][l].astype(BF16),
        "moe_w_down": p["moe_w_down"][l].astype(BF16),
        "ln2_g": row(p["ln2_g"][l]), "ln2_b": row(p["ln2_b"][l]),
    }


def _trunk(x, p, layers, bias_tables):
    B, S, _ = x.shape
    rope = _rope_tables(S)
    fconsts = _fourier_consts(S)
    h = _input_layer_norm(x.reshape(B * S, D_MODEL), p["ln_in_g"], p["ln_in_b"])
    for lw in layers:
        za, sg, q, k, vt, *dil_in = _in_proj(h, B, S, lw, rope)
        fa = _fourier_mix(za, B, S, fconsts)
        oc = _latent_attention(q, k, vt, B, S)
        dil = [_dilated_group(*dil_in[3 * g:3 * g + 3], bias_tables[g], g, d, B, S)
               for g, (_, d) in enumerate(DIL_PAIRS)]
        h = _merge(h, S, fa, sg, oc, dil, lw)
        h = _moe(h, lw)
    return h.reshape(B, S, D_MODEL)


def kernel(x_prompt, x_sample, ln_in_g, ln_in_b, rel_bias, w_in, sg_ln_g, sg_ln_b, sg_w, sg_b, mla_q_norm, mla_kv_norm, mla_w_uq, mla_w_ukv, w_branch_a, w_branch_b, w_branch_c, w_branch_d, w_gate, b_gate, w_o, ln1_g, ln1_b, moe_w_rg, moe_b_rg, moe_w_re, moe_b_re, moe_w_gate, moe_w_up, moe_w_down, ln2_g, ln2_b):
    p = dict(ln_in_g=ln_in_g, ln_in_b=ln_in_b, w_in=w_in, sg_ln_g=sg_ln_g, sg_ln_b=sg_ln_b, sg_w=sg_w, sg_b=sg_b,
             mla_q_norm=mla_q_norm, mla_kv_norm=mla_kv_norm, mla_w_uq=mla_w_uq, mla_w_ukv=mla_w_ukv,
             w_branch_a=w_branch_a, w_branch_b=w_branch_b, w_branch_c=w_branch_c, w_branch_d=w_branch_d,
             w_gate=w_gate, b_gate=b_gate, w_o=w_o, ln1_g=ln1_g, ln1_b=ln1_b,
             moe_w_rg=moe_w_rg, moe_b_rg=moe_b_rg, moe_w_re=moe_w_re, moe_b_re=moe_b_re,
             moe_w_gate=moe_w_gate, moe_w_up=moe_w_up, moe_w_down=moe_w_down, ln2_g=ln2_g, ln2_b=ln2_b)
    layers = [_prep_layer(l, p) for l in range(w_in.shape[0])]
    bias_tables = _dilated_bias_tables(rel_bias)
    return _trunk(x_prompt, p, layers, bias_tables), _trunk(x_sample, p, layers, bias_tables)
```

```python
import functools
import math

import numpy as np
import jax
import jax.numpy as jnp
from jax import lax
from jax.experimental import pallas as pl
from jax.experimental.pallas import tpu as pltpu

F32 = jnp.float32
BF16 = jnp.bfloat16

D_MODEL = 1024
DEPTH = 4
F_GROUPS = 4
F_GROUP_DIM = 128
F_WIDTH = F_GROUPS * F_GROUP_DIM
SG_CHUNK = 128
SG_GROUPS = 4
SG_GROUP_DIM = 64
SG_WIDTH = SG_GROUPS * SG_GROUP_DIM
MLA_HEADS = 8
MLA_Q_RANK = 256
MLA_KV_RANK = 192
MLA_NOPE = 64
MLA_ROPE = 64
MLA_V = 64
MLA_QK_DIM = MLA_NOPE + MLA_ROPE
ROPE_BASE = 10000.0
DIL_PAIRS = ((128, 1), (512, 4), (2048, 16))
DIL_GROUPS = 3
DIL_HEADS = 4
DIL_HEAD_DIM = 32
DIL_Q_WIDTH = DIL_GROUPS * DIL_HEADS * DIL_HEAD_DIM
DIL_KV_WIDTH = DIL_HEADS * DIL_HEAD_DIM
DIL_SIDE = 64
REL_BUCKETS = 32
REL_MAX_DIST = 1024
MIX_WIDTH = F_WIDTH + 2 * SG_WIDTH + MLA_Q_RANK + MLA_KV_RANK + MLA_ROPE + DIL_Q_WIDTH + 2 * DIL_KV_WIDTH
MOE_GROUPS = 4
MOE_EXPERTS_PER_GROUP = 4
MOE_EXPERTS = MOE_GROUPS * MOE_EXPERTS_PER_GROUP
MOE_FF = 512
DN_ALPHA = (2 * DEPTH) ** 0.25
LN_EPS = 1e-5
RMS_EPS = 1e-6

OFF_A = 0
OFF_B = OFF_A + F_WIDTH
OFF_CQ = OFF_B + 2 * SG_WIDTH
OFF_CKV = OFF_CQ + MLA_Q_RANK
OFF_DQ = OFF_CKV + MLA_KV_RANK + MLA_ROPE
OFF_DK = OFF_DQ + DIL_Q_WIDTH
OFF_DV = OFF_DK + DIL_KV_WIDTH

LANES = 128
SUBLANES = 8
VMEM_LIMIT_BYTES = 56 * 1024 * 1024
TOKEN_TILE = 512
ATTN_Q_BLOCK = 512
ATTN_Q_SUB = 256
ATTN_KV_BLOCK = 2048
ATTN_CHUNK = 256
ATTN_CHUNKS_PER_ITER = 4
ATTN_PIPELINE_DEPTH = 6
ATTN_V_ROWS = 80
DIL_BLOCK = 128
DIL_Q_BLOCK = 512
NEG_BIG = -1e30
LOG2_E = math.log2(math.e)


def _cparams(*sem):
    return pltpu.CompilerParams(dimension_semantics=sem, vmem_limit_bytes=VMEM_LIMIT_BYTES)


def _full(shape):
    n = len(shape)
    return pl.BlockSpec(shape, lambda *_: (0,) * n)


def _layer_norm(x, g, b):
    mu = jnp.mean(x, -1, keepdims=True)
    xc = x - mu
    var = jnp.mean(xc * xc, -1, keepdims=True)
    return xc * lax.rsqrt(var + LN_EPS) * g + b


def _dot(a, b):
    return jnp.dot(a, b, preferred_element_type=F32)


def _ln_kernel(x_ref, g_ref, b_ref, o_ref):
    o_ref[...] = _layer_norm(x_ref[...], g_ref[...], b_ref[...])


def _input_layer_norm(x, g, b):
    T = x.shape[0]
    tm = TOKEN_TILE
    return pl.pallas_call(
        _ln_kernel,
        grid=(T // tm,),
        in_specs=[pl.BlockSpec((tm, D_MODEL), lambda i: (i, 0)), _full((1, D_MODEL)), _full((1, D_MODEL))],
        out_specs=pl.BlockSpec((tm, D_MODEL), lambda i: (i, 0)),
        out_shape=jax.ShapeDtypeStruct((T, D_MODEL), F32),
        compiler_params=_cparams("parallel"),
        name="input_layer_norm",
    )(x, g.reshape(1, -1), b.reshape(1, -1))


def _rope_lanes(t, c, sa, sb):
    return t * c + pltpu.roll(t, 96, 1) * sa + pltpu.roll(t, 32, 1) * sb


def _in_proj_kernel(x_ref, w_in_ref, sg_g_ref, sg_b_ref, sg_w_ref, sg_bias_ref, qn_ref, kvn_ref,
                    wuq_ref, wk_ref, wv_ref, rc_ref, rsa_ref, rsb_ref,
                    za_ref, sg_ref, qt_ref, k_ref, vt_ref,
                    dq0_ref, dk0_ref, dv0_ref, dq1_ref, dk1_ref, dv1_ref, dq2_ref, dk2_ref, dv2_ref, dil_ref):
    tm = x_ref.shape[0]
    z = _dot(x_ref[...].astype(BF16), w_in_ref[...])

    za_ref[...] = z[:, OFF_A:OFF_A + F_WIDTH]

    zb = jax.nn.gelu(z[:, OFF_B:OFF_B + 2 * SG_WIDTH])
    u = zb[:, :SG_WIDTH]
    vn = _layer_norm(zb[:, SG_WIDTH:], sg_g_ref[...], sg_b_ref[...]).astype(BF16)
    lane = lax.broadcasted_iota(jnp.int32, (SG_CHUNK, LANES), 1)
    low_half = lane < SG_GROUP_DIM
    for ci in range(tm // SG_CHUNK):
        rows = slice(ci * SG_CHUNK, (ci + 1) * SG_CHUNK)
        for j in range(SG_WIDTH // LANES):
            cols = slice(j * LANES, (j + 1) * LANES)
            vblk = vn[rows, cols]
            mixed = jnp.where(low_half, _dot(sg_w_ref[2 * j], vblk), _dot(sg_w_ref[2 * j + 1], vblk))
            sg_ref[rows, cols] = (u[rows, cols] * (mixed + sg_bias_ref[:, cols])).astype(sg_ref.dtype)

    rc, rsa, rsb = rc_ref[...], rsa_ref[...], rsb_ref[...]

    cq = z[:, OFF_CQ:OFF_CQ + MLA_Q_RANK]
    cq = cq * lax.rsqrt(jnp.mean(cq * cq, -1, keepdims=True) + RMS_EPS) * qn_ref[...]
    q = _dot(cq.astype(BF16), wuq_ref[...]) * (MLA_QK_DIM ** -0.5 * LOG2_E)
    ra = pltpu.roll(q, MLA_HEADS * LANES - 32, 1)
    rb = pltpu.roll(q, 32, 1)
    q_t = jnp.concatenate(
        [q[:, h * LANES:(h + 1) * LANES] * rc + ra[:, h * LANES:(h + 1) * LANES] * rsa
         + rb[:, h * LANES:(h + 1) * LANES] * rsb for h in range(MLA_HEADS)], axis=1).T
    for j in range(tm // ATTN_Q_SUB):
        qt_ref[j] = q_t[:, j * ATTN_Q_SUB:(j + 1) * ATTN_Q_SUB].astype(qt_ref.dtype)

    slab = z[:, OFF_CKV:OFF_CKV + 2 * LANES]
    lane2 = lax.broadcasted_iota(jnp.int32, (tm, 2 * LANES), 1)
    ckv_sq = jnp.where(lane2 < MLA_KV_RANK, slab * slab, 0.0)
    ms = jnp.sum(ckv_sq, -1, keepdims=True) * (1.0 / MLA_KV_RANK)
    ckv = (slab * lax.rsqrt(ms + RMS_EPS) * kvn_ref[...]).astype(BF16)
    k_nope = _dot(ckv, wk_ref[...])
    v_t = _dot(ckv, wv_ref[...]).T.astype(vt_ref.dtype)
    ones = jnp.ones((ATTN_V_ROWS - MLA_V, ATTN_CHUNK), vt_ref.dtype)
    for j in range(tm // ATTN_CHUNK):
        cols = slice(j * ATTN_CHUNK, (j + 1) * ATTN_CHUNK)
        for h in range(MLA_HEADS):
            vt_ref[j, h * ATTN_V_ROWS:h * ATTN_V_ROWS + MLA_V, :] = v_t[h * MLA_V:(h + 1) * MLA_V, cols]
            vt_ref[j, h * ATTN_V_ROWS + MLA_V:(h + 1) * ATTN_V_ROWS, :] = ones
    kr_slab = _rope_lanes(slab[:, LANES:], rc, rsa, rsb)
    kr_slab = jnp.where(lax.broadcasted_iota(jnp.int32, (tm, LANES), 1) >= MLA_NOPE, kr_slab, 0.0)
    for h in range(MLA_HEADS):
        cols = slice(h * LANES, (h + 1) * LANES)
        k_ref[:, cols] = (k_nope[:, cols] + kr_slab).astype(k_ref.dtype)

    for slab in range(DIL_GROUPS + 2):
        t = z[:, OFF_DQ + slab * LANES:OFF_DQ + (slab + 1) * LANES]
        dil_ref[slab] = t * (DIL_HEAD_DIM ** -0.5) if slab < DIL_GROUPS else t
    dq0_ref[...] = dil_ref[0].astype(dq0_ref.dtype)
    dk0_ref[...] = dil_ref[DIL_GROUPS].astype(dk0_ref.dtype)
    dv0_ref[...] = dil_ref[DIL_GROUPS + 1].astype(dv0_ref.dtype)
    for g, (qr, kr, vr) in ((1, (dq1_ref, dk1_ref, dv1_ref)), (2, (dq2_ref, dk2_ref, dv2_ref))):
        d = DIL_PAIRS[g][1]
        for c in range(d):
            rows = pl.ds(c, tm // d, stride=d)
            qr[c] = dil_ref[g, rows, :].astype(qr.dtype)
            kr[c] = dil_ref[DIL_GROUPS, rows, :].astype(kr.dtype)
            vr[c] = dil_ref[DIL_GROUPS + 1, rows, :].astype(vr.dtype)


def _in_proj(x, B, S, lw, rope):
    T = x.shape[0]
    tm = TOKEN_TILE
    nblk = S // tm
    tok = lambda w: pl.BlockSpec((tm, w), lambda i: (i, 0))
    pos = pl.BlockSpec((tm, LANES), lambda i: (i % nblk, 0))
    outs = [(F_WIDTH, F32), (SG_WIDTH, BF16)]
    out_specs = [tok(w) for w, _ in outs]
    out_shape = [jax.ShapeDtypeStruct((T, w), dt) for w, dt in outs]
    out_specs.append(pl.BlockSpec((tm // ATTN_Q_SUB, MLA_HEADS * LANES, ATTN_Q_SUB), lambda i: (i, 0, 0)))
    out_shape.append(jax.ShapeDtypeStruct((T // ATTN_Q_SUB, MLA_HEADS * LANES, ATTN_Q_SUB), BF16))
    out_specs.append(tok(MLA_HEADS * LANES))
    out_shape.append(jax.ShapeDtypeStruct((T, MLA_HEADS * LANES), BF16))
    out_specs.append(pl.BlockSpec((tm // ATTN_CHUNK, MLA_HEADS * ATTN_V_ROWS, ATTN_CHUNK), lambda i: (i, 0, 0)))
    out_shape.append(jax.ShapeDtypeStruct((T // ATTN_CHUNK, MLA_HEADS * ATTN_V_ROWS, ATTN_CHUNK), BF16))
    for _, d in DIL_PAIRS:
        for _ in range(3):
            if d == 1:
                out_specs.append(tok(LANES))
                out_shape.append(jax.ShapeDtypeStruct((T, LANES), BF16))
            else:
                out_specs.append(pl.BlockSpec((None, d, tm // d, LANES), lambda i: (i // nblk, 0, i % nblk, 0)))
                out_shape.append(jax.ShapeDtypeStruct((B, d, S // d, LANES), BF16))
    return pl.pallas_call(
        _in_proj_kernel,
        grid=(T // tm,),
        in_specs=[tok(D_MODEL), _full((D_MODEL, MIX_WIDTH)), _full((1, SG_WIDTH)), _full((1, SG_WIDTH)),
                  _full((SG_GROUPS, SG_CHUNK, SG_CHUNK)), _full((SG_CHUNK, SG_WIDTH)),
                  _full((1, MLA_Q_RANK)), _full((1, 2 * LANES)),
                  _full((MLA_Q_RANK, MLA_HEADS * LANES)), _full((2 * LANES, MLA_HEADS * LANES)),
                  _full((2 * LANES, MLA_HEADS * MLA_V)), pos, pos, pos],
        out_specs=out_specs,
        out_shape=out_shape,
        scratch_shapes=[pltpu.VMEM((DIL_GROUPS + 2, tm, LANES), F32)],
        compiler_params=_cparams("parallel"),
        name="in_proj",
    )(x, lw["w_in"], lw["sg_ln_g"], lw["sg_ln_b"], lw["sg_w"], lw["sg_bias"], lw["q_norm"], lw["kv_norm"],
      lw["w_uq"], lw["w_k"], lw["w_v"], *rope)


FOURIER_ROWS = SUBLANES


def _fourier1_kernel(x_ref, cs_ref, k1_ref, ct_ref, st_ref, gr_ref, gi_ref):
    n1, r, _ = x_ref.shape
    rows = n1 * r
    x = x_ref[...].reshape(rows, F_WIDTH)
    ct, st = ct_ref[...], st_ref[...]
    outs_r, outs_i = [], []
    for g in range(F_GROUPS):
        ab = _dot(x[:, g * LANES:(g + 1) * LANES].astype(BF16), cs_ref[...])
        stacked = jnp.concatenate([ab[:, :LANES], ab[:, LANES:]], axis=0).astype(BF16)
        g2 = _dot(k1_ref[...], stacked)
        gr, gi = g2[:rows], g2[rows:]
        outs_r.append(gr * ct - gi * st)
        outs_i.append(gr * st + gi * ct)
    gr_ref[...] = jnp.concatenate(outs_r, axis=1).reshape(n1, r, F_WIDTH)
    gi_ref[...] = jnp.concatenate(outs_i, axis=1).reshape(n1, r, F_WIDTH)


def _fourier2_kernel(gr_ref, gi_ref, k2_ref, o_ref):
    r, n2, _ = gr_ref.shape
    stacked = jnp.concatenate([gr_ref[...].reshape(r * n2, F_WIDTH), gi_ref[...].reshape(r * n2, F_WIDTH)],
                              axis=0).astype(BF16)
    o_ref[...] = _dot(k2_ref[...], stacked).reshape(n2, r, F_WIDTH)


def _fourier_mix(za, B, S, consts):
    n1 = S // LANES
    n2 = LANES
    r = FOURIER_ROWS
    cs, k1, ct, st, k2 = consts
    x = za.reshape(B, n1, n2, F_WIDTH)
    blk1 = pl.BlockSpec((None, n1, r, F_WIDTH), lambda b, j: (b, 0, j, 0))
    twid = pl.BlockSpec((None, n1 * r, LANES), lambda b, j: (j, 0, 0))
    gr, gi = pl.pallas_call(
        _fourier1_kernel,
        grid=(B, n2 // r),
        in_specs=[blk1, _full((LANES, 2 * LANES)), _full((2 * n1 * r, 2 * n1 * r)), twid, twid],
        out_specs=[blk1, blk1],
        out_shape=[jax.ShapeDtypeStruct(x.shape, F32)] * 2,
        compiler_params=_cparams("parallel", "parallel"),
        name="fourier_stage1",
    )(x, cs, k1, ct, st)
    y = pl.pallas_call(
        _fourier2_kernel,
        grid=(B, n1 // r),
        in_specs=[pl.BlockSpec((None, r, n2, F_WIDTH), lambda b, j: (b, j, 0, 0))] * 2
                 + [_full((n2 * r, 2 * n2 * r))],
        out_specs=pl.BlockSpec((None, n2, r, F_WIDTH), lambda b, j: (b, 0, j, 0)),
        out_shape=jax.ShapeDtypeStruct((B, n2, n1, F_WIDTH), F32),
        compiler_params=_cparams("parallel", "parallel"),
        name="fourier_stage2",
    )(gr, gi, k2)
    return y.reshape(B * S, F_WIDTH)


def _fourier_consts(S):
    n1 = S // LANES
    n2 = LANES
    c = np.arange(F_GROUP_DIM)
    ang_c = 2.0 * np.pi * np.outer(c, c) / F_GROUP_DIM
    norm = 1.0 / math.sqrt(S * F_GROUP_DIM)
    cs = np.concatenate([np.cos(ang_c), np.sin(ang_c)], axis=1) * norm
    r = FOURIER_ROWS
    eye = np.eye(r)
    a1 = np.arange(n1)
    ang1 = 2.0 * np.pi * np.outer(a1, a1) / n1
    c1, s1 = np.kron(np.cos(ang1), eye), np.kron(np.sin(ang1), eye)
    k1 = np.block([[c1, -s1], [s1, c1]])
    a2 = np.arange(n2)
    ang_t = 2.0 * np.pi * np.outer(a1, a2) / S

    def twiddle(t):
        t = t.reshape(n1, n2 // r, r).transpose(1, 0, 2).reshape(n2 // r, n1 * r)
        return np.broadcast_to(t[:, :, None], (n2 // r, n1 * r, LANES))

    ang2 = 2.0 * np.pi * np.outer(a2, a2) / n2
    spread = lambda w: np.einsum("ks,ab->kabs", w, eye).reshape(n2 * r, r * n2)
    k2 = np.concatenate([spread(np.cos(ang2)), spread(-np.sin(ang2))], axis=1)
    return (jnp.asarray(cs, BF16), jnp.asarray(k1, BF16), jnp.asarray(twiddle(np.cos(ang_t)), F32),
            jnp.asarray(twiddle(np.sin(ang_t)), F32), jnp.asarray(k2, BF16))


def _flash_kernel(qt_ref, k_ref, vt_ref, o_ref, m_ref, l_ref, acc_ref):
    ki = pl.program_id(2)
    nsub = qt_ref.shape[0]
    nchunk = k_ref.shape[0] // ATTN_CHUNK

    @pl.when(ki == 0)
    def _():
        m_ref[...] = jnp.full(m_ref.shape, NEG_BIG, F32)
        l_ref[...] = jnp.zeros(l_ref.shape, F32)
        acc_ref[...] = jnp.zeros(acc_ref.shape, F32)

    per_chunk = MLA_HEADS * nsub
    n_tiles = ATTN_CHUNKS_PER_ITER * per_chunk

    def tile_index(it, t):
        cc, rem = divmod(t, per_chunk)
        h, qs = divmod(rem, nsub)
        return it * ATTN_CHUNKS_PER_ITER + cc, h, qs

    def scores(it, t):
        c, h, qs = tile_index(it, t)
        r0 = pl.multiple_of(c * ATTN_CHUNK, ATTN_CHUNK)
        kc = k_ref[pl.ds(r0, ATTN_CHUNK), h * LANES:(h + 1) * LANES]
        return _dot(kc, qt_ref[qs, h * LANES:(h + 1) * LANES, :])

    def accumulate(it, t, s):
        c, h, qs = tile_index(it, t)
        rows = slice(h * MLA_V, (h + 1) * MLA_V)
        m_prev = m_ref[qs, h]
        m_new = jnp.maximum(m_prev, jnp.max(s, axis=0, keepdims=True))
        alpha = jnp.exp2(m_prev - m_new)
        p = jnp.exp2(s - m_new[0:1, :]).astype(BF16)
        m_ref[qs, h] = m_new
        pv = _dot(vt_ref[c, h * ATTN_V_ROWS:(h + 1) * ATTN_V_ROWS, :], p)
        l_ref[qs, h] = alpha * l_ref[qs, h] + pv[MLA_V:MLA_V + SUBLANES, :]
        acc_ref[qs, rows, :] = alpha[0:1, :] * acc_ref[qs, rows, :] + pv[:MLA_V, :]

    def iteration(it, carry):
        pending = {}
        for t in range(n_tiles + ATTN_PIPELINE_DEPTH):
            if t < n_tiles:
                pending[t] = scores(it, t)
            if t >= ATTN_PIPELINE_DEPTH:
                accumulate(it, t - ATTN_PIPELINE_DEPTH, pending.pop(t - ATTN_PIPELINE_DEPTH))
        return carry

    lax.fori_loop(0, nchunk // ATTN_CHUNKS_PER_ITER, iteration, 0)

    @pl.when(ki == pl.num_programs(2) - 1)
    def _():
        for qs in range(nsub):
            out_t = jnp.concatenate(
                [acc_ref[qs, h * MLA_V:(h + 1) * MLA_V, :] * (1.0 / l_ref[qs, h, 0:1, :])
                 for h in range(MLA_HEADS)], axis=0)
            o_ref[qs * ATTN_Q_SUB:(qs + 1) * ATTN_Q_SUB, :] = out_t.T.astype(o_ref.dtype)


def _latent_attention(qt, k, vt, B, S):
    tq = min(ATTN_Q_BLOCK, S)
    tk = min(ATTN_KV_BLOCK, S)
    nsub = tq // ATTN_Q_SUB
    qt = qt.reshape(B, S // ATTN_Q_SUB, MLA_HEADS * LANES, ATTN_Q_SUB)
    k = k.reshape(B, S, MLA_HEADS * LANES)
    vt = vt.reshape(B, S // ATTN_CHUNK, MLA_HEADS * ATTN_V_ROWS, ATTN_CHUNK)
    out = pl.pallas_call(
        _flash_kernel,
        grid=(B, S // tq, S // tk),
        in_specs=[pl.BlockSpec((None, nsub, MLA_HEADS * LANES, ATTN_Q_SUB), lambda b, i, j: (b, i, 0, 0)),
                  pl.BlockSpec((None, tk, MLA_HEADS * LANES), lambda b, i, j: (b, j, 0)),
                  pl.BlockSpec((None, tk // ATTN_CHUNK, MLA_HEADS * ATTN_V_ROWS, ATTN_CHUNK),
                               lambda b, i, j: (b, j, 0, 0))],
        out_specs=pl.BlockSpec((None, tq, MLA_HEADS * MLA_V), lambda b, i, j: (b, i, 0)),
        out_shape=jax.ShapeDtypeStruct((B, S, MLA_HEADS * MLA_V), BF16),
        scratch_shapes=[pltpu.VMEM((nsub, MLA_HEADS, SUBLANES, ATTN_Q_SUB), F32),
                        pltpu.VMEM((nsub, MLA_HEADS, SUBLANES, ATTN_Q_SUB), F32),
                        pltpu.VMEM((nsub, MLA_HEADS * MLA_V, ATTN_Q_SUB), F32)],
        compiler_params=_cparams("parallel", "parallel", "arbitrary"),
        name="latent_attention",
    )(qt, k, vt)
    return out.reshape(B * S, MLA_HEADS * MLA_V)


def _dilated_kernel(q_ref, kp_ref, kc_ref, kn_ref, vp_ref, vc_ref, vn_ref, bias_ref, o_ref, lse_ref, *, rows):
    i = pl.program_id(2)
    qb = q_ref.shape[0]
    k_all = jnp.concatenate([kp_ref[...], kc_ref[...], kn_ref[...]], axis=0)
    v_all = jnp.concatenate([vp_ref[...], vc_ref[...], vn_ref[...]], axis=0)
    v_all_t = v_all.astype(F32).T.astype(BF16)
    row = lax.broadcasted_iota(jnp.int32, (3 * DIL_BLOCK, DIL_BLOCK), 0)
    lane = lax.broadcasted_iota(jnp.int32, (1, LANES), 1)
    dim = lax.broadcasted_iota(jnp.int32, (LANES, 1), 0)
    n_sub = qb // DIL_BLOCK
    scores = {}
    for sub in range(n_sub):
        q = q_ref[sub * DIL_BLOCK:(sub + 1) * DIL_BLOCK, :]
        k3 = k_all[sub * DIL_BLOCK:(sub + 3) * DIL_BLOCK, :]
        for h in range(DIL_HEADS):
            head_lanes = jnp.logical_and(lane >= h * DIL_HEAD_DIM, lane < (h + 1) * DIL_HEAD_DIM)
            qh = jnp.where(head_lanes, q, jnp.zeros_like(q))
            scores[sub, h] = lax.dot_general(k3, qh, (((1,), (1,)), ((), ())), preferred_element_type=F32)
    for sub in range(n_sub):
        inside = 1 <= sub <= n_sub - 2
        key_row = i * qb + (sub - 1) * DIL_BLOCK + row
        in_seq = jnp.logical_and(key_row >= 0, key_row < rows)
        v3_t = v_all_t[:, sub * DIL_BLOCK:(sub + 3) * DIL_BLOCK]
        pvs, lses = [], []
        for h in range(DIL_HEADS):
            s = scores.pop((sub, h)) + bias_ref[h]
            s = s if inside else jnp.where(in_seq, s, NEG_BIG)
            m = jnp.max(s, axis=0, keepdims=True)
            e = jnp.exp(s - m)
            den = jnp.sum(e, axis=0, keepdims=True)
            pvs.append(_dot(v3_t, (e * (1.0 / den)).astype(BF16)))
            lses.append(m + jnp.log(den))
        out_t = jnp.zeros((LANES, DIL_BLOCK), F32)
        lse_t = jnp.zeros((LANES, DIL_BLOCK), F32)
        for h in range(DIL_HEADS):
            head_rows = jnp.logical_and(dim >= h * DIL_HEAD_DIM, dim < (h + 1) * DIL_HEAD_DIM)
            out_t = jnp.where(head_rows, pvs[h], out_t)
            lse_t = jnp.where(head_rows, lses[h], lse_t)
        o_ref[sub * DIL_BLOCK:(sub + 1) * DIL_BLOCK, :] = out_t.T
        lse_ref[sub * DIL_BLOCK:(sub + 1) * DIL_BLOCK, :] = lse_t.T


def _dilated_group(q, k, v, bias_t, g, d, B, S):
    rows = S // d
    qb = min(DIL_Q_BLOCK, rows)
    per = qb // DIL_BLOCK
    last = rows // DIL_BLOCK - 1
    shape = (B, d, rows, LANES)
    q, k, v = q.reshape(shape), k.reshape(shape), v.reshape(shape)
    edge = lambda f: pl.BlockSpec((None, None, DIL_BLOCK, LANES), f)
    main = pl.BlockSpec((None, None, qb, LANES), lambda b, c, i: (b, c, i, 0))
    prev = edge(lambda b, c, i: (b, c, jnp.maximum(i * per - 1, 0), 0))
    nxt = edge(lambda b, c, i: (b, c, jnp.minimum((i + 1) * per, last), 0))
    return pl.pallas_call(
        functools.partial(_dilated_kernel, rows=rows),
        grid=(B, d, rows // qb),
        in_specs=[main, prev, main, nxt, prev, main, nxt, _full((DIL_HEADS, 3 * DIL_BLOCK, DIL_BLOCK))],
        out_specs=[main, main],
        out_shape=[jax.ShapeDtypeStruct(shape, F32)] * 2,
        compiler_params=_cparams("parallel", "parallel", "parallel"),
        name="dilated_attention_g%d" % g,
    )(q, k, k, k, v, v, v, bias_t)


def _t5_bucket(rel):
    nb = REL_BUCKETS // 2
    max_exact = nb // 2
    ret = jnp.where(rel > 0, nb, 0)
    n = jnp.abs(rel)
    nf = jnp.maximum(n, 1).astype(F32)
    large = max_exact + (jnp.log(nf / max_exact) / math.log(REL_MAX_DIST / max_exact) * (nb - max_exact)).astype(jnp.int32)
    large = jnp.minimum(large, nb - 1)
    return ret + jnp.where(n < max_exact, n, large)


def _dilated_bias_tables(rel_bias):
    qi = np.arange(DIL_BLOCK)[None, :]
    kj = np.arange(3 * DIL_BLOCK)[:, None]
    rel = kj - DIL_BLOCK - qi
    in_band = np.abs(rel) <= DIL_SIDE
    idx = np.clip(rel + DIL_SIDE, 0, 2 * DIL_SIDE)
    tables = []
    for g, (_, d) in enumerate(DIL_PAIRS):
        off = d * jnp.arange(-DIL_SIDE, DIL_SIDE + 1, dtype=jnp.int32)
        b = rel_bias[_t5_bucket(off)][:, g * DIL_HEADS:(g + 1) * DIL_HEADS].astype(F32).T
        tables.append(jnp.where(in_band[None], b[:, idx], NEG_BIG))
    return tables


def _merge_kernel(x_ref, fa_ref, sg_ref, oc_ref, o0_ref, l0_ref, o1_ref, l1_ref, o2_ref, l2_ref,
                  wa_ref, wb_ref, wc_ref, wd_ref, wg_ref, bg_ref, wo_ref, g_ref, b_ref, o_ref, nat_ref):
    x = x_ref[...]
    xb = x.astype(BF16)
    tm = x.shape[0]
    for slot, (src, g) in enumerate(((o1_ref, 1), (l1_ref, 1), (o2_ref, 2), (l2_ref, 2))):
        d = DIL_PAIRS[g][1]
        for c in range(d):
            nat_ref[slot, pl.ds(c, tm // d, stride=d), :] = src[c]
    l0, l1, l2 = l0_ref[...], nat_ref[1], nat_ref[3]
    mx = jnp.maximum(jnp.maximum(l0, l1), l2)
    e0, e1, e2 = jnp.exp(l0 - mx), jnp.exp(l1 - mx), jnp.exp(l2 - mx)
    den = e0 + e1 + e2
    od = o0_ref[...] * (e0 / den) + nat_ref[0] * (e1 / den) + nat_ref[2] * (e2 / den)
    branches = ((fa_ref[...].astype(BF16), wa_ref), (sg_ref[...], wb_ref), (oc_ref[...], wc_ref),
                (od.astype(BF16), wd_ref))
    merged = None
    for i, (act, w_ref) in enumerate(branches):
        cols = slice(i * D_MODEL, (i + 1) * D_MODEL)
        gate = jax.nn.sigmoid(_dot(xb, wg_ref[:, cols]) + bg_ref[:, cols])
        term = gate * _dot(act, w_ref[...])
        merged = term if merged is None else merged + term
    y = DN_ALPHA * x + _dot(merged.astype(BF16), wo_ref[...])
    o_ref[...] = _layer_norm(y, g_ref[...], b_ref[...])


def _merge(x, S, fa, sg, oc, dil, lw):
    T = x.shape[0]
    tm = TOKEN_TILE
    nblk = S // tm
    tok = lambda w: pl.BlockSpec((tm, w), lambda i: (i, 0))
    dil_specs, dil_args = [], []
    for (o, lse), (_, d) in zip(dil, DIL_PAIRS):
        for a in (o, lse):
            if d == 1:
                dil_specs.append(tok(LANES))
                dil_args.append(a.reshape(T, LANES))
            else:
                dil_specs.append(pl.BlockSpec((None, d, tm // d, LANES), lambda i: (i // nblk, 0, i % nblk, 0)))
                dil_args.append(a)
    return pl.pallas_call(
        _merge_kernel,
        grid=(T // tm,),
        in_specs=[tok(D_MODEL), tok(F_WIDTH), tok(SG_WIDTH), tok(MLA_HEADS * MLA_V)] + dil_specs + [
                  _full((F_WIDTH, D_MODEL)), _full((SG_WIDTH, D_MODEL)), _full((MLA_HEADS * MLA_V, D_MODEL)),
                  _full((DIL_KV_WIDTH, D_MODEL)), _full((D_MODEL, 4 * D_MODEL)), _full((1, 4 * D_MODEL)),
                  _full((D_MODEL, D_MODEL)), _full((1, D_MODEL)), _full((1, D_MODEL))],
        out_specs=tok(D_MODEL),
        out_shape=jax.ShapeDtypeStruct((T, D_MODEL), F32),
        scratch_shapes=[pltpu.VMEM((4, tm, LANES), F32)],
        compiler_params=_cparams("parallel"),
        name="merge",
    )(x, fa, sg, oc, *dil_args, lw["w_a"], lw["w_b"], lw["w_c"], lw["w_d"], lw["w_gate"], lw["b_gate"],
      lw["w_o"], lw["ln1_g"], lw["ln1_b"])


MOE_PAIRS = tuple((a, b) for a in range(MOE_EXPERTS_PER_GROUP) for b in range(a + 1, MOE_EXPERTS_PER_GROUP))
MOE_CLASSES = MOE_GROUPS * len(MOE_PAIRS)
MOE_ROW_TILE = 512
INFO_CLASS, INFO_RANK, INFO_P_LO, INFO_P_HI = 0, 1, 2, 3
MOE_ROW_WIDTH = D_MODEL + LANES


def _route_kernel(x_ref, wr_ref, br_ref, tri_ref, info_ref, counts_ref, run_ref):
    @pl.when(pl.program_id(0) == 0)
    def _():
        run_ref[...] = jnp.zeros(run_ref.shape, F32)

    logits = jnp.dot(x_ref[...], wr_ref[...], preferred_element_type=F32,
                     precision=lax.Precision.HIGHEST) + br_ref[...]
    tm = logits.shape[0]
    lane = lax.broadcasted_iota(jnp.int32, (tm, LANES), 1)
    is_g = lane < MOE_GROUPS
    gl = jnp.where(is_g, logits, NEG_BIG)
    gmax = jnp.max(gl, -1, keepdims=True)
    g_top = jnp.min(jnp.where(gl == gmax, lane, LANES), -1, keepdims=True)
    p_group = 1.0 / jnp.sum(jnp.where(is_g, jnp.exp(gl - gmax), 0.0), -1, keepdims=True)
    base = MOE_GROUPS + g_top * MOE_EXPERTS_PER_GROUP
    in_grp = jnp.logical_and(lane >= base, lane < base + MOE_EXPERTS_PER_GROUP)
    el = jnp.where(in_grp, logits, NEG_BIG)
    v1 = jnp.max(el, -1, keepdims=True)
    i1 = jnp.min(jnp.where(el == v1, lane, LANES), -1, keepdims=True)
    el2 = jnp.where(lane == i1, NEG_BIG, el)
    v2 = jnp.max(el2, -1, keepdims=True)
    i2 = jnp.min(jnp.where(el2 == v2, lane, LANES), -1, keepdims=True)
    e2 = jnp.exp(v2 - v1)
    p1 = p_group / (1.0 + e2)
    p2 = p_group * e2 / (1.0 + e2)
    a = jnp.minimum(i1, i2) - base
    b = jnp.maximum(i1, i2) - base
    pair = jnp.where(a == 0, 0, jnp.where(a == 1, 3, 5)) + (b - a - 1)
    cls = g_top * len(MOE_PAIRS) + pair
    first_is_lo = i1 < i2
    p_lo = jnp.where(first_is_lo, p1, p2)
    p_hi = jnp.where(first_is_lo, p2, p1)
    onehot = lane == cls
    before = _dot(tri_ref[...], onehot.astype(BF16)) + run_ref[...]
    rank = jnp.sum(jnp.where(onehot, before, 0.0), -1, keepdims=True)
    run_ref[...] += jnp.sum(onehot.astype(F32), axis=0, keepdims=True)
    counts_ref[...] = run_ref[...]
    info_ref[...] = jnp.where(lane == INFO_CLASS, cls.astype(F32),
                              jnp.where(lane == INFO_RANK, rank,
                                        jnp.where(lane == INFO_P_LO, p_lo,
                                                  jnp.where(lane == INFO_P_HI, p_hi, 0.0))))


ROW_COPY_UNROLL = 8


def _row_copies(n, make_copy, whole_copy):
    def start(r, carry):
        make_copy(r).start()
        return carry

    lax.fori_loop(0, n, start, 0, unroll=ROW_COPY_UNROLL)
    whole_copy.wait()


def _dispatch_kernel(dest_ref, x_ref, info_ref, init_hbm, rows_hbm, row_ref, sem):
    del init_hbm
    tm = x_ref.shape[0]
    row_ref[:, :D_MODEL] = x_ref[...]
    row_ref[:, D_MODEL:] = info_ref[...]
    _row_copies(tm, lambda r: pltpu.make_async_copy(
        row_ref.at[pl.ds(r, 1)], rows_hbm.at[pl.ds(dest_ref[0, r], 1)], sem),
        pltpu.make_async_copy(row_ref, rows_hbm.at[pl.ds(0, tm)], sem))


def _expert_kernel(ea_ref, eb_ref, nused_ref, rows_ref, wga_ref, wua_ref, wda_ref, wgb_ref, wub_ref, wdb_ref,
                   y_ref):
    del ea_ref, eb_ref
    i = pl.program_id(0)

    @pl.when(i < nused_ref[0])
    def _():
        x = rows_ref[:, :D_MODEL].astype(BF16)
        p_lo = rows_ref[:, D_MODEL + INFO_P_LO:D_MODEL + INFO_P_LO + 1]
        p_hi = rows_ref[:, D_MODEL + INFO_P_HI:D_MODEL + INFO_P_HI + 1]
        ha = jax.nn.silu(_dot(x, wga_ref[...])) * _dot(x, wua_ref[...]) * p_lo
        hb = jax.nn.silu(_dot(x, wgb_ref[...])) * _dot(x, wub_ref[...]) * p_hi
        y_ref[...] = _dot(ha.astype(BF16), wda_ref[...]) + _dot(hb.astype(BF16), wdb_ref[...])

    @pl.when(i >= nused_ref[0])
    def _():
        y_ref[...] = jnp.zeros(y_ref.shape, F32)


def _combine_kernel(dest_ref, x_ref, g_ref, b_ref, y_hbm, o_ref, y_ref, sem):
    tm = x_ref.shape[0]
    _row_copies(tm, lambda r: pltpu.make_async_copy(
        y_hbm.at[pl.ds(dest_ref[0, r], 1)], y_ref.at[pl.ds(r, 1)], sem),
        pltpu.make_async_copy(y_hbm.at[pl.ds(0, tm)], y_ref, sem))
    o_ref[...] = _layer_norm(DN_ALPHA * x_ref[...] + y_ref[...], g_ref[...], b_ref[...])


def _moe(x, lw):
    T = x.shape[0]
    tm = TOKEN_TILE
    rt = MOE_ROW_TILE
    n_row_tiles = T // rt + MOE_CLASSES
    n_rows = n_row_tiles * rt
    tok = lambda w: pl.BlockSpec((tm, w), lambda i: (i, 0))

    tri = jnp.asarray(np.tril(np.ones((tm, tm)), -1), BF16)
    info, counts = pl.pallas_call(
        _route_kernel,
        grid=(T // tm,),
        in_specs=[tok(D_MODEL), _full((D_MODEL, LANES)), _full((1, LANES)), _full((tm, tm))],
        out_specs=[tok(LANES), _full((1, LANES))],
        out_shape=[jax.ShapeDtypeStruct((T, LANES), F32), jax.ShapeDtypeStruct((1, LANES), F32)],
        scratch_shapes=[pltpu.VMEM((1, LANES), F32)],
        compiler_params=_cparams("arbitrary"),
        name="moe_route",
    )(x, lw["w_router"], lw["b_router"], tri)

    cls = info[:, INFO_CLASS].astype(jnp.int32)
    rank = info[:, INFO_RANK].astype(jnp.int32)
    cnt = counts[0, :MOE_CLASSES].astype(jnp.int32)
    padded = (cnt + rt - 1) // rt * rt
    ends = jnp.cumsum(padded)
    dest = ((ends - padded)[cls] + rank).reshape(T // tm, 1, tm)
    tile_cls = jnp.minimum(jnp.searchsorted(ends, jnp.arange(n_row_tiles, dtype=jnp.int32) * rt, side="right"),
                           MOE_CLASSES - 1)
    group, pair = np.divmod(np.arange(MOE_CLASSES), len(MOE_PAIRS))
    lo_hi = np.asarray(MOE_PAIRS)[pair]
    ea = jnp.asarray(group * MOE_EXPERTS_PER_GROUP + lo_hi[:, 0], jnp.int32)[tile_cls]
    eb = jnp.asarray(group * MOE_EXPERTS_PER_GROUP + lo_hi[:, 1], jnp.int32)[tile_cls]
    n_used = (ends[-1:] // rt).astype(jnp.int32)

    dest_spec = pl.BlockSpec((None, 1, tm), lambda i: (i, 0, 0), memory_space=pltpu.SMEM)
    hbm = pl.BlockSpec(memory_space=pl.ANY)
    rows = pl.pallas_call(
        _dispatch_kernel,
        grid=(T // tm,),
        in_specs=[dest_spec, tok(D_MODEL), tok(LANES), hbm],
        out_specs=hbm,
        out_shape=jax.ShapeDtypeStruct((n_rows, MOE_ROW_WIDTH), F32),
        scratch_shapes=[pltpu.VMEM((tm, MOE_ROW_WIDTH), F32), pltpu.SemaphoreType.DMA],
        input_output_aliases={3: 0},
        compiler_params=_cparams("arbitrary"),
        name="moe_dispatch",
    )(dest, x, info, jnp.zeros((n_rows, MOE_ROW_WIDTH), F32))

    w_up = lambda sel: pl.BlockSpec((None, D_MODEL, MOE_FF), lambda i, ea, eb, nu: (sel(ea, eb)[i], 0, 0))
    w_dn = lambda sel: pl.BlockSpec((None, MOE_FF, D_MODEL), lambda i, ea, eb, nu: (sel(ea, eb)[i], 0, 0))
    first = lambda ea, eb: ea
    second = lambda ea, eb: eb
    y = pl.pallas_call(
        _expert_kernel,
        grid_spec=pltpu.PrefetchScalarGridSpec(
            num_scalar_prefetch=3,
            grid=(n_row_tiles,),
            in_specs=[pl.BlockSpec((rt, MOE_ROW_WIDTH), lambda i, ea, eb, nu: (i, 0)),
                      w_up(first), w_up(first), w_dn(first), w_up(second), w_up(second), w_dn(second)],
            out_specs=pl.BlockSpec((rt, D_MODEL), lambda i, ea, eb, nu: (i, 0)),
        ),
        out_shape=jax.ShapeDtypeStruct((n_rows, D_MODEL), F32),
        compiler_params=_cparams("arbitrary"),
        name="moe_experts",
    )(ea, eb, n_used, rows, lw["moe_w_gate"], lw["moe_w_up"], lw["moe_w_down"],
      lw["moe_w_gate"], lw["moe_w_up"], lw["moe_w_down"])

    return pl.pallas_call(
        _combine_kernel,
        grid=(T // tm,),
        in_specs=[dest_spec, tok(D_MODEL), _full((1, D_MODEL)), _full((1, D_MODEL)), hbm],
        out_specs=tok(D_MODEL),
        out_shape=jax.ShapeDtypeStruct((T, D_MODEL), F32),
        scratch_shapes=[pltpu.VMEM((tm, D_MODEL), F32), pltpu.SemaphoreType.DMA],
        compiler_params=_cparams("arbitrary"),
        name="moe_combine",
    )(dest, x, lw["ln2_g"], lw["ln2_b"], y)


def _rope_tables(S):
    half = MLA_ROPE // 2
    inv = ROPE_BASE ** (-jnp.arange(half, dtype=F32) / half)
    ang = jnp.arange(S, dtype=F32)[:, None] * inv[None, :]
    cos, sin = jnp.cos(ang), jnp.sin(ang)
    one = jnp.ones((S, MLA_NOPE), F32)
    zero = jnp.zeros((S, MLA_NOPE), F32)
    zh = jnp.zeros((S, half), F32)
    rc = jnp.concatenate([one, cos, cos], axis=1)
    rsa = jnp.concatenate([zero, -sin, zh], axis=1)
    rsb = jnp.concatenate([zero, zh, sin], axis=1)
    return rc, rsa, rsb


def _prep_layer(l, p):
    row = lambda a: a.reshape(1, -1).astype(F32)
    w_ukv = p["mla_w_ukv"][l].reshape(MLA_KV_RANK, MLA_HEADS, MLA_NOPE + MLA_V)
    pad_rows = ((0, 2 * LANES - MLA_KV_RANK), (0, 0))
    w_k = jnp.pad(w_ukv[:, :, :MLA_NOPE], ((0, 0), (0, 0), (0, LANES - MLA_NOPE))).reshape(MLA_KV_RANK, -1)
    w_v = w_ukv[:, :, MLA_NOPE:].reshape(MLA_KV_RANK, -1)
    w_router = jnp.concatenate([p["moe_w_rg"][l], p["moe_w_re"][l]], axis=1)
    b_router = jnp.concatenate([p["moe_b_rg"][l], p["moe_b_re"][l]])
    npad = LANES - MOE_GROUPS - MOE_EXPERTS
    return {
        "w_in": p["w_in"][l].astype(BF16),
        "sg_ln_g": row(p["sg_ln_g"][l]), "sg_ln_b": row(p["sg_ln_b"][l]),
        "sg_w": p["sg_w"][l].astype(BF16),
        "sg_bias": jnp.repeat(p["sg_b"][l].T, SG_GROUP_DIM, axis=1).astype(F32),
        "q_norm": row(p["mla_q_norm"][l]),
        "kv_norm": jnp.pad(row(p["mla_kv_norm"][l]), ((0, 0), (0, 2 * LANES - MLA_KV_RANK))),
        "w_uq": p["mla_w_uq"][l].astype(BF16),
        "w_k": jnp.pad(w_k, pad_rows).astype(BF16),
        "w_v": jnp.pad(w_v, pad_rows).astype(BF16),
        "w_a": p["w_branch_a"][l].astype(BF16), "w_b": p["w_branch_b"][l].astype(BF16),
        "w_c": p["w_branch_c"][l].astype(BF16), "w_d": p["w_branch_d"][l].astype(BF16),
        "w_gate": p["w_gate"][l].astype(BF16), "b_gate": row(p["b_gate"][l]),
        "w_o": p["w_o"][l].astype(BF16),
        "ln1_g": row(p["ln1_g"][l]), "ln1_b": row(p["ln1_b"][l]),
        "w_router": jnp.pad(w_router, ((0, 0), (0, npad))).astype(F32),
        "b_router": jnp.pad(b_router, (0, npad)).reshape(1, -1).astype(F32),
        "moe_w_gate": p["moe_w_gate"][l].astype(BF16), "moe_w_up": p["moe_w_up"][l].astype(BF16),
        "moe_w_down": p["moe_w_down"][l].astype(BF16),
        "ln2_g": row(p["ln2_g"][l]), "ln2_b": row(p["ln2_b"][l]),
    }


def _trunk(x, p, layers, bias_tables):
    B, S, _ = x.shape
    rope = _rope_tables(S)
    fconsts = _fourier_consts(S)
    h = _input_layer_norm(x.reshape(B * S, D_MODEL), p["ln_in_g"], p["ln_in_b"])
    for lw in layers:
        za, sg, q, k, vt, *dil_in = _in_proj(h, B, S, lw, rope)
        fa = _fourier_mix(za, B, S, fconsts)
        oc = _latent_attention(q, k, vt, B, S)
        dil = [_dilated_group(*dil_in[3 * g:3 * g + 3], bias_tables[g], g, d, B, S)
               for g, (_, d) in enumerate(DIL_PAIRS)]
        h = _merge(h, S, fa, sg, oc, dil, lw)
        h = _moe(h, lw)
    return h.reshape(B, S, D_MODEL)


def kernel(x_prompt, x_sample, ln_in_g, ln_in_b, rel_bias, w_in, sg_ln_g, sg_ln_b, sg_w, sg_b, mla_q_norm, mla_kv_norm, mla_w_uq, mla_w_ukv, w_branch_a, w_branch_b, w_branch_c, w_branch_d, w_gate, b_gate, w_o, ln1_g, ln1_b, moe_w_rg, moe_b_rg, moe_w_re, moe_b_re, moe_w_gate, moe_w_up, moe_w_down, ln2_g, ln2_b):
    p = dict(ln_in_g=ln_in_g, ln_in_b=ln_in_b, w_in=w_in, sg_ln_g=sg_ln_g, sg_ln_b=sg_ln_b, sg_w=sg_w, sg_b=sg_b,
             mla_q_norm=mla_q_norm, mla_kv_norm=mla_kv_norm, mla_w_uq=mla_w_uq, mla_w_ukv=mla_w_ukv,
             w_branch_a=w_branch_a, w_branch_b=w_branch_b, w_branch_c=w_branch_c, w_branch_d=w_branch_d,
             w_gate=w_gate, b_gate=b_gate, w_o=w_o, ln1_g=ln1_g, ln1_b=ln1_b,
             moe_w_rg=moe_w_rg, moe_b_rg=moe_b_rg, moe_w_re=moe_w_re, moe_b_re=moe_b_re,
             moe_w_gate=moe_w_gate, moe_w_up=moe_w_up, moe_w_down=moe_w_down, ln2_g=ln2_g, ln2_b=ln2_b)
    layers = [_prep_layer(l, p) for l in range(w_in.shape[0])]
    bias_tables = _dilated_bias_tables(rel_bias)
    return _trunk(x_prompt, p, layers, bias_tables), _trunk(x_sample, p, layers, bias_tables)
```

```python
import functools
import math

import numpy as np
import jax
import jax.numpy as jnp
from jax import lax
from jax.experimental import pallas as pl
from jax.experimental.pallas import tpu as pltpu

F32 = jnp.float32
BF16 = jnp.bfloat16

D_MODEL = 1024
DEPTH = 4
F_GROUPS = 4
F_GROUP_DIM = 128
F_WIDTH = F_GROUPS * F_GROUP_DIM
SG_CHUNK = 128
SG_GROUPS = 4
SG_GROUP_DIM = 64
SG_WIDTH = SG_GROUPS * SG_GROUP_DIM
MLA_HEADS = 8
MLA_Q_RANK = 256
MLA_KV_RANK = 192
MLA_NOPE = 64
MLA_ROPE = 64
MLA_V = 64
MLA_QK_DIM = MLA_NOPE + MLA_ROPE
ROPE_BASE = 10000.0
DIL_PAIRS = ((128, 1), (512, 4), (2048, 16))
DIL_GROUPS = 3
DIL_HEADS = 4
DIL_HEAD_DIM = 32
DIL_Q_WIDTH = DIL_GROUPS * DIL_HEADS * DIL_HEAD_DIM
DIL_KV_WIDTH = DIL_HEADS * DIL_HEAD_DIM
DIL_SIDE = 64
REL_BUCKETS = 32
REL_MAX_DIST = 1024
MIX_WIDTH = F_WIDTH + 2 * SG_WIDTH + MLA_Q_RANK + MLA_KV_RANK + MLA_ROPE + DIL_Q_WIDTH + 2 * DIL_KV_WIDTH
MOE_GROUPS = 4
MOE_EXPERTS_PER_GROUP = 4
MOE_EXPERTS = MOE_GROUPS * MOE_EXPERTS_PER_GROUP
MOE_FF = 512
DN_ALPHA = (2 * DEPTH) ** 0.25
LN_EPS = 1e-5
RMS_EPS = 1e-6

OFF_A = 0
OFF_B = OFF_A + F_WIDTH
OFF_CQ = OFF_B + 2 * SG_WIDTH
OFF_CKV = OFF_CQ + MLA_Q_RANK
OFF_DQ = OFF_CKV + MLA_KV_RANK + MLA_ROPE
OFF_DK = OFF_DQ + DIL_Q_WIDTH
OFF_DV = OFF_DK + DIL_KV_WIDTH

LANES = 128
SUBLANES = 8
VMEM_LIMIT_BYTES = 56 * 1024 * 1024
TOKEN_TILE = 512
ATTN_Q_BLOCK = 512
ATTN_Q_SUB = 256
ATTN_KV_BLOCK = 2048
ATTN_CHUNK = 256
ATTN_CHUNKS_PER_ITER = 4
ATTN_PIPELINE_DEPTH = 6
ATTN_V_ROWS = 80
DIL_BLOCK = 128
DIL_Q_BLOCK = 512
NEG_BIG = -1e30
LOG2_E = math.log2(math.e)


def _cparams(*sem):
    return pltpu.CompilerParams(dimension_semantics=sem, vmem_limit_bytes=VMEM_LIMIT_BYTES)


def _full(shape):
    n = len(shape)
    return pl.BlockSpec(shape, lambda *_: (0,) * n)


def _layer_norm(x, g, b):
    mu = jnp.mean(x, -1, keepdims=True)
    xc = x - mu
    var = jnp.mean(xc * xc, -1, keepdims=True)
    return xc * lax.rsqrt(var + LN_EPS) * g + b


def _dot(a, b):
    return jnp.dot(a, b, preferred_element_type=F32)


def _ln_kernel(x_ref, g_ref, b_ref, o_ref):
    o_ref[...] = _layer_norm(x_ref[...], g_ref[...], b_ref[...])


def _input_layer_norm(x, g, b):
    T = x.shape[0]
    tm = TOKEN_TILE
    return pl.pallas_call(
        _ln_kernel,
        grid=(T // tm,),
        in_specs=[pl.BlockSpec((tm, D_MODEL), lambda i: (i, 0)), _full((1, D_MODEL)), _full((1, D_MODEL))],
        out_specs=pl.BlockSpec((tm, D_MODEL), lambda i: (i, 0)),
        out_shape=jax.ShapeDtypeStruct((T, D_MODEL), F32),
        compiler_params=_cparams("parallel"),
        name="input_layer_norm",
    )(x, g.reshape(1, -1), b.reshape(1, -1))


def _rope_lanes(t, c, sa, sb):
    return t * c + pltpu.roll(t, 96, 1) * sa + pltpu.roll(t, 32, 1) * sb


def _in_proj_kernel(x_ref, w_in_ref, sg_g_ref, sg_b_ref, sg_w_ref, sg_bias_ref, qn_ref, kvn_ref,
                    wuq_ref, wk_ref, wv_ref, rc_ref, rsa_ref, rsb_ref,
                    za_ref, sg_ref, qt_ref, k_ref, vt_ref,
                    dq0_ref, dk0_ref, dv0_ref, dq1_ref, dk1_ref, dv1_ref, dq2_ref, dk2_ref, dv2_ref, dil_ref):
    tm = x_ref.shape[0]
    z = _dot(x_ref[...].astype(BF16), w_in_ref[...])

    za_ref[...] = z[:, OFF_A:OFF_A + F_WIDTH]

    zb = jax.nn.gelu(z[:, OFF_B:OFF_B + 2 * SG_WIDTH])
    u = zb[:, :SG_WIDTH]
    vn = _layer_norm(zb[:, SG_WIDTH:], sg_g_ref[...], sg_b_ref[...]).astype(BF16)
    lane = lax.broadcasted_iota(jnp.int32, (SG_CHUNK, LANES), 1)
    low_half = lane < SG_GROUP_DIM
    for ci in range(tm // SG_CHUNK):
        rows = slice(ci * SG_CHUNK, (ci + 1) * SG_CHUNK)
        for j in range(SG_WIDTH // LANES):
            cols = slice(j * LANES, (j + 1) * LANES)
            vblk = vn[rows, cols]
            mixed = jnp.where(low_half, _dot(sg_w_ref[2 * j], vblk), _dot(sg_w_ref[2 * j + 1], vblk))
            sg_ref[rows, cols] = (u[rows, cols] * (mixed + sg_bias_ref[:, cols])).astype(sg_ref.dtype)

    rc, rsa, rsb = rc_ref[...], rsa_ref[...], rsb_ref[...]

    cq = z[:, OFF_CQ:OFF_CQ + MLA_Q_RANK]
    cq = cq * lax.rsqrt(jnp.mean(cq * cq, -1, keepdims=True) + RMS_EPS) * qn_ref[...]
    q = _dot(cq.astype(BF16), wuq_ref[...]) * (MLA_QK_DIM ** -0.5 * LOG2_E)
    ra = pltpu.roll(q, MLA_HEADS * LANES - 32, 1)
    rb = pltpu.roll(q, 32, 1)
    q_t = jnp.concatenate(
        [q[:, h * LANES:(h + 1) * LANES] * rc + ra[:, h * LANES:(h + 1) * LANES] * rsa
         + rb[:, h * LANES:(h + 1) * LANES] * rsb for h in range(MLA_HEADS)], axis=1).T
    for j in range(tm // ATTN_Q_SUB):
        qt_ref[j] = q_t[:, j * ATTN_Q_SUB:(j + 1) * ATTN_Q_SUB].astype(qt_ref.dtype)

    slab = z[:, OFF_CKV:OFF_CKV + 2 * LANES]
    lane2 = lax.broadcasted_iota(jnp.int32, (tm, 2 * LANES), 1)
    ckv_sq = jnp.where(lane2 < MLA_KV_RANK, slab * slab, 0.0)
    ms = jnp.sum(ckv_sq, -1, keepdims=True) * (1.0 / MLA_KV_RANK)
    ckv = (slab * lax.rsqrt(ms + RMS_EPS) * kvn_ref[...]).astype(BF16)
    k_nope = _dot(ckv, wk_ref[...])
    v_t = _dot(ckv, wv_ref[...]).T.astype(vt_ref.dtype)
    ones = jnp.ones((ATTN_V_ROWS - MLA_V, ATTN_CHUNK), vt_ref.dtype)
    for j in range(tm // ATTN_CHUNK):
        cols = slice(j * ATTN_CHUNK, (j + 1) * ATTN_CHUNK)
        for h in range(MLA_HEADS):
            vt_ref[j, h * ATTN_V_ROWS:h * ATTN_V_ROWS + MLA_V, :] = v_t[h * MLA_V:(h + 1) * MLA_V, cols]
            vt_ref[j, h * ATTN_V_ROWS + MLA_V:(h + 1) * ATTN_V_ROWS, :] = ones
    kr_slab = _rope_lanes(slab[:, LANES:], rc, rsa, rsb)
    kr_slab = jnp.where(lax.broadcasted_iota(jnp.int32, (tm, LANES), 1) >= MLA_NOPE, kr_slab, 0.0)
    for h in range(MLA_HEADS):
        cols = slice(h * LANES, (h + 1) * LANES)
        k_ref[:, cols] = (k_nope[:, cols] + kr_slab).astype(k_ref.dtype)

    for slab in range(DIL_GROUPS + 2):
        t = z[:, OFF_DQ + slab * LANES:OFF_DQ + (slab + 1) * LANES]
        dil_ref[slab] = t * (DIL_HEAD_DIM ** -0.5) if slab < DIL_GROUPS else t
    dq0_ref[...] = dil_ref[0].astype(dq0_ref.dtype)
    dk0_ref[...] = dil_ref[DIL_GROUPS].astype(dk0_ref.dtype)
    dv0_ref[...] = dil_ref[DIL_GROUPS + 1].astype(dv0_ref.dtype)
    for g, (qr, kr, vr) in ((1, (dq1_ref, dk1_ref, dv1_ref)), (2, (dq2_ref, dk2_ref, dv2_ref))):
        d = DIL_PAIRS[g][1]
        for c in range(d):
            rows = pl.ds(c, tm // d, stride=d)
            qr[c] = dil_ref[g, rows, :].astype(qr.dtype)
            kr[c] = dil_ref[DIL_GROUPS, rows, :].astype(kr.dtype)
            vr[c] = dil_ref[DIL_GROUPS + 1, rows, :].astype(vr.dtype)


def _in_proj(x, B, S, lw, rope):
    T = x.shape[0]
    tm = TOKEN_TILE
    nblk = S // tm
    tok = lambda w: pl.BlockSpec((tm, w), lambda i: (i, 0))
    pos = pl.BlockSpec((tm, LANES), lambda i: (i % nblk, 0))
    out_specs = [tok(F_WIDTH), tok(SG_WIDTH)]
    out_shape = [jax.ShapeDtypeStruct((T, F_WIDTH), F32), jax.ShapeDtypeStruct((T, SG_WIDTH), BF16)]
    out_specs.append(pl.BlockSpec((tm // ATTN_Q_SUB, MLA_HEADS * LANES, ATTN_Q_SUB), lambda i: (i, 0, 0)))
    out_shape.append(jax.ShapeDtypeStruct((T // ATTN_Q_SUB, MLA_HEADS * LANES, ATTN_Q_SUB), BF16))
    out_specs.append(tok(MLA_HEADS * LANES))
    out_shape.append(jax.ShapeDtypeStruct((T, MLA_HEADS * LANES), BF16))
    out_specs.append(pl.BlockSpec((tm // ATTN_CHUNK, MLA_HEADS * ATTN_V_ROWS, ATTN_CHUNK), lambda i: (i, 0, 0)))
    out_shape.append(jax.ShapeDtypeStruct((T // ATTN_CHUNK, MLA_HEADS * ATTN_V_ROWS, ATTN_CHUNK), BF16))
    for _, d in DIL_PAIRS:
        for _ in range(3):
            if d == 1:
                out_specs.append(tok(LANES))
                out_shape.append(jax.ShapeDtypeStruct((T, LANES), BF16))
            else:
                out_specs.append(pl.BlockSpec((None, d, tm // d, LANES), lambda i: (i // nblk, 0, i % nblk, 0)))
                out_shape.append(jax.ShapeDtypeStruct((B, d, S // d, LANES), BF16))
    return pl.pallas_call(
        _in_proj_kernel,
        grid=(T // tm,),
        in_specs=[tok(D_MODEL), _full((D_MODEL, MIX_WIDTH)), _full((1, SG_WIDTH)), _full((1, SG_WIDTH)),
                  _full((SG_GROUPS, SG_CHUNK, SG_CHUNK)), _full((SG_CHUNK, SG_WIDTH)),
                  _full((1, MLA_Q_RANK)), _full((1, 2 * LANES)),
                  _full((MLA_Q_RANK, MLA_HEADS * LANES)), _full((2 * LANES, MLA_HEADS * LANES)),
                  _full((2 * LANES, MLA_HEADS * MLA_V)), pos, pos, pos],
        out_specs=out_specs,
        out_shape=out_shape,
        scratch_shapes=[pltpu.VMEM((DIL_GROUPS + 2, tm, LANES), F32)],
        compiler_params=_cparams("parallel"),
        name="in_proj",
    )(x, lw["w_in"], lw["sg_ln_g"], lw["sg_ln_b"], lw["sg_w"], lw["sg_bias"], lw["q_norm"], lw["kv_norm"],
      lw["w_uq"], lw["w_k"], lw["w_v"], *rope)


FOURIER_N1 = 16
FOURIER_ROWS = 16
FOURIER_K1_PER_STEP = 2


def _fourier1_kernel(x_ref, cs_ref, k1_ref, ct_ref, st_ref, gr_ref, gi_ref):
    n1, r, _ = x_ref.shape
    rows = n1 * r
    x = x_ref[...].reshape(rows, F_WIDTH)
    ct, st = ct_ref[...], st_ref[...]
    outs_r, outs_i = [], []
    for g in range(F_GROUPS):
        ab = _dot(x[:, g * LANES:(g + 1) * LANES].astype(BF16), cs_ref[...])
        stacked = jnp.concatenate([ab[:, :LANES], ab[:, LANES:]], axis=0).astype(BF16)
        g2 = _dot(k1_ref[...], stacked)
        gr, gi = g2[:rows], g2[rows:]
        outs_r.append(gr * ct - gi * st)
        outs_i.append(gr * st + gi * ct)
    gr_ref[...] = jnp.concatenate(outs_r, axis=1).reshape(n1, r, F_WIDTH)
    gi_ref[...] = jnp.concatenate(outs_i, axis=1).reshape(n1, r, F_WIDTH)


def _fourier2_kernel(gr_ref, gi_ref, w2_ref, o_ref):
    for j in range(gr_ref.shape[0]):
        stacked = jnp.concatenate([gr_ref[j], gi_ref[j]], axis=0).astype(BF16)
        o_ref[j] = _dot(w2_ref[...], stacked)


def _fourier_mix(za, B, S, consts):
    n1 = FOURIER_N1
    n2 = S // n1
    r = FOURIER_ROWS
    kk = FOURIER_K1_PER_STEP
    cs, k1, ct, st, w2 = consts
    x = za.reshape(B, n1, n2, F_WIDTH)
    blk1 = pl.BlockSpec((None, n1, r, F_WIDTH), lambda b, j: (b, 0, j, 0))
    twid = pl.BlockSpec((None, n1 * r, LANES), lambda b, j: (j, 0, 0))
    gr, gi = pl.pallas_call(
        _fourier1_kernel,
        grid=(B, n2 // r),
        in_specs=[blk1, _full((LANES, 2 * LANES)), _full((2 * n1 * r, 2 * n1 * r)), twid, twid],
        out_specs=[blk1, blk1],
        out_shape=[jax.ShapeDtypeStruct(x.shape, F32)] * 2,
        compiler_params=_cparams("parallel", "parallel"),
        name="fourier_stage1",
    )(x, cs, k1, ct, st)
    blk2 = pl.BlockSpec((None, kk, n2, F_WIDTH), lambda b, j: (b, j, 0, 0))
    return pl.pallas_call(
        _fourier2_kernel,
        grid=(B, n1 // kk),
        in_specs=[blk2, blk2, _full((n2, 2 * n2))],
        out_specs=blk2,
        out_shape=jax.ShapeDtypeStruct((B, n1, n2, F_WIDTH), F32),
        compiler_params=_cparams("parallel", "parallel"),
        name="fourier_stage2",
    )(gr, gi, w2)


def _fourier_consts(S):
    n1 = FOURIER_N1
    n2 = S // n1
    r = FOURIER_ROWS
    c = np.arange(F_GROUP_DIM)
    ang_c = 2.0 * np.pi * np.outer(c, c) / F_GROUP_DIM
    norm = 1.0 / math.sqrt(S * F_GROUP_DIM)
    cs = np.concatenate([np.cos(ang_c), np.sin(ang_c)], axis=1) * norm
    eye = np.eye(r)
    a1 = np.arange(n1)
    ang1 = 2.0 * np.pi * np.outer(a1, a1) / n1
    c1, s1 = np.kron(np.cos(ang1), eye), np.kron(np.sin(ang1), eye)
    k1 = np.block([[c1, -s1], [s1, c1]])
    a2 = np.arange(n2)
    ang_t = 2.0 * np.pi * np.outer(a1, a2) / S

    def twiddle(t):
        t = t.reshape(n1, n2 // r, r).transpose(1, 0, 2).reshape(n2 // r, n1 * r)
        return np.broadcast_to(t[:, :, None], (n2 // r, n1 * r, LANES))

    ang2 = 2.0 * np.pi * np.outer(a2, a2) / n2
    w2 = np.concatenate([np.cos(ang2), -np.sin(ang2)], axis=1)
    return (jnp.asarray(cs, BF16), jnp.asarray(k1, BF16), jnp.asarray(twiddle(np.cos(ang_t)), F32),
            jnp.asarray(twiddle(np.sin(ang_t)), F32), jnp.asarray(w2, BF16))


def _flash_kernel(qt_ref, k_ref, vt_ref, o_ref, m_ref, l_ref, acc_ref):
    ki = pl.program_id(2)
    nsub = qt_ref.shape[0]
    nchunk = k_ref.shape[0] // ATTN_CHUNK

    @pl.when(ki == 0)
    def _():
        m_ref[...] = jnp.full(m_ref.shape, NEG_BIG, F32)
        l_ref[...] = jnp.zeros(l_ref.shape, F32)
        acc_ref[...] = jnp.zeros(acc_ref.shape, F32)

    per_chunk = MLA_HEADS * nsub
    n_tiles = ATTN_CHUNKS_PER_ITER * per_chunk

    def tile_index(it, t):
        cc, rem = divmod(t, per_chunk)
        h, qs = divmod(rem, nsub)
        return it * ATTN_CHUNKS_PER_ITER + cc, h, qs

    def scores(it, t):
        c, h, qs = tile_index(it, t)
        r0 = pl.multiple_of(c * ATTN_CHUNK, ATTN_CHUNK)
        kc = k_ref[pl.ds(r0, ATTN_CHUNK), h * LANES:(h + 1) * LANES]
        return _dot(kc, qt_ref[qs, h * LANES:(h + 1) * LANES, :])

    def accumulate(it, t, s):
        c, h, qs = tile_index(it, t)
        rows = slice(h * MLA_V, (h + 1) * MLA_V)
        m_prev = m_ref[qs, h]
        m_new = jnp.maximum(m_prev, jnp.max(s, axis=0, keepdims=True))
        alpha = jnp.exp2(m_prev - m_new)
        p = jnp.exp2(s - m_new[0:1, :]).astype(BF16)
        m_ref[qs, h] = m_new
        pv = _dot(vt_ref[c, h * ATTN_V_ROWS:(h + 1) * ATTN_V_ROWS, :], p)
        l_ref[qs, h] = alpha * l_ref[qs, h] + pv[MLA_V:MLA_V + SUBLANES, :]
        acc_ref[qs, rows, :] = alpha[0:1, :] * acc_ref[qs, rows, :] + pv[:MLA_V, :]

    def iteration(it, carry):
        pending = {}
        for t in range(n_tiles + ATTN_PIPELINE_DEPTH):
            if t < n_tiles:
                pending[t] = scores(it, t)
            if t >= ATTN_PIPELINE_DEPTH:
                accumulate(it, t - ATTN_PIPELINE_DEPTH, pending.pop(t - ATTN_PIPELINE_DEPTH))
        return carry

    lax.fori_loop(0, nchunk // ATTN_CHUNKS_PER_ITER, iteration, 0)

    @pl.when(ki == pl.num_programs(2) - 1)
    def _():
        for qs in range(nsub):
            out_t = jnp.concatenate(
                [acc_ref[qs, h * MLA_V:(h + 1) * MLA_V, :] * (1.0 / l_ref[qs, h, 0:1, :])
                 for h in range(MLA_HEADS)], axis=0)
            o_ref[qs * ATTN_Q_SUB:(qs + 1) * ATTN_Q_SUB, :] = out_t.T.astype(o_ref.dtype)


def _latent_attention(qt, k, vt, B, S):
    tq = min(ATTN_Q_BLOCK, S)
    tk = min(ATTN_KV_BLOCK, S)
    nsub = tq // ATTN_Q_SUB
    qt = qt.reshape(B, S // ATTN_Q_SUB, MLA_HEADS * LANES, ATTN_Q_SUB)
    k = k.reshape(B, S, MLA_HEADS * LANES)
    vt = vt.reshape(B, S // ATTN_CHUNK, MLA_HEADS * ATTN_V_ROWS, ATTN_CHUNK)
    out = pl.pallas_call(
        _flash_kernel,
        grid=(B, S // tq, S // tk),
        in_specs=[pl.BlockSpec((None, nsub, MLA_HEADS * LANES, ATTN_Q_SUB), lambda b, i, j: (b, i, 0, 0)),
                  pl.BlockSpec((None, tk, MLA_HEADS * LANES), lambda b, i, j: (b, j, 0)),
                  pl.BlockSpec((None, tk // ATTN_CHUNK, MLA_HEADS * ATTN_V_ROWS, ATTN_CHUNK),
                               lambda b, i, j: (b, j, 0, 0))],
        out_specs=pl.BlockSpec((None, tq, MLA_HEADS * MLA_V), lambda b, i, j: (b, i, 0)),
        out_shape=jax.ShapeDtypeStruct((B, S, MLA_HEADS * MLA_V), BF16),
        scratch_shapes=[pltpu.VMEM((nsub, MLA_HEADS, SUBLANES, ATTN_Q_SUB), F32),
                        pltpu.VMEM((nsub, MLA_HEADS, SUBLANES, ATTN_Q_SUB), F32),
                        pltpu.VMEM((nsub, MLA_HEADS * MLA_V, ATTN_Q_SUB), F32)],
        compiler_params=_cparams("parallel", "parallel", "arbitrary"),
        name="latent_attention",
    )(qt, k, vt)
    return out.reshape(B * S, MLA_HEADS * MLA_V)


def _dilated_kernel(q_ref, kp_ref, kc_ref, kn_ref, vp_ref, vc_ref, vn_ref, bias_ref, o_ref, lse_ref, *, rows):
    i = pl.program_id(2)
    qb = q_ref.shape[0]
    k_all = jnp.concatenate([kp_ref[...], kc_ref[...], kn_ref[...]], axis=0)
    v_all = jnp.concatenate([vp_ref[...], vc_ref[...], vn_ref[...]], axis=0)
    v_all_t = v_all.astype(F32).T.astype(BF16)
    row = lax.broadcasted_iota(jnp.int32, (3 * DIL_BLOCK, DIL_BLOCK), 0)
    lane = lax.broadcasted_iota(jnp.int32, (1, LANES), 1)
    dim = lax.broadcasted_iota(jnp.int32, (LANES, 1), 0)
    n_sub = qb // DIL_BLOCK
    scores = {}
    for sub in range(n_sub):
        q = q_ref[sub * DIL_BLOCK:(sub + 1) * DIL_BLOCK, :]
        k3 = k_all[sub * DIL_BLOCK:(sub + 3) * DIL_BLOCK, :]
        for h in range(DIL_HEADS):
            head_lanes = jnp.logical_and(lane >= h * DIL_HEAD_DIM, lane < (h + 1) * DIL_HEAD_DIM)
            qh = jnp.where(head_lanes, q, jnp.zeros_like(q))
            scores[sub, h] = lax.dot_general(k3, qh, (((1,), (1,)), ((), ())), preferred_element_type=F32)
    for sub in range(n_sub):
        inside = 1 <= sub <= n_sub - 2
        key_row = i * qb + (sub - 1) * DIL_BLOCK + row
        in_seq = jnp.logical_and(key_row >= 0, key_row < rows)
        v3_t = v_all_t[:, sub * DIL_BLOCK:(sub + 3) * DIL_BLOCK]
        pvs, lses = [], []
        for h in range(DIL_HEADS):
            s = scores.pop((sub, h)) + bias_ref[h]
            s = s if inside else jnp.where(in_seq, s, NEG_BIG)
            m = jnp.max(s, axis=0, keepdims=True)
            e = jnp.exp(s - m)
            den = jnp.sum(e, axis=0, keepdims=True)
            pvs.append(_dot(v3_t, (e * (1.0 / den)).astype(BF16)))
            lses.append(m + jnp.log(den))
        out_t = jnp.zeros((LANES, DIL_BLOCK), F32)
        lse_t = jnp.zeros((LANES, DIL_BLOCK), F32)
        for h in range(DIL_HEADS):
            head_rows = jnp.logical_and(dim >= h * DIL_HEAD_DIM, dim < (h + 1) * DIL_HEAD_DIM)
            out_t = jnp.where(head_rows, pvs[h], out_t)
            lse_t = jnp.where(head_rows, lses[h], lse_t)
        o_ref[sub * DIL_BLOCK:(sub + 1) * DIL_BLOCK, :] = out_t.T
        lse_ref[sub * DIL_BLOCK:(sub + 1) * DIL_BLOCK, :] = lse_t.T


def _dilated_group(q, k, v, bias_t, g, d, B, S):
    rows = S // d
    qb = min(DIL_Q_BLOCK, rows)
    per = qb // DIL_BLOCK
    last = rows // DIL_BLOCK - 1
    shape = (B, d, rows, LANES)
    q, k, v = q.reshape(shape), k.reshape(shape), v.reshape(shape)
    edge = lambda f: pl.BlockSpec((None, None, DIL_BLOCK, LANES), f)
    main = pl.BlockSpec((None, None, qb, LANES), lambda b, c, i: (b, c, i, 0))
    prev = edge(lambda b, c, i: (b, c, jnp.maximum(i * per - 1, 0), 0))
    nxt = edge(lambda b, c, i: (b, c, jnp.minimum((i + 1) * per, last), 0))
    return pl.pallas_call(
        functools.partial(_dilated_kernel, rows=rows),
        grid=(B, d, rows // qb),
        in_specs=[main, prev, main, nxt, prev, main, nxt, _full((DIL_HEADS, 3 * DIL_BLOCK, DIL_BLOCK))],
        out_specs=[main, main],
        out_shape=[jax.ShapeDtypeStruct(shape, F32)] * 2,
        compiler_params=_cparams("parallel", "parallel", "parallel"),
        name="dilated_attention_g%d" % g,
    )(q, k, k, k, v, v, v, bias_t)


def _t5_bucket(rel):
    nb = REL_BUCKETS // 2
    max_exact = nb // 2
    ret = jnp.where(rel > 0, nb, 0)
    n = jnp.abs(rel)
    nf = jnp.maximum(n, 1).astype(F32)
    large = max_exact + (jnp.log(nf / max_exact) / math.log(REL_MAX_DIST / max_exact) * (nb - max_exact)).astype(jnp.int32)
    large = jnp.minimum(large, nb - 1)
    return ret + jnp.where(n < max_exact, n, large)


def _dilated_bias_tables(rel_bias):
    qi = np.arange(DIL_BLOCK)[None, :]
    kj = np.arange(3 * DIL_BLOCK)[:, None]
    rel = kj - DIL_BLOCK - qi
    in_band = np.abs(rel) <= DIL_SIDE
    idx = np.clip(rel + DIL_SIDE, 0, 2 * DIL_SIDE)
    tables = []
    for g, (_, d) in enumerate(DIL_PAIRS):
        off = d * jnp.arange(-DIL_SIDE, DIL_SIDE + 1, dtype=jnp.int32)
        b = rel_bias[_t5_bucket(off)][:, g * DIL_HEADS:(g + 1) * DIL_HEADS].astype(F32).T
        tables.append(jnp.where(in_band[None], b[:, idx], NEG_BIG))
    return tables


def _merge_kernel(x_ref, fa_ref, sg_ref, oc_ref, o0_ref, l0_ref, o1_ref, l1_ref, o2_ref, l2_ref,
                  wa_ref, wb_ref, wc_ref, wd_ref, wg_ref, bg_ref, wo_ref, g_ref, b_ref, o_ref, nat_ref):
    x = x_ref[...]
    xb = x.astype(BF16)
    tm = x.shape[0]
    for slot, (src, g) in enumerate(((o1_ref, 1), (l1_ref, 1), (o2_ref, 2), (l2_ref, 2))):
        d = DIL_PAIRS[g][1]
        for c in range(d):
            nat_ref[slot, pl.ds(c, tm // d, stride=d), :] = src[c]
    l0, l1, l2 = l0_ref[...], nat_ref[1], nat_ref[3]
    mx = jnp.maximum(jnp.maximum(l0, l1), l2)
    e0, e1, e2 = jnp.exp(l0 - mx), jnp.exp(l1 - mx), jnp.exp(l2 - mx)
    den = e0 + e1 + e2
    od = o0_ref[...] * (e0 / den) + nat_ref[0] * (e1 / den) + nat_ref[2] * (e2 / den)
    n_fixed = 4
    for k1 in range(FOURIER_N1):
        for g in range(F_GROUPS):
            nat_ref[n_fixed + g, pl.ds(k1, tm // FOURIER_N1, stride=FOURIER_N1), :] = (
                fa_ref[k1, :, g * LANES:(g + 1) * LANES])
    fa = jnp.concatenate([nat_ref[n_fixed + g] for g in range(F_GROUPS)], axis=1).astype(BF16)
    branches = ((fa, wa_ref), (sg_ref[...], wb_ref), (oc_ref[...], wc_ref), (od.astype(BF16), wd_ref))
    merged = None
    for i, (act, w_ref) in enumerate(branches):
        cols = slice(i * D_MODEL, (i + 1) * D_MODEL)
        gate = jax.nn.sigmoid(_dot(xb, wg_ref[:, cols]) + bg_ref[:, cols])
        term = gate * _dot(act, w_ref[...])
        merged = term if merged is None else merged + term
    y = DN_ALPHA * x + _dot(merged.astype(BF16), wo_ref[...])
    o_ref[...] = _layer_norm(y, g_ref[...], b_ref[...])


def _merge(x, S, fa, sg, oc, dil, lw):
    T = x.shape[0]
    tm = TOKEN_TILE
    nblk = S // tm
    tok = lambda w: pl.BlockSpec((tm, w), lambda i: (i, 0))
    dil_specs, dil_args = [], []
    for (o, lse), (_, d) in zip(dil, DIL_PAIRS):
        for a in (o, lse):
            if d == 1:
                dil_specs.append(tok(LANES))
                dil_args.append(a.reshape(T, LANES))
            else:
                dil_specs.append(pl.BlockSpec((None, d, tm // d, LANES), lambda i: (i // nblk, 0, i % nblk, 0)))
                dil_args.append(a)
    return pl.pallas_call(
        _merge_kernel,
        grid=(T // tm,),
        in_specs=[tok(D_MODEL),
                  pl.BlockSpec((None, FOURIER_N1, tm // FOURIER_N1, F_WIDTH), lambda i: (i // nblk, 0, i % nblk, 0)),
                  tok(SG_WIDTH), tok(MLA_HEADS * MLA_V)] + dil_specs + [
                  _full((F_WIDTH, D_MODEL)), _full((SG_WIDTH, D_MODEL)), _full((MLA_HEADS * MLA_V, D_MODEL)),
                  _full((DIL_KV_WIDTH, D_MODEL)), _full((D_MODEL, 4 * D_MODEL)), _full((1, 4 * D_MODEL)),
                  _full((D_MODEL, D_MODEL)), _full((1, D_MODEL)), _full((1, D_MODEL))],
        out_specs=tok(D_MODEL),
        out_shape=jax.ShapeDtypeStruct((T, D_MODEL), F32),
        scratch_shapes=[pltpu.VMEM((4 + F_GROUPS, tm, LANES), F32)],
        compiler_params=_cparams("parallel"),
        name="merge",
    )(x, fa, sg, oc, *dil_args, lw["w_a"], lw["w_b"], lw["w_c"], lw["w_d"], lw["w_gate"], lw["b_gate"],
      lw["w_o"], lw["ln1_g"], lw["ln1_b"])


MOE_PAIRS = tuple((a, b) for a in range(MOE_EXPERTS_PER_GROUP) for b in range(a + 1, MOE_EXPERTS_PER_GROUP))
MOE_CLASSES = MOE_GROUPS * len(MOE_PAIRS)
MOE_ROW_TILE = 512
INFO_CLASS, INFO_RANK, INFO_P_LO, INFO_P_HI = 0, 1, 2, 3
MOE_ROW_WIDTH = D_MODEL + LANES


def _route_kernel(x_ref, wr_ref, br_ref, tri_ref, info_ref, counts_ref, run_ref):
    @pl.when(pl.program_id(0) == 0)
    def _():
        run_ref[...] = jnp.zeros(run_ref.shape, F32)

    logits = jnp.dot(x_ref[...], wr_ref[...], preferred_element_type=F32,
                     precision=lax.Precision.HIGHEST) + br_ref[...]
    tm = logits.shape[0]
    lane = lax.broadcasted_iota(jnp.int32, (tm, LANES), 1)
    is_g = lane < MOE_GROUPS
    gl = jnp.where(is_g, logits, NEG_BIG)
    gmax = jnp.max(gl, -1, keepdims=True)
    g_top = jnp.min(jnp.where(gl == gmax, lane, LANES), -1, keepdims=True)
    p_group = 1.0 / jnp.sum(jnp.where(is_g, jnp.exp(gl - gmax), 0.0), -1, keepdims=True)
    base = MOE_GROUPS + g_top * MOE_EXPERTS_PER_GROUP
    in_grp = jnp.logical_and(lane >= base, lane < base + MOE_EXPERTS_PER_GROUP)
    el = jnp.where(in_grp, logits, NEG_BIG)
    v1 = jnp.max(el, -1, keepdims=True)
    i1 = jnp.min(jnp.where(el == v1, lane, LANES), -1, keepdims=True)
    el2 = jnp.where(lane == i1, NEG_BIG, el)
    v2 = jnp.max(el2, -1, keepdims=True)
    i2 = jnp.min(jnp.where(el2 == v2, lane, LANES), -1, keepdims=True)
    e2 = jnp.exp(v2 - v1)
    p1 = p_group / (1.0 + e2)
    p2 = p_group * e2 / (1.0 + e2)
    a = jnp.minimum(i1, i2) - base
    b = jnp.maximum(i1, i2) - base
    pair = jnp.where(a == 0, 0, jnp.where(a == 1, 3, 5)) + (b - a - 1)
    cls = g_top * len(MOE_PAIRS) + pair
    first_is_lo = i1 < i2
    p_lo = jnp.where(first_is_lo, p1, p2)
    p_hi = jnp.where(first_is_lo, p2, p1)
    onehot = lane == cls
    before = _dot(tri_ref[...], onehot.astype(BF16)) + run_ref[...]
    rank = jnp.sum(jnp.where(onehot, before, 0.0), -1, keepdims=True)
    run_ref[...] += jnp.sum(onehot.astype(F32), axis=0, keepdims=True)
    counts_ref[...] = run_ref[...]
    info_ref[...] = jnp.where(lane == INFO_CLASS, cls.astype(F32),
                              jnp.where(lane == INFO_RANK, rank,
                                        jnp.where(lane == INFO_P_LO, p_lo,
                                                  jnp.where(lane == INFO_P_HI, p_hi, 0.0))))


ROW_COPY_UNROLL = 8


def _row_copies(n, make_copy, whole_copy):
    def start(r, carry):
        make_copy(r).start()
        return carry

    lax.fori_loop(0, n, start, 0, unroll=ROW_COPY_UNROLL)
    whole_copy.wait()


def _dispatch_kernel(dest_ref, x_ref, info_ref, init_hbm, rows_hbm, row_ref, sem):
    del init_hbm
    tm = x_ref.shape[0]
    row_ref[:, :D_MODEL] = x_ref[...]
    row_ref[:, D_MODEL:] = info_ref[...]
    _row_copies(tm, lambda r: pltpu.make_async_copy(
        row_ref.at[pl.ds(r, 1)], rows_hbm.at[pl.ds(dest_ref[0, r], 1)], sem),
        pltpu.make_async_copy(row_ref, rows_hbm.at[pl.ds(0, tm)], sem))


def _expert_kernel(ea_ref, eb_ref, nused_ref, rows_ref, wga_ref, wua_ref, wda_ref, wgb_ref, wub_ref, wdb_ref,
                   y_ref):
    del ea_ref, eb_ref
    i = pl.program_id(0)

    @pl.when(i < nused_ref[0])
    def _():
        x = rows_ref[:, :D_MODEL].astype(BF16)
        p_lo = rows_ref[:, D_MODEL + INFO_P_LO:D_MODEL + INFO_P_LO + 1]
        p_hi = rows_ref[:, D_MODEL + INFO_P_HI:D_MODEL + INFO_P_HI + 1]
        ha = jax.nn.silu(_dot(x, wga_ref[...])) * _dot(x, wua_ref[...]) * p_lo
        hb = jax.nn.silu(_dot(x, wgb_ref[...])) * _dot(x, wub_ref[...]) * p_hi
        y_ref[...] = _dot(ha.astype(BF16), wda_ref[...]) + _dot(hb.astype(BF16), wdb_ref[...])

    @pl.when(i >= nused_ref[0])
    def _():
        y_ref[...] = jnp.zeros(y_ref.shape, F32)


def _combine_kernel(dest_ref, x_ref, g_ref, b_ref, y_hbm, o_ref, y_ref, sem):
    tm = x_ref.shape[0]
    _row_copies(tm, lambda r: pltpu.make_async_copy(
        y_hbm.at[pl.ds(dest_ref[0, r], 1)], y_ref.at[pl.ds(r, 1)], sem),
        pltpu.make_async_copy(y_hbm.at[pl.ds(0, tm)], y_ref, sem))
    o_ref[...] = _layer_norm(DN_ALPHA * x_ref[...] + y_ref[...], g_ref[...], b_ref[...])


def _moe(x, lw):
    T = x.shape[0]
    tm = TOKEN_TILE
    rt = MOE_ROW_TILE
    n_row_tiles = T // rt + MOE_CLASSES
    n_rows = n_row_tiles * rt
    tok = lambda w: pl.BlockSpec((tm, w), lambda i: (i, 0))

    tri = jnp.asarray(np.tril(np.ones((tm, tm)), -1), BF16)
    info, counts = pl.pallas_call(
        _route_kernel,
        grid=(T // tm,),
        in_specs=[tok(D_MODEL), _full((D_MODEL, LANES)), _full((1, LANES)), _full((tm, tm))],
        out_specs=[tok(LANES), _full((1, LANES))],
        out_shape=[jax.ShapeDtypeStruct((T, LANES), F32), jax.ShapeDtypeStruct((1, LANES), F32)],
        scratch_shapes=[pltpu.VMEM((1, LANES), F32)],
        compiler_params=_cparams("arbitrary"),
        name="moe_route",
    )(x, lw["w_router"], lw["b_router"], tri)

    cls = info[:, INFO_CLASS].astype(jnp.int32)
    rank = info[:, INFO_RANK].astype(jnp.int32)
    cnt = counts[0, :MOE_CLASSES].astype(jnp.int32)
    padded = (cnt + rt - 1) // rt * rt
    ends = jnp.cumsum(padded)
    dest = ((ends - padded)[cls] + rank).reshape(T // tm, 1, tm)
    tile_cls = jnp.minimum(jnp.searchsorted(ends, jnp.arange(n_row_tiles, dtype=jnp.int32) * rt, side="right"),
                           MOE_CLASSES - 1)
    group, pair = np.divmod(np.arange(MOE_CLASSES), len(MOE_PAIRS))
    lo_hi = np.asarray(MOE_PAIRS)[pair]
    ea = jnp.asarray(group * MOE_EXPERTS_PER_GROUP + lo_hi[:, 0], jnp.int32)[tile_cls]
    eb = jnp.asarray(group * MOE_EXPERTS_PER_GROUP + lo_hi[:, 1], jnp.int32)[tile_cls]
    n_used = (ends[-1:] // rt).astype(jnp.int32)

    dest_spec = pl.BlockSpec((None, 1, tm), lambda i: (i, 0, 0), memory_space=pltpu.SMEM)
    hbm = pl.BlockSpec(memory_space=pl.ANY)
    rows = pl.pallas_call(
        _dispatch_kernel,
        grid=(T // tm,),
        in_specs=[dest_spec, tok(D_MODEL), tok(LANES), hbm],
        out_specs=hbm,
        out_shape=jax.ShapeDtypeStruct((n_rows, MOE_ROW_WIDTH), F32),
        scratch_shapes=[pltpu.VMEM((tm, MOE_ROW_WIDTH), F32), pltpu.SemaphoreType.DMA],
        input_output_aliases={3: 0},
        compiler_params=_cparams("arbitrary"),
        name="moe_dispatch",
    )(dest, x, info, jnp.zeros((n_rows, MOE_ROW_WIDTH), F32))

    w_up = lambda sel: pl.BlockSpec((None, D_MODEL, MOE_FF), lambda i, ea, eb, nu: (sel(ea, eb)[i], 0, 0))
    w_dn = lambda sel: pl.BlockSpec((None, MOE_FF, D_MODEL), lambda i, ea, eb, nu: (sel(ea, eb)[i], 0, 0))
    first = lambda ea, eb: ea
    second = lambda ea, eb: eb
    y = pl.pallas_call(
        _expert_kernel,
        grid_spec=pltpu.PrefetchScalarGridSpec(
            num_scalar_prefetch=3,
            grid=(n_row_tiles,),
            in_specs=[pl.BlockSpec((rt, MOE_ROW_WIDTH), lambda i, ea, eb, nu: (i, 0)),
                      w_up(first), w_up(first), w_dn(first), w_up(second), w_up(second), w_dn(second)],
            out_specs=pl.BlockSpec((rt, D_MODEL), lambda i, ea, eb, nu: (i, 0)),
        ),
        out_shape=jax.ShapeDtypeStruct((n_rows, D_MODEL), F32),
        compiler_params=_cparams("arbitrary"),
        name="moe_experts",
    )(ea, eb, n_used, rows, lw["moe_w_gate"], lw["moe_w_up"], lw["moe_w_down"],
      lw["moe_w_gate"], lw["moe_w_up"], lw["moe_w_down"])

    return pl.pallas_call(
        _combine_kernel,
        grid=(T // tm,),
        in_specs=[dest_spec, tok(D_MODEL), _full((1, D_MODEL)), _full((1, D_MODEL)), hbm],
        out_specs=tok(D_MODEL),
        out_shape=jax.ShapeDtypeStruct((T, D_MODEL), F32),
        scratch_shapes=[pltpu.VMEM((tm, D_MODEL), F32), pltpu.SemaphoreType.DMA],
        compiler_params=_cparams("arbitrary"),
        name="moe_combine",
    )(dest, x, lw["ln2_g"], lw["ln2_b"], y)


def _rope_tables(S):
    half = MLA_ROPE // 2
    inv = ROPE_BASE ** (-jnp.arange(half, dtype=F32) / half)
    ang = jnp.arange(S, dtype=F32)[:, None] * inv[None, :]
    cos, sin = jnp.cos(ang), jnp.sin(ang)
    one = jnp.ones((S, MLA_NOPE), F32)
    zero = jnp.zeros((S, MLA_NOPE), F32)
    zh = jnp.zeros((S, half), F32)
    rc = jnp.concatenate([one, cos, cos], axis=1)
    rsa = jnp.concatenate([zero, -sin, zh], axis=1)
    rsb = jnp.concatenate([zero, zh, sin], axis=1)
    return rc, rsa, rsb


def _prep_layer(l, p):
    row = lambda a: a.reshape(1, -1).astype(F32)
    w_ukv = p["mla_w_ukv"][l].reshape(MLA_KV_RANK, MLA_HEADS, MLA_NOPE + MLA_V)
    pad_rows = ((0, 2 * LANES - MLA_KV_RANK), (0, 0))
    w_k = jnp.pad(w_ukv[:, :, :MLA_NOPE], ((0, 0), (0, 0), (0, LANES - MLA_NOPE))).reshape(MLA_KV_RANK, -1)
    w_v = w_ukv[:, :, MLA_NOPE:].reshape(MLA_KV_RANK, -1)
    w_router = jnp.concatenate([p["moe_w_rg"][l], p["moe_w_re"][l]], axis=1)
    b_router = jnp.concatenate([p["moe_b_rg"][l], p["moe_b_re"][l]])
    npad = LANES - MOE_GROUPS - MOE_EXPERTS
    return {
        "w_in": p["w_in"][l].astype(BF16),
        "sg_ln_g": row(p["sg_ln_g"][l]), "sg_ln_b": row(p["sg_ln_b"][l]),
        "sg_w": p["sg_w"][l].astype(BF16),
        "sg_bias": jnp.repeat(p["sg_b"][l].T, SG_GROUP_DIM, axis=1).astype(F32),
        "q_norm": row(p["mla_q_norm"][l]),
        "kv_norm": jnp.pad(row(p["mla_kv_norm"][l]), ((0, 0), (0, 2 * LANES - MLA_KV_RANK))),
        "w_uq": p["mla_w_uq"][l].astype(BF16),
        "w_k": jnp.pad(w_k, pad_rows).astype(BF16),
        "w_v": jnp.pad(w_v, pad_rows).astype(BF16),
        "w_a": p["w_branch_a"][l].astype(BF16), "w_b": p["w_branch_b"][l].astype(BF16),
        "w_c": p["w_branch_c"][l].astype(BF16), "w_d": p["w_branch_d"][l].astype(BF16),
        "w_gate": p["w_gate"][l].astype(BF16), "b_gate": row(p["b_gate"][l]),
        "w_o": p["w_o"][l].astype(BF16),
        "ln1_g": row(p["ln1_g"][l]), "ln1_b": row(p["ln1_b"][l]),
        "w_router": jnp.pad(w_router, ((0, 0), (0, npad))).astype(F32),
        "b_router": jnp.pad(b_router, (0, npad)).reshape(1, -1).astype(F32),
        "moe_w_gate": p["moe_w_gate"][l].astype(BF16), "moe_w_up": p["moe_w_up"][l].astype(BF16),
        "moe_w_down": p["moe_w_down"][l].astype(BF16),
        "ln2_g": row(p["ln2_g"][l]), "ln2_b": row(p["ln2_b"][l]),
    }


def _trunk(x, p, layers, bias_tables):
    B, S, _ = x.shape
    rope = _rope_tables(S)
    fconsts = _fourier_consts(S)
    h = _input_layer_norm(x.reshape(B * S, D_MODEL), p["ln_in_g"], p["ln_in_b"])
    for lw in layers:
        za, sg, q, k, vt, *dil_in = _in_proj(h, B, S, lw, rope)
        fa = _fourier_mix(za, B, S, fconsts)
        oc = _latent_attention(q, k, vt, B, S)
        dil = [_dilated_group(*dil_in[3 * g:3 * g + 3], bias_tables[g], g, d, B, S)
               for g, (_, d) in enumerate(DIL_PAIRS)]
        h = _merge(h, S, fa, sg, oc, dil, lw)
        h = _moe(h, lw)
    return h.reshape(B, S, D_MODEL)


def kernel(x_prompt, x_sample, ln_in_g, ln_in_b, rel_bias, w_in, sg_ln_g, sg_ln_b, sg_w, sg_b, mla_q_norm, mla_kv_norm, mla_w_uq, mla_w_ukv, w_branch_a, w_branch_b, w_branch_c, w_branch_d, w_gate, b_gate, w_o, ln1_g, ln1_b, moe_w_rg, moe_b_rg, moe_w_re, moe_b_re, moe_w_gate, moe_w_up, moe_w_down, ln2_g, ln2_b):
    p = dict(ln_in_g=ln_in_g, ln_in_b=ln_in_b, w_in=w_in, sg_ln_g=sg_ln_g, sg_ln_b=sg_ln_b, sg_w=sg_w, sg_b=sg_b,
             mla_q_norm=mla_q_norm, mla_kv_norm=mla_kv_norm, mla_w_uq=mla_w_uq, mla_w_ukv=mla_w_ukv,
             w_branch_a=w_branch_a, w_branch_b=w_branch_b, w_branch_c=w_branch_c, w_branch_d=w_branch_d,
             w_gate=w_gate, b_gate=b_gate, w_o=w_o, ln1_g=ln1_g, ln1_b=ln1_b,
             moe_w_rg=moe_w_rg, moe_b_rg=moe_b_rg, moe_w_re=moe_w_re, moe_b_re=moe_b_re,
             moe_w_gate=moe_w_gate, moe_w_up=moe_w_up, moe_w_down=moe_w_down, ln2_g=ln2_g, ln2_b=ln2_b)
    layers = [_prep_layer(l, p) for l in range(w_in.shape[0])]
    bias_tables = _dilated_bias_tables(rel_bias)
    return _trunk(x_prompt, p, layers, bias_tables), _trunk(x_sample, p, layers, bias_tables)
```

```python
import functools
import math

import numpy as np
import jax
import jax.numpy as jnp
from jax import lax
from jax.experimental import pallas as pl
from jax.experimental.pallas import tpu as pltpu

F32 = jnp.float32
BF16 = jnp.bfloat16

D_MODEL = 1024
DEPTH = 4
F_GROUPS = 4
F_GROUP_DIM = 128
F_WIDTH = F_GROUPS * F_GROUP_DIM
SG_CHUNK = 128
SG_GROUPS = 4
SG_GROUP_DIM = 64
SG_WIDTH = SG_GROUPS * SG_GROUP_DIM
MLA_HEADS = 8
MLA_Q_RANK = 256
MLA_KV_RANK = 192
MLA_NOPE = 64
MLA_ROPE = 64
MLA_V = 64
MLA_QK_DIM = MLA_NOPE + MLA_ROPE
ROPE_BASE = 10000.0
DIL_PAIRS = ((128, 1), (512, 4), (2048, 16))
DIL_GROUPS = 3
DIL_HEADS = 4
DIL_HEAD_DIM = 32
DIL_Q_WIDTH = DIL_GROUPS * DIL_HEADS * DIL_HEAD_DIM
DIL_KV_WIDTH = DIL_HEADS * DIL_HEAD_DIM
DIL_SIDE = 64
REL_BUCKETS = 32
REL_MAX_DIST = 1024
MIX_WIDTH = F_WIDTH + 2 * SG_WIDTH + MLA_Q_RANK + MLA_KV_RANK + MLA_ROPE + DIL_Q_WIDTH + 2 * DIL_KV_WIDTH
MOE_GROUPS = 4
MOE_EXPERTS_PER_GROUP = 4
MOE_EXPERTS = MOE_GROUPS * MOE_EXPERTS_PER_GROUP
MOE_FF = 512
DN_ALPHA = (2 * DEPTH) ** 0.25
LN_EPS = 1e-5
RMS_EPS = 1e-6

OFF_A = 0
OFF_B = OFF_A + F_WIDTH
OFF_CQ = OFF_B + 2 * SG_WIDTH
OFF_CKV = OFF_CQ + MLA_Q_RANK
OFF_DQ = OFF_CKV + MLA_KV_RANK + MLA_ROPE
OFF_DK = OFF_DQ + DIL_Q_WIDTH
OFF_DV = OFF_DK + DIL_KV_WIDTH

LANES = 128
SUBLANES = 8
VMEM_LIMIT_BYTES = 56 * 1024 * 1024
TOKEN_TILE = 512
ATTN_Q_BLOCK = 512
ATTN_Q_SUB = 256
ATTN_KV_BLOCK = 2048
ATTN_CHUNK = 256
ATTN_CHUNKS_PER_ITER = 4
ATTN_PIPELINE_DEPTH = 6
ATTN_V_ROWS = 80
DIL_BLOCK = 128
DIL_Q_BLOCK = 512
NEG_BIG = -1e30
LOG2_E = math.log2(math.e)


def _cparams(*sem):
    return pltpu.CompilerParams(dimension_semantics=sem, vmem_limit_bytes=VMEM_LIMIT_BYTES)


def _full(shape):
    n = len(shape)
    return pl.BlockSpec(shape, lambda *_: (0,) * n)


def _layer_norm(x, g, b):
    mu = jnp.mean(x, -1, keepdims=True)
    xc = x - mu
    var = jnp.mean(xc * xc, -1, keepdims=True)
    return xc * lax.rsqrt(var + LN_EPS) * g + b


def _dot(a, b):
    return jnp.dot(a, b, preferred_element_type=F32)


def _ln_kernel(x_ref, g_ref, b_ref, o_ref):
    o_ref[...] = _layer_norm(x_ref[...], g_ref[...], b_ref[...])


def _input_layer_norm(x, g, b):
    T = x.shape[0]
    tm = TOKEN_TILE
    return pl.pallas_call(
        _ln_kernel,
        grid=(T // tm,),
        in_specs=[pl.BlockSpec((tm, D_MODEL), lambda i: (i, 0)), _full((1, D_MODEL)), _full((1, D_MODEL))],
        out_specs=pl.BlockSpec((tm, D_MODEL), lambda i: (i, 0)),
        out_shape=jax.ShapeDtypeStruct((T, D_MODEL), F32),
        compiler_params=_cparams("parallel"),
        name="input_layer_norm",
    )(x, g.reshape(1, -1), b.reshape(1, -1))


def _rope_lanes(t, c, sa, sb):
    return t * c + pltpu.roll(t, 96, 1) * sa + pltpu.roll(t, 32, 1) * sb


def _in_proj_kernel(x_ref, w_in_ref, sg_g_ref, sg_b_ref, sg_w_ref, sg_bias_ref, qn_ref, kvn_ref,
                    wuq_ref, wk_ref, wv_ref, rc_ref, rsa_ref, rsb_ref,
                    za_ref, sg_ref, qt_ref, k_ref, vt_ref,
                    dq0_ref, dk0_ref, dv0_ref, dq1_ref, dk1_ref, dv1_ref, dq2_ref, dk2_ref, dv2_ref, dil_ref):
    tm = x_ref.shape[0]
    z = _dot(x_ref[...].astype(BF16), w_in_ref[...])

    za_ref[...] = z[:, OFF_A:OFF_A + F_WIDTH]

    zb = jax.nn.gelu(z[:, OFF_B:OFF_B + 2 * SG_WIDTH])
    u = zb[:, :SG_WIDTH]
    vn = _layer_norm(zb[:, SG_WIDTH:], sg_g_ref[...], sg_b_ref[...]).astype(BF16)
    lane = lax.broadcasted_iota(jnp.int32, (SG_CHUNK, LANES), 1)
    low_half = lane < SG_GROUP_DIM
    for ci in range(tm // SG_CHUNK):
        rows = slice(ci * SG_CHUNK, (ci + 1) * SG_CHUNK)
        for j in range(SG_WIDTH // LANES):
            cols = slice(j * LANES, (j + 1) * LANES)
            vblk = vn[rows, cols]
            mixed = jnp.where(low_half, _dot(sg_w_ref[2 * j], vblk), _dot(sg_w_ref[2 * j + 1], vblk))
            sg_ref[rows, cols] = (u[rows, cols] * (mixed + sg_bias_ref[:, cols])).astype(sg_ref.dtype)

    rc, rsa, rsb = rc_ref[...], rsa_ref[...], rsb_ref[...]

    cq = z[:, OFF_CQ:OFF_CQ + MLA_Q_RANK]
    cq = cq * lax.rsqrt(jnp.mean(cq * cq, -1, keepdims=True) + RMS_EPS) * qn_ref[...]
    q = _dot(cq.astype(BF16), wuq_ref[...]) * (MLA_QK_DIM ** -0.5 * LOG2_E)
    ra = pltpu.roll(q, MLA_HEADS * LANES - 32, 1)
    rb = pltpu.roll(q, 32, 1)
    q_t = jnp.concatenate(
        [q[:, h * LANES:(h + 1) * LANES] * rc + ra[:, h * LANES:(h + 1) * LANES] * rsa
         + rb[:, h * LANES:(h + 1) * LANES] * rsb for h in range(MLA_HEADS)], axis=1).T
    for j in range(tm // ATTN_Q_SUB):
        qt_ref[j] = q_t[:, j * ATTN_Q_SUB:(j + 1) * ATTN_Q_SUB].astype(qt_ref.dtype)

    slab = z[:, OFF_CKV:OFF_CKV + 2 * LANES]
    lane2 = lax.broadcasted_iota(jnp.int32, (tm, 2 * LANES), 1)
    ckv_sq = jnp.where(lane2 < MLA_KV_RANK, slab * slab, 0.0)
    ms = jnp.sum(ckv_sq, -1, keepdims=True) * (1.0 / MLA_KV_RANK)
    ckv = (slab * lax.rsqrt(ms + RMS_EPS) * kvn_ref[...]).astype(BF16)
    k_nope = _dot(ckv, wk_ref[...])
    v_t = _dot(ckv, wv_ref[...]).T.astype(vt_ref.dtype)
    ones = jnp.ones((ATTN_V_ROWS - MLA_V, ATTN_CHUNK), vt_ref.dtype)
    for j in range(tm // ATTN_CHUNK):
        cols = slice(j * ATTN_CHUNK, (j + 1) * ATTN_CHUNK)
        for h in range(MLA_HEADS):
            vt_ref[j, h * ATTN_V_ROWS:h * ATTN_V_ROWS + MLA_V, :] = v_t[h * MLA_V:(h + 1) * MLA_V, cols]
            vt_ref[j, h * ATTN_V_ROWS + MLA_V:(h + 1) * ATTN_V_ROWS, :] = ones
    kr_slab = _rope_lanes(slab[:, LANES:], rc, rsa, rsb)
    kr_slab = jnp.where(lax.broadcasted_iota(jnp.int32, (tm, LANES), 1) >= MLA_NOPE, kr_slab, 0.0)
    for h in range(MLA_HEADS):
        cols = slice(h * LANES, (h + 1) * LANES)
        k_ref[:, cols] = (k_nope[:, cols] + kr_slab).astype(k_ref.dtype)

    for slab in range(DIL_GROUPS + 2):
        t = z[:, OFF_DQ + slab * LANES:OFF_DQ + (slab + 1) * LANES]
        dil_ref[slab] = t * (DIL_HEAD_DIM ** -0.5 * LOG2_E) if slab < DIL_GROUPS else t
    dq0_ref[...] = dil_ref[0].astype(dq0_ref.dtype)
    dk0_ref[...] = dil_ref[DIL_GROUPS].astype(dk0_ref.dtype)
    dv0_ref[...] = dil_ref[DIL_GROUPS + 1].astype(dv0_ref.dtype)
    for g, (qr, kr, vr) in ((1, (dq1_ref, dk1_ref, dv1_ref)), (2, (dq2_ref, dk2_ref, dv2_ref))):
        d = DIL_PAIRS[g][1]
        for c in range(d):
            rows = pl.ds(c, tm // d, stride=d)
            qr[c] = dil_ref[g, rows, :].astype(qr.dtype)
            kr[c] = dil_ref[DIL_GROUPS, rows, :].astype(kr.dtype)
            vr[c] = dil_ref[DIL_GROUPS + 1, rows, :].astype(vr.dtype)


def _in_proj(x, B, S, lw, rope):
    T = x.shape[0]
    tm = TOKEN_TILE
    nblk = S // tm
    tok = lambda w: pl.BlockSpec((tm, w), lambda i: (i, 0))
    pos = pl.BlockSpec((tm, LANES), lambda i: (i % nblk, 0))
    out_specs = [tok(F_WIDTH), tok(SG_WIDTH)]
    out_shape = [jax.ShapeDtypeStruct((T, F_WIDTH), F32), jax.ShapeDtypeStruct((T, SG_WIDTH), BF16)]
    out_specs.append(pl.BlockSpec((tm // ATTN_Q_SUB, MLA_HEADS * LANES, ATTN_Q_SUB), lambda i: (i, 0, 0)))
    out_shape.append(jax.ShapeDtypeStruct((T // ATTN_Q_SUB, MLA_HEADS * LANES, ATTN_Q_SUB), BF16))
    out_specs.append(tok(MLA_HEADS * LANES))
    out_shape.append(jax.ShapeDtypeStruct((T, MLA_HEADS * LANES), BF16))
    out_specs.append(pl.BlockSpec((tm // ATTN_CHUNK, MLA_HEADS * ATTN_V_ROWS, ATTN_CHUNK), lambda i: (i, 0, 0)))
    out_shape.append(jax.ShapeDtypeStruct((T // ATTN_CHUNK, MLA_HEADS * ATTN_V_ROWS, ATTN_CHUNK), BF16))
    for _, d in DIL_PAIRS:
        for _ in range(3):
            if d == 1:
                out_specs.append(tok(LANES))
                out_shape.append(jax.ShapeDtypeStruct((T, LANES), BF16))
            else:
                out_specs.append(pl.BlockSpec((None, d, tm // d, LANES), lambda i: (i // nblk, 0, i % nblk, 0)))
                out_shape.append(jax.ShapeDtypeStruct((B, d, S // d, LANES), BF16))
    return pl.pallas_call(
        _in_proj_kernel,
        grid=(T // tm,),
        in_specs=[tok(D_MODEL), _full((D_MODEL, MIX_WIDTH)), _full((1, SG_WIDTH)), _full((1, SG_WIDTH)),
                  _full((SG_GROUPS, SG_CHUNK, SG_CHUNK)), _full((SG_CHUNK, SG_WIDTH)),
                  _full((1, MLA_Q_RANK)), _full((1, 2 * LANES)),
                  _full((MLA_Q_RANK, MLA_HEADS * LANES)), _full((2 * LANES, MLA_HEADS * LANES)),
                  _full((2 * LANES, MLA_HEADS * MLA_V)), pos, pos, pos],
        out_specs=out_specs,
        out_shape=out_shape,
        scratch_shapes=[pltpu.VMEM((DIL_GROUPS + 2, tm, LANES), F32)],
        compiler_params=_cparams("parallel"),
        name="in_proj",
    )(x, lw["w_in"], lw["sg_ln_g"], lw["sg_ln_b"], lw["sg_w"], lw["sg_bias"], lw["q_norm"], lw["kv_norm"],
      lw["w_uq"], lw["w_k"], lw["w_v"], *rope)


FOURIER_N1 = 16
FOURIER_ROWS = 16
FOURIER_K1_PER_STEP = 2


def _fourier1_kernel(x_ref, cs_ref, k1_ref, ct_ref, st_ref, gr_ref, gi_ref):
    n1, r, _ = x_ref.shape
    rows = n1 * r
    x = x_ref[...].reshape(rows, F_WIDTH)
    ct, st = ct_ref[...], st_ref[...]
    outs_r, outs_i = [], []
    for g in range(F_GROUPS):
        ab = _dot(x[:, g * LANES:(g + 1) * LANES].astype(BF16), cs_ref[...])
        stacked = jnp.concatenate([ab[:, :LANES], ab[:, LANES:]], axis=0).astype(BF16)
        g2 = _dot(k1_ref[...], stacked)
        gr, gi = g2[:rows], g2[rows:]
        outs_r.append(gr * ct - gi * st)
        outs_i.append(gr * st + gi * ct)
    gr_ref[...] = jnp.concatenate(outs_r, axis=1).reshape(n1, r, F_WIDTH)
    gi_ref[...] = jnp.concatenate(outs_i, axis=1).reshape(n1, r, F_WIDTH)


def _fourier2_kernel(gr_ref, gi_ref, w2_ref, o_ref):
    for j in range(gr_ref.shape[0]):
        stacked = jnp.concatenate([gr_ref[j], gi_ref[j]], axis=0).astype(BF16)
        o_ref[j] = _dot(w2_ref[...], stacked)


def _fourier_mix(za, B, S, consts):
    n1 = FOURIER_N1
    n2 = S // n1
    r = FOURIER_ROWS
    kk = FOURIER_K1_PER_STEP
    cs, k1, ct, st, w2 = consts
    x = za.reshape(B, n1, n2, F_WIDTH)
    blk1 = pl.BlockSpec((None, n1, r, F_WIDTH), lambda b, j: (b, 0, j, 0))
    twid = pl.BlockSpec((None, n1 * r, LANES), lambda b, j: (j, 0, 0))
    gr, gi = pl.pallas_call(
        _fourier1_kernel,
        grid=(B, n2 // r),
        in_specs=[blk1, _full((LANES, 2 * LANES)), _full((2 * n1 * r, 2 * n1 * r)), twid, twid],
        out_specs=[blk1, blk1],
        out_shape=[jax.ShapeDtypeStruct(x.shape, F32)] * 2,
        compiler_params=_cparams("parallel", "parallel"),
        name="fourier_stage1",
    )(x, cs, k1, ct, st)
    blk2 = pl.BlockSpec((None, kk, n2, F_WIDTH), lambda b, j: (b, j, 0, 0))
    return pl.pallas_call(
        _fourier2_kernel,
        grid=(B, n1 // kk),
        in_specs=[blk2, blk2, _full((n2, 2 * n2))],
        out_specs=blk2,
        out_shape=jax.ShapeDtypeStruct((B, n1, n2, F_WIDTH), F32),
        compiler_params=_cparams("parallel", "parallel"),
        name="fourier_stage2",
    )(gr, gi, w2)


def _fourier_consts(S):
    n1 = FOURIER_N1
    n2 = S // n1
    r = FOURIER_ROWS
    c = np.arange(F_GROUP_DIM)
    ang_c = 2.0 * np.pi * np.outer(c, c) / F_GROUP_DIM
    norm = 1.0 / math.sqrt(S * F_GROUP_DIM)
    cs = np.concatenate([np.cos(ang_c), np.sin(ang_c)], axis=1) * norm
    eye = np.eye(r)
    a1 = np.arange(n1)
    ang1 = 2.0 * np.pi * np.outer(a1, a1) / n1
    c1, s1 = np.kron(np.cos(ang1), eye), np.kron(np.sin(ang1), eye)
    k1 = np.block([[c1, -s1], [s1, c1]])
    a2 = np.arange(n2)
    ang_t = 2.0 * np.pi * np.outer(a1, a2) / S

    def twiddle(t):
        t = t.reshape(n1, n2 // r, r).transpose(1, 0, 2).reshape(n2 // r, n1 * r)
        return np.broadcast_to(t[:, :, None], (n2 // r, n1 * r, LANES))

    ang2 = 2.0 * np.pi * np.outer(a2, a2) / n2
    w2 = np.concatenate([np.cos(ang2), -np.sin(ang2)], axis=1)
    return (jnp.asarray(cs, BF16), jnp.asarray(k1, BF16), jnp.asarray(twiddle(np.cos(ang_t)), F32),
            jnp.asarray(twiddle(np.sin(ang_t)), F32), jnp.asarray(w2, BF16))


def _flash_kernel(qt_ref, k_ref, vt_ref, o_ref, m_ref, l_ref, acc_ref):
    ki = pl.program_id(2)
    nsub = qt_ref.shape[0]
    nchunk = k_ref.shape[0] // ATTN_CHUNK

    @pl.when(ki == 0)
    def _():
        m_ref[...] = jnp.full(m_ref.shape, NEG_BIG, F32)
        l_ref[...] = jnp.zeros(l_ref.shape, F32)
        acc_ref[...] = jnp.zeros(acc_ref.shape, F32)

    per_chunk = MLA_HEADS * nsub
    n_tiles = ATTN_CHUNKS_PER_ITER * per_chunk

    def tile_index(it, t):
        cc, rem = divmod(t, per_chunk)
        h, qs = divmod(rem, nsub)
        return it * ATTN_CHUNKS_PER_ITER + cc, h, qs

    def scores(it, t):
        c, h, qs = tile_index(it, t)
        r0 = pl.multiple_of(c * ATTN_CHUNK, ATTN_CHUNK)
        kc = k_ref[pl.ds(r0, ATTN_CHUNK), h * LANES:(h + 1) * LANES]
        return _dot(kc, qt_ref[qs, h * LANES:(h + 1) * LANES, :])

    def accumulate(it, t, s):
        c, h, qs = tile_index(it, t)
        rows = slice(h * MLA_V, (h + 1) * MLA_V)
        m_prev = m_ref[qs, h]
        m_new = jnp.maximum(m_prev, jnp.max(s, axis=0, keepdims=True))
        alpha = jnp.exp2(m_prev - m_new)
        p = jnp.exp2(s - m_new[0:1, :]).astype(BF16)
        m_ref[qs, h] = m_new
        pv = _dot(vt_ref[c, h * ATTN_V_ROWS:(h + 1) * ATTN_V_ROWS, :], p)
        l_ref[qs, h] = alpha * l_ref[qs, h] + pv[MLA_V:MLA_V + SUBLANES, :]
        acc_ref[qs, rows, :] = alpha[0:1, :] * acc_ref[qs, rows, :] + pv[:MLA_V, :]

    def iteration(it, carry):
        pending = {}
        for t in range(n_tiles + ATTN_PIPELINE_DEPTH):
            if t < n_tiles:
                pending[t] = scores(it, t)
            if t >= ATTN_PIPELINE_DEPTH:
                accumulate(it, t - ATTN_PIPELINE_DEPTH, pending.pop(t - ATTN_PIPELINE_DEPTH))
        return carry

    lax.fori_loop(0, nchunk // ATTN_CHUNKS_PER_ITER, iteration, 0)

    @pl.when(ki == pl.num_programs(2) - 1)
    def _():
        for qs in range(nsub):
            out_t = jnp.concatenate(
                [acc_ref[qs, h * MLA_V:(h + 1) * MLA_V, :] * (1.0 / l_ref[qs, h, 0:1, :])
                 for h in range(MLA_HEADS)], axis=0)
            o_ref[qs * ATTN_Q_SUB:(qs + 1) * ATTN_Q_SUB, :] = out_t.T.astype(o_ref.dtype)


def _latent_attention(qt, k, vt, B, S):
    tq = min(ATTN_Q_BLOCK, S)
    tk = min(ATTN_KV_BLOCK, S)
    nsub = tq // ATTN_Q_SUB
    qt = qt.reshape(B, S // ATTN_Q_SUB, MLA_HEADS * LANES, ATTN_Q_SUB)
    k = k.reshape(B, S, MLA_HEADS * LANES)
    vt = vt.reshape(B, S // ATTN_CHUNK, MLA_HEADS * ATTN_V_ROWS, ATTN_CHUNK)
    out = pl.pallas_call(
        _flash_kernel,
        grid=(B, S // tq, S // tk),
        in_specs=[pl.BlockSpec((None, nsub, MLA_HEADS * LANES, ATTN_Q_SUB), lambda b, i, j: (b, i, 0, 0)),
                  pl.BlockSpec((None, tk, MLA_HEADS * LANES), lambda b, i, j: (b, j, 0)),
                  pl.BlockSpec((None, tk // ATTN_CHUNK, MLA_HEADS * ATTN_V_ROWS, ATTN_CHUNK),
                               lambda b, i, j: (b, j, 0, 0))],
        out_specs=pl.BlockSpec((None, tq, MLA_HEADS * MLA_V), lambda b, i, j: (b, i, 0)),
        out_shape=jax.ShapeDtypeStruct((B, S, MLA_HEADS * MLA_V), BF16),
        scratch_shapes=[pltpu.VMEM((nsub, MLA_HEADS, SUBLANES, ATTN_Q_SUB), F32),
                        pltpu.VMEM((nsub, MLA_HEADS, SUBLANES, ATTN_Q_SUB), F32),
                        pltpu.VMEM((nsub, MLA_HEADS * MLA_V, ATTN_Q_SUB), F32)],
        compiler_params=_cparams("parallel", "parallel", "arbitrary"),
        name="latent_attention",
    )(qt, k, vt)
    return out.reshape(B * S, MLA_HEADS * MLA_V)


def _dilated_kernel(q_ref, kp_ref, kc_ref, kn_ref, vp_ref, vc_ref, vn_ref, bias_ref, o_ref, lse_ref, *, rows):
    i = pl.program_id(2)
    qb = q_ref.shape[0]
    k_all = jnp.concatenate([kp_ref[...], kc_ref[...], kn_ref[...]], axis=0)
    v_all = jnp.concatenate([vp_ref[...], vc_ref[...], vn_ref[...]], axis=0)
    v_all_t = v_all.astype(F32).T.astype(BF16)
    row = lax.broadcasted_iota(jnp.int32, (3 * DIL_BLOCK, DIL_BLOCK), 0)
    lane = lax.broadcasted_iota(jnp.int32, (1, LANES), 1)
    dim = lax.broadcasted_iota(jnp.int32, (LANES, 1), 0)
    n_sub = qb // DIL_BLOCK
    scores = {}
    for sub in range(n_sub):
        q = q_ref[sub * DIL_BLOCK:(sub + 1) * DIL_BLOCK, :]
        k3 = k_all[sub * DIL_BLOCK:(sub + 3) * DIL_BLOCK, :]
        for h in range(DIL_HEADS):
            head_lanes = jnp.logical_and(lane >= h * DIL_HEAD_DIM, lane < (h + 1) * DIL_HEAD_DIM)
            qh = jnp.where(head_lanes, q, jnp.zeros_like(q))
            scores[sub, h] = lax.dot_general(k3, qh, (((1,), (1,)), ((), ())), preferred_element_type=F32)
    for sub in range(n_sub):
        inside = 1 <= sub <= n_sub - 2
        key_row = i * qb + (sub - 1) * DIL_BLOCK + row
        in_seq = jnp.logical_and(key_row >= 0, key_row < rows)
        v3_t = v_all_t[:, sub * DIL_BLOCK:(sub + 3) * DIL_BLOCK]
        pvs, lses = [], []
        for h in range(DIL_HEADS):
            s = scores.pop((sub, h)) + bias_ref[h]
            s = s if inside else jnp.where(in_seq, s, NEG_BIG)
            m = jnp.max(s, axis=0, keepdims=True)
            e = jnp.exp2(s - m)
            den = jnp.sum(e, axis=0, keepdims=True)
            pvs.append(_dot(v3_t, e.astype(BF16)) * (1.0 / den))
            lses.append((m + jnp.log2(den)) * (1.0 / LOG2_E))
        out_t = jnp.zeros((LANES, DIL_BLOCK), F32)
        lse_t = jnp.zeros((LANES, DIL_BLOCK), F32)
        for h in range(DIL_HEADS):
            head_rows = jnp.logical_and(dim >= h * DIL_HEAD_DIM, dim < (h + 1) * DIL_HEAD_DIM)
            out_t = jnp.where(head_rows, pvs[h], out_t)
            lse_t = jnp.where(head_rows, lses[h], lse_t)
        o_ref[sub * DIL_BLOCK:(sub + 1) * DIL_BLOCK, :] = out_t.T
        lse_ref[sub * DIL_BLOCK:(sub + 1) * DIL_BLOCK, :] = lse_t.T


def _dilated_group(q, k, v, bias_t, g, d, B, S):
    rows = S // d
    qb = min(DIL_Q_BLOCK, rows)
    per = qb // DIL_BLOCK
    last = rows // DIL_BLOCK - 1
    shape = (B, d, rows, LANES)
    q, k, v = q.reshape(shape), k.reshape(shape), v.reshape(shape)
    edge = lambda f: pl.BlockSpec((None, None, DIL_BLOCK, LANES), f)
    main = pl.BlockSpec((None, None, qb, LANES), lambda b, c, i: (b, c, i, 0))
    prev = edge(lambda b, c, i: (b, c, jnp.maximum(i * per - 1, 0), 0))
    nxt = edge(lambda b, c, i: (b, c, jnp.minimum((i + 1) * per, last), 0))
    return pl.pallas_call(
        functools.partial(_dilated_kernel, rows=rows),
        grid=(B, d, rows // qb),
        in_specs=[main, prev, main, nxt, prev, main, nxt, _full((DIL_HEADS, 3 * DIL_BLOCK, DIL_BLOCK))],
        out_specs=[main, main],
        out_shape=[jax.ShapeDtypeStruct(shape, F32)] * 2,
        compiler_params=_cparams("parallel", "parallel", "parallel"),
        name="dilated_attention_g%d" % g,
    )(q, k, k, k, v, v, v, bias_t)


def _t5_bucket(rel):
    nb = REL_BUCKETS // 2
    max_exact = nb // 2
    ret = jnp.where(rel > 0, nb, 0)
    n = jnp.abs(rel)
    nf = jnp.maximum(n, 1).astype(F32)
    large = max_exact + (jnp.log(nf / max_exact) / math.log(REL_MAX_DIST / max_exact) * (nb - max_exact)).astype(jnp.int32)
    large = jnp.minimum(large, nb - 1)
    return ret + jnp.where(n < max_exact, n, large)


def _dilated_bias_tables(rel_bias):
    qi = np.arange(DIL_BLOCK)[None, :]
    kj = np.arange(3 * DIL_BLOCK)[:, None]
    rel = kj - DIL_BLOCK - qi
    in_band = np.abs(rel) <= DIL_SIDE
    idx = np.clip(rel + DIL_SIDE, 0, 2 * DIL_SIDE)
    tables = []
    for g, (_, d) in enumerate(DIL_PAIRS):
        off = d * jnp.arange(-DIL_SIDE, DIL_SIDE + 1, dtype=jnp.int32)
        b = rel_bias[_t5_bucket(off)][:, g * DIL_HEADS:(g + 1) * DIL_HEADS].astype(F32).T
        tables.append(jnp.where(in_band[None], b[:, idx] * LOG2_E, NEG_BIG))
    return tables


def _merge_kernel(x_ref, fa_ref, sg_ref, oc_ref, o0_ref, l0_ref, o1_ref, l1_ref, o2_ref, l2_ref,
                  wa_ref, wb_ref, wc_ref, wd_ref, wg_ref, bg_ref, wo_ref, g_ref, b_ref, o_ref, nat_ref):
    x = x_ref[...]
    xb = x.astype(BF16)
    tm = x.shape[0]
    for slot, (src, g) in enumerate(((o1_ref, 1), (l1_ref, 1), (o2_ref, 2), (l2_ref, 2))):
        d = DIL_PAIRS[g][1]
        for c in range(d):
            nat_ref[slot, pl.ds(c, tm // d, stride=d), :] = src[c]
    l0, l1, l2 = l0_ref[...], nat_ref[1], nat_ref[3]
    mx = jnp.maximum(jnp.maximum(l0, l1), l2)
    e0, e1, e2 = jnp.exp(l0 - mx), jnp.exp(l1 - mx), jnp.exp(l2 - mx)
    den = e0 + e1 + e2
    od = o0_ref[...] * (e0 / den) + nat_ref[0] * (e1 / den) + nat_ref[2] * (e2 / den)
    n_fixed = 4
    for k1 in range(FOURIER_N1):
        for g in range(F_GROUPS):
            nat_ref[n_fixed + g, pl.ds(k1, tm // FOURIER_N1, stride=FOURIER_N1), :] = (
                fa_ref[k1, :, g * LANES:(g + 1) * LANES])
    fa = jnp.concatenate([nat_ref[n_fixed + g] for g in range(F_GROUPS)], axis=1).astype(BF16)
    branches = ((fa, wa_ref), (sg_ref[...], wb_ref), (oc_ref[...], wc_ref), (od.astype(BF16), wd_ref))
    merged = None
    for i, (act, w_ref) in enumerate(branches):
        cols = slice(i * D_MODEL, (i + 1) * D_MODEL)
        gate = jax.nn.sigmoid(_dot(xb, wg_ref[:, cols]) + bg_ref[:, cols])
        term = gate * _dot(act, w_ref[...])
        merged = term if merged is None else merged + term
    y = DN_ALPHA * x + _dot(merged.astype(BF16), wo_ref[...])
    o_ref[...] = _layer_norm(y, g_ref[...], b_ref[...])


def _merge(x, S, fa, sg, oc, dil, lw):
    T = x.shape[0]
    tm = TOKEN_TILE
    nblk = S // tm
    tok = lambda w: pl.BlockSpec((tm, w), lambda i: (i, 0))
    dil_specs, dil_args = [], []
    for (o, lse), (_, d) in zip(dil, DIL_PAIRS):
        for a in (o, lse):
            if d == 1:
                dil_specs.append(tok(LANES))
                dil_args.append(a.reshape(T, LANES))
            else:
                dil_specs.append(pl.BlockSpec((None, d, tm // d, LANES), lambda i: (i // nblk, 0, i % nblk, 0)))
                dil_args.append(a)
    return pl.pallas_call(
        _merge_kernel,
        grid=(T // tm,),
        in_specs=[tok(D_MODEL),
                  pl.BlockSpec((None, FOURIER_N1, tm // FOURIER_N1, F_WIDTH), lambda i: (i // nblk, 0, i % nblk, 0)),
                  tok(SG_WIDTH), tok(MLA_HEADS * MLA_V)] + dil_specs + [
                  _full((F_WIDTH, D_MODEL)), _full((SG_WIDTH, D_MODEL)), _full((MLA_HEADS * MLA_V, D_MODEL)),
                  _full((DIL_KV_WIDTH, D_MODEL)), _full((D_MODEL, 4 * D_MODEL)), _full((1, 4 * D_MODEL)),
                  _full((D_MODEL, D_MODEL)), _full((1, D_MODEL)), _full((1, D_MODEL))],
        out_specs=tok(D_MODEL),
        out_shape=jax.ShapeDtypeStruct((T, D_MODEL), F32),
        scratch_shapes=[pltpu.VMEM((4 + F_GROUPS, tm, LANES), F32)],
        compiler_params=_cparams("parallel"),
        name="merge",
    )(x, fa, sg, oc, *dil_args, lw["w_a"], lw["w_b"], lw["w_c"], lw["w_d"], lw["w_gate"], lw["b_gate"],
      lw["w_o"], lw["ln1_g"], lw["ln1_b"])


MOE_PAIRS = tuple((a, b) for a in range(MOE_EXPERTS_PER_GROUP) for b in range(a + 1, MOE_EXPERTS_PER_GROUP))
MOE_CLASSES = MOE_GROUPS * len(MOE_PAIRS)
MOE_ROW_TILE = 512
INFO_CLASS, INFO_RANK, INFO_P_LO, INFO_P_HI = 0, 1, 2, 3
MOE_ROW_WIDTH = D_MODEL + LANES


def _route_kernel(x_ref, wr_ref, br_ref, tri_ref, info_ref, counts_ref, run_ref):
    @pl.when(pl.program_id(0) == 0)
    def _():
        run_ref[...] = jnp.zeros(run_ref.shape, F32)

    x = x_ref[...]
    x_hi = x.astype(BF16)
    x_lo = (x - x_hi.astype(F32)).astype(BF16)
    logits = (_dot(x_hi, wr_ref[0]) + (_dot(x_hi, wr_ref[1]) + _dot(x_lo, wr_ref[0]))) + br_ref[...]
    tm = logits.shape[0]
    lane = lax.broadcasted_iota(jnp.int32, (tm, LANES), 1)
    is_g = lane < MOE_GROUPS
    gl = jnp.where(is_g, logits, NEG_BIG)
    gmax = jnp.max(gl, -1, keepdims=True)
    g_top = jnp.min(jnp.where(gl == gmax, lane, LANES), -1, keepdims=True)
    p_group = 1.0 / jnp.sum(jnp.where(is_g, jnp.exp(gl - gmax), 0.0), -1, keepdims=True)
    base = MOE_GROUPS + g_top * MOE_EXPERTS_PER_GROUP
    in_grp = jnp.logical_and(lane >= base, lane < base + MOE_EXPERTS_PER_GROUP)
    el = jnp.where(in_grp, logits, NEG_BIG)
    v1 = jnp.max(el, -1, keepdims=True)
    i1 = jnp.min(jnp.where(el == v1, lane, LANES), -1, keepdims=True)
    el2 = jnp.where(lane == i1, NEG_BIG, el)
    v2 = jnp.max(el2, -1, keepdims=True)
    i2 = jnp.min(jnp.where(el2 == v2, lane, LANES), -1, keepdims=True)
    e2 = jnp.exp(v2 - v1)
    p1 = p_group / (1.0 + e2)
    p2 = p_group * e2 / (1.0 + e2)
    a = jnp.minimum(i1, i2) - base
    b = jnp.maximum(i1, i2) - base
    pair = jnp.where(a == 0, 0, jnp.where(a == 1, 3, 5)) + (b - a - 1)
    cls = g_top * len(MOE_PAIRS) + pair
    first_is_lo = i1 < i2
    p_lo = jnp.where(first_is_lo, p1, p2)
    p_hi = jnp.where(first_is_lo, p2, p1)
    onehot = lane == cls
    before = _dot(tri_ref[...], onehot.astype(BF16)) + run_ref[...]
    rank = jnp.sum(jnp.where(onehot, before, 0.0), -1, keepdims=True)
    run_ref[...] += jnp.sum(onehot.astype(F32), axis=0, keepdims=True)
    counts_ref[...] = run_ref[...]
    info_ref[...] = jnp.where(lane == INFO_CLASS, cls.astype(F32),
                              jnp.where(lane == INFO_RANK, rank,
                                        jnp.where(lane == INFO_P_LO, p_lo,
                                                  jnp.where(lane == INFO_P_HI, p_hi, 0.0))))


ROW_COPY_UNROLL = 8


def _row_copies(n, make_copy, whole_copy):
    def start(r, carry):
        make_copy(r).start()
        return carry

    lax.fori_loop(0, n, start, 0, unroll=ROW_COPY_UNROLL)
    whole_copy.wait()


def _dispatch_kernel(dest_ref, x_ref, info_ref, init_hbm, rows_hbm, row_ref, sem):
    del init_hbm
    tm = x_ref.shape[0]
    row_ref[:, :D_MODEL] = x_ref[...]
    row_ref[:, D_MODEL:] = info_ref[...]
    _row_copies(tm, lambda r: pltpu.make_async_copy(
        row_ref.at[pl.ds(r, 1)], rows_hbm.at[pl.ds(dest_ref[0, r], 1)], sem),
        pltpu.make_async_copy(row_ref, rows_hbm.at[pl.ds(0, tm)], sem))


def _expert_kernel(ea_ref, eb_ref, nused_ref, rows_ref, wga_ref, wua_ref, wda_ref, wgb_ref, wub_ref, wdb_ref,
                   y_ref):
    del ea_ref, eb_ref
    i = pl.program_id(0)

    @pl.when(i < nused_ref[0])
    def _():
        x = rows_ref[:, :D_MODEL].astype(BF16)
        p_lo = rows_ref[:, D_MODEL + INFO_P_LO:D_MODEL + INFO_P_LO + 1]
        p_hi = rows_ref[:, D_MODEL + INFO_P_HI:D_MODEL + INFO_P_HI + 1]
        ha = jax.nn.silu(_dot(x, wga_ref[...])) * _dot(x, wua_ref[...]) * p_lo
        hb = jax.nn.silu(_dot(x, wgb_ref[...])) * _dot(x, wub_ref[...]) * p_hi
        y_ref[...] = _dot(ha.astype(BF16), wda_ref[...]) + _dot(hb.astype(BF16), wdb_ref[...])

    @pl.when(i >= nused_ref[0])
    def _():
        y_ref[...] = jnp.zeros(y_ref.shape, F32)


def _combine_kernel(dest_ref, x_ref, g_ref, b_ref, y_hbm, o_ref, y_ref, sem):
    tm = x_ref.shape[0]
    _row_copies(tm, lambda r: pltpu.make_async_copy(
        y_hbm.at[pl.ds(dest_ref[0, r], 1)], y_ref.at[pl.ds(r, 1)], sem),
        pltpu.make_async_copy(y_hbm.at[pl.ds(0, tm)], y_ref, sem))
    o_ref[...] = _layer_norm(DN_ALPHA * x_ref[...] + y_ref[...], g_ref[...], b_ref[...])


def _moe(x, lw):
    T = x.shape[0]
    tm = TOKEN_TILE
    rt = MOE_ROW_TILE
    n_row_tiles = T // rt + MOE_CLASSES
    n_rows = n_row_tiles * rt
    tok = lambda w: pl.BlockSpec((tm, w), lambda i: (i, 0))

    tri = jnp.asarray(np.tril(np.ones((tm, tm)), -1), BF16)
    info, counts = pl.pallas_call(
        _route_kernel,
        grid=(T // tm,),
        in_specs=[tok(D_MODEL), _full((2, D_MODEL, LANES)), _full((1, LANES)), _full((tm, tm))],
        out_specs=[tok(LANES), _full((1, LANES))],
        out_shape=[jax.ShapeDtypeStruct((T, LANES), F32), jax.ShapeDtypeStruct((1, LANES), F32)],
        scratch_shapes=[pltpu.VMEM((1, LANES), F32)],
        compiler_params=_cparams("arbitrary"),
        name="moe_route",
    )(x, lw["w_router"], lw["b_router"], tri)

    cls = info[:, INFO_CLASS].astype(jnp.int32)
    rank = info[:, INFO_RANK].astype(jnp.int32)
    cnt = counts[0, :MOE_CLASSES].astype(jnp.int32)
    padded = (cnt + rt - 1) // rt * rt
    ends = jnp.cumsum(padded)
    classes = jnp.arange(MOE_CLASSES, dtype=jnp.int32)
    pick = lambda table, idx: jnp.sum(jnp.where(idx[:, None] == classes[None, :], table[None, :], 0), axis=1)
    dest = (pick(ends - padded, cls) + rank).reshape(T // tm, 1, tm)
    tile_start = jnp.arange(n_row_tiles, dtype=jnp.int32) * rt
    tile_cls = jnp.minimum(jnp.sum((tile_start[:, None] >= ends[None, :]).astype(jnp.int32), axis=1),
                           MOE_CLASSES - 1)
    group, pair = np.divmod(np.arange(MOE_CLASSES), len(MOE_PAIRS))
    lo_hi = np.asarray(MOE_PAIRS)[pair]
    ea = pick(jnp.asarray(group * MOE_EXPERTS_PER_GROUP + lo_hi[:, 0], jnp.int32), tile_cls)
    eb = pick(jnp.asarray(group * MOE_EXPERTS_PER_GROUP + lo_hi[:, 1], jnp.int32), tile_cls)
    n_used = (ends[-1:] // rt).astype(jnp.int32)

    dest_spec = pl.BlockSpec((None, 1, tm), lambda i: (i, 0, 0), memory_space=pltpu.SMEM)
    hbm = pl.BlockSpec(memory_space=pl.ANY)
    rows = pl.pallas_call(
        _dispatch_kernel,
        grid=(T // tm,),
        in_specs=[dest_spec, tok(D_MODEL), tok(LANES), hbm],
        out_specs=hbm,
        out_shape=jax.ShapeDtypeStruct((n_rows, MOE_ROW_WIDTH), F32),
        scratch_shapes=[pltpu.VMEM((tm, MOE_ROW_WIDTH), F32), pltpu.SemaphoreType.DMA],
        input_output_aliases={3: 0},
        compiler_params=_cparams("arbitrary"),
        name="moe_dispatch",
    )(dest, x, info, jnp.zeros((n_rows, MOE_ROW_WIDTH), F32))

    w_up = lambda sel: pl.BlockSpec((None, D_MODEL, MOE_FF), lambda i, ea, eb, nu: (sel(ea, eb)[i], 0, 0))
    w_dn = lambda sel: pl.BlockSpec((None, MOE_FF, D_MODEL), lambda i, ea, eb, nu: (sel(ea, eb)[i], 0, 0))
    first = lambda ea, eb: ea
    second = lambda ea, eb: eb
    y = pl.pallas_call(
        _expert_kernel,
        grid_spec=pltpu.PrefetchScalarGridSpec(
            num_scalar_prefetch=3,
            grid=(n_row_tiles,),
            in_specs=[pl.BlockSpec((rt, MOE_ROW_WIDTH), lambda i, ea, eb, nu: (i, 0)),
                      w_up(first), w_up(first), w_dn(first), w_up(second), w_up(second), w_dn(second)],
            out_specs=pl.BlockSpec((rt, D_MODEL), lambda i, ea, eb, nu: (i, 0)),
        ),
        out_shape=jax.ShapeDtypeStruct((n_rows, D_MODEL), F32),
        compiler_params=_cparams("arbitrary"),
        name="moe_experts",
    )(ea, eb, n_used, rows, lw["moe_w_gate"], lw["moe_w_up"], lw["moe_w_down"],
      lw["moe_w_gate"], lw["moe_w_up"], lw["moe_w_down"])

    return pl.pallas_call(
        _combine_kernel,
        grid=(T // tm,),
        in_specs=[dest_spec, tok(D_MODEL), _full((1, D_MODEL)), _full((1, D_MODEL)), hbm],
        out_specs=tok(D_MODEL),
        out_shape=jax.ShapeDtypeStruct((T, D_MODEL), F32),
        scratch_shapes=[pltpu.VMEM((tm, D_MODEL), F32), pltpu.SemaphoreType.DMA],
        compiler_params=_cparams("arbitrary"),
        name="moe_combine",
    )(dest, x, lw["ln2_g"], lw["ln2_b"], y)


def _rope_tables(S):
    half = MLA_ROPE // 2
    inv = ROPE_BASE ** (-jnp.arange(half, dtype=F32) / half)
    ang = jnp.arange(S, dtype=F32)[:, None] * inv[None, :]
    cos, sin = jnp.cos(ang), jnp.sin(ang)
    one = jnp.ones((S, MLA_NOPE), F32)
    zero = jnp.zeros((S, MLA_NOPE), F32)
    zh = jnp.zeros((S, half), F32)
    rc = jnp.concatenate([one, cos, cos], axis=1)
    rsa = jnp.concatenate([zero, -sin, zh], axis=1)
    rsb = jnp.concatenate([zero, zh, sin], axis=1)
    return rc, rsa, rsb


def _split_bf16(w):
    hi = w.astype(BF16)
    return jnp.stack([hi, (w - hi.astype(F32)).astype(BF16)])


def _prep_layer(l, p):
    row = lambda a: a.reshape(1, -1).astype(F32)
    w_ukv = p["mla_w_ukv"][l].reshape(MLA_KV_RANK, MLA_HEADS, MLA_NOPE + MLA_V)
    pad_rows = ((0, 2 * LANES - MLA_KV_RANK), (0, 0))
    w_k = jnp.pad(w_ukv[:, :, :MLA_NOPE], ((0, 0), (0, 0), (0, LANES - MLA_NOPE))).reshape(MLA_KV_RANK, -1)
    w_v = w_ukv[:, :, MLA_NOPE:].reshape(MLA_KV_RANK, -1)
    w_router = jnp.concatenate([p["moe_w_rg"][l], p["moe_w_re"][l]], axis=1)
    b_router = jnp.concatenate([p["moe_b_rg"][l], p["moe_b_re"][l]])
    npad = LANES - MOE_GROUPS - MOE_EXPERTS
    return {
        "w_in": p["w_in"][l].astype(BF16),
        "sg_ln_g": row(p["sg_ln_g"][l]), "sg_ln_b": row(p["sg_ln_b"][l]),
        "sg_w": p["sg_w"][l].astype(BF16),
        "sg_bias": jnp.repeat(p["sg_b"][l].T, SG_GROUP_DIM, axis=1).astype(F32),
        "q_norm": row(p["mla_q_norm"][l]),
        "kv_norm": jnp.pad(row(p["mla_kv_norm"][l]), ((0, 0), (0, 2 * LANES - MLA_KV_RANK))),
        "w_uq": p["mla_w_uq"][l].astype(BF16),
        "w_k": jnp.pad(w_k, pad_rows).astype(BF16),
        "w_v": jnp.pad(w_v, pad_rows).astype(BF16),
        "w_a": p["w_branch_a"][l].astype(BF16), "w_b": p["w_branch_b"][l].astype(BF16),
        "w_c": p["w_branch_c"][l].astype(BF16), "w_d": p["w_branch_d"][l].astype(BF16),
        "w_gate": p["w_gate"][l].astype(BF16), "b_gate": row(p["b_gate"][l]),
        "w_o": p["w_o"][l].astype(BF16),
        "ln1_g": row(p["ln1_g"][l]), "ln1_b": row(p["ln1_b"][l]),
        "w_router": _split_bf16(jnp.pad(w_router, ((0, 0), (0, npad))).astype(F32)),
        "b_router": jnp.pad(b_router, (0, npad)).reshape(1, -1).astype(F32),
        "moe_w_gate": p["moe_w_gate"][l].astype(BF16), "moe_w_up": p["moe_w_up"][l].astype(BF16),
        "moe_w_down": p["moe_w_down"][l].astype(BF16),
        "ln2_g": row(p["ln2_g"][l]), "ln2_b": row(p["ln2_b"][l]),
    }


def _trunk(x, p, layers, bias_tables):
    B, S, _ = x.shape
    rope = _rope_tables(S)
    fconsts = _fourier_consts(S)
    h = _input_layer_norm(x.reshape(B * S, D_MODEL), p["ln_in_g"], p["ln_in_b"])
    for lw in layers:
        za, sg, q, k, vt, *dil_in = _in_proj(h, B, S, lw, rope)
        fa = _fourier_mix(za, B, S, fconsts)
        oc = _latent_attention(q, k, vt, B, S)
        dil = [_dilated_group(*dil_in[3 * g:3 * g + 3], bias_tables[g], g, d, B, S)
               for g, (_, d) in enumerate(DIL_PAIRS)]
        h = _merge(h, S, fa, sg, oc, dil, lw)
        h = _moe(h, lw)
    return h.reshape(B, S, D_MODEL)


def kernel(x_prompt, x_sample, ln_in_g, ln_in_b, rel_bias, w_in, sg_ln_g, sg_ln_b, sg_w, sg_b, mla_q_norm, mla_kv_norm, mla_w_uq, mla_w_ukv, w_branch_a, w_branch_b, w_branch_c, w_branch_d, w_gate, b_gate, w_o, ln1_g, ln1_b, moe_w_rg, moe_b_rg, moe_w_re, moe_b_re, moe_w_gate, moe_w_up, moe_w_down, ln2_g, ln2_b):
    p = dict(ln_in_g=ln_in_g, ln_in_b=ln_in_b, w_in=w_in, sg_ln_g=sg_ln_g, sg_ln_b=sg_ln_b, sg_w=sg_w, sg_b=sg_b,
             mla_q_norm=mla_q_norm, mla_kv_norm=mla_kv_norm, mla_w_uq=mla_w_uq, mla_w_ukv=mla_w_ukv,
             w_branch_a=w_branch_a, w_branch_b=w_branch_b, w_branch_c=w_branch_c, w_branch_d=w_branch_d,
             w_gate=w_gate, b_gate=b_gate, w_o=w_o, ln1_g=ln1_g, ln1_b=ln1_b,
             moe_w_rg=moe_w_rg, moe_b_rg=moe_b_rg, moe_w_re=moe_w_re, moe_b_re=moe_b_re,
             moe_w_gate=moe_w_gate, moe_w_up=moe_w_up, moe_w_down=moe_w_down, ln2_g=ln2_g, ln2_b=ln2_b)
    layers = [_prep_layer(l, p) for l in range(w_in.shape[0])]
    bias_tables = _dilated_bias_tables(rel_bias)
    return _trunk(x_prompt, p, layers, bias_tables), _trunk(x_sample, p, layers, bias_tables)
```

```python
import functools
import math

import numpy as np
import jax
import jax.numpy as jnp
from jax import lax
from jax.experimental import pallas as pl
from jax.experimental.pallas import tpu as pltpu

F32 = jnp.float32
BF16 = jnp.bfloat16

D_MODEL = 1024
DEPTH = 4
F_GROUPS = 4
F_GROUP_DIM = 128
F_WIDTH = F_GROUPS * F_GROUP_DIM
SG_CHUNK = 128
SG_GROUPS = 4
SG_GROUP_DIM = 64
SG_WIDTH = SG_GROUPS * SG_GROUP_DIM
MLA_HEADS = 8
MLA_Q_RANK = 256
MLA_KV_RANK = 192
MLA_NOPE = 64
MLA_ROPE = 64
MLA_V = 64
MLA_QK_DIM = MLA_NOPE + MLA_ROPE
ROPE_BASE = 10000.0
DIL_PAIRS = ((128, 1), (512, 4), (2048, 16))
DIL_GROUPS = 3
DIL_HEADS = 4
DIL_HEAD_DIM = 32
DIL_Q_WIDTH = DIL_GROUPS * DIL_HEADS * DIL_HEAD_DIM
DIL_KV_WIDTH = DIL_HEADS * DIL_HEAD_DIM
DIL_SIDE = 64
REL_BUCKETS = 32
REL_MAX_DIST = 1024
MIX_WIDTH = F_WIDTH + 2 * SG_WIDTH + MLA_Q_RANK + MLA_KV_RANK + MLA_ROPE + DIL_Q_WIDTH + 2 * DIL_KV_WIDTH
MOE_GROUPS = 4
MOE_EXPERTS_PER_GROUP = 4
MOE_EXPERTS = MOE_GROUPS * MOE_EXPERTS_PER_GROUP
MOE_FF = 512
DN_ALPHA = (2 * DEPTH) ** 0.25
LN_EPS = 1e-5
RMS_EPS = 1e-6

OFF_A = 0
OFF_B = OFF_A + F_WIDTH
OFF_CQ = OFF_B + 2 * SG_WIDTH
OFF_CKV = OFF_CQ + MLA_Q_RANK
OFF_DQ = OFF_CKV + MLA_KV_RANK + MLA_ROPE
OFF_DK = OFF_DQ + DIL_Q_WIDTH
OFF_DV = OFF_DK + DIL_KV_WIDTH

LANES = 128
SUBLANES = 8
VMEM_LIMIT_BYTES = 56 * 1024 * 1024
TOKEN_TILE = 512
ATTN_Q_BLOCK = 512
ATTN_Q_SUB = 256
ATTN_KV_BLOCK = 2048
ATTN_CHUNK = 256
ATTN_CHUNKS_PER_ITER = 8
ATTN_PIPELINE_DEPTH = 6
ATTN_V_ROWS = 80
DIL_BLOCK = 128
DIL_Q_BLOCK = 512
NEG_BIG = -1e30
LOG2_E = math.log2(math.e)


def _cparams(*sem):
    return pltpu.CompilerParams(dimension_semantics=sem, vmem_limit_bytes=VMEM_LIMIT_BYTES)


def _full(shape):
    n = len(shape)
    return pl.BlockSpec(shape, lambda *_: (0,) * n)


def _layer_norm(x, g, b):
    mu = jnp.mean(x, -1, keepdims=True)
    xc = x - mu
    var = jnp.mean(xc * xc, -1, keepdims=True)
    return xc * lax.rsqrt(var + LN_EPS) * g + b


def _dot(a, b):
    return jnp.dot(a, b, preferred_element_type=F32)


def _ln_kernel(x_ref, g_ref, b_ref, o_ref):
    o_ref[...] = _layer_norm(x_ref[...], g_ref[...], b_ref[...])


def _input_layer_norm(x, g, b):
    T = x.shape[0]
    tm = TOKEN_TILE
    return pl.pallas_call(
        _ln_kernel,
        grid=(T // tm,),
        in_specs=[pl.BlockSpec((tm, D_MODEL), lambda i: (i, 0)), _full((1, D_MODEL)), _full((1, D_MODEL))],
        out_specs=pl.BlockSpec((tm, D_MODEL), lambda i: (i, 0)),
        out_shape=jax.ShapeDtypeStruct((T, D_MODEL), F32),
        compiler_params=_cparams("parallel"),
        name="input_layer_norm",
    )(x, g.reshape(1, -1), b.reshape(1, -1))


def _rope_lanes(t, c, sa, sb):
    return t * c + pltpu.roll(t, 96, 1) * sa + pltpu.roll(t, 32, 1) * sb


def _in_proj_kernel(x_ref, w_in_ref, sg_g_ref, sg_b_ref, sg_w_ref, sg_bias_ref, qn_ref, kvn_ref,
                    wuq_ref, wk_ref, wv_ref, rc_ref, rsa_ref, rsb_ref,
                    za_ref, sg_ref, qt_ref, k_ref, vt_ref,
                    dq0_ref, dk0_ref, dv0_ref, dq1_ref, dk1_ref, dv1_ref, dq2_ref, dk2_ref, dv2_ref, dil_ref):
    tm = x_ref.shape[0]
    z = _dot(x_ref[...].astype(BF16), w_in_ref[...])

    za_ref[...] = z[:, OFF_A:OFF_A + F_WIDTH]

    zb = jax.nn.gelu(z[:, OFF_B:OFF_B + 2 * SG_WIDTH])
    u = zb[:, :SG_WIDTH]
    vn = _layer_norm(zb[:, SG_WIDTH:], sg_g_ref[...], sg_b_ref[...]).astype(BF16)
    lane = lax.broadcasted_iota(jnp.int32, (SG_CHUNK, LANES), 1)
    low_half = lane < SG_GROUP_DIM
    for ci in range(tm // SG_CHUNK):
        rows = slice(ci * SG_CHUNK, (ci + 1) * SG_CHUNK)
        for j in range(SG_WIDTH // LANES):
            cols = slice(j * LANES, (j + 1) * LANES)
            vblk = vn[rows, cols]
            mixed = jnp.where(low_half, _dot(sg_w_ref[2 * j], vblk), _dot(sg_w_ref[2 * j + 1], vblk))
            sg_ref[rows, cols] = (u[rows, cols] * (mixed + sg_bias_ref[:, cols])).astype(sg_ref.dtype)

    rc, rsa, rsb = rc_ref[...], rsa_ref[...], rsb_ref[...]

    cq = z[:, OFF_CQ:OFF_CQ + MLA_Q_RANK]
    cq = cq * lax.rsqrt(jnp.mean(cq * cq, -1, keepdims=True) + RMS_EPS) * qn_ref[...]
    q = _dot(cq.astype(BF16), wuq_ref[...]) * (MLA_QK_DIM ** -0.5 * LOG2_E)
    ra = pltpu.roll(q, MLA_HEADS * LANES - 32, 1)
    rb = pltpu.roll(q, 32, 1)
    q_t = jnp.concatenate(
        [q[:, h * LANES:(h + 1) * LANES] * rc + ra[:, h * LANES:(h + 1) * LANES] * rsa
         + rb[:, h * LANES:(h + 1) * LANES] * rsb for h in range(MLA_HEADS)], axis=1).T
    for j in range(tm // ATTN_Q_SUB):
        qt_ref[j] = q_t[:, j * ATTN_Q_SUB:(j + 1) * ATTN_Q_SUB].astype(qt_ref.dtype)

    slab = z[:, OFF_CKV:OFF_CKV + 2 * LANES]
    lane2 = lax.broadcasted_iota(jnp.int32, (tm, 2 * LANES), 1)
    ckv_sq = jnp.where(lane2 < MLA_KV_RANK, slab * slab, 0.0)
    ms = jnp.sum(ckv_sq, -1, keepdims=True) * (1.0 / MLA_KV_RANK)
    ckv = (slab * lax.rsqrt(ms + RMS_EPS) * kvn_ref[...]).astype(BF16)
    k_nope = _dot(ckv, wk_ref[...])
    v_t = _dot(ckv, wv_ref[...]).T.astype(vt_ref.dtype)
    ones = jnp.ones((ATTN_V_ROWS - MLA_V, ATTN_CHUNK), vt_ref.dtype)
    for j in range(tm // ATTN_CHUNK):
        cols = slice(j * ATTN_CHUNK, (j + 1) * ATTN_CHUNK)
        for h in range(MLA_HEADS):
            vt_ref[j, h * ATTN_V_ROWS:h * ATTN_V_ROWS + MLA_V, :] = v_t[h * MLA_V:(h + 1) * MLA_V, cols]
            vt_ref[j, h * ATTN_V_ROWS + MLA_V:(h + 1) * ATTN_V_ROWS, :] = ones
    kr_slab = _rope_lanes(slab[:, LANES:], rc, rsa, rsb)
    kr_slab = jnp.where(lax.broadcasted_iota(jnp.int32, (tm, LANES), 1) >= MLA_NOPE, kr_slab, 0.0)
    for h in range(MLA_HEADS):
        cols = slice(h * LANES, (h + 1) * LANES)
        k_ref[:, cols] = (k_nope[:, cols] + kr_slab).astype(k_ref.dtype)

    for slab in range(DIL_GROUPS + 2):
        t = z[:, OFF_DQ + slab * LANES:OFF_DQ + (slab + 1) * LANES]
        dil_ref[slab] = t * (DIL_HEAD_DIM ** -0.5 * LOG2_E) if slab < DIL_GROUPS else t
    dq0_ref[...] = dil_ref[0].astype(dq0_ref.dtype)
    dk0_ref[...] = dil_ref[DIL_GROUPS].astype(dk0_ref.dtype)
    dv0_ref[...] = dil_ref[DIL_GROUPS + 1].astype(dv0_ref.dtype)
    for g, (qr, kr, vr) in ((1, (dq1_ref, dk1_ref, dv1_ref)), (2, (dq2_ref, dk2_ref, dv2_ref))):
        d = DIL_PAIRS[g][1]
        for c in range(d):
            rows = pl.ds(c, tm // d, stride=d)
            qr[c] = dil_ref[g, rows, :].astype(qr.dtype)
            kr[c] = dil_ref[DIL_GROUPS, rows, :].astype(kr.dtype)
            vr[c] = dil_ref[DIL_GROUPS + 1, rows, :].astype(vr.dtype)


def _in_proj(x, B, S, lw, rope):
    T = x.shape[0]
    tm = TOKEN_TILE
    nblk = S // tm
    tok = lambda w: pl.BlockSpec((tm, w), lambda i: (i, 0))
    pos = pl.BlockSpec((tm, LANES), lambda i: (i % nblk, 0))
    out_specs = [tok(F_WIDTH), tok(SG_WIDTH)]
    out_shape = [jax.ShapeDtypeStruct((T, F_WIDTH), F32), jax.ShapeDtypeStruct((T, SG_WIDTH), BF16)]
    out_specs.append(pl.BlockSpec((tm // ATTN_Q_SUB, MLA_HEADS * LANES, ATTN_Q_SUB), lambda i: (i, 0, 0)))
    out_shape.append(jax.ShapeDtypeStruct((T // ATTN_Q_SUB, MLA_HEADS * LANES, ATTN_Q_SUB), BF16))
    out_specs.append(tok(MLA_HEADS * LANES))
    out_shape.append(jax.ShapeDtypeStruct((T, MLA_HEADS * LANES), BF16))
    out_specs.append(pl.BlockSpec((tm // ATTN_CHUNK, MLA_HEADS * ATTN_V_ROWS, ATTN_CHUNK), lambda i: (i, 0, 0)))
    out_shape.append(jax.ShapeDtypeStruct((T // ATTN_CHUNK, MLA_HEADS * ATTN_V_ROWS, ATTN_CHUNK), BF16))
    for _, d in DIL_PAIRS:
        for _ in range(3):
            if d == 1:
                out_specs.append(tok(LANES))
                out_shape.append(jax.ShapeDtypeStruct((T, LANES), BF16))
            else:
                out_specs.append(pl.BlockSpec((None, d, tm // d, LANES), lambda i: (i // nblk, 0, i % nblk, 0)))
                out_shape.append(jax.ShapeDtypeStruct((B, d, S // d, LANES), BF16))
    return pl.pallas_call(
        _in_proj_kernel,
        grid=(T // tm,),
        in_specs=[tok(D_MODEL), _full((D_MODEL, MIX_WIDTH)), _full((1, SG_WIDTH)), _full((1, SG_WIDTH)),
                  _full((SG_GROUPS, SG_CHUNK, SG_CHUNK)), _full((SG_CHUNK, SG_WIDTH)),
                  _full((1, MLA_Q_RANK)), _full((1, 2 * LANES)),
                  _full((MLA_Q_RANK, MLA_HEADS * LANES)), _full((2 * LANES, MLA_HEADS * LANES)),
                  _full((2 * LANES, MLA_HEADS * MLA_V)), pos, pos, pos],
        out_specs=out_specs,
        out_shape=out_shape,
        scratch_shapes=[pltpu.VMEM((DIL_GROUPS + 2, tm, LANES), F32)],
        compiler_params=_cparams("parallel"),
        name="in_proj",
    )(x, lw["w_in"], lw["sg_ln_g"], lw["sg_ln_b"], lw["sg_w"], lw["sg_bias"], lw["q_norm"], lw["kv_norm"],
      lw["w_uq"], lw["w_k"], lw["w_v"], *rope)


FOURIER_N1 = 16
FOURIER_ROWS = 16
FOURIER_K1_PER_STEP = 2


def _fourier1_kernel(x_ref, cs_ref, k1_ref, ct_ref, st_ref, gr_ref, gi_ref):
    n1, r, _ = x_ref.shape
    rows = n1 * r
    x = x_ref[...].reshape(rows, F_WIDTH)
    ct, st = ct_ref[...], st_ref[...]
    outs_r, outs_i = [], []
    for g in range(F_GROUPS):
        ab = _dot(x[:, g * LANES:(g + 1) * LANES].astype(BF16), cs_ref[...])
        stacked = jnp.concatenate([ab[:, :LANES], ab[:, LANES:]], axis=0).astype(BF16)
        g2 = _dot(k1_ref[...], stacked)
        gr, gi = g2[:rows], g2[rows:]
        outs_r.append(gr * ct - gi * st)
        outs_i.append(gr * st + gi * ct)
    gr_ref[...] = jnp.concatenate(outs_r, axis=1).reshape(n1, r, F_WIDTH)
    gi_ref[...] = jnp.concatenate(outs_i, axis=1).reshape(n1, r, F_WIDTH)


def _fourier2_kernel(gr_ref, gi_ref, w2_ref, o_ref):
    for j in range(gr_ref.shape[0]):
        stacked = jnp.concatenate([gr_ref[j], gi_ref[j]], axis=0).astype(BF16)
        o_ref[j] = _dot(w2_ref[...], stacked)


def _fourier_mix(za, B, S, consts):
    n1 = FOURIER_N1
    n2 = S // n1
    r = FOURIER_ROWS
    kk = FOURIER_K1_PER_STEP
    cs, k1, ct, st, w2 = consts
    x = za.reshape(B, n1, n2, F_WIDTH)
    blk1 = pl.BlockSpec((None, n1, r, F_WIDTH), lambda b, j: (b, 0, j, 0))
    twid = pl.BlockSpec((None, n1 * r, LANES), lambda b, j: (j, 0, 0))
    gr, gi = pl.pallas_call(
        _fourier1_kernel,
        grid=(B, n2 // r),
        in_specs=[blk1, _full((LANES, 2 * LANES)), _full((2 * n1 * r, 2 * n1 * r)), twid, twid],
        out_specs=[blk1, blk1],
        out_shape=[jax.ShapeDtypeStruct(x.shape, F32)] * 2,
        compiler_params=_cparams("parallel", "parallel"),
        name="fourier_stage1",
    )(x, cs, k1, ct, st)
    blk2 = pl.BlockSpec((None, kk, n2, F_WIDTH), lambda b, j: (b, j, 0, 0))
    return pl.pallas_call(
        _fourier2_kernel,
        grid=(B, n1 // kk),
        in_specs=[blk2, blk2, _full((n2, 2 * n2))],
        out_specs=blk2,
        out_shape=jax.ShapeDtypeStruct((B, n1, n2, F_WIDTH), F32),
        compiler_params=_cparams("parallel", "parallel"),
        name="fourier_stage2",
    )(gr, gi, w2)


def _fourier_consts(S):
    n1 = FOURIER_N1
    n2 = S // n1
    r = FOURIER_ROWS
    c = np.arange(F_GROUP_DIM)
    ang_c = 2.0 * np.pi * np.outer(c, c) / F_GROUP_DIM
    norm = 1.0 / math.sqrt(S * F_GROUP_DIM)
    cs = np.concatenate([np.cos(ang_c), np.sin(ang_c)], axis=1) * norm
    eye = np.eye(r)
    a1 = np.arange(n1)
    ang1 = 2.0 * np.pi * np.outer(a1, a1) / n1
    c1, s1 = np.kron(np.cos(ang1), eye), np.kron(np.sin(ang1), eye)
    k1 = np.block([[c1, -s1], [s1, c1]])
    a2 = np.arange(n2)
    ang_t = 2.0 * np.pi * np.outer(a1, a2) / S

    def twiddle(t):
        t = t.reshape(n1, n2 // r, r).transpose(1, 0, 2).reshape(n2 // r, n1 * r)
        return np.broadcast_to(t[:, :, None], (n2 // r, n1 * r, LANES))

    ang2 = 2.0 * np.pi * np.outer(a2, a2) / n2
    w2 = np.concatenate([np.cos(ang2), -np.sin(ang2)], axis=1)
    return (jnp.asarray(cs, BF16), jnp.asarray(k1, BF16), jnp.asarray(twiddle(np.cos(ang_t)), F32),
            jnp.asarray(twiddle(np.sin(ang_t)), F32), jnp.asarray(w2, BF16))


def _flash_kernel(qt_ref, k_ref, vt_ref, o_ref, m_ref, l_ref, acc_ref):
    ki = pl.program_id(2)
    nsub = qt_ref.shape[0]
    nchunk = k_ref.shape[0] // ATTN_CHUNK

    @pl.when(ki == 0)
    def _():
        m_ref[...] = jnp.full(m_ref.shape, NEG_BIG, F32)
        l_ref[...] = jnp.zeros(l_ref.shape, F32)
        acc_ref[...] = jnp.zeros(acc_ref.shape, F32)

    per_chunk = MLA_HEADS * nsub
    n_tiles = ATTN_CHUNKS_PER_ITER * per_chunk

    def tile_index(it, t):
        cc, rem = divmod(t, per_chunk)
        h, qs = divmod(rem, nsub)
        return it * ATTN_CHUNKS_PER_ITER + cc, h, qs

    def scores(it, t):
        c, h, qs = tile_index(it, t)
        r0 = pl.multiple_of(c * ATTN_CHUNK, ATTN_CHUNK)
        kc = k_ref[pl.ds(r0, ATTN_CHUNK), h * LANES:(h + 1) * LANES]
        return _dot(kc, qt_ref[qs, h * LANES:(h + 1) * LANES, :])

    def accumulate(it, t, s):
        c, h, qs = tile_index(it, t)
        rows = slice(h * MLA_V, (h + 1) * MLA_V)
        m_prev = m_ref[qs, h]
        m_new = jnp.maximum(m_prev, jnp.max(s, axis=0, keepdims=True))
        alpha = jnp.exp2(m_prev - m_new)
        p = jnp.exp2(s - m_new[0:1, :]).astype(BF16)
        m_ref[qs, h] = m_new
        pv = _dot(vt_ref[c, h * ATTN_V_ROWS:(h + 1) * ATTN_V_ROWS, :], p)
        l_ref[qs, h] = alpha * l_ref[qs, h] + pv[MLA_V:MLA_V + SUBLANES, :]
        acc_ref[qs, rows, :] = alpha[0:1, :] * acc_ref[qs, rows, :] + pv[:MLA_V, :]

    def iteration(it, carry):
        pending = {}
        for t in range(n_tiles + ATTN_PIPELINE_DEPTH):
            if t < n_tiles:
                pending[t] = scores(it, t)
            if t >= ATTN_PIPELINE_DEPTH:
                accumulate(it, t - ATTN_PIPELINE_DEPTH, pending.pop(t - ATTN_PIPELINE_DEPTH))
        return carry

    lax.fori_loop(0, nchunk // ATTN_CHUNKS_PER_ITER, iteration, 0)

    @pl.when(ki == pl.num_programs(2) - 1)
    def _():
        for qs in range(nsub):
            out_t = jnp.concatenate(
                [acc_ref[qs, h * MLA_V:(h + 1) * MLA_V, :] * (1.0 / l_ref[qs, h, 0:1, :])
                 for h in range(MLA_HEADS)], axis=0)
            o_ref[qs * ATTN_Q_SUB:(qs + 1) * ATTN_Q_SUB, :] = out_t.T.astype(o_ref.dtype)


def _latent_attention(qt, k, vt, B, S):
    tq = min(ATTN_Q_BLOCK, S)
    tk = min(ATTN_KV_BLOCK, S)
    nsub = tq // ATTN_Q_SUB
    qt = qt.reshape(B, S // ATTN_Q_SUB, MLA_HEADS * LANES, ATTN_Q_SUB)
    k = k.reshape(B, S, MLA_HEADS * LANES)
    vt = vt.reshape(B, S // ATTN_CHUNK, MLA_HEADS * ATTN_V_ROWS, ATTN_CHUNK)
    out = pl.pallas_call(
        _flash_kernel,
        grid=(B, S // tq, S // tk),
        in_specs=[pl.BlockSpec((None, nsub, MLA_HEADS * LANES, ATTN_Q_SUB), lambda b, i, j: (b, i, 0, 0)),
                  pl.BlockSpec((None, tk, MLA_HEADS * LANES), lambda b, i, j: (b, j, 0)),
                  pl.BlockSpec((None, tk // ATTN_CHUNK, MLA_HEADS * ATTN_V_ROWS, ATTN_CHUNK),
                               lambda b, i, j: (b, j, 0, 0))],
        out_specs=pl.BlockSpec((None, tq, MLA_HEADS * MLA_V), lambda b, i, j: (b, i, 0)),
        out_shape=jax.ShapeDtypeStruct((B, S, MLA_HEADS * MLA_V), BF16),
        scratch_shapes=[pltpu.VMEM((nsub, MLA_HEADS, SUBLANES, ATTN_Q_SUB), F32),
                        pltpu.VMEM((nsub, MLA_HEADS, SUBLANES, ATTN_Q_SUB), F32),
                        pltpu.VMEM((nsub, MLA_HEADS * MLA_V, ATTN_Q_SUB), F32)],
        compiler_params=_cparams("parallel", "parallel", "arbitrary"),
        name="latent_attention",
    )(qt, k, vt)
    return out.reshape(B * S, MLA_HEADS * MLA_V)


def _dilated_kernel(q_ref, kp_ref, kc_ref, kn_ref, vp_ref, vc_ref, vn_ref, bias_ref, o_ref, lse_ref, *, rows):
    i = pl.program_id(2)
    qb = q_ref.shape[0]
    k_all = jnp.concatenate([kp_ref[...], kc_ref[...], kn_ref[...]], axis=0)
    v_all = jnp.concatenate([vp_ref[...], vc_ref[...], vn_ref[...]], axis=0)
    v_all_t = v_all.astype(F32).T.astype(BF16)
    row = lax.broadcasted_iota(jnp.int32, (3 * DIL_BLOCK, DIL_BLOCK), 0)
    lane = lax.broadcasted_iota(jnp.int32, (1, LANES), 1)
    dim = lax.broadcasted_iota(jnp.int32, (LANES, 1), 0)
    n_sub = qb // DIL_BLOCK
    scores = {}
    for sub in range(n_sub):
        q = q_ref[sub * DIL_BLOCK:(sub + 1) * DIL_BLOCK, :]
        k3 = k_all[sub * DIL_BLOCK:(sub + 3) * DIL_BLOCK, :]
        for h in range(DIL_HEADS):
            head_lanes = jnp.logical_and(lane >= h * DIL_HEAD_DIM, lane < (h + 1) * DIL_HEAD_DIM)
            qh = jnp.where(head_lanes, q, jnp.zeros_like(q))
            scores[sub, h] = lax.dot_general(k3, qh, (((1,), (1,)), ((), ())), preferred_element_type=F32)
    for sub in range(n_sub):
        inside = 1 <= sub <= n_sub - 2
        key_row = i * qb + (sub - 1) * DIL_BLOCK + row
        in_seq = jnp.logical_and(key_row >= 0, key_row < rows)
        v3_t = v_all_t[:, sub * DIL_BLOCK:(sub + 3) * DIL_BLOCK]
        pvs, lses = [], []
        for h in range(DIL_HEADS):
            s = scores.pop((sub, h)) + bias_ref[h]
            s = s if inside else jnp.where(in_seq, s, NEG_BIG)
            m = jnp.max(s, axis=0, keepdims=True)
            e = jnp.exp2(s - m)
            den = jnp.sum(e, axis=0, keepdims=True)
            pvs.append(_dot(v3_t, e.astype(BF16)) * (1.0 / den))
            lses.append((m + jnp.log2(den)) * (1.0 / LOG2_E))
        out_t = jnp.zeros((LANES, DIL_BLOCK), F32)
        lse_t = jnp.zeros((LANES, DIL_BLOCK), F32)
        for h in range(DIL_HEADS):
            head_rows = jnp.logical_and(dim >= h * DIL_HEAD_DIM, dim < (h + 1) * DIL_HEAD_DIM)
            out_t = jnp.where(head_rows, pvs[h], out_t)
            lse_t = jnp.where(head_rows, lses[h], lse_t)
        o_ref[sub * DIL_BLOCK:(sub + 1) * DIL_BLOCK, :] = out_t.T
        lse_ref[sub * DIL_BLOCK:(sub + 1) * DIL_BLOCK, :] = lse_t.T


def _dilated_group(q, k, v, bias_t, g, d, B, S):
    rows = S // d
    qb = min(DIL_Q_BLOCK, rows)
    per = qb // DIL_BLOCK
    last = rows // DIL_BLOCK - 1
    shape = (B, d, rows, LANES)
    q, k, v = q.reshape(shape), k.reshape(shape), v.reshape(shape)
    edge = lambda f: pl.BlockSpec((None, None, DIL_BLOCK, LANES), f)
    main = pl.BlockSpec((None, None, qb, LANES), lambda b, c, i: (b, c, i, 0))
    prev = edge(lambda b, c, i: (b, c, jnp.maximum(i * per - 1, 0), 0))
    nxt = edge(lambda b, c, i: (b, c, jnp.minimum((i + 1) * per, last), 0))
    return pl.pallas_call(
        functools.partial(_dilated_kernel, rows=rows),
        grid=(B, d, rows // qb),
        in_specs=[main, prev, main, nxt, prev, main, nxt, _full((DIL_HEADS, 3 * DIL_BLOCK, DIL_BLOCK))],
        out_specs=[main, main],
        out_shape=[jax.ShapeDtypeStruct(shape, F32)] * 2,
        compiler_params=_cparams("parallel", "parallel", "parallel"),
        name="dilated_attention_g%d" % g,
    )(q, k, k, k, v, v, v, bias_t)


def _t5_bucket(rel):
    nb = REL_BUCKETS // 2
    max_exact = nb // 2
    ret = jnp.where(rel > 0, nb, 0)
    n = jnp.abs(rel)
    nf = jnp.maximum(n, 1).astype(F32)
    large = max_exact + (jnp.log(nf / max_exact) / math.log(REL_MAX_DIST / max_exact) * (nb - max_exact)).astype(jnp.int32)
    large = jnp.minimum(large, nb - 1)
    return ret + jnp.where(n < max_exact, n, large)


def _dilated_bias_tables(rel_bias):
    shape = (3 * DIL_BLOCK, DIL_BLOCK)
    rel = lax.broadcasted_iota(jnp.int32, shape, 0) - DIL_BLOCK - lax.broadcasted_iota(jnp.int32, shape, 1)
    in_band = jnp.abs(rel) <= DIL_SIDE
    steps = jnp.arange(-DIL_SIDE, DIL_SIDE + 1, dtype=jnp.int32)
    band_onehot = (rel[None] == steps[:, None, None]).astype(F32)
    exact = lax.Precision.HIGHEST
    tables = []
    for g, (_, d) in enumerate(DIL_PAIRS):
        bucket_onehot = (_t5_bucket(d * steps)[:, None] == jnp.arange(REL_BUCKETS)[None, :]).astype(F32)
        b = jnp.dot(bucket_onehot, rel_bias[:, g * DIL_HEADS:(g + 1) * DIL_HEADS].astype(F32), precision=exact)
        table = jnp.einsum("nh,nkq->hkq", b, band_onehot, precision=exact)
        tables.append(jnp.where(in_band[None], table * LOG2_E, NEG_BIG))
    return tables


def _merge_kernel(x_ref, fa_ref, sg_ref, oc_ref, o0_ref, l0_ref, o1_ref, l1_ref, o2_ref, l2_ref,
                  wa_ref, wb_ref, wc_ref, wd_ref, wg_ref, bg_ref, wo_ref, g_ref, b_ref, o_ref, nat_ref):
    x = x_ref[...]
    xb = x.astype(BF16)
    tm = x.shape[0]
    for slot, (src, g) in enumerate(((o1_ref, 1), (l1_ref, 1), (o2_ref, 2), (l2_ref, 2))):
        d = DIL_PAIRS[g][1]
        for c in range(d):
            nat_ref[slot, pl.ds(c, tm // d, stride=d), :] = src[c]
    l0, l1, l2 = l0_ref[...], nat_ref[1], nat_ref[3]
    mx = jnp.maximum(jnp.maximum(l0, l1), l2)
    e0, e1, e2 = jnp.exp(l0 - mx), jnp.exp(l1 - mx), jnp.exp(l2 - mx)
    den = e0 + e1 + e2
    od = o0_ref[...] * (e0 / den) + nat_ref[0] * (e1 / den) + nat_ref[2] * (e2 / den)
    n_fixed = 4
    for k1 in range(FOURIER_N1):
        for g in range(F_GROUPS):
            nat_ref[n_fixed + g, pl.ds(k1, tm // FOURIER_N1, stride=FOURIER_N1), :] = (
                fa_ref[k1, :, g * LANES:(g + 1) * LANES])
    fa = jnp.concatenate([nat_ref[n_fixed + g] for g in range(F_GROUPS)], axis=1).astype(BF16)
    branches = ((fa, wa_ref), (sg_ref[...], wb_ref), (oc_ref[...], wc_ref), (od.astype(BF16), wd_ref))
    merged = None
    for i, (act, w_ref) in enumerate(branches):
        cols = slice(i * D_MODEL, (i + 1) * D_MODEL)
        gate = jax.nn.sigmoid(_dot(xb, wg_ref[:, cols]) + bg_ref[:, cols])
        term = gate * _dot(act, w_ref[...])
        merged = term if merged is None else merged + term
    y = DN_ALPHA * x + _dot(merged.astype(BF16), wo_ref[...])
    o_ref[...] = _layer_norm(y, g_ref[...], b_ref[...])


def _merge(x, S, fa, sg, oc, dil, lw):
    T = x.shape[0]
    tm = TOKEN_TILE
    nblk = S // tm
    tok = lambda w: pl.BlockSpec((tm, w), lambda i: (i, 0))
    dil_specs, dil_args = [], []
    for (o, lse), (_, d) in zip(dil, DIL_PAIRS):
        for a in (o, lse):
            if d == 1:
                dil_specs.append(tok(LANES))
                dil_args.append(a.reshape(T, LANES))
            else:
                dil_specs.append(pl.BlockSpec((None, d, tm // d, LANES), lambda i: (i // nblk, 0, i % nblk, 0)))
                dil_args.append(a)
    return pl.pallas_call(
        _merge_kernel,
        grid=(T // tm,),
        in_specs=[tok(D_MODEL),
                  pl.BlockSpec((None, FOURIER_N1, tm // FOURIER_N1, F_WIDTH), lambda i: (i // nblk, 0, i % nblk, 0)),
                  tok(SG_WIDTH), tok(MLA_HEADS * MLA_V)] + dil_specs + [
                  _full((F_WIDTH, D_MODEL)), _full((SG_WIDTH, D_MODEL)), _full((MLA_HEADS * MLA_V, D_MODEL)),
                  _full((DIL_KV_WIDTH, D_MODEL)), _full((D_MODEL, 4 * D_MODEL)), _full((1, 4 * D_MODEL)),
                  _full((D_MODEL, D_MODEL)), _full((1, D_MODEL)), _full((1, D_MODEL))],
        out_specs=tok(D_MODEL),
        out_shape=jax.ShapeDtypeStruct((T, D_MODEL), F32),
        scratch_shapes=[pltpu.VMEM((4 + F_GROUPS, tm, LANES), F32)],
        compiler_params=_cparams("parallel"),
        name="merge",
    )(x, fa, sg, oc, *dil_args, lw["w_a"], lw["w_b"], lw["w_c"], lw["w_d"], lw["w_gate"], lw["b_gate"],
      lw["w_o"], lw["ln1_g"], lw["ln1_b"])


MOE_PAIRS = tuple((a, b) for a in range(MOE_EXPERTS_PER_GROUP) for b in range(a + 1, MOE_EXPERTS_PER_GROUP))
MOE_CLASSES = MOE_GROUPS * len(MOE_PAIRS)
MOE_ROW_TILE = 512
INFO_CLASS, INFO_RANK, INFO_P_LO, INFO_P_HI = 0, 1, 2, 3
MOE_ROW_WIDTH = D_MODEL + LANES


def _route_kernel(x_ref, wr_ref, br_ref, tri_ref, info_ref, counts_ref, run_ref):
    @pl.when(pl.program_id(0) == 0)
    def _():
        run_ref[...] = jnp.zeros(run_ref.shape, F32)

    x = x_ref[...]
    x_hi = x.astype(BF16)
    x_lo = (x - x_hi.astype(F32)).astype(BF16)
    logits = (_dot(x_hi, wr_ref[0]) + (_dot(x_hi, wr_ref[1]) + _dot(x_lo, wr_ref[0]))) + br_ref[...]
    tm = logits.shape[0]
    lane = lax.broadcasted_iota(jnp.int32, (tm, LANES), 1)
    is_g = lane < MOE_GROUPS
    gl = jnp.where(is_g, logits, NEG_BIG)
    gmax = jnp.max(gl, -1, keepdims=True)
    g_top = jnp.min(jnp.where(gl == gmax, lane, LANES), -1, keepdims=True)
    p_group = 1.0 / jnp.sum(jnp.where(is_g, jnp.exp(gl - gmax), 0.0), -1, keepdims=True)
    base = MOE_GROUPS + g_top * MOE_EXPERTS_PER_GROUP
    in_grp = jnp.logical_and(lane >= base, lane < base + MOE_EXPERTS_PER_GROUP)
    el = jnp.where(in_grp, logits, NEG_BIG)
    v1 = jnp.max(el, -1, keepdims=True)
    i1 = jnp.min(jnp.where(el == v1, lane, LANES), -1, keepdims=True)
    el2 = jnp.where(lane == i1, NEG_BIG, el)
    v2 = jnp.max(el2, -1, keepdims=True)
    i2 = jnp.min(jnp.where(el2 == v2, lane, LANES), -1, keepdims=True)
    e2 = jnp.exp(v2 - v1)
    p1 = p_group / (1.0 + e2)
    p2 = p_group * e2 / (1.0 + e2)
    a = jnp.minimum(i1, i2) - base
    b = jnp.maximum(i1, i2) - base
    pair = jnp.where(a == 0, 0, jnp.where(a == 1, 3, 5)) + (b - a - 1)
    cls = g_top * len(MOE_PAIRS) + pair
    first_is_lo = i1 < i2
    p_lo = jnp.where(first_is_lo, p1, p2)
    p_hi = jnp.where(first_is_lo, p2, p1)
    onehot = lane == cls
    before = _dot(tri_ref[...], onehot.astype(BF16)) + run_ref[...]
    rank = jnp.sum(jnp.where(onehot, before, 0.0), -1, keepdims=True)
    run_ref[...] += jnp.sum(onehot.astype(F32), axis=0, keepdims=True)
    counts_ref[...] = run_ref[...]
    info_ref[...] = jnp.where(lane == INFO_CLASS, cls.astype(F32),
                              jnp.where(lane == INFO_RANK, rank,
                                        jnp.where(lane == INFO_P_LO, p_lo,
                                                  jnp.where(lane == INFO_P_HI, p_hi, 0.0))))


ROW_COPY_UNROLL = 8


def _row_copies(n, make_copy, whole_copy):
    def start(r, carry):
        make_copy(r).start()
        return carry

    lax.fori_loop(0, n, start, 0, unroll=ROW_COPY_UNROLL)
    whole_copy.wait()


def _dispatch_kernel(dest_ref, x_ref, info_ref, init_hbm, rows_hbm, row_ref, sem):
    del init_hbm
    tm = x_ref.shape[0]
    row_ref[:, :D_MODEL] = x_ref[...]
    row_ref[:, D_MODEL:] = info_ref[...]
    _row_copies(tm, lambda r: pltpu.make_async_copy(
        row_ref.at[pl.ds(r, 1)], rows_hbm.at[pl.ds(dest_ref[0, r], 1)], sem),
        pltpu.make_async_copy(row_ref, rows_hbm.at[pl.ds(0, tm)], sem))


def _expert_kernel(ea_ref, eb_ref, nused_ref, rows_ref, wga_ref, wua_ref, wda_ref, wgb_ref, wub_ref, wdb_ref,
                   y_ref):
    del ea_ref, eb_ref
    i = pl.program_id(0)

    @pl.when(i < nused_ref[0])
    def _():
        x = rows_ref[:, :D_MODEL].astype(BF16)
        p_lo = rows_ref[:, D_MODEL + INFO_P_LO:D_MODEL + INFO_P_LO + 1]
        p_hi = rows_ref[:, D_MODEL + INFO_P_HI:D_MODEL + INFO_P_HI + 1]
        ha = jax.nn.silu(_dot(x, wga_ref[...])) * _dot(x, wua_ref[...]) * p_lo
        hb = jax.nn.silu(_dot(x, wgb_ref[...])) * _dot(x, wub_ref[...]) * p_hi
        y_ref[...] = _dot(ha.astype(BF16), wda_ref[...]) + _dot(hb.astype(BF16), wdb_ref[...])

    @pl.when(i >= nused_ref[0])
    def _():
        y_ref[...] = jnp.zeros(y_ref.shape, F32)


def _combine_kernel(dest_ref, x_ref, g_ref, b_ref, y_hbm, o_ref, y_ref, sem):
    tm = x_ref.shape[0]
    _row_copies(tm, lambda r: pltpu.make_async_copy(
        y_hbm.at[pl.ds(dest_ref[0, r], 1)], y_ref.at[pl.ds(r, 1)], sem),
        pltpu.make_async_copy(y_hbm.at[pl.ds(0, tm)], y_ref, sem))
    o_ref[...] = _layer_norm(DN_ALPHA * x_ref[...] + y_ref[...], g_ref[...], b_ref[...])


def _moe(x, lw):
    T = x.shape[0]
    tm = TOKEN_TILE
    rt = MOE_ROW_TILE
    n_row_tiles = T // rt + MOE_CLASSES
    n_rows = n_row_tiles * rt
    tok = lambda w: pl.BlockSpec((tm, w), lambda i: (i, 0))

    tri = jnp.asarray(np.tril(np.ones((tm, tm)), -1), BF16)
    info, counts = pl.pallas_call(
        _route_kernel,
        grid=(T // tm,),
        in_specs=[tok(D_MODEL), _full((2, D_MODEL, LANES)), _full((1, LANES)), _full((tm, tm))],
        out_specs=[tok(LANES), _full((1, LANES))],
        out_shape=[jax.ShapeDtypeStruct((T, LANES), F32), jax.ShapeDtypeStruct((1, LANES), F32)],
        scratch_shapes=[pltpu.VMEM((1, LANES), F32)],
        compiler_params=_cparams("arbitrary"),
        name="moe_route",
    )(x, lw["w_router"], lw["b_router"], tri)

    cls = info[:, INFO_CLASS].astype(jnp.int32)
    rank = info[:, INFO_RANK].astype(jnp.int32)
    cnt = counts[0, :MOE_CLASSES].astype(jnp.int32)
    padded = (cnt + rt - 1) // rt * rt
    ends = jnp.cumsum(padded)
    classes = jnp.arange(MOE_CLASSES, dtype=jnp.int32)
    pick = lambda table, idx: jnp.sum(jnp.where(idx[:, None] == classes[None, :], table[None, :], 0), axis=1)
    dest = (pick(ends - padded, cls) + rank).reshape(T // tm, 1, tm)
    tile_start = jnp.arange(n_row_tiles, dtype=jnp.int32) * rt
    tile_cls = jnp.minimum(jnp.sum((tile_start[:, None] >= ends[None, :]).astype(jnp.int32), axis=1),
                           MOE_CLASSES - 1)
    group, pair = np.divmod(np.arange(MOE_CLASSES), len(MOE_PAIRS))
    lo_hi = np.asarray(MOE_PAIRS)[pair]
    ea = pick(jnp.asarray(group * MOE_EXPERTS_PER_GROUP + lo_hi[:, 0], jnp.int32), tile_cls)
    eb = pick(jnp.asarray(group * MOE_EXPERTS_PER_GROUP + lo_hi[:, 1], jnp.int32), tile_cls)
    n_used = (ends[-1:] // rt).astype(jnp.int32)

    dest_spec = pl.BlockSpec((None, 1, tm), lambda i: (i, 0, 0), memory_space=pltpu.SMEM)
    hbm = pl.BlockSpec(memory_space=pl.ANY)
    rows = pl.pallas_call(
        _dispatch_kernel,
        grid=(T // tm,),
        in_specs=[dest_spec, tok(D_MODEL), tok(LANES), hbm],
        out_specs=hbm,
        out_shape=jax.ShapeDtypeStruct((n_rows, MOE_ROW_WIDTH), F32),
        scratch_shapes=[pltpu.VMEM((tm, MOE_ROW_WIDTH), F32), pltpu.SemaphoreType.DMA],
        input_output_aliases={3: 0},
        compiler_params=_cparams("arbitrary"),
        name="moe_dispatch",
    )(dest, x, info, jnp.zeros((n_rows, MOE_ROW_WIDTH), F32))

    w_up = lambda sel: pl.BlockSpec((None, D_MODEL, MOE_FF), lambda i, ea, eb, nu: (sel(ea, eb)[i], 0, 0))
    w_dn = lambda sel: pl.BlockSpec((None, MOE_FF, D_MODEL), lambda i, ea, eb, nu: (sel(ea, eb)[i], 0, 0))
    first = lambda ea, eb: ea
    second = lambda ea, eb: eb
    y = pl.pallas_call(
        _expert_kernel,
        grid_spec=pltpu.PrefetchScalarGridSpec(
            num_scalar_prefetch=3,
            grid=(n_row_tiles,),
            in_specs=[pl.BlockSpec((rt, MOE_ROW_WIDTH), lambda i, ea, eb, nu: (i, 0)),
                      w_up(first), w_up(first), w_dn(first), w_up(second), w_up(second), w_dn(second)],
            out_specs=pl.BlockSpec((rt, D_MODEL), lambda i, ea, eb, nu: (i, 0)),
        ),
        out_shape=jax.ShapeDtypeStruct((n_rows, D_MODEL), F32),
        compiler_params=_cparams("arbitrary"),
        name="moe_experts",
    )(ea, eb, n_used, rows, lw["moe_w_gate"], lw["moe_w_up"], lw["moe_w_down"],
      lw["moe_w_gate"], lw["moe_w_up"], lw["moe_w_down"])

    return pl.pallas_call(
        _combine_kernel,
        grid=(T // tm,),
        in_specs=[dest_spec, tok(D_MODEL), _full((1, D_MODEL)), _full((1, D_MODEL)), hbm],
        out_specs=tok(D_MODEL),
        out_shape=jax.ShapeDtypeStruct((T, D_MODEL), F32),
        scratch_shapes=[pltpu.VMEM((tm, D_MODEL), F32), pltpu.SemaphoreType.DMA],
        compiler_params=_cparams("arbitrary"),
        name="moe_combine",
    )(dest, x, lw["ln2_g"], lw["ln2_b"], y)


def _rope_tables(S):
    half = MLA_ROPE // 2
    inv = ROPE_BASE ** (-jnp.arange(half, dtype=F32) / half)
    ang = jnp.arange(S, dtype=F32)[:, None] * inv[None, :]
    cos, sin = jnp.cos(ang), jnp.sin(ang)
    one = jnp.ones((S, MLA_NOPE), F32)
    zero = jnp.zeros((S, MLA_NOPE), F32)
    zh = jnp.zeros((S, half), F32)
    rc = jnp.concatenate([one, cos, cos], axis=1)
    rsa = jnp.concatenate([zero, -sin, zh], axis=1)
    rsb = jnp.concatenate([zero, zh, sin], axis=1)
    return rc, rsa, rsb


def _split_bf16(w):
    hi = w.astype(BF16)
    return jnp.stack([hi, (w - hi.astype(F32)).astype(BF16)])


def _prep_layer(l, p):
    row = lambda a: a.reshape(1, -1).astype(F32)
    w_ukv = p["mla_w_ukv"][l].reshape(MLA_KV_RANK, MLA_HEADS, MLA_NOPE + MLA_V)
    pad_rows = ((0, 2 * LANES - MLA_KV_RANK), (0, 0))
    w_k = jnp.pad(w_ukv[:, :, :MLA_NOPE], ((0, 0), (0, 0), (0, LANES - MLA_NOPE))).reshape(MLA_KV_RANK, -1)
    w_v = w_ukv[:, :, MLA_NOPE:].reshape(MLA_KV_RANK, -1)
    w_router = jnp.concatenate([p["moe_w_rg"][l], p["moe_w_re"][l]], axis=1)
    b_router = jnp.concatenate([p["moe_b_rg"][l], p["moe_b_re"][l]])
    npad = LANES - MOE_GROUPS - MOE_EXPERTS
    return {
        "w_in": p["w_in"][l].astype(BF16),
        "sg_ln_g": row(p["sg_ln_g"][l]), "sg_ln_b": row(p["sg_ln_b"][l]),
        "sg_w": p["sg_w"][l].astype(BF16),
        "sg_bias": jnp.repeat(p["sg_b"][l].T, SG_GROUP_DIM, axis=1).astype(F32),
        "q_norm": row(p["mla_q_norm"][l]),
        "kv_norm": jnp.pad(row(p["mla_kv_norm"][l]), ((0, 0), (0, 2 * LANES - MLA_KV_RANK))),
        "w_uq": p["mla_w_uq"][l].astype(BF16),
        "w_k": jnp.pad(w_k, pad_rows).astype(BF16),
        "w_v": jnp.pad(w_v, pad_rows).astype(BF16),
        "w_a": p["w_branch_a"][l].astype(BF16), "w_b": p["w_branch_b"][l].astype(BF16),
        "w_c": p["w_branch_c"][l].astype(BF16), "w_d": p["w_branch_d"][l].astype(BF16),
        "w_gate": p["w_gate"][l].astype(BF16), "b_gate": row(p["b_gate"][l]),
        "w_o": p["w_o"][l].astype(BF16),
        "ln1_g": row(p["ln1_g"][l]), "ln1_b": row(p["ln1_b"][l]),
        "w_router": _split_bf16(jnp.pad(w_router, ((0, 0), (0, npad))).astype(F32)),
        "b_router": jnp.pad(b_router, (0, npad)).reshape(1, -1).astype(F32),
        "moe_w_gate": p["moe_w_gate"][l].astype(BF16), "moe_w_up": p["moe_w_up"][l].astype(BF16),
        "moe_w_down": p["moe_w_down"][l].astype(BF16),
        "ln2_g": row(p["ln2_g"][l]), "ln2_b": row(p["ln2_b"][l]),
    }


def _trunk(x, p, layers, bias_tables):
    B, S, _ = x.shape
    rope = _rope_tables(S)
    fconsts = _fourier_consts(S)
    h = _input_layer_norm(x.reshape(B * S, D_MODEL), p["ln_in_g"], p["ln_in_b"])
    for lw in layers:
        za, sg, q, k, vt, *dil_in = _in_proj(h, B, S, lw, rope)
        fa = _fourier_mix(za, B, S, fconsts)
        oc = _latent_attention(q, k, vt, B, S)
        dil = [_dilated_group(*dil_in[3 * g:3 * g + 3], bias_tables[g], g, d, B, S)
               for g, (_, d) in enumerate(DIL_PAIRS)]
        h = _merge(h, S, fa, sg, oc, dil, lw)
        h = _moe(h, lw)
    return h.reshape(B, S, D_MODEL)


def kernel(x_prompt, x_sample, ln_in_g, ln_in_b, rel_bias, w_in, sg_ln_g, sg_ln_b, sg_w, sg_b, mla_q_norm, mla_kv_norm, mla_w_uq, mla_w_ukv, w_branch_a, w_branch_b, w_branch_c, w_branch_d, w_gate, b_gate, w_o, ln1_g, ln1_b, moe_w_rg, moe_b_rg, moe_w_re, moe_b_re, moe_w_gate, moe_w_up, moe_w_down, ln2_g, ln2_b):
    p = dict(ln_in_g=ln_in_g, ln_in_b=ln_in_b, w_in=w_in, sg_ln_g=sg_ln_g, sg_ln_b=sg_ln_b, sg_w=sg_w, sg_b=sg_b,
             mla_q_norm=mla_q_norm, mla_kv_norm=mla_kv_norm, mla_w_uq=mla_w_uq, mla_w_ukv=mla_w_ukv,
             w_branch_a=w_branch_a, w_branch_b=w_branch_b, w_branch_c=w_branch_c, w_branch_d=w_branch_d,
             w_gate=w_gate, b_gate=b_gate, w_o=w_o, ln1_g=ln1_g, ln1_b=ln1_b,
             moe_w_rg=moe_w_rg, moe_b_rg=moe_b_rg, moe_w_re=moe_w_re, moe_b_re=moe_b_re,
             moe_w_gate=moe_w_gate, moe_w_up=moe_w_up, moe_w_down=moe_w_down, ln2_g=ln2_g, ln2_b=ln2_b)
    layers = [_prep_layer(l, p) for l in range(w_in.shape[0])]
    bias_tables = _dilated_bias_tables(rel_bias)
    return _trunk(x_prompt, p, layers, bias_tables), _trunk(x_sample, p, layers, bias_tables)
```

```python
import functools
import math

import numpy as np
import jax
import jax.numpy as jnp
from jax import lax
from jax.experimental import pallas as pl
from jax.experimental.pallas import tpu as pltpu

F32 = jnp.float32
BF16 = jnp.bfloat16

D_MODEL = 1024
DEPTH = 4
F_GROUPS = 4
F_GROUP_DIM = 128
F_WIDTH = F_GROUPS * F_GROUP_DIM
SG_CHUNK = 128
SG_GROUPS = 4
SG_GROUP_DIM = 64
SG_WIDTH = SG_GROUPS * SG_GROUP_DIM
MLA_HEADS = 8
MLA_Q_RANK = 256
MLA_KV_RANK = 192
MLA_NOPE = 64
MLA_ROPE = 64
MLA_V = 64
MLA_QK_DIM = MLA_NOPE + MLA_ROPE
ROPE_BASE = 10000.0
DIL_PAIRS = ((128, 1), (512, 4), (2048, 16))
DIL_GROUPS = 3
DIL_HEADS = 4
DIL_HEAD_DIM = 32
DIL_Q_WIDTH = DIL_GROUPS * DIL_HEADS * DIL_HEAD_DIM
DIL_KV_WIDTH = DIL_HEADS * DIL_HEAD_DIM
DIL_SIDE = 64
REL_BUCKETS = 32
REL_MAX_DIST = 1024
MIX_WIDTH = F_WIDTH + 2 * SG_WIDTH + MLA_Q_RANK + MLA_KV_RANK + MLA_ROPE + DIL_Q_WIDTH + 2 * DIL_KV_WIDTH
MOE_GROUPS = 4
MOE_EXPERTS_PER_GROUP = 4
MOE_EXPERTS = MOE_GROUPS * MOE_EXPERTS_PER_GROUP
MOE_FF = 512
DN_ALPHA = (2 * DEPTH) ** 0.25
LN_EPS = 1e-5
RMS_EPS = 1e-6

OFF_A = 0
OFF_B = OFF_A + F_WIDTH
OFF_CQ = OFF_B + 2 * SG_WIDTH
OFF_CKV = OFF_CQ + MLA_Q_RANK
OFF_DQ = OFF_CKV + MLA_KV_RANK + MLA_ROPE
OFF_DK = OFF_DQ + DIL_Q_WIDTH
OFF_DV = OFF_DK + DIL_KV_WIDTH

LANES = 128
SUBLANES = 8
VMEM_LIMIT_BYTES = 56 * 1024 * 1024
TOKEN_TILE = 512
ATTN_Q_BLOCK = 512
ATTN_Q_SUB = 256
ATTN_KV_BLOCK = 2048
ATTN_CHUNK = 256
ATTN_CHUNKS_PER_ITER = 8
ATTN_PIPELINE_DEPTH = 6
ATTN_V_ROWS = 80
DIL_BLOCK = 128
DIL_Q_BLOCK = 512
NEG_BIG = -1e30
LOG2_E = math.log2(math.e)


def _cparams(*sem):
    return pltpu.CompilerParams(dimension_semantics=sem, vmem_limit_bytes=VMEM_LIMIT_BYTES)


def _full(shape):
    n = len(shape)
    return pl.BlockSpec(shape, lambda *_: (0,) * n)


def _layer_norm(x, g, b):
    mu = jnp.mean(x, -1, keepdims=True)
    xc = x - mu
    var = jnp.mean(xc * xc, -1, keepdims=True)
    return xc * lax.rsqrt(var + LN_EPS) * g + b


def _dot(a, b):
    return jnp.dot(a, b, preferred_element_type=F32)


def _ln_kernel(x_ref, g_ref, b_ref, o_ref):
    o_ref[...] = _layer_norm(x_ref[...], g_ref[...], b_ref[...])


def _input_layer_norm(x, g, b):
    T = x.shape[0]
    tm = TOKEN_TILE
    return pl.pallas_call(
        _ln_kernel,
        grid=(T // tm,),
        in_specs=[pl.BlockSpec((tm, D_MODEL), lambda i: (i, 0)), _full((1, D_MODEL)), _full((1, D_MODEL))],
        out_specs=pl.BlockSpec((tm, D_MODEL), lambda i: (i, 0)),
        out_shape=jax.ShapeDtypeStruct((T, D_MODEL), F32),
        compiler_params=_cparams("parallel"),
        name="input_layer_norm",
    )(x, g.reshape(1, -1), b.reshape(1, -1))


def _rope_lanes(t, c, sa, sb):
    return t * c + pltpu.roll(t, 96, 1) * sa + pltpu.roll(t, 32, 1) * sb


def _in_proj_kernel(x_ref, w_in_ref, sg_g_ref, sg_b_ref, sg_w_ref, sg_bias_ref, qn_ref, kvn_ref,
                    wuq_ref, wk_ref, wv_ref, rc_ref, rsa_ref, rsb_ref,
                    za_ref, sg_ref, qt_ref, k_ref, vt_ref,
                    dq0_ref, dk0_ref, dv0_ref, dq1_ref, dk1_ref, dv1_ref, dq2_ref, dk2_ref, dv2_ref, dil_ref):
    tm = x_ref.shape[0]
    z = _dot(x_ref[...].astype(BF16), w_in_ref[...])

    za_ref[...] = z[:, OFF_A:OFF_A + F_WIDTH]

    zb = jax.nn.gelu(z[:, OFF_B:OFF_B + 2 * SG_WIDTH])
    u = zb[:, :SG_WIDTH]
    vn = _layer_norm(zb[:, SG_WIDTH:], sg_g_ref[...], sg_b_ref[...]).astype(BF16)
    lane = lax.broadcasted_iota(jnp.int32, (SG_CHUNK, LANES), 1)
    low_half = lane < SG_GROUP_DIM
    for ci in range(tm // SG_CHUNK):
        rows = slice(ci * SG_CHUNK, (ci + 1) * SG_CHUNK)
        for j in range(SG_WIDTH // LANES):
            cols = slice(j * LANES, (j + 1) * LANES)
            vblk = vn[rows, cols]
            mixed = jnp.where(low_half, _dot(sg_w_ref[2 * j], vblk), _dot(sg_w_ref[2 * j + 1], vblk))
            sg_ref[rows, cols] = (u[rows, cols] * (mixed + sg_bias_ref[:, cols])).astype(sg_ref.dtype)

    rc, rsa, rsb = rc_ref[...], rsa_ref[...], rsb_ref[...]

    cq = z[:, OFF_CQ:OFF_CQ + MLA_Q_RANK]
    cq = cq * lax.rsqrt(jnp.mean(cq * cq, -1, keepdims=True) + RMS_EPS) * qn_ref[...]
    q = _dot(cq.astype(BF16), wuq_ref[...]) * (MLA_QK_DIM ** -0.5 * LOG2_E)
    ra = pltpu.roll(q, MLA_HEADS * LANES - 32, 1)
    rb = pltpu.roll(q, 32, 1)
    q_t = jnp.concatenate(
        [q[:, h * LANES:(h + 1) * LANES] * rc + ra[:, h * LANES:(h + 1) * LANES] * rsa
         + rb[:, h * LANES:(h + 1) * LANES] * rsb for h in range(MLA_HEADS)], axis=1).T
    for j in range(tm // ATTN_Q_SUB):
        qt_ref[j] = q_t[:, j * ATTN_Q_SUB:(j + 1) * ATTN_Q_SUB].astype(qt_ref.dtype)

    slab = z[:, OFF_CKV:OFF_CKV + 2 * LANES]
    lane2 = lax.broadcasted_iota(jnp.int32, (tm, 2 * LANES), 1)
    ckv_sq = jnp.where(lane2 < MLA_KV_RANK, slab * slab, 0.0)
    ms = jnp.sum(ckv_sq, -1, keepdims=True) * (1.0 / MLA_KV_RANK)
    ckv = (slab * lax.rsqrt(ms + RMS_EPS) * kvn_ref[...]).astype(BF16)
    k_nope = _dot(ckv, wk_ref[...])
    v_t = _dot(ckv, wv_ref[...]).T.astype(vt_ref.dtype)
    ones = jnp.ones((ATTN_V_ROWS - MLA_V, ATTN_CHUNK), vt_ref.dtype)
    for j in range(tm // ATTN_CHUNK):
        cols = slice(j * ATTN_CHUNK, (j + 1) * ATTN_CHUNK)
        for h in range(MLA_HEADS):
            vt_ref[j, h * ATTN_V_ROWS:h * ATTN_V_ROWS + MLA_V, :] = v_t[h * MLA_V:(h + 1) * MLA_V, cols]
            vt_ref[j, h * ATTN_V_ROWS + MLA_V:(h + 1) * ATTN_V_ROWS, :] = ones
    kr_slab = _rope_lanes(slab[:, LANES:], rc, rsa, rsb)
    kr_slab = jnp.where(lax.broadcasted_iota(jnp.int32, (tm, LANES), 1) >= MLA_NOPE, kr_slab, 0.0)
    for h in range(MLA_HEADS):
        cols = slice(h * LANES, (h + 1) * LANES)
        k_ref[:, cols] = (k_nope[:, cols] + kr_slab).astype(k_ref.dtype)

    for slab in range(DIL_GROUPS + 2):
        t = z[:, OFF_DQ + slab * LANES:OFF_DQ + (slab + 1) * LANES]
        dil_ref[slab] = t * (DIL_HEAD_DIM ** -0.5 * LOG2_E) if slab < DIL_GROUPS else t
    dq0_ref[...] = dil_ref[0].astype(dq0_ref.dtype)
    dk0_ref[...] = dil_ref[DIL_GROUPS].astype(dk0_ref.dtype)
    dv0_ref[...] = dil_ref[DIL_GROUPS + 1].astype(dv0_ref.dtype)
    for g, (qr, kr, vr) in ((1, (dq1_ref, dk1_ref, dv1_ref)), (2, (dq2_ref, dk2_ref, dv2_ref))):
        d = DIL_PAIRS[g][1]
        for c in range(d):
            rows = pl.ds(c, tm // d, stride=d)
            qr[c] = dil_ref[g, rows, :].astype(qr.dtype)
            kr[c] = dil_ref[DIL_GROUPS, rows, :].astype(kr.dtype)
            vr[c] = dil_ref[DIL_GROUPS + 1, rows, :].astype(vr.dtype)


def _in_proj(x, B, S, lw, rope):
    T = x.shape[0]
    tm = TOKEN_TILE
    nblk = S // tm
    tok = lambda w: pl.BlockSpec((tm, w), lambda i: (i, 0))
    pos = pl.BlockSpec((tm, LANES), lambda i: (i % nblk, 0))
    out_specs = [tok(F_WIDTH), tok(SG_WIDTH)]
    out_shape = [jax.ShapeDtypeStruct((T, F_WIDTH), F32), jax.ShapeDtypeStruct((T, SG_WIDTH), BF16)]
    out_specs.append(pl.BlockSpec((tm // ATTN_Q_SUB, MLA_HEADS * LANES, ATTN_Q_SUB), lambda i: (i, 0, 0)))
    out_shape.append(jax.ShapeDtypeStruct((T // ATTN_Q_SUB, MLA_HEADS * LANES, ATTN_Q_SUB), BF16))
    out_specs.append(tok(MLA_HEADS * LANES))
    out_shape.append(jax.ShapeDtypeStruct((T, MLA_HEADS * LANES), BF16))
    out_specs.append(pl.BlockSpec((tm // ATTN_CHUNK, MLA_HEADS * ATTN_V_ROWS, ATTN_CHUNK), lambda i: (i, 0, 0)))
    out_shape.append(jax.ShapeDtypeStruct((T // ATTN_CHUNK, MLA_HEADS * ATTN_V_ROWS, ATTN_CHUNK), BF16))
    for _, d in DIL_PAIRS:
        for _ in range(3):
            if d == 1:
                out_specs.append(tok(LANES))
                out_shape.append(jax.ShapeDtypeStruct((T, LANES), BF16))
            else:
                out_specs.append(pl.BlockSpec((None, d, tm // d, LANES), lambda i: (i // nblk, 0, i % nblk, 0)))
                out_shape.append(jax.ShapeDtypeStruct((B, d, S // d, LANES), BF16))
    return pl.pallas_call(
        _in_proj_kernel,
        grid=(T // tm,),
        in_specs=[tok(D_MODEL), _full((D_MODEL, MIX_WIDTH)), _full((1, SG_WIDTH)), _full((1, SG_WIDTH)),
                  _full((SG_GROUPS, SG_CHUNK, SG_CHUNK)), _full((SG_CHUNK, SG_WIDTH)),
                  _full((1, MLA_Q_RANK)), _full((1, 2 * LANES)),
                  _full((MLA_Q_RANK, MLA_HEADS * LANES)), _full((2 * LANES, MLA_HEADS * LANES)),
                  _full((2 * LANES, MLA_HEADS * MLA_V)), pos, pos, pos],
        out_specs=out_specs,
        out_shape=out_shape,
        scratch_shapes=[pltpu.VMEM((DIL_GROUPS + 2, tm, LANES), F32)],
        compiler_params=_cparams("parallel"),
        name="in_proj",
    )(x, lw["w_in"], lw["sg_ln_g"], lw["sg_ln_b"], lw["sg_w"], lw["sg_bias"], lw["q_norm"], lw["kv_norm"],
      lw["w_uq"], lw["w_k"], lw["w_v"], *rope)


FOURIER_N1 = 16
FOURIER_ROWS = 16
FOURIER_K1_PER_STEP = 2


def _fourier1_kernel(x_ref, cs_ref, k1_ref, ct_ref, st_ref, gr_ref, gi_ref):
    n1, r, _ = x_ref.shape
    rows = n1 * r
    x = x_ref[...].reshape(rows, F_WIDTH)
    ct, st = ct_ref[...], st_ref[...]
    outs_r, outs_i = [], []
    for g in range(F_GROUPS):
        ab = _dot(x[:, g * LANES:(g + 1) * LANES].astype(BF16), cs_ref[...])
        stacked = jnp.concatenate([ab[:, :LANES], ab[:, LANES:]], axis=0).astype(BF16)
        g2 = _dot(k1_ref[...], stacked)
        gr, gi = g2[:rows], g2[rows:]
        outs_r.append(gr * ct - gi * st)
        outs_i.append(gr * st + gi * ct)
    gr_ref[...] = jnp.concatenate(outs_r, axis=1).reshape(n1, r, F_WIDTH)
    gi_ref[...] = jnp.concatenate(outs_i, axis=1).reshape(n1, r, F_WIDTH)


def _fourier2_kernel(gr_ref, gi_ref, w2_ref, o_ref):
    for j in range(gr_ref.shape[0]):
        stacked = jnp.concatenate([gr_ref[j], gi_ref[j]], axis=0).astype(BF16)
        o_ref[j] = _dot(w2_ref[...], stacked)


def _fourier_mix(za, B, S, consts):
    n1 = FOURIER_N1
    n2 = S // n1
    r = FOURIER_ROWS
    kk = FOURIER_K1_PER_STEP
    cs, k1, ct, st, w2 = consts
    x = za.reshape(B, n1, n2, F_WIDTH)
    blk1 = pl.BlockSpec((None, n1, r, F_WIDTH), lambda b, j: (b, 0, j, 0))
    twid = pl.BlockSpec((None, n1 * r, LANES), lambda b, j: (j, 0, 0))
    gr, gi = pl.pallas_call(
        _fourier1_kernel,
        grid=(B, n2 // r),
        in_specs=[blk1, _full((LANES, 2 * LANES)), _full((2 * n1 * r, 2 * n1 * r)), twid, twid],
        out_specs=[blk1, blk1],
        out_shape=[jax.ShapeDtypeStruct(x.shape, F32)] * 2,
        compiler_params=_cparams("parallel", "parallel"),
        name="fourier_stage1",
    )(x, cs, k1, ct, st)
    blk2 = pl.BlockSpec((None, kk, n2, F_WIDTH), lambda b, j: (b, j, 0, 0))
    return pl.pallas_call(
        _fourier2_kernel,
        grid=(B, n1 // kk),
        in_specs=[blk2, blk2, _full((n2, 2 * n2))],
        out_specs=blk2,
        out_shape=jax.ShapeDtypeStruct((B, n1, n2, F_WIDTH), F32),
        compiler_params=_cparams("parallel", "parallel"),
        name="fourier_stage2",
    )(gr, gi, w2)


def _fourier_consts(S):
    n1 = FOURIER_N1
    n2 = S // n1
    r = FOURIER_ROWS
    c = np.arange(F_GROUP_DIM)
    ang_c = 2.0 * np.pi * np.outer(c, c) / F_GROUP_DIM
    norm = 1.0 / math.sqrt(S * F_GROUP_DIM)
    cs = np.concatenate([np.cos(ang_c), np.sin(ang_c)], axis=1) * norm
    eye = np.eye(r)
    a1 = np.arange(n1)
    ang1 = 2.0 * np.pi * np.outer(a1, a1) / n1
    c1, s1 = np.kron(np.cos(ang1), eye), np.kron(np.sin(ang1), eye)
    k1 = np.block([[c1, -s1], [s1, c1]])
    a2 = np.arange(n2)
    ang_t = 2.0 * np.pi * np.outer(a1, a2) / S

    def twiddle(t):
        t = t.reshape(n1, n2 // r, r).transpose(1, 0, 2).reshape(n2 // r, n1 * r)
        return np.broadcast_to(t[:, :, None], (n2 // r, n1 * r, LANES))

    ang2 = 2.0 * np.pi * np.outer(a2, a2) / n2
    w2 = np.concatenate([np.cos(ang2), -np.sin(ang2)], axis=1)
    return (jnp.asarray(cs, BF16), jnp.asarray(k1, BF16), jnp.asarray(twiddle(np.cos(ang_t)), F32),
            jnp.asarray(twiddle(np.sin(ang_t)), F32), jnp.asarray(w2, BF16))


def _flash_kernel(qt_ref, k_ref, vt_ref, o_ref, m_ref, l_ref, acc_ref):
    ki = pl.program_id(2)
    nsub = qt_ref.shape[0]
    nchunk = k_ref.shape[0] // ATTN_CHUNK

    @pl.when(ki == 0)
    def _():
        m_ref[...] = jnp.full(m_ref.shape, NEG_BIG, F32)
        l_ref[...] = jnp.zeros(l_ref.shape, F32)
        acc_ref[...] = jnp.zeros(acc_ref.shape, F32)

    per_chunk = MLA_HEADS * nsub
    n_tiles = ATTN_CHUNKS_PER_ITER * per_chunk

    def tile_index(it, t):
        cc, rem = divmod(t, per_chunk)
        h, qs = divmod(rem, nsub)
        return it * ATTN_CHUNKS_PER_ITER + cc, h, qs

    def scores(it, t):
        c, h, qs = tile_index(it, t)
        r0 = pl.multiple_of(c * ATTN_CHUNK, ATTN_CHUNK)
        kc = k_ref[pl.ds(r0, ATTN_CHUNK), h * LANES:(h + 1) * LANES]
        return _dot(kc, qt_ref[qs, h * LANES:(h + 1) * LANES, :])

    def accumulate(it, t, s):
        c, h, qs = tile_index(it, t)
        rows = slice(h * MLA_V, (h + 1) * MLA_V)
        m_prev = m_ref[qs, h]
        m_new = jnp.maximum(m_prev, jnp.max(s, axis=0, keepdims=True))
        alpha = jnp.exp2(m_prev - m_new)
        p = jnp.exp2(s - m_new[0:1, :]).astype(BF16)
        m_ref[qs, h] = m_new
        pv = _dot(vt_ref[c, h * ATTN_V_ROWS:(h + 1) * ATTN_V_ROWS, :], p)
        l_ref[qs, h] = alpha * l_ref[qs, h] + pv[MLA_V:MLA_V + SUBLANES, :]
        acc_ref[qs, rows, :] = alpha[0:1, :] * acc_ref[qs, rows, :] + pv[:MLA_V, :]

    def iteration(it, carry):
        pending = {}
        for t in range(n_tiles + ATTN_PIPELINE_DEPTH):
            if t < n_tiles:
                pending[t] = scores(it, t)
            if t >= ATTN_PIPELINE_DEPTH:
                accumulate(it, t - ATTN_PIPELINE_DEPTH, pending.pop(t - ATTN_PIPELINE_DEPTH))
        return carry

    lax.fori_loop(0, nchunk // ATTN_CHUNKS_PER_ITER, iteration, 0)

    @pl.when(ki == pl.num_programs(2) - 1)
    def _():
        for qs in range(nsub):
            out_t = jnp.concatenate(
                [acc_ref[qs, h * MLA_V:(h + 1) * MLA_V, :] * (1.0 / l_ref[qs, h, 0:1, :])
                 for h in range(MLA_HEADS)], axis=0)
            o_ref[qs * ATTN_Q_SUB:(qs + 1) * ATTN_Q_SUB, :] = out_t.T.astype(o_ref.dtype)


def _latent_attention(qt, k, vt, B, S):
    tq = min(ATTN_Q_BLOCK, S)
    tk = min(ATTN_KV_BLOCK, S)
    nsub = tq // ATTN_Q_SUB
    qt = qt.reshape(B, S // ATTN_Q_SUB, MLA_HEADS * LANES, ATTN_Q_SUB)
    k = k.reshape(B, S, MLA_HEADS * LANES)
    vt = vt.reshape(B, S // ATTN_CHUNK, MLA_HEADS * ATTN_V_ROWS, ATTN_CHUNK)
    out = pl.pallas_call(
        _flash_kernel,
        grid=(B, S // tq, S // tk),
        in_specs=[pl.BlockSpec((None, nsub, MLA_HEADS * LANES, ATTN_Q_SUB), lambda b, i, j: (b, i, 0, 0)),
                  pl.BlockSpec((None, tk, MLA_HEADS * LANES), lambda b, i, j: (b, j, 0)),
                  pl.BlockSpec((None, tk // ATTN_CHUNK, MLA_HEADS * ATTN_V_ROWS, ATTN_CHUNK),
                               lambda b, i, j: (b, j, 0, 0))],
        out_specs=pl.BlockSpec((None, tq, MLA_HEADS * MLA_V), lambda b, i, j: (b, i, 0)),
        out_shape=jax.ShapeDtypeStruct((B, S, MLA_HEADS * MLA_V), BF16),
        scratch_shapes=[pltpu.VMEM((nsub, MLA_HEADS, SUBLANES, ATTN_Q_SUB), F32),
                        pltpu.VMEM((nsub, MLA_HEADS, SUBLANES, ATTN_Q_SUB), F32),
                        pltpu.VMEM((nsub, MLA_HEADS * MLA_V, ATTN_Q_SUB), F32)],
        compiler_params=_cparams("parallel", "parallel", "arbitrary"),
        name="latent_attention",
    )(qt, k, vt)
    return out.reshape(B * S, MLA_HEADS * MLA_V)


def _dilated_kernel(q_ref, kp_ref, kc_ref, kn_ref, vp_ref, vc_ref, vn_ref, bias_ref, o_ref, lse_ref, *, rows):
    i = pl.program_id(2)
    qb = q_ref.shape[0]
    k_all = jnp.concatenate([kp_ref[...], kc_ref[...], kn_ref[...]], axis=0)
    v_all = jnp.concatenate([vp_ref[...], vc_ref[...], vn_ref[...]], axis=0)
    v_all_t = v_all.astype(F32).T.astype(BF16)
    row = lax.broadcasted_iota(jnp.int32, (3 * DIL_BLOCK, DIL_BLOCK), 0)
    lane = lax.broadcasted_iota(jnp.int32, (1, LANES), 1)
    dim = lax.broadcasted_iota(jnp.int32, (LANES, 1), 0)
    n_sub = qb // DIL_BLOCK
    scores = {}
    for sub in range(n_sub):
        q = q_ref[sub * DIL_BLOCK:(sub + 1) * DIL_BLOCK, :]
        k3 = k_all[sub * DIL_BLOCK:(sub + 3) * DIL_BLOCK, :]
        for h in range(DIL_HEADS):
            head_lanes = jnp.logical_and(lane >= h * DIL_HEAD_DIM, lane < (h + 1) * DIL_HEAD_DIM)
            qh = jnp.where(head_lanes, q, jnp.zeros_like(q))
            scores[sub, h] = lax.dot_general(k3, qh, (((1,), (1,)), ((), ())), preferred_element_type=F32)
    for sub in range(n_sub):
        inside = 1 <= sub <= n_sub - 2
        key_row = i * qb + (sub - 1) * DIL_BLOCK + row
        in_seq = jnp.logical_and(key_row >= 0, key_row < rows)
        v3_t = v_all_t[:, sub * DIL_BLOCK:(sub + 3) * DIL_BLOCK]
        pvs, lses = [], []
        for h in range(DIL_HEADS):
            s = scores.pop((sub, h)) + bias_ref[h]
            s = s if inside else jnp.where(in_seq, s, NEG_BIG)
            m = jnp.max(s, axis=0, keepdims=True)
            e = jnp.exp2(s - m)
            den = jnp.sum(e, axis=0, keepdims=True)
            pvs.append(_dot(v3_t, e.astype(BF16)) * (1.0 / den))
            lses.append((m + jnp.log2(den)) * (1.0 / LOG2_E))
        out_t = jnp.zeros((LANES, DIL_BLOCK), F32)
        lse_t = jnp.zeros((LANES, DIL_BLOCK), F32)
        for h in range(DIL_HEADS):
            head_rows = jnp.logical_and(dim >= h * DIL_HEAD_DIM, dim < (h + 1) * DIL_HEAD_DIM)
            out_t = jnp.where(head_rows, pvs[h], out_t)
            lse_t = jnp.where(head_rows, lses[h], lse_t)
        o_ref[sub * DIL_BLOCK:(sub + 1) * DIL_BLOCK, :] = out_t.T
        lse_ref[sub * DIL_BLOCK:(sub + 1) * DIL_BLOCK, :] = lse_t.T


def _dilated_group(q, k, v, bias_t, g, d, B, S):
    rows = S // d
    qb = min(DIL_Q_BLOCK, rows)
    per = qb // DIL_BLOCK
    last = rows // DIL_BLOCK - 1
    shape = (B, d, rows, LANES)
    q, k, v = q.reshape(shape), k.reshape(shape), v.reshape(shape)
    edge = lambda f: pl.BlockSpec((None, None, DIL_BLOCK, LANES), f)
    main = pl.BlockSpec((None, None, qb, LANES), lambda b, c, i: (b, c, i, 0))
    prev = edge(lambda b, c, i: (b, c, jnp.maximum(i * per - 1, 0), 0))
    nxt = edge(lambda b, c, i: (b, c, jnp.minimum((i + 1) * per, last), 0))
    return pl.pallas_call(
        functools.partial(_dilated_kernel, rows=rows),
        grid=(B, d, rows // qb),
        in_specs=[main, prev, main, nxt, prev, main, nxt, _full((DIL_HEADS, 3 * DIL_BLOCK, DIL_BLOCK))],
        out_specs=[main, main],
        out_shape=[jax.ShapeDtypeStruct(shape, F32)] * 2,
        compiler_params=_cparams("parallel", "parallel", "parallel"),
        name="dilated_attention_g%d" % g,
    )(q, k, k, k, v, v, v, bias_t)


def _t5_bucket(rel):
    nb = REL_BUCKETS // 2
    max_exact = nb // 2
    ret = jnp.where(rel > 0, nb, 0)
    n = jnp.abs(rel)
    nf = jnp.maximum(n, 1).astype(F32)
    large = max_exact + (jnp.log(nf / max_exact) / math.log(REL_MAX_DIST / max_exact) * (nb - max_exact)).astype(jnp.int32)
    large = jnp.minimum(large, nb - 1)
    return ret + jnp.where(n < max_exact, n, large)


def _dilated_bias_tables(rel_bias):
    shape = (3 * DIL_BLOCK, DIL_BLOCK)
    rel = lax.broadcasted_iota(jnp.int32, shape, 0) - DIL_BLOCK - lax.broadcasted_iota(jnp.int32, shape, 1)
    in_band = jnp.abs(rel) <= DIL_SIDE
    steps = jnp.arange(-DIL_SIDE, DIL_SIDE + 1, dtype=jnp.int32)
    band_onehot = (rel[None] == steps[:, None, None]).astype(F32)
    exact = lax.Precision.HIGHEST
    tables = []
    for g, (_, d) in enumerate(DIL_PAIRS):
        bucket_onehot = (_t5_bucket(d * steps)[:, None] == jnp.arange(REL_BUCKETS)[None, :]).astype(F32)
        b = jnp.dot(bucket_onehot, rel_bias[:, g * DIL_HEADS:(g + 1) * DIL_HEADS].astype(F32), precision=exact)
        table = jnp.einsum("nh,nkq->hkq", b, band_onehot, precision=exact)
        tables.append(jnp.where(in_band[None], table * LOG2_E, NEG_BIG))
    return tables


def _merge_kernel(x_ref, fa_ref, sg_ref, oc_ref, o0_ref, l0_ref, o1_ref, l1_ref, o2_ref, l2_ref,
                  wa_ref, wb_ref, wc_ref, wd_ref, wg_ref, bg_ref, wo_ref, g_ref, b_ref, o_ref, nat_ref):
    x = x_ref[...]
    xb = x.astype(BF16)
    tm = x.shape[0]
    for slot, (src, g) in enumerate(((o1_ref, 1), (l1_ref, 1), (o2_ref, 2), (l2_ref, 2))):
        d = DIL_PAIRS[g][1]
        for c in range(d):
            nat_ref[slot, pl.ds(c, tm // d, stride=d), :] = src[c]
    l0, l1, l2 = l0_ref[...], nat_ref[1], nat_ref[3]
    mx = jnp.maximum(jnp.maximum(l0, l1), l2)
    e0, e1, e2 = jnp.exp(l0 - mx), jnp.exp(l1 - mx), jnp.exp(l2 - mx)
    den = e0 + e1 + e2
    od = o0_ref[...] * (e0 / den) + nat_ref[0] * (e1 / den) + nat_ref[2] * (e2 / den)
    n_fixed = 4
    for k1 in range(FOURIER_N1):
        for g in range(F_GROUPS):
            nat_ref[n_fixed + g, pl.ds(k1, tm // FOURIER_N1, stride=FOURIER_N1), :] = (
                fa_ref[k1, :, g * LANES:(g + 1) * LANES])
    fa = jnp.concatenate([nat_ref[n_fixed + g] for g in range(F_GROUPS)], axis=1).astype(BF16)
    branches = ((fa, wa_ref), (sg_ref[...], wb_ref), (oc_ref[...], wc_ref), (od.astype(BF16), wd_ref))
    merged = None
    for i, (act, w_ref) in enumerate(branches):
        cols = slice(i * D_MODEL, (i + 1) * D_MODEL)
        gate = jax.nn.sigmoid(_dot(xb, wg_ref[:, cols]) + bg_ref[:, cols])
        term = gate * _dot(act, w_ref[...])
        merged = term if merged is None else merged + term
    y = DN_ALPHA * x + _dot(merged.astype(BF16), wo_ref[...])
    o_ref[...] = _layer_norm(y, g_ref[...], b_ref[...])


def _merge(x, S, fa, sg, oc, dil, lw):
    T = x.shape[0]
    tm = TOKEN_TILE
    nblk = S // tm
    tok = lambda w: pl.BlockSpec((tm, w), lambda i: (i, 0))
    dil_specs, dil_args = [], []
    for (o, lse), (_, d) in zip(dil, DIL_PAIRS):
        for a in (o, lse):
            if d == 1:
                dil_specs.append(tok(LANES))
                dil_args.append(a.reshape(T, LANES))
            else:
                dil_specs.append(pl.BlockSpec((None, d, tm // d, LANES), lambda i: (i // nblk, 0, i % nblk, 0)))
                dil_args.append(a)
    return pl.pallas_call(
        _merge_kernel,
        grid=(T // tm,),
        in_specs=[tok(D_MODEL),
                  pl.BlockSpec((None, FOURIER_N1, tm // FOURIER_N1, F_WIDTH), lambda i: (i // nblk, 0, i % nblk, 0)),
                  tok(SG_WIDTH), tok(MLA_HEADS * MLA_V)] + dil_specs + [
                  _full((F_WIDTH, D_MODEL)), _full((SG_WIDTH, D_MODEL)), _full((MLA_HEADS * MLA_V, D_MODEL)),
                  _full((DIL_KV_WIDTH, D_MODEL)), _full((D_MODEL, 4 * D_MODEL)), _full((1, 4 * D_MODEL)),
                  _full((D_MODEL, D_MODEL)), _full((1, D_MODEL)), _full((1, D_MODEL))],
        out_specs=tok(D_MODEL),
        out_shape=jax.ShapeDtypeStruct((T, D_MODEL), F32),
        scratch_shapes=[pltpu.VMEM((4 + F_GROUPS, tm, LANES), F32)],
        compiler_params=_cparams("parallel"),
        name="merge",
    )(x, fa, sg, oc, *dil_args, lw["w_a"], lw["w_b"], lw["w_c"], lw["w_d"], lw["w_gate"], lw["b_gate"],
      lw["w_o"], lw["ln1_g"], lw["ln1_b"])


MOE_PAIRS = tuple((a, b) for a in range(MOE_EXPERTS_PER_GROUP) for b in range(a + 1, MOE_EXPERTS_PER_GROUP))
MOE_CLASSES = MOE_GROUPS * len(MOE_PAIRS)
MOE_ROW_TILE = 512
INFO_CLASS, INFO_RANK, INFO_P_LO, INFO_P_HI = 0, 1, 2, 3
MOE_ROW_WIDTH = D_MODEL + LANES


def _route_kernel(x_ref, wr_ref, br_ref, tri_ref, info_ref, counts_ref, run_ref):
    @pl.when(pl.program_id(0) == 0)
    def _():
        run_ref[...] = jnp.zeros(run_ref.shape, F32)

    x = x_ref[...]
    x_hi = x.astype(BF16)
    x_lo = (x - x_hi.astype(F32)).astype(BF16)
    logits = (_dot(x_hi, wr_ref[0]) + (_dot(x_hi, wr_ref[1]) + _dot(x_lo, wr_ref[0]))) + br_ref[...]
    tm = logits.shape[0]
    lane = lax.broadcasted_iota(jnp.int32, (tm, LANES), 1)
    is_g = lane < MOE_GROUPS
    gl = jnp.where(is_g, logits, NEG_BIG)
    gmax = jnp.max(gl, -1, keepdims=True)
    g_top = jnp.min(jnp.where(gl == gmax, lane, LANES), -1, keepdims=True)
    p_group = 1.0 / jnp.sum(jnp.where(is_g, jnp.exp(gl - gmax), 0.0), -1, keepdims=True)
    base = MOE_GROUPS + g_top * MOE_EXPERTS_PER_GROUP
    in_grp = jnp.logical_and(lane >= base, lane < base + MOE_EXPERTS_PER_GROUP)
    el = jnp.where(in_grp, logits, NEG_BIG)
    v1 = jnp.max(el, -1, keepdims=True)
    i1 = jnp.min(jnp.where(el == v1, lane, LANES), -1, keepdims=True)
    el2 = jnp.where(lane == i1, NEG_BIG, el)
    v2 = jnp.max(el2, -1, keepdims=True)
    i2 = jnp.min(jnp.where(el2 == v2, lane, LANES), -1, keepdims=True)
    e2 = jnp.exp(v2 - v1)
    p1 = p_group / (1.0 + e2)
    p2 = p_group * e2 / (1.0 + e2)
    a = jnp.minimum(i1, i2) - base
    b = jnp.maximum(i1, i2) - base
    pair = jnp.where(a == 0, 0, jnp.where(a == 1, 3, 5)) + (b - a - 1)
    cls = g_top * len(MOE_PAIRS) + pair
    first_is_lo = i1 < i2
    p_lo = jnp.where(first_is_lo, p1, p2)
    p_hi = jnp.where(first_is_lo, p2, p1)
    onehot = lane == cls
    before = _dot(tri_ref[...], onehot.astype(BF16)) + run_ref[...]
    rank = jnp.sum(jnp.where(onehot, before, 0.0), -1, keepdims=True)
    run_ref[...] += jnp.sum(onehot.astype(F32), axis=0, keepdims=True)
    counts_ref[...] = run_ref[...]
    info_ref[...] = jnp.where(lane == INFO_CLASS, cls.astype(F32),
                              jnp.where(lane == INFO_RANK, rank,
                                        jnp.where(lane == INFO_P_LO, p_lo,
                                                  jnp.where(lane == INFO_P_HI, p_hi, 0.0))))


ROW_COPY_UNROLL = 8


def _start_row_copies(n, make_copy):
    def start(r, carry):
        make_copy(r).start()
        return carry

    lax.fori_loop(0, n, start, 0, unroll=ROW_COPY_UNROLL)


def _dispatch_kernel(dest_ref, x_ref, info_ref, init_hbm, rows_hbm, row_ref, sem):
    del init_hbm
    i = pl.program_id(0)
    n = pl.num_programs(0)
    tm = x_ref.shape[0]
    slot = i % 2

    def wait_slot(s):
        pltpu.make_async_copy(row_ref.at[s], rows_hbm.at[pl.ds(0, tm)], sem.at[s]).wait()

    @pl.when(i >= 2)
    def _():
        wait_slot(slot)

    row_ref[slot, :, :D_MODEL] = x_ref[...]
    row_ref[slot, :, D_MODEL:] = info_ref[...]
    _start_row_copies(tm, lambda r: pltpu.make_async_copy(
        row_ref.at[slot, pl.ds(r, 1)], rows_hbm.at[pl.ds(dest_ref[0, r], 1)], sem.at[slot]))

    @pl.when(i == n - 1)
    def _():
        wait_slot(slot)

        @pl.when(n > 1)
        def _():
            wait_slot(1 - slot)


def _expert_kernel(ea_ref, eb_ref, nused_ref, rows_ref, wga_ref, wua_ref, wda_ref, wgb_ref, wub_ref, wdb_ref,
                   y_ref):
    del ea_ref, eb_ref
    i = pl.program_id(0)

    @pl.when(i < nused_ref[0])
    def _():
        x = rows_ref[:, :D_MODEL].astype(BF16)
        p_lo = rows_ref[:, D_MODEL + INFO_P_LO:D_MODEL + INFO_P_LO + 1]
        p_hi = rows_ref[:, D_MODEL + INFO_P_HI:D_MODEL + INFO_P_HI + 1]
        ha = jax.nn.silu(_dot(x, wga_ref[...])) * _dot(x, wua_ref[...]) * p_lo
        hb = jax.nn.silu(_dot(x, wgb_ref[...])) * _dot(x, wub_ref[...]) * p_hi
        y_ref[...] = _dot(ha.astype(BF16), wda_ref[...]) + _dot(hb.astype(BF16), wdb_ref[...])

    @pl.when(i >= nused_ref[0])
    def _():
        y_ref[...] = jnp.zeros(y_ref.shape, F32)


def _combine_kernel(dest_ref, next_dest_ref, x_ref, g_ref, b_ref, y_hbm, o_ref, y_ref, sem):
    i = pl.program_id(0)
    n = pl.num_programs(0)
    tm = x_ref.shape[0]
    slot = i % 2

    def fetch(indices_ref, s):
        _start_row_copies(tm, lambda r: pltpu.make_async_copy(
            y_hbm.at[pl.ds(indices_ref[0, r], 1)], y_ref.at[s, pl.ds(r, 1)], sem.at[s]))

    @pl.when(i == 0)
    def _():
        fetch(dest_ref, slot)

    @pl.when(i + 1 < n)
    def _():
        fetch(next_dest_ref, 1 - slot)

    pltpu.make_async_copy(y_hbm.at[pl.ds(0, tm)], y_ref.at[slot], sem.at[slot]).wait()
    o_ref[...] = _layer_norm(DN_ALPHA * x_ref[...] + y_ref[slot], g_ref[...], b_ref[...])


def _moe(x, lw):
    T = x.shape[0]
    tm = TOKEN_TILE
    rt = MOE_ROW_TILE
    n_row_tiles = T // rt + MOE_CLASSES
    n_rows = n_row_tiles * rt
    tok = lambda w: pl.BlockSpec((tm, w), lambda i: (i, 0))

    tri = jnp.asarray(np.tril(np.ones((tm, tm)), -1), BF16)
    info, counts = pl.pallas_call(
        _route_kernel,
        grid=(T // tm,),
        in_specs=[tok(D_MODEL), _full((2, D_MODEL, LANES)), _full((1, LANES)), _full((tm, tm))],
        out_specs=[tok(LANES), _full((1, LANES))],
        out_shape=[jax.ShapeDtypeStruct((T, LANES), F32), jax.ShapeDtypeStruct((1, LANES), F32)],
        scratch_shapes=[pltpu.VMEM((1, LANES), F32)],
        compiler_params=_cparams("arbitrary"),
        name="moe_route",
    )(x, lw["w_router"], lw["b_router"], tri)

    cls = info[:, INFO_CLASS].astype(jnp.int32)
    rank = info[:, INFO_RANK].astype(jnp.int32)
    cnt = counts[0, :MOE_CLASSES].astype(jnp.int32)
    padded = (cnt + rt - 1) // rt * rt
    ends = jnp.cumsum(padded)
    classes = jnp.arange(MOE_CLASSES, dtype=jnp.int32)
    pick = lambda table, idx: jnp.sum(jnp.where(idx[:, None] == classes[None, :], table[None, :], 0), axis=1)
    dest = (pick(ends - padded, cls) + rank).reshape(T // tm, 1, tm)
    tile_start = jnp.arange(n_row_tiles, dtype=jnp.int32) * rt
    tile_cls = jnp.minimum(jnp.sum((tile_start[:, None] >= ends[None, :]).astype(jnp.int32), axis=1),
                           MOE_CLASSES - 1)
    group, pair = np.divmod(np.arange(MOE_CLASSES), len(MOE_PAIRS))
    lo_hi = np.asarray(MOE_PAIRS)[pair]
    ea = pick(jnp.asarray(group * MOE_EXPERTS_PER_GROUP + lo_hi[:, 0], jnp.int32), tile_cls)
    eb = pick(jnp.asarray(group * MOE_EXPERTS_PER_GROUP + lo_hi[:, 1], jnp.int32), tile_cls)
    n_used = (ends[-1:] // rt).astype(jnp.int32)

    dest_spec = pl.BlockSpec((None, 1, tm), lambda i: (i, 0, 0), memory_space=pltpu.SMEM)
    next_dest_spec = pl.BlockSpec((None, 1, tm), lambda i: (jnp.minimum(i + 1, T // tm - 1), 0, 0),
                                  memory_space=pltpu.SMEM)
    hbm = pl.BlockSpec(memory_space=pl.ANY)
    rows = pl.pallas_call(
        _dispatch_kernel,
        grid=(T // tm,),
        in_specs=[dest_spec, tok(D_MODEL), tok(LANES), hbm],
        out_specs=hbm,
        out_shape=jax.ShapeDtypeStruct((n_rows, MOE_ROW_WIDTH), F32),
        scratch_shapes=[pltpu.VMEM((2, tm, MOE_ROW_WIDTH), F32), pltpu.SemaphoreType.DMA((2,))],
        input_output_aliases={3: 0},
        compiler_params=_cparams("arbitrary"),
        name="moe_dispatch",
    )(dest, x, info, jnp.zeros((n_rows, MOE_ROW_WIDTH), F32))

    w_up = lambda sel: pl.BlockSpec((None, D_MODEL, MOE_FF), lambda i, ea, eb, nu: (sel(ea, eb)[i], 0, 0))
    w_dn = lambda sel: pl.BlockSpec((None, MOE_FF, D_MODEL), lambda i, ea, eb, nu: (sel(ea, eb)[i], 0, 0))
    first = lambda ea, eb: ea
    second = lambda ea, eb: eb
    y = pl.pallas_call(
        _expert_kernel,
        grid_spec=pltpu.PrefetchScalarGridSpec(
            num_scalar_prefetch=3,
            grid=(n_row_tiles,),
            in_specs=[pl.BlockSpec((rt, MOE_ROW_WIDTH), lambda i, ea, eb, nu: (i, 0)),
                      w_up(first), w_up(first), w_dn(first), w_up(second), w_up(second), w_dn(second)],
            out_specs=pl.BlockSpec((rt, D_MODEL), lambda i, ea, eb, nu: (i, 0)),
        ),
        out_shape=jax.ShapeDtypeStruct((n_rows, D_MODEL), F32),
        compiler_params=_cparams("arbitrary"),
        name="moe_experts",
    )(ea, eb, n_used, rows, lw["moe_w_gate"], lw["moe_w_up"], lw["moe_w_down"],
      lw["moe_w_gate"], lw["moe_w_up"], lw["moe_w_down"])

    return pl.pallas_call(
        _combine_kernel,
        grid=(T // tm,),
        in_specs=[dest_spec, next_dest_spec, tok(D_MODEL), _full((1, D_MODEL)), _full((1, D_MODEL)), hbm],
        out_specs=tok(D_MODEL),
        out_shape=jax.ShapeDtypeStruct((T, D_MODEL), F32),
        scratch_shapes=[pltpu.VMEM((2, tm, D_MODEL), F32), pltpu.SemaphoreType.DMA((2,))],
        compiler_params=_cparams("arbitrary"),
        name="moe_combine",
    )(dest, dest, x, lw["ln2_g"], lw["ln2_b"], y)


def _rope_tables(S):
    half = MLA_ROPE // 2
    inv = ROPE_BASE ** (-jnp.arange(half, dtype=F32) / half)
    ang = jnp.arange(S, dtype=F32)[:, None] * inv[None, :]
    cos, sin = jnp.cos(ang), jnp.sin(ang)
    one = jnp.ones((S, MLA_NOPE), F32)
    zero = jnp.zeros((S, MLA_NOPE), F32)
    zh = jnp.zeros((S, half), F32)
    rc = jnp.concatenate([one, cos, cos], axis=1)
    rsa = jnp.concatenate([zero, -sin, zh], axis=1)
    rsb = jnp.concatenate([zero, zh, sin], axis=1)
    return rc, rsa, rsb


def _split_bf16(w):
    hi = w.astype(BF16)
    return jnp.stack([hi, (w - hi.astype(F32)).astype(BF16)])


def _prep_layer(l, p):
    row = lambda a: a.reshape(1, -1).astype(F32)
    w_ukv = p["mla_w_ukv"][l].reshape(MLA_KV_RANK, MLA_HEADS, MLA_NOPE + MLA_V)
    pad_rows = ((0, 2 * LANES - MLA_KV_RANK), (0, 0))
    w_k = jnp.pad(w_ukv[:, :, :MLA_NOPE], ((0, 0), (0, 0), (0, LANES - MLA_NOPE))).reshape(MLA_KV_RANK, -1)
    w_v = w_ukv[:, :, MLA_NOPE:].reshape(MLA_KV_RANK, -1)
    w_router = jnp.concatenate([p["moe_w_rg"][l], p["moe_w_re"][l]], axis=1)
    b_router = jnp.concatenate([p["moe_b_rg"][l], p["moe_b_re"][l]])
    npad = LANES - MOE_GROUPS - MOE_EXPERTS
    return {
        "w_in": p["w_in"][l].astype(BF16),
        "sg_ln_g": row(p["sg_ln_g"][l]), "sg_ln_b": row(p["sg_ln_b"][l]),
        "sg_w": p["sg_w"][l].astype(BF16),
        "sg_bias": jnp.repeat(p["sg_b"][l].T, SG_GROUP_DIM, axis=1).astype(F32),
        "q_norm": row(p["mla_q_norm"][l]),
        "kv_norm": jnp.pad(row(p["mla_kv_norm"][l]), ((0, 0), (0, 2 * LANES - MLA_KV_RANK))),
        "w_uq": p["mla_w_uq"][l].astype(BF16),
        "w_k": jnp.pad(w_k, pad_rows).astype(BF16),
        "w_v": jnp.pad(w_v, pad_rows).astype(BF16),
        "w_a": p["w_branch_a"][l].astype(BF16), "w_b": p["w_branch_b"][l].astype(BF16),
        "w_c": p["w_branch_c"][l].astype(BF16), "w_d": p["w_branch_d"][l].astype(BF16),
        "w_gate": p["w_gate"][l].astype(BF16), "b_gate": row(p["b_gate"][l]),
        "w_o": p["w_o"][l].astype(BF16),
        "ln1_g": row(p["ln1_g"][l]), "ln1_b": row(p["ln1_b"][l]),
        "w_router": _split_bf16(jnp.pad(w_router, ((0, 0), (0, npad))).astype(F32)),
        "b_router": jnp.pad(b_router, (0, npad)).reshape(1, -1).astype(F32),
        "moe_w_gate": p["moe_w_gate"][l].astype(BF16), "moe_w_up": p["moe_w_up"][l].astype(BF16),
        "moe_w_down": p["moe_w_down"][l].astype(BF16),
        "ln2_g": row(p["ln2_g"][l]), "ln2_b": row(p["ln2_b"][l]),
    }


def _trunk(x, p, layers, bias_tables):
    B, S, _ = x.shape
    rope = _rope_tables(S)
    fconsts = _fourier_consts(S)
    h = _input_layer_norm(x.reshape(B * S, D_MODEL), p["ln_in_g"], p["ln_in_b"])
    for lw in layers:
        za, sg, q, k, vt, *dil_in = _in_proj(h, B, S, lw, rope)
        fa = _fourier_mix(za, B, S, fconsts)
        oc = _latent_attention(q, k, vt, B, S)
        dil = [_dilated_group(*dil_in[3 * g:3 * g + 3], bias_tables[g], g, d, B, S)
               for g, (_, d) in enumerate(DIL_PAIRS)]
        h = _merge(h, S, fa, sg, oc, dil, lw)
        h = _moe(h, lw)
    return h.reshape(B, S, D_MODEL)


def kernel(x_prompt, x_sample, ln_in_g, ln_in_b, rel_bias, w_in, sg_ln_g, sg_ln_b, sg_w, sg_b, mla_q_norm, mla_kv_norm, mla_w_uq, mla_w_ukv, w_branch_a, w_branch_b, w_branch_c, w_branch_d, w_gate, b_gate, w_o, ln1_g, ln1_b, moe_w_rg, moe_b_rg, moe_w_re, moe_b_re, moe_w_gate, moe_w_up, moe_w_down, ln2_g, ln2_b):
    p = dict(ln_in_g=ln_in_g, ln_in_b=ln_in_b, w_in=w_in, sg_ln_g=sg_ln_g, sg_ln_b=sg_ln_b, sg_w=sg_w, sg_b=sg_b,
             mla_q_norm=mla_q_norm, mla_kv_norm=mla_kv_norm, mla_w_uq=mla_w_uq, mla_w_ukv=mla_w_ukv,
             w_branch_a=w_branch_a, w_branch_b=w_branch_b, w_branch_c=w_branch_c, w_branch_d=w_branch_d,
             w_gate=w_gate, b_gate=b_gate, w_o=w_o, ln1_g=ln1_g, ln1_b=ln1_b,
             moe_w_rg=moe_w_rg, moe_b_rg=moe_b_rg, moe_w_re=moe_w_re, moe_b_re=moe_b_re,
             moe_w_gate=moe_w_gate, moe_w_up=moe_w_up, moe_w_down=moe_w_down, ln2_g=ln2_g, ln2_b=ln2_b)
    layers = [_prep_layer(l, p) for l in range(w_in.shape[0])]
    bias_tables = _dilated_bias_tables(rel_bias)
    return _trunk(x_prompt, p, layers, bias_tables), _trunk(x_sample, p, layers, bias_tables)
```

```python
import functools
import math

import numpy as np
import jax
import jax.numpy as jnp
from jax import lax
from jax.experimental import pallas as pl
from jax.experimental.pallas import tpu as pltpu

F32 = jnp.float32
BF16 = jnp.bfloat16

D_MODEL = 1024
DEPTH = 4
F_GROUPS = 4
F_GROUP_DIM = 128
F_WIDTH = F_GROUPS * F_GROUP_DIM
SG_CHUNK = 128
SG_GROUPS = 4
SG_GROUP_DIM = 64
SG_WIDTH = SG_GROUPS * SG_GROUP_DIM
MLA_HEADS = 8
MLA_Q_RANK = 256
MLA_KV_RANK = 192
MLA_NOPE = 64
MLA_ROPE = 64
MLA_V = 64
MLA_QK_DIM = MLA_NOPE + MLA_ROPE
ROPE_BASE = 10000.0
DIL_PAIRS = ((128, 1), (512, 4), (2048, 16))
DIL_GROUPS = 3
DIL_HEADS = 4
DIL_HEAD_DIM = 32
DIL_Q_WIDTH = DIL_GROUPS * DIL_HEADS * DIL_HEAD_DIM
DIL_KV_WIDTH = DIL_HEADS * DIL_HEAD_DIM
DIL_SIDE = 64
REL_BUCKETS = 32
REL_MAX_DIST = 1024
MIX_WIDTH = F_WIDTH + 2 * SG_WIDTH + MLA_Q_RANK + MLA_KV_RANK + MLA_ROPE + DIL_Q_WIDTH + 2 * DIL_KV_WIDTH
MOE_GROUPS = 4
MOE_EXPERTS_PER_GROUP = 4
MOE_EXPERTS = MOE_GROUPS * MOE_EXPERTS_PER_GROUP
MOE_FF = 512
DN_ALPHA = (2 * DEPTH) ** 0.25
LN_EPS = 1e-5
RMS_EPS = 1e-6

OFF_A = 0
OFF_B = OFF_A + F_WIDTH
OFF_CQ = OFF_B + 2 * SG_WIDTH
OFF_CKV = OFF_CQ + MLA_Q_RANK
OFF_DQ = OFF_CKV + MLA_KV_RANK + MLA_ROPE
OFF_DK = OFF_DQ + DIL_Q_WIDTH
OFF_DV = OFF_DK + DIL_KV_WIDTH

LANES = 128
SUBLANES = 8
VMEM_LIMIT_BYTES = 56 * 1024 * 1024
TOKEN_TILE = 512
ATTN_Q_BLOCK = 512
ATTN_Q_SUB = 256
ATTN_KV_BLOCK = 2048
ATTN_CHUNK = 256
ATTN_CHUNKS_PER_ITER = 8
ATTN_PIPELINE_DEPTH = 6
ATTN_V_ROWS = 80
DIL_BLOCK = 128
DIL_Q_BLOCK = 512
NEG_BIG = -1e30
LOG2_E = math.log2(math.e)


def _cparams(*sem):
    return pltpu.CompilerParams(dimension_semantics=sem, vmem_limit_bytes=VMEM_LIMIT_BYTES)


def _full(shape):
    n = len(shape)
    return pl.BlockSpec(shape, lambda *_: (0,) * n)


def _layer_norm(x, g, b):
    mu = jnp.mean(x, -1, keepdims=True)
    xc = x - mu
    var = jnp.mean(xc * xc, -1, keepdims=True)
    return xc * lax.rsqrt(var + LN_EPS) * g + b


def _dot(a, b):
    return jnp.dot(a, b, preferred_element_type=F32)


def _ln_kernel(x_ref, g_ref, b_ref, o_ref):
    o_ref[...] = _layer_norm(x_ref[...], g_ref[...], b_ref[...])


def _input_layer_norm(x, g, b):
    T = x.shape[0]
    tm = TOKEN_TILE
    return pl.pallas_call(
        _ln_kernel,
        grid=(T // tm,),
        in_specs=[pl.BlockSpec((tm, D_MODEL), lambda i: (i, 0)), _full((1, D_MODEL)), _full((1, D_MODEL))],
        out_specs=pl.BlockSpec((tm, D_MODEL), lambda i: (i, 0)),
        out_shape=jax.ShapeDtypeStruct((T, D_MODEL), F32),
        compiler_params=_cparams("parallel"),
        name="input_layer_norm",
    )(x, g.reshape(1, -1), b.reshape(1, -1))


def _rope_lanes(t, c, sa, sb):
    return t * c + pltpu.roll(t, 96, 1) * sa + pltpu.roll(t, 32, 1) * sb


def _in_proj_kernel(x_ref, w_in_ref, sg_g_ref, sg_b_ref, sg_w_ref, sg_bias_ref, qn_ref, kvn_ref,
                    wuqt_ref, wk_ref, wvt_ref, rc_ref, rsa_ref, rsb_ref, cost_ref, sint_ref,
                    za_ref, sg_ref, qt_ref, k_ref, vt_ref,
                    dq0_ref, dk0_ref, dv0_ref, dq1_ref, dk1_ref, dv1_ref, dq2_ref, dk2_ref, dv2_ref, dil_ref):
    tm = x_ref.shape[0]
    z = _dot(x_ref[...].astype(BF16), w_in_ref[...])

    za_ref[...] = z[:, OFF_A:OFF_A + F_WIDTH]

    zb = jax.nn.gelu(z[:, OFF_B:OFF_B + 2 * SG_WIDTH])
    u = zb[:, :SG_WIDTH]
    vn = _layer_norm(zb[:, SG_WIDTH:], sg_g_ref[...], sg_b_ref[...]).astype(BF16)
    lane = lax.broadcasted_iota(jnp.int32, (SG_CHUNK, LANES), 1)
    low_half = lane < SG_GROUP_DIM
    for ci in range(tm // SG_CHUNK):
        rows = slice(ci * SG_CHUNK, (ci + 1) * SG_CHUNK)
        for j in range(SG_WIDTH // LANES):
            cols = slice(j * LANES, (j + 1) * LANES)
            vblk = vn[rows, cols]
            mixed = jnp.where(low_half, _dot(sg_w_ref[2 * j], vblk), _dot(sg_w_ref[2 * j + 1], vblk))
            sg_ref[rows, cols] = (u[rows, cols] * (mixed + sg_bias_ref[:, cols])).astype(sg_ref.dtype)

    rc, rsa, rsb = rc_ref[...], rsa_ref[...], rsb_ref[...]

    cq = z[:, OFF_CQ:OFF_CQ + MLA_Q_RANK]
    cq = cq * lax.rsqrt(jnp.mean(cq * cq, -1, keepdims=True) + RMS_EPS) * qn_ref[...]
    q_t = _dot(wuqt_ref[...], cq.T.astype(BF16)) * (MLA_QK_DIM ** -0.5 * LOG2_E)
    cos_t, sin_t = cost_ref[...], sint_ref[...]
    half = MLA_ROPE // 2
    pieces = []
    for h in range(MLA_HEADS):
        base = h * MLA_QK_DIM
        x1 = q_t[base + MLA_NOPE:base + MLA_NOPE + half, :]
        x2 = q_t[base + MLA_NOPE + half:base + MLA_QK_DIM, :]
        pieces += [q_t[base:base + MLA_NOPE, :], x1 * cos_t - x2 * sin_t, x1 * sin_t + x2 * cos_t]
    q_t = jnp.concatenate(pieces, axis=0)
    for j in range(tm // ATTN_Q_SUB):
        qt_ref[j] = q_t[:, j * ATTN_Q_SUB:(j + 1) * ATTN_Q_SUB].astype(qt_ref.dtype)

    slab = z[:, OFF_CKV:OFF_CKV + 2 * LANES]
    lane2 = lax.broadcasted_iota(jnp.int32, (tm, 2 * LANES), 1)
    ckv_sq = jnp.where(lane2 < MLA_KV_RANK, slab * slab, 0.0)
    ms = jnp.sum(ckv_sq, -1, keepdims=True) * (1.0 / MLA_KV_RANK)
    ckv = (slab * lax.rsqrt(ms + RMS_EPS) * kvn_ref[...]).astype(BF16)
    k_nope = _dot(ckv, wk_ref[...])
    ckv_t = (slab * lax.rsqrt(ms + RMS_EPS) * kvn_ref[...]).T.astype(BF16)
    v_t = _dot(wvt_ref[...], ckv_t).astype(vt_ref.dtype)
    ones = jnp.ones((ATTN_V_ROWS - MLA_V, ATTN_CHUNK), vt_ref.dtype)
    for j in range(tm // ATTN_CHUNK):
        cols = slice(j * ATTN_CHUNK, (j + 1) * ATTN_CHUNK)
        for h in range(MLA_HEADS):
            vt_ref[j, h * ATTN_V_ROWS:h * ATTN_V_ROWS + MLA_V, :] = v_t[h * MLA_V:(h + 1) * MLA_V, cols]
            vt_ref[j, h * ATTN_V_ROWS + MLA_V:(h + 1) * ATTN_V_ROWS, :] = ones
    kr_slab = _rope_lanes(slab[:, LANES:], rc, rsa, rsb)
    kr_slab = jnp.where(lax.broadcasted_iota(jnp.int32, (tm, LANES), 1) >= MLA_NOPE, kr_slab, 0.0)
    for h in range(MLA_HEADS):
        cols = slice(h * LANES, (h + 1) * LANES)
        k_ref[:, cols] = (k_nope[:, cols] + kr_slab).astype(k_ref.dtype)

    for slab in range(DIL_GROUPS + 2):
        t = z[:, OFF_DQ + slab * LANES:OFF_DQ + (slab + 1) * LANES]
        dil_ref[slab] = t * (DIL_HEAD_DIM ** -0.5 * LOG2_E) if slab < DIL_GROUPS else t
    dq0_ref[...] = dil_ref[0].astype(dq0_ref.dtype)
    dk0_ref[...] = dil_ref[DIL_GROUPS].astype(dk0_ref.dtype)
    dv0_ref[...] = dil_ref[DIL_GROUPS + 1].astype(dv0_ref.dtype)
    for g, (qr, kr, vr) in ((1, (dq1_ref, dk1_ref, dv1_ref)), (2, (dq2_ref, dk2_ref, dv2_ref))):
        d = DIL_PAIRS[g][1]
        for c in range(d):
            rows = pl.ds(c, tm // d, stride=d)
            qr[c] = dil_ref[g, rows, :].astype(qr.dtype)
            kr[c] = dil_ref[DIL_GROUPS, rows, :].astype(kr.dtype)
            vr[c] = dil_ref[DIL_GROUPS + 1, rows, :].astype(vr.dtype)


def _in_proj(x, B, S, lw, rope):
    T = x.shape[0]
    tm = TOKEN_TILE
    nblk = S // tm
    tok = lambda w: pl.BlockSpec((tm, w), lambda i: (i, 0))
    pos = pl.BlockSpec((tm, LANES), lambda i: (i % nblk, 0))
    pos_t = pl.BlockSpec((MLA_ROPE // 2, tm), lambda i: (0, i % nblk))
    out_specs = [tok(F_WIDTH), tok(SG_WIDTH)]
    out_shape = [jax.ShapeDtypeStruct((T, F_WIDTH), F32), jax.ShapeDtypeStruct((T, SG_WIDTH), BF16)]
    out_specs.append(pl.BlockSpec((tm // ATTN_Q_SUB, MLA_HEADS * LANES, ATTN_Q_SUB), lambda i: (i, 0, 0)))
    out_shape.append(jax.ShapeDtypeStruct((T // ATTN_Q_SUB, MLA_HEADS * LANES, ATTN_Q_SUB), BF16))
    out_specs.append(tok(MLA_HEADS * LANES))
    out_shape.append(jax.ShapeDtypeStruct((T, MLA_HEADS * LANES), BF16))
    out_specs.append(pl.BlockSpec((tm // ATTN_CHUNK, MLA_HEADS * ATTN_V_ROWS, ATTN_CHUNK), lambda i: (i, 0, 0)))
    out_shape.append(jax.ShapeDtypeStruct((T // ATTN_CHUNK, MLA_HEADS * ATTN_V_ROWS, ATTN_CHUNK), BF16))
    for _, d in DIL_PAIRS:
        for _ in range(3):
            if d == 1:
                out_specs.append(tok(LANES))
                out_shape.append(jax.ShapeDtypeStruct((T, LANES), BF16))
            else:
                out_specs.append(pl.BlockSpec((None, d, tm // d, LANES), lambda i: (i // nblk, 0, i % nblk, 0)))
                out_shape.append(jax.ShapeDtypeStruct((B, d, S // d, LANES), BF16))
    return pl.pallas_call(
        _in_proj_kernel,
        grid=(T // tm,),
        in_specs=[tok(D_MODEL), _full((D_MODEL, MIX_WIDTH)), _full((1, SG_WIDTH)), _full((1, SG_WIDTH)),
                  _full((SG_GROUPS, SG_CHUNK, SG_CHUNK)), _full((SG_CHUNK, SG_WIDTH)),
                  _full((1, MLA_Q_RANK)), _full((1, 2 * LANES)),
                  _full((MLA_HEADS * LANES, MLA_Q_RANK)), _full((2 * LANES, MLA_HEADS * LANES)),
                  _full((MLA_HEADS * MLA_V, 2 * LANES)), pos, pos, pos, pos_t, pos_t],
        out_specs=out_specs,
        out_shape=out_shape,
        scratch_shapes=[pltpu.VMEM((DIL_GROUPS + 2, tm, LANES), F32)],
        compiler_params=_cparams("parallel"),
        name="in_proj",
    )(x, lw["w_in"], lw["sg_ln_g"], lw["sg_ln_b"], lw["sg_w"], lw["sg_bias"], lw["q_norm"], lw["kv_norm"],
      lw["w_uq"], lw["w_k"], lw["w_v"], *rope)


FOURIER_N1 = 16
FOURIER_ROWS = 16
FOURIER_K1_PER_STEP = 2


def _fourier1_kernel(x_ref, cs_ref, k1_ref, ct_ref, st_ref, gr_ref, gi_ref):
    n1, r, _ = x_ref.shape
    rows = n1 * r
    x = x_ref[...].reshape(rows, F_WIDTH)
    ct, st = ct_ref[...], st_ref[...]
    outs_r, outs_i = [], []
    for g in range(F_GROUPS):
        ab = _dot(x[:, g * LANES:(g + 1) * LANES].astype(BF16), cs_ref[...])
        stacked = jnp.concatenate([ab[:, :LANES], ab[:, LANES:]], axis=0).astype(BF16)
        g2 = _dot(k1_ref[...], stacked)
        gr, gi = g2[:rows], g2[rows:]
        outs_r.append(gr * ct - gi * st)
        outs_i.append(gr * st + gi * ct)
    gr_ref[...] = jnp.concatenate(outs_r, axis=1).reshape(n1, r, F_WIDTH)
    gi_ref[...] = jnp.concatenate(outs_i, axis=1).reshape(n1, r, F_WIDTH)


def _fourier2_kernel(gr_ref, gi_ref, w2_ref, o_ref):
    for j in range(gr_ref.shape[0]):
        stacked = jnp.concatenate([gr_ref[j], gi_ref[j]], axis=0).astype(BF16)
        o_ref[j] = _dot(w2_ref[...], stacked)


def _fourier_mix(za, B, S, consts):
    n1 = FOURIER_N1
    n2 = S // n1
    r = FOURIER_ROWS
    kk = FOURIER_K1_PER_STEP
    cs, k1, ct, st, w2 = consts
    x = za.reshape(B, n1, n2, F_WIDTH)
    blk1 = pl.BlockSpec((None, n1, r, F_WIDTH), lambda b, j: (b, 0, j, 0))
    twid = pl.BlockSpec((None, n1 * r, LANES), lambda b, j: (j, 0, 0))
    gr, gi = pl.pallas_call(
        _fourier1_kernel,
        grid=(B, n2 // r),
        in_specs=[blk1, _full((LANES, 2 * LANES)), _full((2 * n1 * r, 2 * n1 * r)), twid, twid],
        out_specs=[blk1, blk1],
        out_shape=[jax.ShapeDtypeStruct(x.shape, F32)] * 2,
        compiler_params=_cparams("parallel", "parallel"),
        name="fourier_stage1",
    )(x, cs, k1, ct, st)
    blk2 = pl.BlockSpec((None, kk, n2, F_WIDTH), lambda b, j: (b, j, 0, 0))
    return pl.pallas_call(
        _fourier2_kernel,
        grid=(B, n1 // kk),
        in_specs=[blk2, blk2, _full((n2, 2 * n2))],
        out_specs=blk2,
        out_shape=jax.ShapeDtypeStruct((B, n1, n2, F_WIDTH), F32),
        compiler_params=_cparams("parallel", "parallel"),
        name="fourier_stage2",
    )(gr, gi, w2)


def _fourier_consts(S):
    n1 = FOURIER_N1
    n2 = S // n1
    r = FOURIER_ROWS
    c = np.arange(F_GROUP_DIM)
    ang_c = 2.0 * np.pi * np.outer(c, c) / F_GROUP_DIM
    norm = 1.0 / math.sqrt(S * F_GROUP_DIM)
    cs = np.concatenate([np.cos(ang_c), np.sin(ang_c)], axis=1) * norm
    eye = np.eye(r)
    a1 = np.arange(n1)
    ang1 = 2.0 * np.pi * np.outer(a1, a1) / n1
    c1, s1 = np.kron(np.cos(ang1), eye), np.kron(np.sin(ang1), eye)
    k1 = np.block([[c1, -s1], [s1, c1]])
    a2 = np.arange(n2)
    ang_t = 2.0 * np.pi * np.outer(a1, a2) / S

    def twiddle(t):
        t = t.reshape(n1, n2 // r, r).transpose(1, 0, 2).reshape(n2 // r, n1 * r)
        return np.broadcast_to(t[:, :, None], (n2 // r, n1 * r, LANES))

    ang2 = 2.0 * np.pi * np.outer(a2, a2) / n2
    w2 = np.concatenate([np.cos(ang2), -np.sin(ang2)], axis=1)
    return (jnp.asarray(cs, BF16), jnp.asarray(k1, BF16), jnp.asarray(twiddle(np.cos(ang_t)), F32),
            jnp.asarray(twiddle(np.sin(ang_t)), F32), jnp.asarray(w2, BF16))


def _flash_kernel(qt_ref, k_ref, vt_ref, o_ref, m_ref, l_ref, acc_ref):
    ki = pl.program_id(2)
    nsub = qt_ref.shape[0]
    nchunk = k_ref.shape[0] // ATTN_CHUNK

    @pl.when(ki == 0)
    def _():
        m_ref[...] = jnp.full(m_ref.shape, NEG_BIG, F32)
        l_ref[...] = jnp.zeros(l_ref.shape, F32)
        acc_ref[...] = jnp.zeros(acc_ref.shape, F32)

    per_chunk = MLA_HEADS * nsub
    n_tiles = ATTN_CHUNKS_PER_ITER * per_chunk

    def tile_index(it, t):
        cc, rem = divmod(t, per_chunk)
        h, qs = divmod(rem, nsub)
        return it * ATTN_CHUNKS_PER_ITER + cc, h, qs

    def scores(it, t):
        c, h, qs = tile_index(it, t)
        r0 = pl.multiple_of(c * ATTN_CHUNK, ATTN_CHUNK)
        kc = k_ref[pl.ds(r0, ATTN_CHUNK), h * LANES:(h + 1) * LANES]
        return _dot(kc, qt_ref[qs, h * LANES:(h + 1) * LANES, :])

    def accumulate(it, t, s):
        c, h, qs = tile_index(it, t)
        rows = slice(h * MLA_V, (h + 1) * MLA_V)
        m_prev = m_ref[qs, h]
        m_new = jnp.maximum(m_prev, jnp.max(s, axis=0, keepdims=True))
        alpha = jnp.exp2(m_prev - m_new)
        p = jnp.exp2(s - m_new[0:1, :]).astype(BF16)
        m_ref[qs, h] = m_new
        pv = _dot(vt_ref[c, h * ATTN_V_ROWS:(h + 1) * ATTN_V_ROWS, :], p)
        l_ref[qs, h] = alpha * l_ref[qs, h] + pv[MLA_V:MLA_V + SUBLANES, :]
        acc_ref[qs, rows, :] = alpha[0:1, :] * acc_ref[qs, rows, :] + pv[:MLA_V, :]

    def iteration(it, carry):
        pending = {}
        for t in range(n_tiles + ATTN_PIPELINE_DEPTH):
            if t < n_tiles:
                pending[t] = scores(it, t)
            if t >= ATTN_PIPELINE_DEPTH:
                accumulate(it, t - ATTN_PIPELINE_DEPTH, pending.pop(t - ATTN_PIPELINE_DEPTH))
        return carry

    lax.fori_loop(0, nchunk // ATTN_CHUNKS_PER_ITER, iteration, 0)

    @pl.when(ki == pl.num_programs(2) - 1)
    def _():
        for qs in range(nsub):
            out_t = jnp.concatenate(
                [acc_ref[qs, h * MLA_V:(h + 1) * MLA_V, :] * (1.0 / l_ref[qs, h, 0:1, :])
                 for h in range(MLA_HEADS)], axis=0)
            o_ref[qs * ATTN_Q_SUB:(qs + 1) * ATTN_Q_SUB, :] = out_t.T.astype(o_ref.dtype)


def _latent_attention(qt, k, vt, B, S):
    tq = min(ATTN_Q_BLOCK, S)
    tk = min(ATTN_KV_BLOCK, S)
    nsub = tq // ATTN_Q_SUB
    qt = qt.reshape(B, S // ATTN_Q_SUB, MLA_HEADS * LANES, ATTN_Q_SUB)
    k = k.reshape(B, S, MLA_HEADS * LANES)
    vt = vt.reshape(B, S // ATTN_CHUNK, MLA_HEADS * ATTN_V_ROWS, ATTN_CHUNK)
    out = pl.pallas_call(
        _flash_kernel,
        grid=(B, S // tq, S // tk),
        in_specs=[pl.BlockSpec((None, nsub, MLA_HEADS * LANES, ATTN_Q_SUB), lambda b, i, j: (b, i, 0, 0)),
                  pl.BlockSpec((None, tk, MLA_HEADS * LANES), lambda b, i, j: (b, j, 0)),
                  pl.BlockSpec((None, tk // ATTN_CHUNK, MLA_HEADS * ATTN_V_ROWS, ATTN_CHUNK),
                               lambda b, i, j: (b, j, 0, 0))],
        out_specs=pl.BlockSpec((None, tq, MLA_HEADS * MLA_V), lambda b, i, j: (b, i, 0)),
        out_shape=jax.ShapeDtypeStruct((B, S, MLA_HEADS * MLA_V), BF16),
        scratch_shapes=[pltpu.VMEM((nsub, MLA_HEADS, SUBLANES, ATTN_Q_SUB), F32),
                        pltpu.VMEM((nsub, MLA_HEADS, SUBLANES, ATTN_Q_SUB), F32),
                        pltpu.VMEM((nsub, MLA_HEADS * MLA_V, ATTN_Q_SUB), F32)],
        compiler_params=_cparams("parallel", "parallel", "arbitrary"),
        name="latent_attention",
    )(qt, k, vt)
    return out.reshape(B * S, MLA_HEADS * MLA_V)


def _dilated_kernel(q_ref, kp_ref, kc_ref, kn_ref, vp_ref, vc_ref, vn_ref, bias_ref, o_ref, lse_ref, *, rows):
    i = pl.program_id(2)
    qb = q_ref.shape[0]
    k_all = jnp.concatenate([kp_ref[...], kc_ref[...], kn_ref[...]], axis=0)
    v_all = jnp.concatenate([vp_ref[...], vc_ref[...], vn_ref[...]], axis=0)
    v_all_t = v_all.astype(F32).T.astype(BF16)
    row = lax.broadcasted_iota(jnp.int32, (3 * DIL_BLOCK, DIL_BLOCK), 0)
    lane = lax.broadcasted_iota(jnp.int32, (1, LANES), 1)
    dim = lax.broadcasted_iota(jnp.int32, (LANES, 1), 0)
    n_sub = qb // DIL_BLOCK
    scores = {}
    for sub in range(n_sub):
        q = q_ref[sub * DIL_BLOCK:(sub + 1) * DIL_BLOCK, :]
        k3 = k_all[sub * DIL_BLOCK:(sub + 3) * DIL_BLOCK, :]
        for h in range(DIL_HEADS):
            head_lanes = jnp.logical_and(lane >= h * DIL_HEAD_DIM, lane < (h + 1) * DIL_HEAD_DIM)
            qh = jnp.where(head_lanes, q, jnp.zeros_like(q))
            scores[sub, h] = lax.dot_general(k3, qh, (((1,), (1,)), ((), ())), preferred_element_type=F32)
    for sub in range(n_sub):
        inside = 1 <= sub <= n_sub - 2
        key_row = i * qb + (sub - 1) * DIL_BLOCK + row
        in_seq = jnp.logical_and(key_row >= 0, key_row < rows)
        v3_t = v_all_t[:, sub * DIL_BLOCK:(sub + 3) * DIL_BLOCK]
        pvs, lses = [], []
        for h in range(DIL_HEADS):
            s = scores.pop((sub, h)) + bias_ref[h]
            s = s if inside else jnp.where(in_seq, s, NEG_BIG)
            m = jnp.max(s, axis=0, keepdims=True)
            e = jnp.exp2(s - m)
            den = jnp.sum(e, axis=0, keepdims=True)
            pvs.append(_dot(v3_t, e.astype(BF16)) * (1.0 / den))
            lses.append((m + jnp.log2(den)) * (1.0 / LOG2_E))
        out_t = jnp.zeros((LANES, DIL_BLOCK), F32)
        lse_t = jnp.zeros((LANES, DIL_BLOCK), F32)
        for h in range(DIL_HEADS):
            head_rows = jnp.logical_and(dim >= h * DIL_HEAD_DIM, dim < (h + 1) * DIL_HEAD_DIM)
            out_t = jnp.where(head_rows, pvs[h], out_t)
            lse_t = jnp.where(head_rows, lses[h], lse_t)
        o_ref[sub * DIL_BLOCK:(sub + 1) * DIL_BLOCK, :] = out_t.T
        lse_ref[sub * DIL_BLOCK:(sub + 1) * DIL_BLOCK, :] = lse_t.T


def _dilated_group(q, k, v, bias_t, g, d, B, S):
    rows = S // d
    qb = min(DIL_Q_BLOCK, rows)
    per = qb // DIL_BLOCK
    last = rows // DIL_BLOCK - 1
    shape = (B, d, rows, LANES)
    q, k, v = q.reshape(shape), k.reshape(shape), v.reshape(shape)
    edge = lambda f: pl.BlockSpec((None, None, DIL_BLOCK, LANES), f)
    main = pl.BlockSpec((None, None, qb, LANES), lambda b, c, i: (b, c, i, 0))
    prev = edge(lambda b, c, i: (b, c, jnp.maximum(i * per - 1, 0), 0))
    nxt = edge(lambda b, c, i: (b, c, jnp.minimum((i + 1) * per, last), 0))
    return pl.pallas_call(
        functools.partial(_dilated_kernel, rows=rows),
        grid=(B, d, rows // qb),
        in_specs=[main, prev, main, nxt, prev, main, nxt, _full((DIL_HEADS, 3 * DIL_BLOCK, DIL_BLOCK))],
        out_specs=[main, main],
        out_shape=[jax.ShapeDtypeStruct(shape, F32)] * 2,
        compiler_params=_cparams("parallel", "parallel", "parallel"),
        name="dilated_attention_g%d" % g,
    )(q, k, k, k, v, v, v, bias_t)


def _t5_bucket(rel):
    nb = REL_BUCKETS // 2
    max_exact = nb // 2
    ret = jnp.where(rel > 0, nb, 0)
    n = jnp.abs(rel)
    nf = jnp.maximum(n, 1).astype(F32)
    large = max_exact + (jnp.log(nf / max_exact) / math.log(REL_MAX_DIST / max_exact) * (nb - max_exact)).astype(jnp.int32)
    large = jnp.minimum(large, nb - 1)
    return ret + jnp.where(n < max_exact, n, large)


def _dilated_bias_tables(rel_bias):
    shape = (3 * DIL_BLOCK, DIL_BLOCK)
    rel = lax.broadcasted_iota(jnp.int32, shape, 0) - DIL_BLOCK - lax.broadcasted_iota(jnp.int32, shape, 1)
    in_band = jnp.abs(rel) <= DIL_SIDE
    steps = jnp.arange(-DIL_SIDE, DIL_SIDE + 1, dtype=jnp.int32)
    band_onehot = (rel[None] == steps[:, None, None]).astype(F32)
    exact = lax.Precision.HIGHEST
    tables = []
    for g, (_, d) in enumerate(DIL_PAIRS):
        bucket_onehot = (_t5_bucket(d * steps)[:, None] == jnp.arange(REL_BUCKETS)[None, :]).astype(F32)
        b = jnp.dot(bucket_onehot, rel_bias[:, g * DIL_HEADS:(g + 1) * DIL_HEADS].astype(F32), precision=exact)
        table = jnp.einsum("nh,nkq->hkq", b, band_onehot, precision=exact)
        tables.append(jnp.where(in_band[None], table * LOG2_E, NEG_BIG))
    return tables


def _merge_kernel(x_ref, fa_ref, sg_ref, oc_ref, o0_ref, l0_ref, o1_ref, l1_ref, o2_ref, l2_ref,
                  wa_ref, wb_ref, wc_ref, wd_ref, wg_ref, bg_ref, wo_ref, g_ref, b_ref, o_ref, nat_ref):
    x = x_ref[...]
    xb = x.astype(BF16)
    tm = x.shape[0]
    for slot, (src, g) in enumerate(((o1_ref, 1), (l1_ref, 1), (o2_ref, 2), (l2_ref, 2))):
        d = DIL_PAIRS[g][1]
        for c in range(d):
            nat_ref[slot, pl.ds(c, tm // d, stride=d), :] = src[c]
    l0, l1, l2 = l0_ref[...], nat_ref[1], nat_ref[3]
    mx = jnp.maximum(jnp.maximum(l0, l1), l2)
    e0, e1, e2 = jnp.exp(l0 - mx), jnp.exp(l1 - mx), jnp.exp(l2 - mx)
    den = e0 + e1 + e2
    od = o0_ref[...] * (e0 / den) + nat_ref[0] * (e1 / den) + nat_ref[2] * (e2 / den)
    n_fixed = 4
    for k1 in range(FOURIER_N1):
        for g in range(F_GROUPS):
            nat_ref[n_fixed + g, pl.ds(k1, tm // FOURIER_N1, stride=FOURIER_N1), :] = (
                fa_ref[k1, :, g * LANES:(g + 1) * LANES])
    fa = jnp.concatenate([nat_ref[n_fixed + g] for g in range(F_GROUPS)], axis=1).astype(BF16)
    branches = ((fa, wa_ref), (sg_ref[...], wb_ref), (oc_ref[...], wc_ref), (od.astype(BF16), wd_ref))
    merged = None
    for i, (act, w_ref) in enumerate(branches):
        cols = slice(i * D_MODEL, (i + 1) * D_MODEL)
        gate = jax.nn.sigmoid(_dot(xb, wg_ref[:, cols]) + bg_ref[:, cols])
        term = gate * _dot(act, w_ref[...])
        merged = term if merged is None else merged + term
    y = DN_ALPHA * x + _dot(merged.astype(BF16), wo_ref[...])
    o_ref[...] = _layer_norm(y, g_ref[...], b_ref[...])


def _merge(x, S, fa, sg, oc, dil, lw):
    T = x.shape[0]
    tm = TOKEN_TILE
    nblk = S // tm
    tok = lambda w: pl.BlockSpec((tm, w), lambda i: (i, 0))
    dil_specs, dil_args = [], []
    for (o, lse), (_, d) in zip(dil, DIL_PAIRS):
        for a in (o, lse):
            if d == 1:
                dil_specs.append(tok(LANES))
                dil_args.append(a.reshape(T, LANES))
            else:
                dil_specs.append(pl.BlockSpec((None, d, tm // d, LANES), lambda i: (i // nblk, 0, i % nblk, 0)))
                dil_args.append(a)
    return pl.pallas_call(
        _merge_kernel,
        grid=(T // tm,),
        in_specs=[tok(D_MODEL),
                  pl.BlockSpec((None, FOURIER_N1, tm // FOURIER_N1, F_WIDTH), lambda i: (i // nblk, 0, i % nblk, 0)),
                  tok(SG_WIDTH), tok(MLA_HEADS * MLA_V)] + dil_specs + [
                  _full((F_WIDTH, D_MODEL)), _full((SG_WIDTH, D_MODEL)), _full((MLA_HEADS * MLA_V, D_MODEL)),
                  _full((DIL_KV_WIDTH, D_MODEL)), _full((D_MODEL, 4 * D_MODEL)), _full((1, 4 * D_MODEL)),
                  _full((D_MODEL, D_MODEL)), _full((1, D_MODEL)), _full((1, D_MODEL))],
        out_specs=tok(D_MODEL),
        out_shape=jax.ShapeDtypeStruct((T, D_MODEL), F32),
        scratch_shapes=[pltpu.VMEM((4 + F_GROUPS, tm, LANES), F32)],
        compiler_params=_cparams("parallel"),
        name="merge",
    )(x, fa, sg, oc, *dil_args, lw["w_a"], lw["w_b"], lw["w_c"], lw["w_d"], lw["w_gate"], lw["b_gate"],
      lw["w_o"], lw["ln1_g"], lw["ln1_b"])


MOE_PAIRS = tuple((a, b) for a in range(MOE_EXPERTS_PER_GROUP) for b in range(a + 1, MOE_EXPERTS_PER_GROUP))
MOE_CLASSES = MOE_GROUPS * len(MOE_PAIRS)
MOE_ROW_TILE = 512
INFO_CLASS, INFO_RANK, INFO_P_LO, INFO_P_HI = 0, 1, 2, 3
MOE_ROW_WIDTH = D_MODEL + LANES


def _route_kernel(x_ref, wr_ref, br_ref, tri_ref, info_ref, counts_ref, run_ref):
    @pl.when(pl.program_id(0) == 0)
    def _():
        run_ref[...] = jnp.zeros(run_ref.shape, F32)

    x = x_ref[...]
    x_hi = x.astype(BF16)
    x_lo = (x - x_hi.astype(F32)).astype(BF16)
    logits = (_dot(x_hi, wr_ref[0]) + (_dot(x_hi, wr_ref[1]) + _dot(x_lo, wr_ref[0]))) + br_ref[...]
    tm = logits.shape[0]
    lane = lax.broadcasted_iota(jnp.int32, (tm, LANES), 1)
    is_g = lane < MOE_GROUPS
    gl = jnp.where(is_g, logits, NEG_BIG)
    gmax = jnp.max(gl, -1, keepdims=True)
    g_top = jnp.min(jnp.where(gl == gmax, lane, LANES), -1, keepdims=True)
    p_group = 1.0 / jnp.sum(jnp.where(is_g, jnp.exp(gl - gmax), 0.0), -1, keepdims=True)
    base = MOE_GROUPS + g_top * MOE_EXPERTS_PER_GROUP
    in_grp = jnp.logical_and(lane >= base, lane < base + MOE_EXPERTS_PER_GROUP)
    el = jnp.where(in_grp, logits, NEG_BIG)
    v1 = jnp.max(el, -1, keepdims=True)
    i1 = jnp.min(jnp.where(el == v1, lane, LANES), -1, keepdims=True)
    el2 = jnp.where(lane == i1, NEG_BIG, el)
    v2 = jnp.max(el2, -1, keepdims=True)
    i2 = jnp.min(jnp.where(el2 == v2, lane, LANES), -1, keepdims=True)
    e2 = jnp.exp(v2 - v1)
    p1 = p_group / (1.0 + e2)
    p2 = p_group * e2 / (1.0 + e2)
    a = jnp.minimum(i1, i2) - base
    b = jnp.maximum(i1, i2) - base
    pair = jnp.where(a == 0, 0, jnp.where(a == 1, 3, 5)) + (b - a - 1)
    cls = g_top * len(MOE_PAIRS) + pair
    first_is_lo = i1 < i2
    p_lo = jnp.where(first_is_lo, p1, p2)
    p_hi = jnp.where(first_is_lo, p2, p1)
    onehot = lane == cls
    before = _dot(tri_ref[...], onehot.astype(BF16)) + run_ref[...]
    rank = jnp.sum(jnp.where(onehot, before, 0.0), -1, keepdims=True)
    run_ref[...] += jnp.sum(onehot.astype(F32), axis=0, keepdims=True)
    counts_ref[...] = run_ref[...]
    info_ref[...] = jnp.where(lane == INFO_CLASS, cls.astype(F32),
                              jnp.where(lane == INFO_RANK, rank,
                                        jnp.where(lane == INFO_P_LO, p_lo,
                                                  jnp.where(lane == INFO_P_HI, p_hi, 0.0))))


ROW_COPY_UNROLL = 8


def _start_row_copies(n, make_copy):
    def start(r, carry):
        make_copy(r).start()
        return carry

    lax.fori_loop(0, n, start, 0, unroll=ROW_COPY_UNROLL)


def _dispatch_kernel(dest_ref, x_ref, info_ref, init_hbm, rows_hbm, row_ref, sem):
    del init_hbm
    i = pl.program_id(0)
    n = pl.num_programs(0)
    tm = x_ref.shape[0]
    slot = i % 2

    def wait_slot(s):
        pltpu.make_async_copy(row_ref.at[s], rows_hbm.at[pl.ds(0, tm)], sem.at[s]).wait()

    @pl.when(i >= 2)
    def _():
        wait_slot(slot)

    row_ref[slot, :, :D_MODEL] = x_ref[...]
    row_ref[slot, :, D_MODEL:] = info_ref[...]
    _start_row_copies(tm, lambda r: pltpu.make_async_copy(
        row_ref.at[slot, pl.ds(r, 1)], rows_hbm.at[pl.ds(dest_ref[0, r], 1)], sem.at[slot]))

    @pl.when(i == n - 1)
    def _():
        wait_slot(slot)

        @pl.when(n > 1)
        def _():
            wait_slot(1 - slot)


def _expert_kernel(ea_ref, eb_ref, nused_ref, rows_ref, wga_ref, wua_ref, wda_ref, wgb_ref, wub_ref, wdb_ref,
                   y_ref):
    del ea_ref, eb_ref
    i = pl.program_id(0)

    @pl.when(i < nused_ref[0])
    def _():
        x = rows_ref[:, :D_MODEL].astype(BF16)
        p_lo = rows_ref[:, D_MODEL + INFO_P_LO:D_MODEL + INFO_P_LO + 1]
        p_hi = rows_ref[:, D_MODEL + INFO_P_HI:D_MODEL + INFO_P_HI + 1]
        ha = jax.nn.silu(_dot(x, wga_ref[...])) * _dot(x, wua_ref[...]) * p_lo
        hb = jax.nn.silu(_dot(x, wgb_ref[...])) * _dot(x, wub_ref[...]) * p_hi
        y_ref[...] = _dot(ha.astype(BF16), wda_ref[...]) + _dot(hb.astype(BF16), wdb_ref[...])

    @pl.when(i >= nused_ref[0])
    def _():
        y_ref[...] = jnp.zeros(y_ref.shape, F32)


def _combine_kernel(dest_ref, next_dest_ref, x_ref, g_ref, b_ref, y_hbm, o_ref, y_ref, sem):
    i = pl.program_id(0)
    n = pl.num_programs(0)
    tm = x_ref.shape[0]
    slot = i % 2

    def fetch(indices_ref, s):
        _start_row_copies(tm, lambda r: pltpu.make_async_copy(
            y_hbm.at[pl.ds(indices_ref[0, r], 1)], y_ref.at[s, pl.ds(r, 1)], sem.at[s]))

    @pl.when(i == 0)
    def _():
        fetch(dest_ref, slot)

    @pl.when(i + 1 < n)
    def _():
        fetch(next_dest_ref, 1 - slot)

    pltpu.make_async_copy(y_hbm.at[pl.ds(0, tm)], y_ref.at[slot], sem.at[slot]).wait()
    o_ref[...] = _layer_norm(DN_ALPHA * x_ref[...] + y_ref[slot], g_ref[...], b_ref[...])


def _moe(x, lw):
    T = x.shape[0]
    tm = TOKEN_TILE
    rt = MOE_ROW_TILE
    n_row_tiles = T // rt + MOE_CLASSES
    n_rows = n_row_tiles * rt
    tok = lambda w: pl.BlockSpec((tm, w), lambda i: (i, 0))

    tri = jnp.asarray(np.tril(np.ones((tm, tm)), -1), BF16)
    info, counts = pl.pallas_call(
        _route_kernel,
        grid=(T // tm,),
        in_specs=[tok(D_MODEL), _full((2, D_MODEL, LANES)), _full((1, LANES)), _full((tm, tm))],
        out_specs=[tok(LANES), _full((1, LANES))],
        out_shape=[jax.ShapeDtypeStruct((T, LANES), F32), jax.ShapeDtypeStruct((1, LANES), F32)],
        scratch_shapes=[pltpu.VMEM((1, LANES), F32)],
        compiler_params=_cparams("arbitrary"),
        name="moe_route",
    )(x, lw["w_router"], lw["b_router"], tri)

    cls = info[:, INFO_CLASS].astype(jnp.int32)
    rank = info[:, INFO_RANK].astype(jnp.int32)
    cnt = counts[0, :MOE_CLASSES].astype(jnp.int32)
    padded = (cnt + rt - 1) // rt * rt
    ends = jnp.cumsum(padded)
    classes = jnp.arange(MOE_CLASSES, dtype=jnp.int32)
    pick = lambda table, idx: jnp.sum(jnp.where(idx[:, None] == classes[None, :], table[None, :], 0), axis=1)
    dest = (pick(ends - padded, cls) + rank).reshape(T // tm, 1, tm)
    tile_start = jnp.arange(n_row_tiles, dtype=jnp.int32) * rt
    tile_cls = jnp.minimum(jnp.sum((tile_start[:, None] >= ends[None, :]).astype(jnp.int32), axis=1),
                           MOE_CLASSES - 1)
    group, pair = np.divmod(np.arange(MOE_CLASSES), len(MOE_PAIRS))
    lo_hi = np.asarray(MOE_PAIRS)[pair]
    ea = pick(jnp.asarray(group * MOE_EXPERTS_PER_GROUP + lo_hi[:, 0], jnp.int32), tile_cls)
    eb = pick(jnp.asarray(group * MOE_EXPERTS_PER_GROUP + lo_hi[:, 1], jnp.int32), tile_cls)
    n_used = (ends[-1:] // rt).astype(jnp.int32)

    dest_spec = pl.BlockSpec((None, 1, tm), lambda i: (i, 0, 0), memory_space=pltpu.SMEM)
    next_dest_spec = pl.BlockSpec((None, 1, tm), lambda i: (jnp.minimum(i + 1, T // tm - 1), 0, 0),
                                  memory_space=pltpu.SMEM)
    hbm = pl.BlockSpec(memory_space=pl.ANY)
    rows = pl.pallas_call(
        _dispatch_kernel,
        grid=(T // tm,),
        in_specs=[dest_spec, tok(D_MODEL), tok(LANES), hbm],
        out_specs=hbm,
        out_shape=jax.ShapeDtypeStruct((n_rows, MOE_ROW_WIDTH), F32),
        scratch_shapes=[pltpu.VMEM((2, tm, MOE_ROW_WIDTH), F32), pltpu.SemaphoreType.DMA((2,))],
        input_output_aliases={3: 0},
        compiler_params=_cparams("arbitrary"),
        name="moe_dispatch",
    )(dest, x, info, jnp.zeros((n_rows, MOE_ROW_WIDTH), F32))

    w_up = lambda sel: pl.BlockSpec((None, D_MODEL, MOE_FF), lambda i, ea, eb, nu: (sel(ea, eb)[i], 0, 0))
    w_dn = lambda sel: pl.BlockSpec((None, MOE_FF, D_MODEL), lambda i, ea, eb, nu: (sel(ea, eb)[i], 0, 0))
    first = lambda ea, eb: ea
    second = lambda ea, eb: eb
    y = pl.pallas_call(
        _expert_kernel,
        grid_spec=pltpu.PrefetchScalarGridSpec(
            num_scalar_prefetch=3,
            grid=(n_row_tiles,),
            in_specs=[pl.BlockSpec((rt, MOE_ROW_WIDTH), lambda i, ea, eb, nu: (i, 0)),
                      w_up(first), w_up(first), w_dn(first), w_up(second), w_up(second), w_dn(second)],
            out_specs=pl.BlockSpec((rt, D_MODEL), lambda i, ea, eb, nu: (i, 0)),
        ),
        out_shape=jax.ShapeDtypeStruct((n_rows, D_MODEL), F32),
        compiler_params=_cparams("arbitrary"),
        name="moe_experts",
    )(ea, eb, n_used, rows, lw["moe_w_gate"], lw["moe_w_up"], lw["moe_w_down"],
      lw["moe_w_gate"], lw["moe_w_up"], lw["moe_w_down"])

    return pl.pallas_call(
        _combine_kernel,
        grid=(T // tm,),
        in_specs=[dest_spec, next_dest_spec, tok(D_MODEL), _full((1, D_MODEL)), _full((1, D_MODEL)), hbm],
        out_specs=tok(D_MODEL),
        out_shape=jax.ShapeDtypeStruct((T, D_MODEL), F32),
        scratch_shapes=[pltpu.VMEM((2, tm, D_MODEL), F32), pltpu.SemaphoreType.DMA((2,))],
        compiler_params=_cparams("arbitrary"),
        name="moe_combine",
    )(dest, dest, x, lw["ln2_g"], lw["ln2_b"], y)


def _rope_tables(S):
    half = MLA_ROPE // 2
    inv = ROPE_BASE ** (-jnp.arange(half, dtype=F32) / half)
    ang = jnp.arange(S, dtype=F32)[:, None] * inv[None, :]
    cos, sin = jnp.cos(ang), jnp.sin(ang)
    one = jnp.ones((S, MLA_NOPE), F32)
    zero = jnp.zeros((S, MLA_NOPE), F32)
    zh = jnp.zeros((S, half), F32)
    rc = jnp.concatenate([one, cos, cos], axis=1)
    rsa = jnp.concatenate([zero, -sin, zh], axis=1)
    rsb = jnp.concatenate([zero, zh, sin], axis=1)
    return rc, rsa, rsb, cos.T, sin.T


def _split_bf16(w):
    hi = w.astype(BF16)
    return jnp.stack([hi, (w - hi.astype(F32)).astype(BF16)])


def _prep_layer(l, p):
    row = lambda a: a.reshape(1, -1).astype(F32)
    w_ukv = p["mla_w_ukv"][l].reshape(MLA_KV_RANK, MLA_HEADS, MLA_NOPE + MLA_V)
    pad_rows = ((0, 2 * LANES - MLA_KV_RANK), (0, 0))
    w_k = jnp.pad(w_ukv[:, :, :MLA_NOPE], ((0, 0), (0, 0), (0, LANES - MLA_NOPE))).reshape(MLA_KV_RANK, -1)
    w_v = w_ukv[:, :, MLA_NOPE:].reshape(MLA_KV_RANK, -1)
    w_router = jnp.concatenate([p["moe_w_rg"][l], p["moe_w_re"][l]], axis=1)
    b_router = jnp.concatenate([p["moe_b_rg"][l], p["moe_b_re"][l]])
    npad = LANES - MOE_GROUPS - MOE_EXPERTS
    return {
        "w_in": p["w_in"][l].astype(BF16),
        "sg_ln_g": row(p["sg_ln_g"][l]), "sg_ln_b": row(p["sg_ln_b"][l]),
        "sg_w": p["sg_w"][l].astype(BF16),
        "sg_bias": jnp.repeat(p["sg_b"][l].T, SG_GROUP_DIM, axis=1).astype(F32),
        "q_norm": row(p["mla_q_norm"][l]),
        "kv_norm": jnp.pad(row(p["mla_kv_norm"][l]), ((0, 0), (0, 2 * LANES - MLA_KV_RANK))),
        "w_uq": p["mla_w_uq"][l].T.astype(BF16),
        "w_k": jnp.pad(w_k, pad_rows).astype(BF16),
        "w_v": jnp.pad(w_v, pad_rows).T.astype(BF16),
        "w_a": p["w_branch_a"][l].astype(BF16), "w_b": p["w_branch_b"][l].astype(BF16),
        "w_c": p["w_branch_c"][l].astype(BF16), "w_d": p["w_branch_d"][l].astype(BF16),
        "w_gate": p["w_gate"][l].astype(BF16), "b_gate": row(p["b_gate"][l]),
        "w_o": p["w_o"][l].astype(BF16),
        "ln1_g": row(p["ln1_g"][l]), "ln1_b": row(p["ln1_b"][l]),
        "w_router": _split_bf16(jnp.pad(w_router, ((0, 0), (0, npad))).astype(F32)),
        "b_router": jnp.pad(b_router, (0, npad)).reshape(1, -1).astype(F32),
        "moe_w_gate": p["moe_w_gate"][l].astype(BF16), "moe_w_up": p["moe_w_up"][l].astype(BF16),
        "moe_w_down": p["moe_w_down"][l].astype(BF16),
        "ln2_g": row(p["ln2_g"][l]), "ln2_b": row(p["ln2_b"][l]),
    }


def _trunk(x, p, layers, bias_tables):
    B, S, _ = x.shape
    rope = _rope_tables(S)
    fconsts = _fourier_consts(S)
    h = _input_layer_norm(x.reshape(B * S, D_MODEL), p["ln_in_g"], p["ln_in_b"])
    for lw in layers:
        za, sg, q, k, vt, *dil_in = _in_proj(h, B, S, lw, rope)
        fa = _fourier_mix(za, B, S, fconsts)
        oc = _latent_attention(q, k, vt, B, S)
        dil = [_dilated_group(*dil_in[3 * g:3 * g + 3], bias_tables[g], g, d, B, S)
               for g, (_, d) in enumerate(DIL_PAIRS)]
        h = _merge(h, S, fa, sg, oc, dil, lw)
        h = _moe(h, lw)
    return h.reshape(B, S, D_MODEL)


def kernel(x_prompt, x_sample, ln_in_g, ln_in_b, rel_bias, w_in, sg_ln_g, sg_ln_b, sg_w, sg_b, mla_q_norm, mla_kv_norm, mla_w_uq, mla_w_ukv, w_branch_a, w_branch_b, w_branch_c, w_branch_d, w_gate, b_gate, w_o, ln1_g, ln1_b, moe_w_rg, moe_b_rg, moe_w_re, moe_b_re, moe_w_gate, moe_w_up, moe_w_down, ln2_g, ln2_b):
    p = dict(ln_in_g=ln_in_g, ln_in_b=ln_in_b, w_in=w_in, sg_ln_g=sg_ln_g, sg_ln_b=sg_ln_b, sg_w=sg_w, sg_b=sg_b,
             mla_q_norm=mla_q_norm, mla_kv_norm=mla_kv_norm, mla_w_uq=mla_w_uq, mla_w_ukv=mla_w_ukv,
             w_branch_a=w_branch_a, w_branch_b=w_branch_b, w_branch_c=w_branch_c, w_branch_d=w_branch_d,
             w_gate=w_gate, b_gate=b_gate, w_o=w_o, ln1_g=ln1_g, ln1_b=ln1_b,
             moe_w_rg=moe_w_rg, moe_b_rg=moe_b_rg, moe_w_re=moe_w_re, moe_b_re=moe_b_re,
             moe_w_gate=moe_w_gate, moe_w_up=moe_w_up, moe_w_down=moe_w_down, ln2_g=ln2_g, ln2_b=ln2_b)
    layers = [_prep_layer(l, p) for l in range(w_in.shape[0])]
    bias_tables = _dilated_bias_tables(rel_bias)
    return _trunk(x_prompt, p, layers, bias_tables), _trunk(x_sample, p, layers, bias_tables)
```

```python
import functools
import math

import numpy as np
import jax
import jax.numpy as jnp
from jax import lax
from jax.experimental import pallas as pl
from jax.experimental.pallas import tpu as pltpu

F32 = jnp.float32
BF16 = jnp.bfloat16

D_MODEL = 1024
DEPTH = 4
F_GROUPS = 4
F_GROUP_DIM = 128
F_WIDTH = F_GROUPS * F_GROUP_DIM
SG_CHUNK = 128
SG_GROUPS = 4
SG_GROUP_DIM = 64
SG_WIDTH = SG_GROUPS * SG_GROUP_DIM
MLA_HEADS = 8
MLA_Q_RANK = 256
MLA_KV_RANK = 192
MLA_NOPE = 64
MLA_ROPE = 64
MLA_V = 64
MLA_QK_DIM = MLA_NOPE + MLA_ROPE
ROPE_BASE = 10000.0
DIL_PAIRS = ((128, 1), (512, 4), (2048, 16))
DIL_GROUPS = 3
DIL_HEADS = 4
DIL_HEAD_DIM = 32
DIL_Q_WIDTH = DIL_GROUPS * DIL_HEADS * DIL_HEAD_DIM
DIL_KV_WIDTH = DIL_HEADS * DIL_HEAD_DIM
DIL_SIDE = 64
REL_BUCKETS = 32
REL_MAX_DIST = 1024
MIX_WIDTH = F_WIDTH + 2 * SG_WIDTH + MLA_Q_RANK + MLA_KV_RANK + MLA_ROPE + DIL_Q_WIDTH + 2 * DIL_KV_WIDTH
MOE_GROUPS = 4
MOE_EXPERTS_PER_GROUP = 4
MOE_EXPERTS = MOE_GROUPS * MOE_EXPERTS_PER_GROUP
MOE_FF = 512
DN_ALPHA = (2 * DEPTH) ** 0.25
LN_EPS = 1e-5
RMS_EPS = 1e-6

OFF_A = 0
OFF_B = OFF_A + F_WIDTH
OFF_CQ = OFF_B + 2 * SG_WIDTH
OFF_CKV = OFF_CQ + MLA_Q_RANK
OFF_DQ = OFF_CKV + MLA_KV_RANK + MLA_ROPE

LANES = 128
SUBLANES = 8
VMEM_LIMIT_BYTES = 56 * 1024 * 1024
TOKEN_TILE = 512
ATTN_Q_BLOCK = 512
ATTN_Q_SUB = 256
ATTN_KV_BLOCK = 2048
ATTN_CHUNK = 256
ATTN_CHUNKS_PER_ITER = 8
ATTN_PIPELINE_DEPTH = 6
ATTN_V_ROWS = 80
DIL_BLOCK = 128
DIL_Q_BLOCK = 512
NEG_BIG = -1e30
LOG2_E = math.log2(math.e)


def _cparams(*sem):
    return pltpu.CompilerParams(dimension_semantics=sem, vmem_limit_bytes=VMEM_LIMIT_BYTES)


def _full(shape):
    n = len(shape)
    return pl.BlockSpec(shape, lambda *_: (0,) * n)


def _layer_norm(x, g, b):
    mu = jnp.mean(x, -1, keepdims=True)
    xc = x - mu
    var = jnp.mean(xc * xc, -1, keepdims=True)
    return xc * lax.rsqrt(var + LN_EPS) * g + b


def _dot(a, b):
    return jnp.dot(a, b, preferred_element_type=F32)


def _ln_kernel(x_ref, g_ref, b_ref, o_ref):
    o_ref[...] = _layer_norm(x_ref[...], g_ref[...], b_ref[...])


def _input_layer_norm(x, g, b):
    T = x.shape[0]
    tm = TOKEN_TILE
    return pl.pallas_call(
        _ln_kernel,
        grid=(T // tm,),
        in_specs=[pl.BlockSpec((tm, D_MODEL), lambda i: (i, 0)), _full((1, D_MODEL)), _full((1, D_MODEL))],
        out_specs=pl.BlockSpec((tm, D_MODEL), lambda i: (i, 0)),
        out_shape=jax.ShapeDtypeStruct((T, D_MODEL), F32),
        compiler_params=_cparams("parallel"),
        name="input_layer_norm",
    )(x, g.reshape(1, -1), b.reshape(1, -1))


def _rope_lanes(t, c, sa, sb):
    half = MLA_ROPE // 2
    return t * c + pltpu.roll(t, LANES - half, 1) * sa + pltpu.roll(t, half, 1) * sb


def _in_proj_kernel(x_ref, w_in_ref, sg_g_ref, sg_b_ref, sg_w_ref, sg_bias_ref, qn_ref, kvn_ref,
                    wuqt_ref, wk_ref, wvt_ref, rc_ref, rsa_ref, rsb_ref, cost_ref, sint_ref,
                    za_ref, sg_ref, qt_ref, k_ref, vt_ref,
                    dq0_ref, dk0_ref, dv0_ref, dq1_ref, dk1_ref, dv1_ref, dq2_ref, dk2_ref, dv2_ref, dil_ref):
    tm = x_ref.shape[0]
    z = _dot(x_ref[...].astype(BF16), w_in_ref[...])

    za_ref[...] = z[:, OFF_A:OFF_A + F_WIDTH]

    zb = jax.nn.gelu(z[:, OFF_B:OFF_B + 2 * SG_WIDTH])
    u = zb[:, :SG_WIDTH]
    vn = _layer_norm(zb[:, SG_WIDTH:], sg_g_ref[...], sg_b_ref[...]).astype(BF16)
    lane = lax.broadcasted_iota(jnp.int32, (SG_CHUNK, LANES), 1)
    low_half = lane < SG_GROUP_DIM
    for ci in range(tm // SG_CHUNK):
        rows = slice(ci * SG_CHUNK, (ci + 1) * SG_CHUNK)
        for j in range(SG_WIDTH // LANES):
            cols = slice(j * LANES, (j + 1) * LANES)
            vblk = vn[rows, cols]
            mixed = jnp.where(low_half, _dot(sg_w_ref[2 * j], vblk), _dot(sg_w_ref[2 * j + 1], vblk))
            sg_ref[rows, cols] = (u[rows, cols] * (mixed + sg_bias_ref[:, cols])).astype(sg_ref.dtype)

    rc, rsa, rsb = rc_ref[...], rsa_ref[...], rsb_ref[...]

    cq = z[:, OFF_CQ:OFF_CQ + MLA_Q_RANK]
    cq = cq * lax.rsqrt(jnp.mean(cq * cq, -1, keepdims=True) + RMS_EPS) * qn_ref[...]
    q_t = _dot(wuqt_ref[...], cq.T.astype(BF16)) * (MLA_QK_DIM ** -0.5 * LOG2_E)
    cos_t, sin_t = cost_ref[...], sint_ref[...]
    half = MLA_ROPE // 2
    pieces = []
    for h in range(MLA_HEADS):
        base = h * MLA_QK_DIM
        x1 = q_t[base + MLA_NOPE:base + MLA_NOPE + half, :]
        x2 = q_t[base + MLA_NOPE + half:base + MLA_QK_DIM, :]
        pieces += [q_t[base:base + MLA_NOPE, :], x1 * cos_t - x2 * sin_t, x1 * sin_t + x2 * cos_t]
    q_t = jnp.concatenate(pieces, axis=0)
    for j in range(tm // ATTN_Q_SUB):
        qt_ref[j] = q_t[:, j * ATTN_Q_SUB:(j + 1) * ATTN_Q_SUB].astype(qt_ref.dtype)

    slab = z[:, OFF_CKV:OFF_CKV + 2 * LANES]
    lane2 = lax.broadcasted_iota(jnp.int32, (tm, 2 * LANES), 1)
    ckv_sq = jnp.where(lane2 < MLA_KV_RANK, slab * slab, 0.0)
    ms = jnp.sum(ckv_sq, -1, keepdims=True) * (1.0 / MLA_KV_RANK)
    ckv = (slab * lax.rsqrt(ms + RMS_EPS) * kvn_ref[...]).astype(BF16)
    k_nope = _dot(ckv, wk_ref[...])
    ckv_t = (slab * lax.rsqrt(ms + RMS_EPS) * kvn_ref[...]).T.astype(BF16)
    v_t = _dot(wvt_ref[...], ckv_t).astype(vt_ref.dtype)
    ones = jnp.ones((ATTN_V_ROWS - MLA_V, ATTN_CHUNK), vt_ref.dtype)
    for j in range(tm // ATTN_CHUNK):
        cols = slice(j * ATTN_CHUNK, (j + 1) * ATTN_CHUNK)
        for h in range(MLA_HEADS):
            vt_ref[j, h * ATTN_V_ROWS:h * ATTN_V_ROWS + MLA_V, :] = v_t[h * MLA_V:(h + 1) * MLA_V, cols]
            vt_ref[j, h * ATTN_V_ROWS + MLA_V:(h + 1) * ATTN_V_ROWS, :] = ones
    kr_slab = _rope_lanes(slab[:, LANES:], rc, rsa, rsb)
    kr_slab = jnp.where(lax.broadcasted_iota(jnp.int32, (tm, LANES), 1) >= MLA_NOPE, kr_slab, 0.0)
    for h in range(MLA_HEADS):
        cols = slice(h * LANES, (h + 1) * LANES)
        k_ref[:, cols] = (k_nope[:, cols] + kr_slab).astype(k_ref.dtype)

    for slab in range(DIL_GROUPS + 2):
        t = z[:, OFF_DQ + slab * LANES:OFF_DQ + (slab + 1) * LANES]
        dil_ref[slab] = t * (DIL_HEAD_DIM ** -0.5 * LOG2_E) if slab < DIL_GROUPS else t
    dq0_ref[...] = dil_ref[0].astype(dq0_ref.dtype)
    dk0_ref[...] = dil_ref[DIL_GROUPS].astype(dk0_ref.dtype)
    dv0_ref[...] = dil_ref[DIL_GROUPS + 1].astype(dv0_ref.dtype)
    for g, (qr, kr, vr) in ((1, (dq1_ref, dk1_ref, dv1_ref)), (2, (dq2_ref, dk2_ref, dv2_ref))):
        d = DIL_PAIRS[g][1]
        for c in range(d):
            rows = pl.ds(c, tm // d, stride=d)
            qr[c] = dil_ref[g, rows, :].astype(qr.dtype)
            kr[c] = dil_ref[DIL_GROUPS, rows, :].astype(kr.dtype)
            vr[c] = dil_ref[DIL_GROUPS + 1, rows, :].astype(vr.dtype)


def _in_proj(x, B, S, lw, rope):
    T = x.shape[0]
    tm = TOKEN_TILE
    nblk = S // tm
    tok = lambda w: pl.BlockSpec((tm, w), lambda i: (i, 0))
    pos = pl.BlockSpec((tm, LANES), lambda i: (i % nblk, 0))
    pos_t = pl.BlockSpec((MLA_ROPE // 2, tm), lambda i: (0, i % nblk))
    out_specs = [tok(F_WIDTH), tok(SG_WIDTH)]
    out_shape = [jax.ShapeDtypeStruct((T, F_WIDTH), F32), jax.ShapeDtypeStruct((T, SG_WIDTH), BF16)]
    out_specs.append(pl.BlockSpec((tm // ATTN_Q_SUB, MLA_HEADS * LANES, ATTN_Q_SUB), lambda i: (i, 0, 0)))
    out_shape.append(jax.ShapeDtypeStruct((T // ATTN_Q_SUB, MLA_HEADS * LANES, ATTN_Q_SUB), BF16))
    out_specs.append(tok(MLA_HEADS * LANES))
    out_shape.append(jax.ShapeDtypeStruct((T, MLA_HEADS * LANES), BF16))
    out_specs.append(pl.BlockSpec((tm // ATTN_CHUNK, MLA_HEADS * ATTN_V_ROWS, ATTN_CHUNK), lambda i: (i, 0, 0)))
    out_shape.append(jax.ShapeDtypeStruct((T // ATTN_CHUNK, MLA_HEADS * ATTN_V_ROWS, ATTN_CHUNK), BF16))
    for _, d in DIL_PAIRS:
        for _ in range(3):
            if d == 1:
                out_specs.append(tok(LANES))
                out_shape.append(jax.ShapeDtypeStruct((T, LANES), BF16))
            else:
                out_specs.append(pl.BlockSpec((None, d, tm // d, LANES), lambda i: (i // nblk, 0, i % nblk, 0)))
                out_shape.append(jax.ShapeDtypeStruct((B, d, S // d, LANES), BF16))
    return pl.pallas_call(
        _in_proj_kernel,
        grid=(T // tm,),
        in_specs=[tok(D_MODEL), _full((D_MODEL, MIX_WIDTH)), _full((1, SG_WIDTH)), _full((1, SG_WIDTH)),
                  _full((SG_GROUPS, SG_CHUNK, SG_CHUNK)), _full((SG_CHUNK, SG_WIDTH)),
                  _full((1, MLA_Q_RANK)), _full((1, 2 * LANES)),
                  _full((MLA_HEADS * LANES, MLA_Q_RANK)), _full((2 * LANES, MLA_HEADS * LANES)),
                  _full((MLA_HEADS * MLA_V, 2 * LANES)), pos, pos, pos, pos_t, pos_t],
        out_specs=out_specs,
        out_shape=out_shape,
        scratch_shapes=[pltpu.VMEM((DIL_GROUPS + 2, tm, LANES), F32)],
        compiler_params=_cparams("parallel"),
        name="in_proj",
    )(x, lw["w_in"], lw["sg_ln_g"], lw["sg_ln_b"], lw["sg_w"], lw["sg_bias"], lw["q_norm"], lw["kv_norm"],
      lw["w_uq"], lw["w_k"], lw["w_v"], *rope)


FOURIER_N1 = 16
FOURIER_ROWS = 16
FOURIER_K1_PER_STEP = 2


def _fourier1_kernel(x_ref, cs_ref, k1_ref, ct_ref, st_ref, gr_ref, gi_ref):
    n1, r, _ = x_ref.shape
    rows = n1 * r
    x = x_ref[...].reshape(rows, F_WIDTH)
    ct, st = ct_ref[...], st_ref[...]
    outs_r, outs_i = [], []
    for g in range(F_GROUPS):
        ab = _dot(x[:, g * LANES:(g + 1) * LANES].astype(BF16), cs_ref[...])
        stacked = jnp.concatenate([ab[:, :LANES], ab[:, LANES:]], axis=0).astype(BF16)
        g2 = _dot(k1_ref[...], stacked)
        gr, gi = g2[:rows], g2[rows:]
        outs_r.append(gr * ct - gi * st)
        outs_i.append(gr * st + gi * ct)
    gr_ref[...] = jnp.concatenate(outs_r, axis=1).reshape(n1, r, F_WIDTH)
    gi_ref[...] = jnp.concatenate(outs_i, axis=1).reshape(n1, r, F_WIDTH)


def _fourier2_kernel(gr_ref, gi_ref, w2_ref, o_ref):
    for j in range(gr_ref.shape[0]):
        stacked = jnp.concatenate([gr_ref[j], gi_ref[j]], axis=0).astype(BF16)
        o_ref[j] = _dot(w2_ref[...], stacked)


def _fourier_mix(za, B, S, consts):
    n1 = FOURIER_N1
    n2 = S // n1
    r = FOURIER_ROWS
    kk = FOURIER_K1_PER_STEP
    cs, k1, ct, st, w2 = consts
    x = za.reshape(B, n1, n2, F_WIDTH)
    blk1 = pl.BlockSpec((None, n1, r, F_WIDTH), lambda b, j: (b, 0, j, 0))
    twid = pl.BlockSpec((None, n1 * r, LANES), lambda b, j: (j, 0, 0))
    gr, gi = pl.pallas_call(
        _fourier1_kernel,
        grid=(B, n2 // r),
        in_specs=[blk1, _full((LANES, 2 * LANES)), _full((2 * n1 * r, 2 * n1 * r)), twid, twid],
        out_specs=[blk1, blk1],
        out_shape=[jax.ShapeDtypeStruct(x.shape, F32)] * 2,
        compiler_params=_cparams("parallel", "parallel"),
        name="fourier_stage1",
    )(x, cs, k1, ct, st)
    blk2 = pl.BlockSpec((None, kk, n2, F_WIDTH), lambda b, j: (b, j, 0, 0))
    return pl.pallas_call(
        _fourier2_kernel,
        grid=(B, n1 // kk),
        in_specs=[blk2, blk2, _full((n2, 2 * n2))],
        out_specs=blk2,
        out_shape=jax.ShapeDtypeStruct((B, n1, n2, F_WIDTH), F32),
        compiler_params=_cparams("parallel", "parallel"),
        name="fourier_stage2",
    )(gr, gi, w2)


def _fourier_consts(S):
    n1 = FOURIER_N1
    n2 = S // n1
    r = FOURIER_ROWS
    c = np.arange(F_GROUP_DIM)
    ang_c = 2.0 * np.pi * np.outer(c, c) / F_GROUP_DIM
    norm = 1.0 / math.sqrt(S * F_GROUP_DIM)
    cs = np.concatenate([np.cos(ang_c), np.sin(ang_c)], axis=1) * norm
    eye = np.eye(r)
    a1 = np.arange(n1)
    ang1 = 2.0 * np.pi * np.outer(a1, a1) / n1
    c1, s1 = np.kron(np.cos(ang1), eye), np.kron(np.sin(ang1), eye)
    k1 = np.block([[c1, -s1], [s1, c1]])
    a2 = np.arange(n2)
    ang_t = 2.0 * np.pi * np.outer(a1, a2) / S

    def twiddle(t):
        t = t.reshape(n1, n2 // r, r).transpose(1, 0, 2).reshape(n2 // r, n1 * r)
        return np.broadcast_to(t[:, :, None], (n2 // r, n1 * r, LANES))

    ang2 = 2.0 * np.pi * np.outer(a2, a2) / n2
    w2 = np.concatenate([np.cos(ang2), -np.sin(ang2)], axis=1)
    return (jnp.asarray(cs, BF16), jnp.asarray(k1, BF16), jnp.asarray(twiddle(np.cos(ang_t)), F32),
            jnp.asarray(twiddle(np.sin(ang_t)), F32), jnp.asarray(w2, BF16))


def _flash_kernel(qt_ref, k_ref, vt_ref, o_ref, m_ref, l_ref, acc_ref):
    ki = pl.program_id(2)
    nsub = qt_ref.shape[0]
    nchunk = k_ref.shape[0] // ATTN_CHUNK

    @pl.when(ki == 0)
    def _():
        m_ref[...] = jnp.full(m_ref.shape, NEG_BIG, F32)
        l_ref[...] = jnp.zeros(l_ref.shape, F32)
        acc_ref[...] = jnp.zeros(acc_ref.shape, F32)

    per_chunk = MLA_HEADS * nsub
    n_tiles = ATTN_CHUNKS_PER_ITER * per_chunk

    def tile_index(it, t):
        cc, rem = divmod(t, per_chunk)
        h, qs = divmod(rem, nsub)
        return it * ATTN_CHUNKS_PER_ITER + cc, h, qs

    def scores(it, t):
        c, h, qs = tile_index(it, t)
        r0 = pl.multiple_of(c * ATTN_CHUNK, ATTN_CHUNK)
        kc = k_ref[pl.ds(r0, ATTN_CHUNK), h * LANES:(h + 1) * LANES]
        return _dot(kc, qt_ref[qs, h * LANES:(h + 1) * LANES, :])

    def accumulate(it, t, s):
        c, h, qs = tile_index(it, t)
        rows = slice(h * MLA_V, (h + 1) * MLA_V)
        m_prev = m_ref[qs, h]
        m_new = jnp.maximum(m_prev, jnp.max(s, axis=0, keepdims=True))
        alpha = jnp.exp2(m_prev - m_new)
        p = jnp.exp2(s - m_new[0:1, :]).astype(BF16)
        m_ref[qs, h] = m_new
        pv = _dot(vt_ref[c, h * ATTN_V_ROWS:(h + 1) * ATTN_V_ROWS, :], p)
        l_ref[qs, h] = alpha * l_ref[qs, h] + pv[MLA_V:MLA_V + SUBLANES, :]
        acc_ref[qs, rows, :] = alpha[0:1, :] * acc_ref[qs, rows, :] + pv[:MLA_V, :]

    def iteration(it, carry):
        pending = {}
        for t in range(n_tiles + ATTN_PIPELINE_DEPTH):
            if t < n_tiles:
                pending[t] = scores(it, t)
            if t >= ATTN_PIPELINE_DEPTH:
                accumulate(it, t - ATTN_PIPELINE_DEPTH, pending.pop(t - ATTN_PIPELINE_DEPTH))
        return carry

    lax.fori_loop(0, nchunk // ATTN_CHUNKS_PER_ITER, iteration, 0)

    @pl.when(ki == pl.num_programs(2) - 1)
    def _():
        for qs in range(nsub):
            out_t = jnp.concatenate(
                [acc_ref[qs, h * MLA_V:(h + 1) * MLA_V, :] * (1.0 / l_ref[qs, h, 0:1, :])
                 for h in range(MLA_HEADS)], axis=0)
            o_ref[qs * ATTN_Q_SUB:(qs + 1) * ATTN_Q_SUB, :] = out_t.T.astype(o_ref.dtype)


def _latent_attention(qt, k, vt, B, S):
    tq = min(ATTN_Q_BLOCK, S)
    tk = min(ATTN_KV_BLOCK, S)
    nsub = tq // ATTN_Q_SUB
    qt = qt.reshape(B, S // ATTN_Q_SUB, MLA_HEADS * LANES, ATTN_Q_SUB)
    k = k.reshape(B, S, MLA_HEADS * LANES)
    vt = vt.reshape(B, S // ATTN_CHUNK, MLA_HEADS * ATTN_V_ROWS, ATTN_CHUNK)
    out = pl.pallas_call(
        _flash_kernel,
        grid=(B, S // tq, S // tk),
        in_specs=[pl.BlockSpec((None, nsub, MLA_HEADS * LANES, ATTN_Q_SUB), lambda b, i, j: (b, i, 0, 0)),
                  pl.BlockSpec((None, tk, MLA_HEADS * LANES), lambda b, i, j: (b, j, 0)),
                  pl.BlockSpec((None, tk // ATTN_CHUNK, MLA_HEADS * ATTN_V_ROWS, ATTN_CHUNK),
                               lambda b, i, j: (b, j, 0, 0))],
        out_specs=pl.BlockSpec((None, tq, MLA_HEADS * MLA_V), lambda b, i, j: (b, i, 0)),
        out_shape=jax.ShapeDtypeStruct((B, S, MLA_HEADS * MLA_V), BF16),
        scratch_shapes=[pltpu.VMEM((nsub, MLA_HEADS, SUBLANES, ATTN_Q_SUB), F32),
                        pltpu.VMEM((nsub, MLA_HEADS, SUBLANES, ATTN_Q_SUB), F32),
                        pltpu.VMEM((nsub, MLA_HEADS * MLA_V, ATTN_Q_SUB), F32)],
        compiler_params=_cparams("parallel", "parallel", "arbitrary"),
        name="latent_attention",
    )(qt, k, vt)
    return out.reshape(B * S, MLA_HEADS * MLA_V)


def _dilated_kernel(q_ref, kp_ref, kc_ref, kn_ref, vp_ref, vc_ref, vn_ref, bias_ref, o_ref, lse_ref, *, rows):
    i = pl.program_id(2)
    qb = q_ref.shape[0]
    k_all = jnp.concatenate([kp_ref[...], kc_ref[...], kn_ref[...]], axis=0)
    v_all = jnp.concatenate([vp_ref[...], vc_ref[...], vn_ref[...]], axis=0)
    v_all_t = v_all.astype(F32).T.astype(BF16)
    row = lax.broadcasted_iota(jnp.int32, (3 * DIL_BLOCK, DIL_BLOCK), 0)
    lane = lax.broadcasted_iota(jnp.int32, (1, LANES), 1)
    dim = lax.broadcasted_iota(jnp.int32, (LANES, 1), 0)
    n_sub = qb // DIL_BLOCK
    scores = {}
    for sub in range(n_sub):
        q = q_ref[sub * DIL_BLOCK:(sub + 1) * DIL_BLOCK, :]
        k3 = k_all[sub * DIL_BLOCK:(sub + 3) * DIL_BLOCK, :]
        for h in range(DIL_HEADS):
            head_lanes = jnp.logical_and(lane >= h * DIL_HEAD_DIM, lane < (h + 1) * DIL_HEAD_DIM)
            qh = jnp.where(head_lanes, q, jnp.zeros_like(q))
            scores[sub, h] = lax.dot_general(k3, qh, (((1,), (1,)), ((), ())), preferred_element_type=F32)
    for sub in range(n_sub):
        inside = 1 <= sub <= n_sub - 2
        key_row = i * qb + (sub - 1) * DIL_BLOCK + row
        in_seq = jnp.logical_and(key_row >= 0, key_row < rows)
        v3_t = v_all_t[:, sub * DIL_BLOCK:(sub + 3) * DIL_BLOCK]
        pvs, lses = [], []
        for h in range(DIL_HEADS):
            s = scores.pop((sub, h)) + bias_ref[h]
            s = s if inside else jnp.where(in_seq, s, NEG_BIG)
            m = jnp.max(s, axis=0, keepdims=True)
            e = jnp.exp2(s - m)
            den = jnp.sum(e, axis=0, keepdims=True)
            pvs.append(_dot(v3_t, e.astype(BF16)) * (1.0 / den))
            lses.append((m + jnp.log2(den)) * (1.0 / LOG2_E))
        out_t = jnp.zeros((LANES, DIL_BLOCK), F32)
        lse_t = jnp.zeros((LANES, DIL_BLOCK), F32)
        for h in range(DIL_HEADS):
            head_rows = jnp.logical_and(dim >= h * DIL_HEAD_DIM, dim < (h + 1) * DIL_HEAD_DIM)
            out_t = jnp.where(head_rows, pvs[h], out_t)
            lse_t = jnp.where(head_rows, lses[h], lse_t)
        o_ref[sub * DIL_BLOCK:(sub + 1) * DIL_BLOCK, :] = out_t.T
        lse_ref[sub * DIL_BLOCK:(sub + 1) * DIL_BLOCK, :] = lse_t.T


def _dilated_group(q, k, v, bias_t, g, d, B, S):
    rows = S // d
    qb = min(DIL_Q_BLOCK, rows)
    per = qb // DIL_BLOCK
    last = rows // DIL_BLOCK - 1
    shape = (B, d, rows, LANES)
    q, k, v = q.reshape(shape), k.reshape(shape), v.reshape(shape)
    edge = lambda f: pl.BlockSpec((None, None, DIL_BLOCK, LANES), f)
    main = pl.BlockSpec((None, None, qb, LANES), lambda b, c, i: (b, c, i, 0))
    prev = edge(lambda b, c, i: (b, c, jnp.maximum(i * per - 1, 0), 0))
    nxt = edge(lambda b, c, i: (b, c, jnp.minimum((i + 1) * per, last), 0))
    return pl.pallas_call(
        functools.partial(_dilated_kernel, rows=rows),
        grid=(B, d, rows // qb),
        in_specs=[main, prev, main, nxt, prev, main, nxt, _full((DIL_HEADS, 3 * DIL_BLOCK, DIL_BLOCK))],
        out_specs=[main, main],
        out_shape=[jax.ShapeDtypeStruct(shape, F32)] * 2,
        compiler_params=_cparams("parallel", "parallel", "parallel"),
        name="dilated_attention_g%d" % g,
    )(q, k, k, k, v, v, v, bias_t)


def _t5_bucket(rel):
    nb = REL_BUCKETS // 2
    max_exact = nb // 2
    ret = jnp.where(rel > 0, nb, 0)
    n = jnp.abs(rel)
    nf = jnp.maximum(n, 1).astype(F32)
    large = max_exact + (jnp.log(nf / max_exact) / math.log(REL_MAX_DIST / max_exact) * (nb - max_exact)).astype(jnp.int32)
    large = jnp.minimum(large, nb - 1)
    return ret + jnp.where(n < max_exact, n, large)


def _dilated_bias_tables(rel_bias):
    shape = (3 * DIL_BLOCK, DIL_BLOCK)
    rel = lax.broadcasted_iota(jnp.int32, shape, 0) - DIL_BLOCK - lax.broadcasted_iota(jnp.int32, shape, 1)
    in_band = jnp.abs(rel) <= DIL_SIDE
    steps = jnp.arange(-DIL_SIDE, DIL_SIDE + 1, dtype=jnp.int32)
    band_onehot = (rel[None] == steps[:, None, None]).astype(F32)
    exact = lax.Precision.HIGHEST
    tables = []
    for g, (_, d) in enumerate(DIL_PAIRS):
        bucket_onehot = (_t5_bucket(d * steps)[:, None] == jnp.arange(REL_BUCKETS)[None, :]).astype(F32)
        b = jnp.dot(bucket_onehot, rel_bias[:, g * DIL_HEADS:(g + 1) * DIL_HEADS].astype(F32), precision=exact)
        table = jnp.einsum("nh,nkq->hkq", b, band_onehot, precision=exact)
        tables.append(jnp.where(in_band[None], table * LOG2_E, NEG_BIG))
    return tables


MERGE_DIL_SLOTS = 4


def _merge_kernel(x_ref, fa_ref, sg_ref, oc_ref, o0_ref, l0_ref, o1_ref, l1_ref, o2_ref, l2_ref,
                  wa_ref, wb_ref, wc_ref, wd_ref, wg_ref, bg_ref, wo_ref, g_ref, b_ref, o_ref, nat_ref):
    x = x_ref[...]
    xb = x.astype(BF16)
    tm = x.shape[0]
    for slot, (src, g) in enumerate(((o1_ref, 1), (l1_ref, 1), (o2_ref, 2), (l2_ref, 2))):
        d = DIL_PAIRS[g][1]
        for c in range(d):
            nat_ref[slot, pl.ds(c, tm // d, stride=d), :] = src[c]
    l0, l1, l2 = l0_ref[...], nat_ref[1], nat_ref[3]
    mx = jnp.maximum(jnp.maximum(l0, l1), l2)
    e0, e1, e2 = jnp.exp(l0 - mx), jnp.exp(l1 - mx), jnp.exp(l2 - mx)
    den = e0 + e1 + e2
    od = o0_ref[...] * (e0 / den) + nat_ref[0] * (e1 / den) + nat_ref[2] * (e2 / den)
    for k1 in range(FOURIER_N1):
        for g in range(F_GROUPS):
            nat_ref[MERGE_DIL_SLOTS + g, pl.ds(k1, tm // FOURIER_N1, stride=FOURIER_N1), :] = (
                fa_ref[k1, :, g * LANES:(g + 1) * LANES])
    fa = jnp.concatenate([nat_ref[MERGE_DIL_SLOTS + g] for g in range(F_GROUPS)], axis=1).astype(BF16)
    branches = ((fa, wa_ref), (sg_ref[...], wb_ref), (oc_ref[...], wc_ref), (od.astype(BF16), wd_ref))
    merged = None
    for i, (act, w_ref) in enumerate(branches):
        cols = slice(i * D_MODEL, (i + 1) * D_MODEL)
        gate = jax.nn.sigmoid(_dot(xb, wg_ref[:, cols]) + bg_ref[:, cols])
        term = gate * _dot(act, w_ref[...])
        merged = term if merged is None else merged + term
    y = DN_ALPHA * x + _dot(merged.astype(BF16), wo_ref[...])
    o_ref[...] = _layer_norm(y, g_ref[...], b_ref[...])


def _merge(x, S, fa, sg, oc, dil, lw):
    T = x.shape[0]
    tm = TOKEN_TILE
    nblk = S // tm
    tok = lambda w: pl.BlockSpec((tm, w), lambda i: (i, 0))
    dil_specs, dil_args = [], []
    for (o, lse), (_, d) in zip(dil, DIL_PAIRS):
        for a in (o, lse):
            if d == 1:
                dil_specs.append(tok(LANES))
                dil_args.append(a.reshape(T, LANES))
            else:
                dil_specs.append(pl.BlockSpec((None, d, tm // d, LANES), lambda i: (i // nblk, 0, i % nblk, 0)))
                dil_args.append(a)
    return pl.pallas_call(
        _merge_kernel,
        grid=(T // tm,),
        in_specs=[tok(D_MODEL),
                  pl.BlockSpec((None, FOURIER_N1, tm // FOURIER_N1, F_WIDTH), lambda i: (i // nblk, 0, i % nblk, 0)),
                  tok(SG_WIDTH), tok(MLA_HEADS * MLA_V)] + dil_specs + [
                  _full((F_WIDTH, D_MODEL)), _full((SG_WIDTH, D_MODEL)), _full((MLA_HEADS * MLA_V, D_MODEL)),
                  _full((DIL_KV_WIDTH, D_MODEL)), _full((D_MODEL, 4 * D_MODEL)), _full((1, 4 * D_MODEL)),
                  _full((D_MODEL, D_MODEL)), _full((1, D_MODEL)), _full((1, D_MODEL))],
        out_specs=tok(D_MODEL),
        out_shape=jax.ShapeDtypeStruct((T, D_MODEL), F32),
        scratch_shapes=[pltpu.VMEM((MERGE_DIL_SLOTS + F_GROUPS, tm, LANES), F32)],
        compiler_params=_cparams("parallel"),
        name="merge",
    )(x, fa, sg, oc, *dil_args, lw["w_a"], lw["w_b"], lw["w_c"], lw["w_d"], lw["w_gate"], lw["b_gate"],
      lw["w_o"], lw["ln1_g"], lw["ln1_b"])


MOE_PAIRS = tuple((a, b) for a in range(MOE_EXPERTS_PER_GROUP) for b in range(a + 1, MOE_EXPERTS_PER_GROUP))
MOE_CLASSES = MOE_GROUPS * len(MOE_PAIRS)
MOE_ROW_TILE = 512
INFO_CLASS, INFO_RANK, INFO_P_LO, INFO_P_HI = 0, 1, 2, 3
MOE_ROW_WIDTH = D_MODEL + LANES


def _route_kernel(x_ref, wr_ref, br_ref, tri_ref, info_ref, counts_ref, run_ref):
    @pl.when(pl.program_id(0) == 0)
    def _():
        run_ref[...] = jnp.zeros(run_ref.shape, F32)

    x = x_ref[...]
    x_hi = x.astype(BF16)
    x_lo = (x - x_hi.astype(F32)).astype(BF16)
    logits = (_dot(x_hi, wr_ref[0]) + (_dot(x_hi, wr_ref[1]) + _dot(x_lo, wr_ref[0]))) + br_ref[...]
    tm = logits.shape[0]
    lane = lax.broadcasted_iota(jnp.int32, (tm, LANES), 1)
    is_g = lane < MOE_GROUPS
    gl = jnp.where(is_g, logits, NEG_BIG)
    gmax = jnp.max(gl, -1, keepdims=True)
    g_top = jnp.min(jnp.where(gl == gmax, lane, LANES), -1, keepdims=True)
    p_group = 1.0 / jnp.sum(jnp.where(is_g, jnp.exp(gl - gmax), 0.0), -1, keepdims=True)
    base = MOE_GROUPS + g_top * MOE_EXPERTS_PER_GROUP
    in_grp = jnp.logical_and(lane >= base, lane < base + MOE_EXPERTS_PER_GROUP)
    el = jnp.where(in_grp, logits, NEG_BIG)
    v1 = jnp.max(el, -1, keepdims=True)
    i1 = jnp.min(jnp.where(el == v1, lane, LANES), -1, keepdims=True)
    el2 = jnp.where(lane == i1, NEG_BIG, el)
    v2 = jnp.max(el2, -1, keepdims=True)
    i2 = jnp.min(jnp.where(el2 == v2, lane, LANES), -1, keepdims=True)
    e2 = jnp.exp(v2 - v1)
    p1 = p_group / (1.0 + e2)
    p2 = p_group * e2 / (1.0 + e2)
    a = jnp.minimum(i1, i2) - base
    b = jnp.maximum(i1, i2) - base
    pair = jnp.where(a == 0, 0, jnp.where(a == 1, 3, 5)) + (b - a - 1)
    cls = g_top * len(MOE_PAIRS) + pair
    first_is_lo = i1 < i2
    p_lo = jnp.where(first_is_lo, p1, p2)
    p_hi = jnp.where(first_is_lo, p2, p1)
    onehot = lane == cls
    before = _dot(tri_ref[...], onehot.astype(BF16)) + run_ref[...]
    rank = jnp.sum(jnp.where(onehot, before, 0.0), -1, keepdims=True)
    run_ref[...] += jnp.sum(onehot.astype(F32), axis=0, keepdims=True)
    counts_ref[...] = run_ref[...]
    info_ref[...] = jnp.where(lane == INFO_CLASS, cls.astype(F32),
                              jnp.where(lane == INFO_RANK, rank,
                                        jnp.where(lane == INFO_P_LO, p_lo,
                                                  jnp.where(lane == INFO_P_HI, p_hi, 0.0))))


ROW_COPY_UNROLL = 8


def _start_row_copies(n, make_copy):
    def start(r, carry):
        make_copy(r).start()
        return carry

    lax.fori_loop(0, n, start, 0, unroll=ROW_COPY_UNROLL)


def _dispatch_kernel(dest_ref, x_ref, info_ref, init_hbm, rows_hbm, row_ref, sem):
    del init_hbm
    i = pl.program_id(0)
    n = pl.num_programs(0)
    tm = x_ref.shape[0]
    slot = i % 2

    def wait_slot(s):
        pltpu.make_async_copy(row_ref.at[s], rows_hbm.at[pl.ds(0, tm)], sem.at[s]).wait()

    @pl.when(i >= 2)
    def _():
        wait_slot(slot)

    row_ref[slot, :, :D_MODEL] = x_ref[...]
    row_ref[slot, :, D_MODEL:] = info_ref[...]
    _start_row_copies(tm, lambda r: pltpu.make_async_copy(
        row_ref.at[slot, pl.ds(r, 1)], rows_hbm.at[pl.ds(dest_ref[0, r], 1)], sem.at[slot]))

    @pl.when(i == n - 1)
    def _():
        wait_slot(slot)

        @pl.when(n > 1)
        def _():
            wait_slot(1 - slot)


def _expert_kernel(ea_ref, eb_ref, nused_ref, rows_ref, wga_ref, wua_ref, wda_ref, wgb_ref, wub_ref, wdb_ref,
                   y_ref):
    del ea_ref, eb_ref
    i = pl.program_id(0)

    @pl.when(i < nused_ref[0])
    def _():
        x = rows_ref[:, :D_MODEL].astype(BF16)
        p_lo = rows_ref[:, D_MODEL + INFO_P_LO:D_MODEL + INFO_P_LO + 1]
        p_hi = rows_ref[:, D_MODEL + INFO_P_HI:D_MODEL + INFO_P_HI + 1]
        ha = jax.nn.silu(_dot(x, wga_ref[...])) * _dot(x, wua_ref[...]) * p_lo
        hb = jax.nn.silu(_dot(x, wgb_ref[...])) * _dot(x, wub_ref[...]) * p_hi
        y_ref[...] = _dot(ha.astype(BF16), wda_ref[...]) + _dot(hb.astype(BF16), wdb_ref[...])

    @pl.when(i >= nused_ref[0])
    def _():
        y_ref[...] = jnp.zeros(y_ref.shape, F32)


def _combine_kernel(dest_ref, next_dest_ref, x_ref, g_ref, b_ref, y_hbm, o_ref, y_ref, sem):
    i = pl.program_id(0)
    n = pl.num_programs(0)
    tm = x_ref.shape[0]
    slot = i % 2

    def fetch(indices_ref, s):
        _start_row_copies(tm, lambda r: pltpu.make_async_copy(
            y_hbm.at[pl.ds(indices_ref[0, r], 1)], y_ref.at[s, pl.ds(r, 1)], sem.at[s]))

    @pl.when(i == 0)
    def _():
        fetch(dest_ref, slot)

    @pl.when(i + 1 < n)
    def _():
        fetch(next_dest_ref, 1 - slot)

    pltpu.make_async_copy(y_hbm.at[pl.ds(0, tm)], y_ref.at[slot], sem.at[slot]).wait()
    o_ref[...] = _layer_norm(DN_ALPHA * x_ref[...] + y_ref[slot], g_ref[...], b_ref[...])


def _moe(x, lw):
    T = x.shape[0]
    tm = TOKEN_TILE
    rt = MOE_ROW_TILE
    n_row_tiles = T // rt + MOE_CLASSES
    n_rows = n_row_tiles * rt
    tok = lambda w: pl.BlockSpec((tm, w), lambda i: (i, 0))

    tri = jnp.asarray(np.tril(np.ones((tm, tm)), -1), BF16)
    info, counts = pl.pallas_call(
        _route_kernel,
        grid=(T // tm,),
        in_specs=[tok(D_MODEL), _full((2, D_MODEL, LANES)), _full((1, LANES)), _full((tm, tm))],
        out_specs=[tok(LANES), _full((1, LANES))],
        out_shape=[jax.ShapeDtypeStruct((T, LANES), F32), jax.ShapeDtypeStruct((1, LANES), F32)],
        scratch_shapes=[pltpu.VMEM((1, LANES), F32)],
        compiler_params=_cparams("arbitrary"),
        name="moe_route",
    )(x, lw["w_router"], lw["b_router"], tri)

    cls = info[:, INFO_CLASS].astype(jnp.int32)
    rank = info[:, INFO_RANK].astype(jnp.int32)
    cnt = counts[0, :MOE_CLASSES].astype(jnp.int32)
    padded = (cnt + rt - 1) // rt * rt
    ends = jnp.cumsum(padded)
    classes = jnp.arange(MOE_CLASSES, dtype=jnp.int32)
    pick = lambda table, idx: jnp.sum(jnp.where(idx[:, None] == classes[None, :], table[None, :], 0), axis=1)
    dest = (pick(ends - padded, cls) + rank).reshape(T // tm, 1, tm)
    tile_start = jnp.arange(n_row_tiles, dtype=jnp.int32) * rt
    tile_cls = jnp.minimum(jnp.sum((tile_start[:, None] >= ends[None, :]).astype(jnp.int32), axis=1),
                           MOE_CLASSES - 1)
    group, pair = np.divmod(np.arange(MOE_CLASSES), len(MOE_PAIRS))
    lo_hi = np.asarray(MOE_PAIRS)[pair]
    ea = pick(jnp.asarray(group * MOE_EXPERTS_PER_GROUP + lo_hi[:, 0], jnp.int32), tile_cls)
    eb = pick(jnp.asarray(group * MOE_EXPERTS_PER_GROUP + lo_hi[:, 1], jnp.int32), tile_cls)
    n_used = (ends[-1:] // rt).astype(jnp.int32)

    dest_spec = pl.BlockSpec((None, 1, tm), lambda i: (i, 0, 0), memory_space=pltpu.SMEM)
    next_dest_spec = pl.BlockSpec((None, 1, tm), lambda i: (jnp.minimum(i + 1, T // tm - 1), 0, 0),
                                  memory_space=pltpu.SMEM)
    hbm = pl.BlockSpec(memory_space=pl.ANY)
    rows = pl.pallas_call(
        _dispatch_kernel,
        grid=(T // tm,),
        in_specs=[dest_spec, tok(D_MODEL), tok(LANES), hbm],
        out_specs=hbm,
        out_shape=jax.ShapeDtypeStruct((n_rows, MOE_ROW_WIDTH), F32),
        scratch_shapes=[pltpu.VMEM((2, tm, MOE_ROW_WIDTH), F32), pltpu.SemaphoreType.DMA((2,))],
        input_output_aliases={3: 0},
        compiler_params=_cparams("arbitrary"),
        name="moe_dispatch",
    )(dest, x, info, jnp.zeros((n_rows, MOE_ROW_WIDTH), F32))

    w_up = lambda sel: pl.BlockSpec((None, D_MODEL, MOE_FF), lambda i, ea, eb, nu: (sel(ea, eb)[i], 0, 0))
    w_dn = lambda sel: pl.BlockSpec((None, MOE_FF, D_MODEL), lambda i, ea, eb, nu: (sel(ea, eb)[i], 0, 0))
    first = lambda ea, eb: ea
    second = lambda ea, eb: eb
    y = pl.pallas_call(
        _expert_kernel,
        grid_spec=pltpu.PrefetchScalarGridSpec(
            num_scalar_prefetch=3,
            grid=(n_row_tiles,),
            in_specs=[pl.BlockSpec((rt, MOE_ROW_WIDTH), lambda i, ea, eb, nu: (i, 0)),
                      w_up(first), w_up(first), w_dn(first), w_up(second), w_up(second), w_dn(second)],
            out_specs=pl.BlockSpec((rt, D_MODEL), lambda i, ea, eb, nu: (i, 0)),
        ),
        out_shape=jax.ShapeDtypeStruct((n_rows, D_MODEL), F32),
        compiler_params=_cparams("arbitrary"),
        name="moe_experts",
    )(ea, eb, n_used, rows, lw["moe_w_gate"], lw["moe_w_up"], lw["moe_w_down"],
      lw["moe_w_gate"], lw["moe_w_up"], lw["moe_w_down"])

    return pl.pallas_call(
        _combine_kernel,
        grid=(T // tm,),
        in_specs=[dest_spec, next_dest_spec, tok(D_MODEL), _full((1, D_MODEL)), _full((1, D_MODEL)), hbm],
        out_specs=tok(D_MODEL),
        out_shape=jax.ShapeDtypeStruct((T, D_MODEL), F32),
        scratch_shapes=[pltpu.VMEM((2, tm, D_MODEL), F32), pltpu.SemaphoreType.DMA((2,))],
        compiler_params=_cparams("arbitrary"),
        name="moe_combine",
    )(dest, dest, x, lw["ln2_g"], lw["ln2_b"], y)


def _rope_tables(S):
    half = MLA_ROPE // 2
    inv = ROPE_BASE ** (-jnp.arange(half, dtype=F32) / half)
    ang = jnp.arange(S, dtype=F32)[:, None] * inv[None, :]
    cos, sin = jnp.cos(ang), jnp.sin(ang)
    one = jnp.ones((S, MLA_NOPE), F32)
    zero = jnp.zeros((S, MLA_NOPE), F32)
    zh = jnp.zeros((S, half), F32)
    rc = jnp.concatenate([one, cos, cos], axis=1)
    rsa = jnp.concatenate([zero, -sin, zh], axis=1)
    rsb = jnp.concatenate([zero, zh, sin], axis=1)
    return rc, rsa, rsb, cos.T, sin.T


def _split_bf16(w):
    hi = w.astype(BF16)
    return jnp.stack([hi, (w - hi.astype(F32)).astype(BF16)])


def _prep_layer(l, p):
    row = lambda a: a.reshape(1, -1).astype(F32)
    w_ukv = p["mla_w_ukv"][l].reshape(MLA_KV_RANK, MLA_HEADS, MLA_NOPE + MLA_V)
    pad_rows = ((0, 2 * LANES - MLA_KV_RANK), (0, 0))
    w_k = jnp.pad(w_ukv[:, :, :MLA_NOPE], ((0, 0), (0, 0), (0, LANES - MLA_NOPE))).reshape(MLA_KV_RANK, -1)
    w_v = w_ukv[:, :, MLA_NOPE:].reshape(MLA_KV_RANK, -1)
    w_router = jnp.concatenate([p["moe_w_rg"][l], p["moe_w_re"][l]], axis=1)
    b_router = jnp.concatenate([p["moe_b_rg"][l], p["moe_b_re"][l]])
    npad = LANES - MOE_GROUPS - MOE_EXPERTS
    return {
        "w_in": p["w_in"][l].astype(BF16),
        "sg_ln_g": row(p["sg_ln_g"][l]), "sg_ln_b": row(p["sg_ln_b"][l]),
        "sg_w": p["sg_w"][l].astype(BF16),
        "sg_bias": jnp.repeat(p["sg_b"][l].T, SG_GROUP_DIM, axis=1).astype(F32),
        "q_norm": row(p["mla_q_norm"][l]),
        "kv_norm": jnp.pad(row(p["mla_kv_norm"][l]), ((0, 0), (0, 2 * LANES - MLA_KV_RANK))),
        "w_uq": p["mla_w_uq"][l].T.astype(BF16),
        "w_k": jnp.pad(w_k, pad_rows).astype(BF16),
        "w_v": jnp.pad(w_v, pad_rows).T.astype(BF16),
        "w_a": p["w_branch_a"][l].astype(BF16), "w_b": p["w_branch_b"][l].astype(BF16),
        "w_c": p["w_branch_c"][l].astype(BF16), "w_d": p["w_branch_d"][l].astype(BF16),
        "w_gate": p["w_gate"][l].astype(BF16), "b_gate": row(p["b_gate"][l]),
        "w_o": p["w_o"][l].astype(BF16),
        "ln1_g": row(p["ln1_g"][l]), "ln1_b": row(p["ln1_b"][l]),
        "w_router": _split_bf16(jnp.pad(w_router, ((0, 0), (0, npad))).astype(F32)),
        "b_router": jnp.pad(b_router, (0, npad)).reshape(1, -1).astype(F32),
        "moe_w_gate": p["moe_w_gate"][l].astype(BF16), "moe_w_up": p["moe_w_up"][l].astype(BF16),
        "moe_w_down": p["moe_w_down"][l].astype(BF16),
        "ln2_g": row(p["ln2_g"][l]), "ln2_b": row(p["ln2_b"][l]),
    }


def _trunk(x, p, layers, bias_tables):
    B, S, _ = x.shape
    rope = _rope_tables(S)
    fconsts = _fourier_consts(S)
    h = _input_layer_norm(x.reshape(B * S, D_MODEL), p["ln_in_g"], p["ln_in_b"])
    for lw in layers:
        za, sg, q, k, vt, *dil_in = _in_proj(h, B, S, lw, rope)
        fa = _fourier_mix(za, B, S, fconsts)
        oc = _latent_attention(q, k, vt, B, S)
        dil = [_dilated_group(*dil_in[3 * g:3 * g + 3], bias_tables[g], g, d, B, S)
               for g, (_, d) in enumerate(DIL_PAIRS)]
        h = _merge(h, S, fa, sg, oc, dil, lw)
        h = _moe(h, lw)
    return h.reshape(B, S, D_MODEL)


def kernel(x_prompt, x_sample, ln_in_g, ln_in_b, rel_bias, w_in, sg_ln_g, sg_ln_b, sg_w, sg_b, mla_q_norm, mla_kv_norm, mla_w_uq, mla_w_ukv, w_branch_a, w_branch_b, w_branch_c, w_branch_d, w_gate, b_gate, w_o, ln1_g, ln1_b, moe_w_rg, moe_b_rg, moe_w_re, moe_b_re, moe_w_gate, moe_w_up, moe_w_down, ln2_g, ln2_b):
    p = dict(ln_in_g=ln_in_g, ln_in_b=ln_in_b, w_in=w_in, sg_ln_g=sg_ln_g, sg_ln_b=sg_ln_b, sg_w=sg_w, sg_b=sg_b,
             mla_q_norm=mla_q_norm, mla_kv_norm=mla_kv_norm, mla_w_uq=mla_w_uq, mla_w_ukv=mla_w_ukv,
             w_branch_a=w_branch_a, w_branch_b=w_branch_b, w_branch_c=w_branch_c, w_branch_d=w_branch_d,
             w_gate=w_gate, b_gate=b_gate, w_o=w_o, ln1_g=ln1_g, ln1_b=ln1_b,
             moe_w_rg=moe_w_rg, moe_b_rg=moe_b_rg, moe_w_re=moe_w_re, moe_b_re=moe_b_re,
             moe_w_gate=moe_w_gate, moe_w_up=moe_w_up, moe_w_down=moe_w_down, ln2_g=ln2_g, ln2_b=ln2_b)
    layers = [_prep_layer(l, p) for l in range(w_in.shape[0])]
    bias_tables = _dilated_bias_tables(rel_bias)
    return _trunk(x_prompt, p, layers, bias_tables), _trunk(x_sample, p, layers, bias_tables)
```

```python
import functools
import math

import numpy as np
import jax
import jax.numpy as jnp
from jax import lax
from jax.experimental import pallas as pl
from jax.experimental.pallas import tpu as pltpu

F32 = jnp.float32
BF16 = jnp.bfloat16

D_MODEL = 1024
DEPTH = 4
F_GROUPS = 4
F_GROUP_DIM = 128
F_WIDTH = F_GROUPS * F_GROUP_DIM
SG_CHUNK = 128
SG_GROUPS = 4
SG_GROUP_DIM = 64
SG_WIDTH = SG_GROUPS * SG_GROUP_DIM
MLA_HEADS = 8
MLA_Q_RANK = 256
MLA_KV_RANK = 192
MLA_NOPE = 64
MLA_ROPE = 64
MLA_V = 64
MLA_QK_DIM = MLA_NOPE + MLA_ROPE
ROPE_BASE = 10000.0
DIL_PAIRS = ((128, 1), (512, 4), (2048, 16))
DIL_GROUPS = 3
DIL_HEADS = 4
DIL_HEAD_DIM = 32
DIL_Q_WIDTH = DIL_GROUPS * DIL_HEADS * DIL_HEAD_DIM
DIL_KV_WIDTH = DIL_HEADS * DIL_HEAD_DIM
DIL_SIDE = 64
REL_BUCKETS = 32
REL_MAX_DIST = 1024
MIX_WIDTH = F_WIDTH + 2 * SG_WIDTH + MLA_Q_RANK + MLA_KV_RANK + MLA_ROPE + DIL_Q_WIDTH + 2 * DIL_KV_WIDTH
MOE_GROUPS = 4
MOE_EXPERTS_PER_GROUP = 4
MOE_EXPERTS = MOE_GROUPS * MOE_EXPERTS_PER_GROUP
MOE_FF = 512
DN_ALPHA = (2 * DEPTH) ** 0.25
LN_EPS = 1e-5
RMS_EPS = 1e-6

OFF_A = 0
OFF_B = OFF_A + F_WIDTH
OFF_CQ = OFF_B + 2 * SG_WIDTH
OFF_CKV = OFF_CQ + MLA_Q_RANK
OFF_DQ = OFF_CKV + MLA_KV_RANK + MLA_ROPE

LANES = 128
SUBLANES = 8
VMEM_LIMIT_BYTES = 56 * 1024 * 1024
TOKEN_TILE = 512
ATTN_Q_BLOCK = 512
ATTN_Q_SUB = 256
ATTN_KV_BLOCK = 2048
ATTN_CHUNK = 256
ATTN_CHUNKS_PER_ITER = 8
ATTN_PIPELINE_DEPTH = 6
ATTN_V_ROWS = 80
DIL_BLOCK = 128
DIL_Q_BLOCK = 1024
NEG_BIG = -1e30
LOG2_E = math.log2(math.e)


def _cparams(*sem):
    return pltpu.CompilerParams(dimension_semantics=sem, vmem_limit_bytes=VMEM_LIMIT_BYTES)


def _full(shape):
    n = len(shape)
    return pl.BlockSpec(shape, lambda *_: (0,) * n)


def _layer_norm(x, g, b):
    mu = jnp.mean(x, -1, keepdims=True)
    xc = x - mu
    var = jnp.mean(xc * xc, -1, keepdims=True)
    return xc * lax.rsqrt(var + LN_EPS) * g + b


def _dot(a, b):
    return jnp.dot(a, b, preferred_element_type=F32)


def _ln_kernel(x_ref, g_ref, b_ref, o_ref):
    o_ref[...] = _layer_norm(x_ref[...], g_ref[...], b_ref[...])


def _input_layer_norm(x, g, b):
    T = x.shape[0]
    tm = TOKEN_TILE
    return pl.pallas_call(
        _ln_kernel,
        grid=(T // tm,),
        in_specs=[pl.BlockSpec((tm, D_MODEL), lambda i: (i, 0)), _full((1, D_MODEL)), _full((1, D_MODEL))],
        out_specs=pl.BlockSpec((tm, D_MODEL), lambda i: (i, 0)),
        out_shape=jax.ShapeDtypeStruct((T, D_MODEL), F32),
        compiler_params=_cparams("parallel"),
        name="input_layer_norm",
    )(x, g.reshape(1, -1), b.reshape(1, -1))


def _rope_lanes(t, c, sa, sb):
    half = MLA_ROPE // 2
    return t * c + pltpu.roll(t, LANES - half, 1) * sa + pltpu.roll(t, half, 1) * sb


def _in_proj_kernel(x_ref, w_in_ref, sg_g_ref, sg_b_ref, sg_w_ref, sg_bias_ref, qn_ref, kvn_ref,
                    wuqt_ref, wk_ref, wvt_ref, rc_ref, rsa_ref, rsb_ref, cost_ref, sint_ref,
                    za_ref, sg_ref, qt_ref, k_ref, vt_ref,
                    dq0_ref, dk0_ref, dv0_ref, dq1_ref, dk1_ref, dv1_ref, dq2_ref, dk2_ref, dv2_ref, dil_ref):
    tm = x_ref.shape[0]
    z = _dot(x_ref[...].astype(BF16), w_in_ref[...])

    za_ref[...] = z[:, OFF_A:OFF_A + F_WIDTH]

    zb = jax.nn.gelu(z[:, OFF_B:OFF_B + 2 * SG_WIDTH])
    u = zb[:, :SG_WIDTH]
    vn = _layer_norm(zb[:, SG_WIDTH:], sg_g_ref[...], sg_b_ref[...]).astype(BF16)
    lane = lax.broadcasted_iota(jnp.int32, (SG_CHUNK, LANES), 1)
    low_half = lane < SG_GROUP_DIM
    for ci in range(tm // SG_CHUNK):
        rows = slice(ci * SG_CHUNK, (ci + 1) * SG_CHUNK)
        for j in range(SG_WIDTH // LANES):
            cols = slice(j * LANES, (j + 1) * LANES)
            vblk = vn[rows, cols]
            mixed = jnp.where(low_half, _dot(sg_w_ref[2 * j], vblk), _dot(sg_w_ref[2 * j + 1], vblk))
            sg_ref[rows, cols] = (u[rows, cols] * (mixed + sg_bias_ref[:, cols])).astype(sg_ref.dtype)

    rc, rsa, rsb = rc_ref[...], rsa_ref[...], rsb_ref[...]

    cq = z[:, OFF_CQ:OFF_CQ + MLA_Q_RANK]
    cq = cq * lax.rsqrt(jnp.mean(cq * cq, -1, keepdims=True) + RMS_EPS) * qn_ref[...]
    q_t = _dot(wuqt_ref[...], cq.T.astype(BF16)) * (MLA_QK_DIM ** -0.5 * LOG2_E)
    cos_t, sin_t = cost_ref[...], sint_ref[...]
    half = MLA_ROPE // 2
    pieces = []
    for h in range(MLA_HEADS):
        base = h * MLA_QK_DIM
        x1 = q_t[base + MLA_NOPE:base + MLA_NOPE + half, :]
        x2 = q_t[base + MLA_NOPE + half:base + MLA_QK_DIM, :]
        pieces += [q_t[base:base + MLA_NOPE, :], x1 * cos_t - x2 * sin_t, x1 * sin_t + x2 * cos_t]
    q_t = jnp.concatenate(pieces, axis=0)
    for j in range(tm // ATTN_Q_SUB):
        qt_ref[j] = q_t[:, j * ATTN_Q_SUB:(j + 1) * ATTN_Q_SUB].astype(qt_ref.dtype)

    slab = z[:, OFF_CKV:OFF_CKV + 2 * LANES]
    lane2 = lax.broadcasted_iota(jnp.int32, (tm, 2 * LANES), 1)
    ckv_sq = jnp.where(lane2 < MLA_KV_RANK, slab * slab, 0.0)
    ms = jnp.sum(ckv_sq, -1, keepdims=True) * (1.0 / MLA_KV_RANK)
    ckv = (slab * lax.rsqrt(ms + RMS_EPS) * kvn_ref[...]).astype(BF16)
    k_nope = _dot(ckv, wk_ref[...])
    ckv_t = (slab * lax.rsqrt(ms + RMS_EPS) * kvn_ref[...]).T.astype(BF16)
    v_t = _dot(wvt_ref[...], ckv_t).astype(vt_ref.dtype)
    ones = jnp.ones((ATTN_V_ROWS - MLA_V, ATTN_CHUNK), vt_ref.dtype)
    for j in range(tm // ATTN_CHUNK):
        cols = slice(j * ATTN_CHUNK, (j + 1) * ATTN_CHUNK)
        for h in range(MLA_HEADS):
            vt_ref[j, h * ATTN_V_ROWS:h * ATTN_V_ROWS + MLA_V, :] = v_t[h * MLA_V:(h + 1) * MLA_V, cols]
            vt_ref[j, h * ATTN_V_ROWS + MLA_V:(h + 1) * ATTN_V_ROWS, :] = ones
    kr_slab = _rope_lanes(slab[:, LANES:], rc, rsa, rsb)
    kr_slab = jnp.where(lax.broadcasted_iota(jnp.int32, (tm, LANES), 1) >= MLA_NOPE, kr_slab, 0.0)
    for h in range(MLA_HEADS):
        cols = slice(h * LANES, (h + 1) * LANES)
        k_ref[:, cols] = (k_nope[:, cols] + kr_slab).astype(k_ref.dtype)

    for slab in range(DIL_GROUPS + 2):
        t = z[:, OFF_DQ + slab * LANES:OFF_DQ + (slab + 1) * LANES]
        dil_ref[slab] = t * (DIL_HEAD_DIM ** -0.5 * LOG2_E) if slab < DIL_GROUPS else t
    dq0_ref[...] = dil_ref[0].astype(dq0_ref.dtype)
    dk0_ref[...] = dil_ref[DIL_GROUPS].astype(dk0_ref.dtype)
    dv0_ref[...] = dil_ref[DIL_GROUPS + 1].astype(dv0_ref.dtype)
    for g, (qr, kr, vr) in ((1, (dq1_ref, dk1_ref, dv1_ref)), (2, (dq2_ref, dk2_ref, dv2_ref))):
        d = DIL_PAIRS[g][1]
        for c in range(d):
            rows = pl.ds(c, tm // d, stride=d)
            qr[c] = dil_ref[g, rows, :].astype(qr.dtype)
            kr[c] = dil_ref[DIL_GROUPS, rows, :].astype(kr.dtype)
            vr[c] = dil_ref[DIL_GROUPS + 1, rows, :].astype(vr.dtype)


def _in_proj(x, B, S, lw, rope):
    T = x.shape[0]
    tm = TOKEN_TILE
    nblk = S // tm
    tok = lambda w: pl.BlockSpec((tm, w), lambda i: (i, 0))
    pos = pl.BlockSpec((tm, LANES), lambda i: (i % nblk, 0))
    pos_t = pl.BlockSpec((MLA_ROPE // 2, tm), lambda i: (0, i % nblk))
    out_specs = [tok(F_WIDTH), tok(SG_WIDTH)]
    out_shape = [jax.ShapeDtypeStruct((T, F_WIDTH), F32), jax.ShapeDtypeStruct((T, SG_WIDTH), BF16)]
    out_specs.append(pl.BlockSpec((tm // ATTN_Q_SUB, MLA_HEADS * LANES, ATTN_Q_SUB), lambda i: (i, 0, 0)))
    out_shape.append(jax.ShapeDtypeStruct((T // ATTN_Q_SUB, MLA_HEADS * LANES, ATTN_Q_SUB), BF16))
    out_specs.append(tok(MLA_HEADS * LANES))
    out_shape.append(jax.ShapeDtypeStruct((T, MLA_HEADS * LANES), BF16))
    out_specs.append(pl.BlockSpec((tm // ATTN_CHUNK, MLA_HEADS * ATTN_V_ROWS, ATTN_CHUNK), lambda i: (i, 0, 0)))
    out_shape.append(jax.ShapeDtypeStruct((T // ATTN_CHUNK, MLA_HEADS * ATTN_V_ROWS, ATTN_CHUNK), BF16))
    for _, d in DIL_PAIRS:
        for _ in range(3):
            if d == 1:
                out_specs.append(tok(LANES))
                out_shape.append(jax.ShapeDtypeStruct((T, LANES), BF16))
            else:
                out_specs.append(pl.BlockSpec((None, d, tm // d, LANES), lambda i: (i // nblk, 0, i % nblk, 0)))
                out_shape.append(jax.ShapeDtypeStruct((B, d, S // d, LANES), BF16))
    return pl.pallas_call(
        _in_proj_kernel,
        grid=(T // tm,),
        in_specs=[tok(D_MODEL), _full((D_MODEL, MIX_WIDTH)), _full((1, SG_WIDTH)), _full((1, SG_WIDTH)),
                  _full((SG_GROUPS, SG_CHUNK, SG_CHUNK)), _full((SG_CHUNK, SG_WIDTH)),
                  _full((1, MLA_Q_RANK)), _full((1, 2 * LANES)),
                  _full((MLA_HEADS * LANES, MLA_Q_RANK)), _full((2 * LANES, MLA_HEADS * LANES)),
                  _full((MLA_HEADS * MLA_V, 2 * LANES)), pos, pos, pos, pos_t, pos_t],
        out_specs=out_specs,
        out_shape=out_shape,
        scratch_shapes=[pltpu.VMEM((DIL_GROUPS + 2, tm, LANES), F32)],
        compiler_params=_cparams("parallel"),
        name="in_proj",
    )(x, lw["w_in"], lw["sg_ln_g"], lw["sg_ln_b"], lw["sg_w"], lw["sg_bias"], lw["q_norm"], lw["kv_norm"],
      lw["w_uq"], lw["w_k"], lw["w_v"], *rope)


FOURIER_N1 = 16
FOURIER_ROWS = 16
FOURIER_K1_PER_STEP = 2


def _fourier1_kernel(x_ref, cs_ref, k1_ref, ct_ref, st_ref, gr_ref, gi_ref):
    n1, r, _ = x_ref.shape
    rows = n1 * r
    x = x_ref[...].reshape(rows, F_WIDTH)
    ct, st = ct_ref[...], st_ref[...]
    outs_r, outs_i = [], []
    for g in range(F_GROUPS):
        ab = _dot(x[:, g * LANES:(g + 1) * LANES].astype(BF16), cs_ref[...])
        stacked = jnp.concatenate([ab[:, :LANES], ab[:, LANES:]], axis=0).astype(BF16)
        g2 = _dot(k1_ref[...], stacked)
        gr, gi = g2[:rows], g2[rows:]
        outs_r.append(gr * ct - gi * st)
        outs_i.append(gr * st + gi * ct)
    gr_ref[...] = jnp.concatenate(outs_r, axis=1).reshape(n1, r, F_WIDTH)
    gi_ref[...] = jnp.concatenate(outs_i, axis=1).reshape(n1, r, F_WIDTH)


def _fourier2_kernel(gr_ref, gi_ref, w2_ref, o_ref):
    for j in range(gr_ref.shape[0]):
        stacked = jnp.concatenate([gr_ref[j], gi_ref[j]], axis=0).astype(BF16)
        o_ref[j] = _dot(w2_ref[...], stacked)


def _fourier_mix(za, B, S, consts):
    n1 = FOURIER_N1
    n2 = S // n1
    r = FOURIER_ROWS
    kk = FOURIER_K1_PER_STEP
    cs, k1, ct, st, w2 = consts
    x = za.reshape(B, n1, n2, F_WIDTH)
    blk1 = pl.BlockSpec((None, n1, r, F_WIDTH), lambda b, j: (b, 0, j, 0))
    twid = pl.BlockSpec((None, n1 * r, LANES), lambda b, j: (j, 0, 0))
    gr, gi = pl.pallas_call(
        _fourier1_kernel,
        grid=(B, n2 // r),
        in_specs=[blk1, _full((LANES, 2 * LANES)), _full((2 * n1 * r, 2 * n1 * r)), twid, twid],
        out_specs=[blk1, blk1],
        out_shape=[jax.ShapeDtypeStruct(x.shape, F32)] * 2,
        compiler_params=_cparams("parallel", "parallel"),
        name="fourier_stage1",
    )(x, cs, k1, ct, st)
    blk2 = pl.BlockSpec((None, kk, n2, F_WIDTH), lambda b, j: (b, j, 0, 0))
    return pl.pallas_call(
        _fourier2_kernel,
        grid=(B, n1 // kk),
        in_specs=[blk2, blk2, _full((n2, 2 * n2))],
        out_specs=blk2,
        out_shape=jax.ShapeDtypeStruct((B, n1, n2, F_WIDTH), F32),
        compiler_params=_cparams("parallel", "parallel"),
        name="fourier_stage2",
    )(gr, gi, w2)


def _fourier_consts(S):
    n1 = FOURIER_N1
    n2 = S // n1
    r = FOURIER_ROWS
    c = np.arange(F_GROUP_DIM)
    ang_c = 2.0 * np.pi * np.outer(c, c) / F_GROUP_DIM
    norm = 1.0 / math.sqrt(S * F_GROUP_DIM)
    cs = np.concatenate([np.cos(ang_c), np.sin(ang_c)], axis=1) * norm
    eye = np.eye(r)
    a1 = np.arange(n1)
    ang1 = 2.0 * np.pi * np.outer(a1, a1) / n1
    c1, s1 = np.kron(np.cos(ang1), eye), np.kron(np.sin(ang1), eye)
    k1 = np.block([[c1, -s1], [s1, c1]])
    a2 = np.arange(n2)
    ang_t = 2.0 * np.pi * np.outer(a1, a2) / S

    def twiddle(t):
        t = t.reshape(n1, n2 // r, r).transpose(1, 0, 2).reshape(n2 // r, n1 * r)
        return np.broadcast_to(t[:, :, None], (n2 // r, n1 * r, LANES))

    ang2 = 2.0 * np.pi * np.outer(a2, a2) / n2
    w2 = np.concatenate([np.cos(ang2), -np.sin(ang2)], axis=1)
    return (jnp.asarray(cs, BF16), jnp.asarray(k1, BF16), jnp.asarray(twiddle(np.cos(ang_t)), F32),
            jnp.asarray(twiddle(np.sin(ang_t)), F32), jnp.asarray(w2, BF16))


def _flash_kernel(qt_ref, k_ref, vt_ref, o_ref, m_ref, l_ref, acc_ref):
    ki = pl.program_id(2)
    nsub = qt_ref.shape[0]
    nchunk = k_ref.shape[0] // ATTN_CHUNK

    @pl.when(ki == 0)
    def _():
        m_ref[...] = jnp.full(m_ref.shape, NEG_BIG, F32)
        l_ref[...] = jnp.zeros(l_ref.shape, F32)
        acc_ref[...] = jnp.zeros(acc_ref.shape, F32)

    per_chunk = MLA_HEADS * nsub
    n_tiles = ATTN_CHUNKS_PER_ITER * per_chunk

    def tile_index(it, t):
        cc, rem = divmod(t, per_chunk)
        h, qs = divmod(rem, nsub)
        return it * ATTN_CHUNKS_PER_ITER + cc, h, qs

    def scores(it, t):
        c, h, qs = tile_index(it, t)
        r0 = pl.multiple_of(c * ATTN_CHUNK, ATTN_CHUNK)
        kc = k_ref[pl.ds(r0, ATTN_CHUNK), h * LANES:(h + 1) * LANES]
        return _dot(kc, qt_ref[qs, h * LANES:(h + 1) * LANES, :])

    def accumulate(it, t, s):
        c, h, qs = tile_index(it, t)
        rows = slice(h * MLA_V, (h + 1) * MLA_V)
        m_prev = m_ref[qs, h]
        m_new = jnp.maximum(m_prev, jnp.max(s, axis=0, keepdims=True))
        alpha = jnp.exp2(m_prev - m_new)
        p = jnp.exp2(s - m_new[0:1, :]).astype(BF16)
        m_ref[qs, h] = m_new
        pv = _dot(vt_ref[c, h * ATTN_V_ROWS:(h + 1) * ATTN_V_ROWS, :], p)
        l_ref[qs, h] = alpha * l_ref[qs, h] + pv[MLA_V:MLA_V + SUBLANES, :]
        acc_ref[qs, rows, :] = alpha[0:1, :] * acc_ref[qs, rows, :] + pv[:MLA_V, :]

    def iteration(it, carry):
        pending = {}
        for t in range(n_tiles + ATTN_PIPELINE_DEPTH):
            if t < n_tiles:
                pending[t] = scores(it, t)
            if t >= ATTN_PIPELINE_DEPTH:
                accumulate(it, t - ATTN_PIPELINE_DEPTH, pending.pop(t - ATTN_PIPELINE_DEPTH))
        return carry

    lax.fori_loop(0, nchunk // ATTN_CHUNKS_PER_ITER, iteration, 0)

    @pl.when(ki == pl.num_programs(2) - 1)
    def _():
        for qs in range(nsub):
            out_t = jnp.concatenate(
                [acc_ref[qs, h * MLA_V:(h + 1) * MLA_V, :] * (1.0 / l_ref[qs, h, 0:1, :])
                 for h in range(MLA_HEADS)], axis=0)
            o_ref[qs * ATTN_Q_SUB:(qs + 1) * ATTN_Q_SUB, :] = out_t.T.astype(o_ref.dtype)


def _latent_attention(qt, k, vt, B, S):
    tq = min(ATTN_Q_BLOCK, S)
    tk = min(ATTN_KV_BLOCK, S)
    nsub = tq // ATTN_Q_SUB
    qt = qt.reshape(B, S // ATTN_Q_SUB, MLA_HEADS * LANES, ATTN_Q_SUB)
    k = k.reshape(B, S, MLA_HEADS * LANES)
    vt = vt.reshape(B, S // ATTN_CHUNK, MLA_HEADS * ATTN_V_ROWS, ATTN_CHUNK)
    out = pl.pallas_call(
        _flash_kernel,
        grid=(B, S // tq, S // tk),
        in_specs=[pl.BlockSpec((None, nsub, MLA_HEADS * LANES, ATTN_Q_SUB), lambda b, i, j: (b, i, 0, 0)),
                  pl.BlockSpec((None, tk, MLA_HEADS * LANES), lambda b, i, j: (b, j, 0)),
                  pl.BlockSpec((None, tk // ATTN_CHUNK, MLA_HEADS * ATTN_V_ROWS, ATTN_CHUNK),
                               lambda b, i, j: (b, j, 0, 0))],
        out_specs=pl.BlockSpec((None, tq, MLA_HEADS * MLA_V), lambda b, i, j: (b, i, 0)),
        out_shape=jax.ShapeDtypeStruct((B, S, MLA_HEADS * MLA_V), BF16),
        scratch_shapes=[pltpu.VMEM((nsub, MLA_HEADS, SUBLANES, ATTN_Q_SUB), F32),
                        pltpu.VMEM((nsub, MLA_HEADS, SUBLANES, ATTN_Q_SUB), F32),
                        pltpu.VMEM((nsub, MLA_HEADS * MLA_V, ATTN_Q_SUB), F32)],
        compiler_params=_cparams("parallel", "parallel", "arbitrary"),
        name="latent_attention",
    )(qt, k, vt)
    return out.reshape(B * S, MLA_HEADS * MLA_V)


def _dilated_kernel(q_ref, kp_ref, kc_ref, kn_ref, vp_ref, vc_ref, vn_ref, bias_ref, o_ref, lse_ref, *, rows):
    i = pl.program_id(2)
    qb = q_ref.shape[0]
    k_all = jnp.concatenate([kp_ref[...], kc_ref[...], kn_ref[...]], axis=0)
    v_all = jnp.concatenate([vp_ref[...], vc_ref[...], vn_ref[...]], axis=0)
    v_all_t = v_all.astype(F32).T.astype(BF16)
    row = lax.broadcasted_iota(jnp.int32, (3 * DIL_BLOCK, DIL_BLOCK), 0)
    lane = lax.broadcasted_iota(jnp.int32, (1, LANES), 1)
    dim = lax.broadcasted_iota(jnp.int32, (LANES, 1), 0)
    n_sub = qb // DIL_BLOCK
    scores = {}
    for sub in range(n_sub):
        q = q_ref[sub * DIL_BLOCK:(sub + 1) * DIL_BLOCK, :]
        k3 = k_all[sub * DIL_BLOCK:(sub + 3) * DIL_BLOCK, :]
        for h in range(DIL_HEADS):
            head_lanes = jnp.logical_and(lane >= h * DIL_HEAD_DIM, lane < (h + 1) * DIL_HEAD_DIM)
            qh = jnp.where(head_lanes, q, jnp.zeros_like(q))
            scores[sub, h] = lax.dot_general(k3, qh, (((1,), (1,)), ((), ())), preferred_element_type=F32)
    for sub in range(n_sub):
        inside = 1 <= sub <= n_sub - 2
        key_row = i * qb + (sub - 1) * DIL_BLOCK + row
        in_seq = jnp.logical_and(key_row >= 0, key_row < rows)
        v3_t = v_all_t[:, sub * DIL_BLOCK:(sub + 3) * DIL_BLOCK]
        pvs, lses = [], []
        for h in range(DIL_HEADS):
            s = scores.pop((sub, h)) + bias_ref[h]
            s = s if inside else jnp.where(in_seq, s, NEG_BIG)
            m = jnp.max(s, axis=0, keepdims=True)
            e = jnp.exp2(s - m)
            den = jnp.sum(e, axis=0, keepdims=True)
            pvs.append(_dot(v3_t, e.astype(BF16)) * (1.0 / den))
            lses.append((m + jnp.log2(den)) * (1.0 / LOG2_E))
        out_t = jnp.zeros((LANES, DIL_BLOCK), F32)
        lse_t = jnp.zeros((LANES, DIL_BLOCK), F32)
        for h in range(DIL_HEADS):
            head_rows = jnp.logical_and(dim >= h * DIL_HEAD_DIM, dim < (h + 1) * DIL_HEAD_DIM)
            out_t = jnp.where(head_rows, pvs[h], out_t)
            lse_t = jnp.where(head_rows, lses[h], lse_t)
        o_ref[sub * DIL_BLOCK:(sub + 1) * DIL_BLOCK, :] = out_t.T
        lse_ref[sub * DIL_BLOCK:(sub + 1) * DIL_BLOCK, :] = lse_t.T


def _dilated_group(q, k, v, bias_t, g, d, B, S):
    rows = S // d
    qb = min(DIL_Q_BLOCK, rows)
    per = qb // DIL_BLOCK
    last = rows // DIL_BLOCK - 1
    shape = (B, d, rows, LANES)
    q, k, v = q.reshape(shape), k.reshape(shape), v.reshape(shape)
    edge = lambda f: pl.BlockSpec((None, None, DIL_BLOCK, LANES), f)
    main = pl.BlockSpec((None, None, qb, LANES), lambda b, c, i: (b, c, i, 0))
    prev = edge(lambda b, c, i: (b, c, jnp.maximum(i * per - 1, 0), 0))
    nxt = edge(lambda b, c, i: (b, c, jnp.minimum((i + 1) * per, last), 0))
    return pl.pallas_call(
        functools.partial(_dilated_kernel, rows=rows),
        grid=(B, d, rows // qb),
        in_specs=[main, prev, main, nxt, prev, main, nxt, _full((DIL_HEADS, 3 * DIL_BLOCK, DIL_BLOCK))],
        out_specs=[main, main],
        out_shape=[jax.ShapeDtypeStruct(shape, F32)] * 2,
        compiler_params=_cparams("parallel", "parallel", "parallel"),
        name="dilated_attention_g%d" % g,
    )(q, k, k, k, v, v, v, bias_t)


def _t5_bucket(rel):
    nb = REL_BUCKETS // 2
    max_exact = nb // 2
    ret = jnp.where(rel > 0, nb, 0)
    n = jnp.abs(rel)
    nf = jnp.maximum(n, 1).astype(F32)
    large = max_exact + (jnp.log(nf / max_exact) / math.log(REL_MAX_DIST / max_exact) * (nb - max_exact)).astype(jnp.int32)
    large = jnp.minimum(large, nb - 1)
    return ret + jnp.where(n < max_exact, n, large)


def _dilated_bias_tables(rel_bias):
    shape = (3 * DIL_BLOCK, DIL_BLOCK)
    rel = lax.broadcasted_iota(jnp.int32, shape, 0) - DIL_BLOCK - lax.broadcasted_iota(jnp.int32, shape, 1)
    in_band = jnp.abs(rel) <= DIL_SIDE
    steps = jnp.arange(-DIL_SIDE, DIL_SIDE + 1, dtype=jnp.int32)
    band_onehot = (rel[None] == steps[:, None, None]).astype(F32)
    exact = lax.Precision.HIGHEST
    tables = []
    for g, (_, d) in enumerate(DIL_PAIRS):
        bucket_onehot = (_t5_bucket(d * steps)[:, None] == jnp.arange(REL_BUCKETS)[None, :]).astype(F32)
        b = jnp.dot(bucket_onehot, rel_bias[:, g * DIL_HEADS:(g + 1) * DIL_HEADS].astype(F32), precision=exact)
        table = jnp.einsum("nh,nkq->hkq", b, band_onehot, precision=exact)
        tables.append(jnp.where(in_band[None], table * LOG2_E, NEG_BIG))
    return tables


MERGE_DIL_SLOTS = 4


def _merge_kernel(x_ref, fa_ref, sg_ref, oc_ref, o0_ref, l0_ref, o1_ref, l1_ref, o2_ref, l2_ref,
                  wa_ref, wb_ref, wc_ref, wd_ref, wg_ref, bg_ref, wo_ref, g_ref, b_ref, o_ref, nat_ref):
    x = x_ref[...]
    xb = x.astype(BF16)
    tm = x.shape[0]
    for slot, (src, g) in enumerate(((o1_ref, 1), (l1_ref, 1), (o2_ref, 2), (l2_ref, 2))):
        d = DIL_PAIRS[g][1]
        for c in range(d):
            nat_ref[slot, pl.ds(c, tm // d, stride=d), :] = src[c]
    l0, l1, l2 = l0_ref[...], nat_ref[1], nat_ref[3]
    mx = jnp.maximum(jnp.maximum(l0, l1), l2)
    e0, e1, e2 = jnp.exp(l0 - mx), jnp.exp(l1 - mx), jnp.exp(l2 - mx)
    den = e0 + e1 + e2
    od = o0_ref[...] * (e0 / den) + nat_ref[0] * (e1 / den) + nat_ref[2] * (e2 / den)
    for k1 in range(FOURIER_N1):
        for g in range(F_GROUPS):
            nat_ref[MERGE_DIL_SLOTS + g, pl.ds(k1, tm // FOURIER_N1, stride=FOURIER_N1), :] = (
                fa_ref[k1, :, g * LANES:(g + 1) * LANES])
    fa = jnp.concatenate([nat_ref[MERGE_DIL_SLOTS + g] for g in range(F_GROUPS)], axis=1).astype(BF16)
    branches = ((fa, wa_ref), (sg_ref[...], wb_ref), (oc_ref[...], wc_ref), (od.astype(BF16), wd_ref))
    merged = None
    for i, (act, w_ref) in enumerate(branches):
        cols = slice(i * D_MODEL, (i + 1) * D_MODEL)
        gate = jax.nn.sigmoid(_dot(xb, wg_ref[:, cols]) + bg_ref[:, cols])
        term = gate * _dot(act, w_ref[...])
        merged = term if merged is None else merged + term
    y = DN_ALPHA * x + _dot(merged.astype(BF16), wo_ref[...])
    o_ref[...] = _layer_norm(y, g_ref[...], b_ref[...])


def _merge(x, S, fa, sg, oc, dil, lw):
    T = x.shape[0]
    tm = TOKEN_TILE
    nblk = S // tm
    tok = lambda w: pl.BlockSpec((tm, w), lambda i: (i, 0))
    dil_specs, dil_args = [], []
    for (o, lse), (_, d) in zip(dil, DIL_PAIRS):
        for a in (o, lse):
            if d == 1:
                dil_specs.append(tok(LANES))
                dil_args.append(a.reshape(T, LANES))
            else:
                dil_specs.append(pl.BlockSpec((None, d, tm // d, LANES), lambda i: (i // nblk, 0, i % nblk, 0)))
                dil_args.append(a)
    return pl.pallas_call(
        _merge_kernel,
        grid=(T // tm,),
        in_specs=[tok(D_MODEL),
                  pl.BlockSpec((None, FOURIER_N1, tm // FOURIER_N1, F_WIDTH), lambda i: (i // nblk, 0, i % nblk, 0)),
                  tok(SG_WIDTH), tok(MLA_HEADS * MLA_V)] + dil_specs + [
                  _full((F_WIDTH, D_MODEL)), _full((SG_WIDTH, D_MODEL)), _full((MLA_HEADS * MLA_V, D_MODEL)),
                  _full((DIL_KV_WIDTH, D_MODEL)), _full((D_MODEL, 4 * D_MODEL)), _full((1, 4 * D_MODEL)),
                  _full((D_MODEL, D_MODEL)), _full((1, D_MODEL)), _full((1, D_MODEL))],
        out_specs=tok(D_MODEL),
        out_shape=jax.ShapeDtypeStruct((T, D_MODEL), F32),
        scratch_shapes=[pltpu.VMEM((MERGE_DIL_SLOTS + F_GROUPS, tm, LANES), F32)],
        compiler_params=_cparams("parallel"),
        name="merge",
    )(x, fa, sg, oc, *dil_args, lw["w_a"], lw["w_b"], lw["w_c"], lw["w_d"], lw["w_gate"], lw["b_gate"],
      lw["w_o"], lw["ln1_g"], lw["ln1_b"])


MOE_PAIRS = tuple((a, b) for a in range(MOE_EXPERTS_PER_GROUP) for b in range(a + 1, MOE_EXPERTS_PER_GROUP))
MOE_CLASSES = MOE_GROUPS * len(MOE_PAIRS)
MOE_ROW_TILE = 512
INFO_CLASS, INFO_RANK, INFO_P_LO, INFO_P_HI = 0, 1, 2, 3
MOE_ROW_WIDTH = D_MODEL + LANES


def _route_kernel(x_ref, wr_ref, br_ref, tri_ref, info_ref, counts_ref, run_ref):
    @pl.when(pl.program_id(0) == 0)
    def _():
        run_ref[...] = jnp.zeros(run_ref.shape, F32)

    x = x_ref[...]
    x_hi = x.astype(BF16)
    x_lo = (x - x_hi.astype(F32)).astype(BF16)
    logits = (_dot(x_hi, wr_ref[0]) + (_dot(x_hi, wr_ref[1]) + _dot(x_lo, wr_ref[0]))) + br_ref[...]
    tm = logits.shape[0]
    lane = lax.broadcasted_iota(jnp.int32, (tm, LANES), 1)
    is_g = lane < MOE_GROUPS
    gl = jnp.where(is_g, logits, NEG_BIG)
    gmax = jnp.max(gl, -1, keepdims=True)
    g_top = jnp.min(jnp.where(gl == gmax, lane, LANES), -1, keepdims=True)
    p_group = 1.0 / jnp.sum(jnp.where(is_g, jnp.exp(gl - gmax), 0.0), -1, keepdims=True)
    base = MOE_GROUPS + g_top * MOE_EXPERTS_PER_GROUP
    in_grp = jnp.logical_and(lane >= base, lane < base + MOE_EXPERTS_PER_GROUP)
    el = jnp.where(in_grp, logits, NEG_BIG)
    v1 = jnp.max(el, -1, keepdims=True)
    i1 = jnp.min(jnp.where(el == v1, lane, LANES), -1, keepdims=True)
    el2 = jnp.where(lane == i1, NEG_BIG, el)
    v2 = jnp.max(el2, -1, keepdims=True)
    i2 = jnp.min(jnp.where(el2 == v2, lane, LANES), -1, keepdims=True)
    e2 = jnp.exp(v2 - v1)
    p1 = p_group / (1.0 + e2)
    p2 = p_group * e2 / (1.0 + e2)
    a = jnp.minimum(i1, i2) - base
    b = jnp.maximum(i1, i2) - base
    pair = jnp.where(a == 0, 0, jnp.where(a == 1, 3, 5)) + (b - a - 1)
    cls = g_top * len(MOE_PAIRS) + pair
    first_is_lo = i1 < i2
    p_lo = jnp.where(first_is_lo, p1, p2)
    p_hi = jnp.where(first_is_lo, p2, p1)
    onehot = lane == cls
    before = _dot(tri_ref[...], onehot.astype(BF16)) + run_ref[...]
    rank = jnp.sum(jnp.where(onehot, before, 0.0), -1, keepdims=True)
    run_ref[...] += jnp.sum(onehot.astype(F32), axis=0, keepdims=True)
    counts_ref[...] = run_ref[...]
    info_ref[...] = jnp.where(lane == INFO_CLASS, cls.astype(F32),
                              jnp.where(lane == INFO_RANK, rank,
                                        jnp.where(lane == INFO_P_LO, p_lo,
                                                  jnp.where(lane == INFO_P_HI, p_hi, 0.0))))


ROW_COPY_UNROLL = 8


def _start_row_copies(n, make_copy):
    def start(r, carry):
        make_copy(r).start()
        return carry

    lax.fori_loop(0, n, start, 0, unroll=ROW_COPY_UNROLL)


def _dispatch_kernel(dest_ref, x_ref, info_ref, init_hbm, rows_hbm, row_ref, sem):
    del init_hbm
    i = pl.program_id(0)
    n = pl.num_programs(0)
    tm = x_ref.shape[0]
    slot = i % 2

    def wait_slot(s):
        pltpu.make_async_copy(row_ref.at[s], rows_hbm.at[pl.ds(0, tm)], sem.at[s]).wait()

    @pl.when(i >= 2)
    def _():
        wait_slot(slot)

    row_ref[slot, :, :D_MODEL] = x_ref[...]
    row_ref[slot, :, D_MODEL:] = info_ref[...]
    _start_row_copies(tm, lambda r: pltpu.make_async_copy(
        row_ref.at[slot, pl.ds(r, 1)], rows_hbm.at[pl.ds(dest_ref[0, r], 1)], sem.at[slot]))

    @pl.when(i == n - 1)
    def _():
        wait_slot(slot)

        @pl.when(n > 1)
        def _():
            wait_slot(1 - slot)


def _expert_kernel(ea_ref, eb_ref, nused_ref, rows_ref, wga_ref, wua_ref, wda_ref, wgb_ref, wub_ref, wdb_ref,
                   y_ref):
    del ea_ref, eb_ref
    i = pl.program_id(0)

    @pl.when(i < nused_ref[0])
    def _():
        x = rows_ref[:, :D_MODEL].astype(BF16)
        p_lo = rows_ref[:, D_MODEL + INFO_P_LO:D_MODEL + INFO_P_LO + 1]
        p_hi = rows_ref[:, D_MODEL + INFO_P_HI:D_MODEL + INFO_P_HI + 1]
        ha = jax.nn.silu(_dot(x, wga_ref[...])) * _dot(x, wua_ref[...]) * p_lo
        hb = jax.nn.silu(_dot(x, wgb_ref[...])) * _dot(x, wub_ref[...]) * p_hi
        y_ref[...] = _dot(ha.astype(BF16), wda_ref[...]) + _dot(hb.astype(BF16), wdb_ref[...])

    @pl.when(i >= nused_ref[0])
    def _():
        y_ref[...] = jnp.zeros(y_ref.shape, F32)


def _combine_kernel(dest_ref, next_dest_ref, x_ref, g_ref, b_ref, y_hbm, o_ref, y_ref, sem):
    i = pl.program_id(0)
    n = pl.num_programs(0)
    tm = x_ref.shape[0]
    slot = i % 2

    def fetch(indices_ref, s):
        _start_row_copies(tm, lambda r: pltpu.make_async_copy(
            y_hbm.at[pl.ds(indices_ref[0, r], 1)], y_ref.at[s, pl.ds(r, 1)], sem.at[s]))

    @pl.when(i == 0)
    def _():
        fetch(dest_ref, slot)

    @pl.when(i + 1 < n)
    def _():
        fetch(next_dest_ref, 1 - slot)

    pltpu.make_async_copy(y_hbm.at[pl.ds(0, tm)], y_ref.at[slot], sem.at[slot]).wait()
    o_ref[...] = _layer_norm(DN_ALPHA * x_ref[...] + y_ref[slot], g_ref[...], b_ref[...])


def _moe(x, lw):
    T = x.shape[0]
    tm = TOKEN_TILE
    rt = MOE_ROW_TILE
    n_row_tiles = T // rt + MOE_CLASSES
    n_rows = n_row_tiles * rt
    tok = lambda w: pl.BlockSpec((tm, w), lambda i: (i, 0))

    tri = jnp.asarray(np.tril(np.ones((tm, tm)), -1), BF16)
    info, counts = pl.pallas_call(
        _route_kernel,
        grid=(T // tm,),
        in_specs=[tok(D_MODEL), _full((2, D_MODEL, LANES)), _full((1, LANES)), _full((tm, tm))],
        out_specs=[tok(LANES), _full((1, LANES))],
        out_shape=[jax.ShapeDtypeStruct((T, LANES), F32), jax.ShapeDtypeStruct((1, LANES), F32)],
        scratch_shapes=[pltpu.VMEM((1, LANES), F32)],
        compiler_params=_cparams("arbitrary"),
        name="moe_route",
    )(x, lw["w_router"], lw["b_router"], tri)

    cls = info[:, INFO_CLASS].astype(jnp.int32)
    rank = info[:, INFO_RANK].astype(jnp.int32)
    cnt = counts[0, :MOE_CLASSES].astype(jnp.int32)
    padded = (cnt + rt - 1) // rt * rt
    ends = jnp.cumsum(padded)
    classes = jnp.arange(MOE_CLASSES, dtype=jnp.int32)
    pick = lambda table, idx: jnp.sum(jnp.where(idx[:, None] == classes[None, :], table[None, :], 0), axis=1)
    dest = (pick(ends - padded, cls) + rank).reshape(T // tm, 1, tm)
    tile_start = jnp.arange(n_row_tiles, dtype=jnp.int32) * rt
    tile_cls = jnp.minimum(jnp.sum((tile_start[:, None] >= ends[None, :]).astype(jnp.int32), axis=1),
                           MOE_CLASSES - 1)
    group, pair = np.divmod(np.arange(MOE_CLASSES), len(MOE_PAIRS))
    lo_hi = np.asarray(MOE_PAIRS)[pair]
    ea = pick(jnp.asarray(group * MOE_EXPERTS_PER_GROUP + lo_hi[:, 0], jnp.int32), tile_cls)
    eb = pick(jnp.asarray(group * MOE_EXPERTS_PER_GROUP + lo_hi[:, 1], jnp.int32), tile_cls)
    n_used = (ends[-1:] // rt).astype(jnp.int32)

    dest_spec = pl.BlockSpec((None, 1, tm), lambda i: (i, 0, 0), memory_space=pltpu.SMEM)
    next_dest_spec = pl.BlockSpec((None, 1, tm), lambda i: (jnp.minimum(i + 1, T // tm - 1), 0, 0),
                                  memory_space=pltpu.SMEM)
    hbm = pl.BlockSpec(memory_space=pl.ANY)
    rows = pl.pallas_call(
        _dispatch_kernel,
        grid=(T // tm,),
        in_specs=[dest_spec, tok(D_MODEL), tok(LANES), hbm],
        out_specs=hbm,
        out_shape=jax.ShapeDtypeStruct((n_rows, MOE_ROW_WIDTH), F32),
        scratch_shapes=[pltpu.VMEM((2, tm, MOE_ROW_WIDTH), F32), pltpu.SemaphoreType.DMA((2,))],
        input_output_aliases={3: 0},
        compiler_params=_cparams("arbitrary"),
        name="moe_dispatch",
    )(dest, x, info, jnp.zeros((n_rows, MOE_ROW_WIDTH), F32))

    w_up = lambda sel: pl.BlockSpec((None, D_MODEL, MOE_FF), lambda i, ea, eb, nu: (sel(ea, eb)[i], 0, 0))
    w_dn = lambda sel: pl.BlockSpec((None, MOE_FF, D_MODEL), lambda i, ea, eb, nu: (sel(ea, eb)[i], 0, 0))
    first = lambda ea, eb: ea
    second = lambda ea, eb: eb
    y = pl.pallas_call(
        _expert_kernel,
        grid_spec=pltpu.PrefetchScalarGridSpec(
            num_scalar_prefetch=3,
            grid=(n_row_tiles,),
            in_specs=[pl.BlockSpec((rt, MOE_ROW_WIDTH), lambda i, ea, eb, nu: (i, 0)),
                      w_up(first), w_up(first), w_dn(first), w_up(second), w_up(second), w_dn(second)],
            out_specs=pl.BlockSpec((rt, D_MODEL), lambda i, ea, eb, nu: (i, 0)),
        ),
        out_shape=jax.ShapeDtypeStruct((n_rows, D_MODEL), F32),
        compiler_params=_cparams("arbitrary"),
        name="moe_experts",
    )(ea, eb, n_used, rows, lw["moe_w_gate"], lw["moe_w_up"], lw["moe_w_down"],
      lw["moe_w_gate"], lw["moe_w_up"], lw["moe_w_down"])

    return pl.pallas_call(
        _combine_kernel,
        grid=(T // tm,),
        in_specs=[dest_spec, next_dest_spec, tok(D_MODEL), _full((1, D_MODEL)), _full((1, D_MODEL)), hbm],
        out_specs=tok(D_MODEL),
        out_shape=jax.ShapeDtypeStruct((T, D_MODEL), F32),
        scratch_shapes=[pltpu.VMEM((2, tm, D_MODEL), F32), pltpu.SemaphoreType.DMA((2,))],
        compiler_params=_cparams("arbitrary"),
        name="moe_combine",
    )(dest, dest, x, lw["ln2_g"], lw["ln2_b"], y)


def _rope_tables(S):
    half = MLA_ROPE // 2
    inv = ROPE_BASE ** (-jnp.arange(half, dtype=F32) / half)
    ang = jnp.arange(S, dtype=F32)[:, None] * inv[None, :]
    cos, sin = jnp.cos(ang), jnp.sin(ang)
    one = jnp.ones((S, MLA_NOPE), F32)
    zero = jnp.zeros((S, MLA_NOPE), F32)
    zh = jnp.zeros((S, half), F32)
    rc = jnp.concatenate([one, cos, cos], axis=1)
    rsa = jnp.concatenate([zero, -sin, zh], axis=1)
    rsb = jnp.concatenate([zero, zh, sin], axis=1)
    return rc, rsa, rsb, cos.T, sin.T


def _split_bf16(w):
    hi = w.astype(BF16)
    return jnp.stack([hi, (w - hi.astype(F32)).astype(BF16)])


def _prep_layer(l, p):
    row = lambda a: a.reshape(1, -1).astype(F32)
    w_ukv = p["mla_w_ukv"][l].reshape(MLA_KV_RANK, MLA_HEADS, MLA_NOPE + MLA_V)
    pad_rows = ((0, 2 * LANES - MLA_KV_RANK), (0, 0))
    w_k = jnp.pad(w_ukv[:, :, :MLA_NOPE], ((0, 0), (0, 0), (0, LANES - MLA_NOPE))).reshape(MLA_KV_RANK, -1)
    w_v = w_ukv[:, :, MLA_NOPE:].reshape(MLA_KV_RANK, -1)
    w_router = jnp.concatenate([p["moe_w_rg"][l], p["moe_w_re"][l]], axis=1)
    b_router = jnp.concatenate([p["moe_b_rg"][l], p["moe_b_re"][l]])
    npad = LANES - MOE_GROUPS - MOE_EXPERTS
    return {
        "w_in": p["w_in"][l].astype(BF16),
        "sg_ln_g": row(p["sg_ln_g"][l]), "sg_ln_b": row(p["sg_ln_b"][l]),
        "sg_w": p["sg_w"][l].astype(BF16),
        "sg_bias": jnp.repeat(p["sg_b"][l].T, SG_GROUP_DIM, axis=1).astype(F32),
        "q_norm": row(p["mla_q_norm"][l]),
        "kv_norm": jnp.pad(row(p["mla_kv_norm"][l]), ((0, 0), (0, 2 * LANES - MLA_KV_RANK))),
        "w_uq": p["mla_w_uq"][l].T.astype(BF16),
        "w_k": jnp.pad(w_k, pad_rows).astype(BF16),
        "w_v": jnp.pad(w_v, pad_rows).T.astype(BF16),
        "w_a": p["w_branch_a"][l].astype(BF16), "w_b": p["w_branch_b"][l].astype(BF16),
        "w_c": p["w_branch_c"][l].astype(BF16), "w_d": p["w_branch_d"][l].astype(BF16),
        "w_gate": p["w_gate"][l].astype(BF16), "b_gate": row(p["b_gate"][l]),
        "w_o": p["w_o"][l].astype(BF16),
        "ln1_g": row(p["ln1_g"][l]), "ln1_b": row(p["ln1_b"][l]),
        "w_router": _split_bf16(jnp.pad(w_router, ((0, 0), (0, npad))).astype(F32)),
        "b_router": jnp.pad(b_router, (0, npad)).reshape(1, -1).astype(F32),
        "moe_w_gate": p["moe_w_gate"][l].astype(BF16), "moe_w_up": p["moe_w_up"][l].astype(BF16),
        "moe_w_down": p["moe_w_down"][l].astype(BF16),
        "ln2_g": row(p["ln2_g"][l]), "ln2_b": row(p["ln2_b"][l]),
    }


def _trunk(x, p, layers, bias_tables):
    B, S, _ = x.shape
    rope = _rope_tables(S)
    fconsts = _fourier_consts(S)
    h = _input_layer_norm(x.reshape(B * S, D_MODEL), p["ln_in_g"], p["ln_in_b"])
    for lw in layers:
        za, sg, q, k, vt, *dil_in = _in_proj(h, B, S, lw, rope)
        fa = _fourier_mix(za, B, S, fconsts)
        oc = _latent_attention(q, k, vt, B, S)
        dil = [_dilated_group(*dil_in[3 * g:3 * g + 3], bias_tables[g], g, d, B, S)
               for g, (_, d) in enumerate(DIL_PAIRS)]
        h = _merge(h, S, fa, sg, oc, dil, lw)
        h = _moe(h, lw)
    return h.reshape(B, S, D_MODEL)


def kernel(x_prompt, x_sample, ln_in_g, ln_in_b, rel_bias, w_in, sg_ln_g, sg_ln_b, sg_w, sg_b, mla_q_norm, mla_kv_norm, mla_w_uq, mla_w_ukv, w_branch_a, w_branch_b, w_branch_c, w_branch_d, w_gate, b_gate, w_o, ln1_g, ln1_b, moe_w_rg, moe_b_rg, moe_w_re, moe_b_re, moe_w_gate, moe_w_up, moe_w_down, ln2_g, ln2_b):
    p = dict(ln_in_g=ln_in_g, ln_in_b=ln_in_b, w_in=w_in, sg_ln_g=sg_ln_g, sg_ln_b=sg_ln_b, sg_w=sg_w, sg_b=sg_b,
             mla_q_norm=mla_q_norm, mla_kv_norm=mla_kv_norm, mla_w_uq=mla_w_uq, mla_w_ukv=mla_w_ukv,
             w_branch_a=w_branch_a, w_branch_b=w_branch_b, w_branch_c=w_branch_c, w_branch_d=w_branch_d,
             w_gate=w_gate, b_gate=b_gate, w_o=w_o, ln1_g=ln1_g, ln1_b=ln1_b,
             moe_w_rg=moe_w_rg, moe_b_rg=moe_b_rg, moe_w_re=moe_w_re, moe_b_re=moe_b_re,
             moe_w_gate=moe_w_gate, moe_w_up=moe_w_up, moe_w_down=moe_w_down, ln2_g=ln2_g, ln2_b=ln2_b)
    layers = [_prep_layer(l, p) for l in range(w_in.shape[0])]
    bias_tables = _dilated_bias_tables(rel_bias)
    return _trunk(x_prompt, p, layers, bias_tables), _trunk(x_sample, p, layers, bias_tables)
```

```python
import functools
import math

import numpy as np
import jax
import jax.numpy as jnp
from jax import lax
from jax.experimental import pallas as pl
from jax.experimental.pallas import tpu as pltpu

F32 = jnp.float32
BF16 = jnp.bfloat16

D_MODEL = 1024
DEPTH = 4
F_GROUPS = 4
F_GROUP_DIM = 128
F_WIDTH = F_GROUPS * F_GROUP_DIM
SG_CHUNK = 128
SG_GROUPS = 4
SG_GROUP_DIM = 64
SG_WIDTH = SG_GROUPS * SG_GROUP_DIM
MLA_HEADS = 8
MLA_Q_RANK = 256
MLA_KV_RANK = 192
MLA_NOPE = 64
MLA_ROPE = 64
MLA_V = 64
MLA_QK_DIM = MLA_NOPE + MLA_ROPE
ROPE_BASE = 10000.0
DIL_PAIRS = ((128, 1), (512, 4), (2048, 16))
DIL_GROUPS = 3
DIL_HEADS = 4
DIL_HEAD_DIM = 32
DIL_Q_WIDTH = DIL_GROUPS * DIL_HEADS * DIL_HEAD_DIM
DIL_KV_WIDTH = DIL_HEADS * DIL_HEAD_DIM
DIL_SIDE = 64
REL_BUCKETS = 32
REL_MAX_DIST = 1024
MIX_WIDTH = F_WIDTH + 2 * SG_WIDTH + MLA_Q_RANK + MLA_KV_RANK + MLA_ROPE + DIL_Q_WIDTH + 2 * DIL_KV_WIDTH
MOE_GROUPS = 4
MOE_EXPERTS_PER_GROUP = 4
MOE_EXPERTS = MOE_GROUPS * MOE_EXPERTS_PER_GROUP
MOE_FF = 512
DN_ALPHA = (2 * DEPTH) ** 0.25
LN_EPS = 1e-5
RMS_EPS = 1e-6

OFF_A = 0
OFF_B = OFF_A + F_WIDTH
OFF_CQ = OFF_B + 2 * SG_WIDTH
OFF_CKV = OFF_CQ + MLA_Q_RANK
OFF_DQ = OFF_CKV + MLA_KV_RANK + MLA_ROPE

LANES = 128
SUBLANES = 8
VMEM_LIMIT_BYTES = 56 * 1024 * 1024
TOKEN_TILE = 512
ATTN_Q_BLOCK = 512
ATTN_Q_SUB = 256
ATTN_KV_BLOCK = 2048
ATTN_CHUNK = 256
ATTN_CHUNKS_PER_ITER = 8
ATTN_PIPELINE_DEPTH = 6
ATTN_V_ROWS = 80
DIL_BLOCK = 128
DIL_Q_BLOCK = 1024
NEG_BIG = -1e30
LOG2_E = math.log2(math.e)


def _cparams(*sem):
    return pltpu.CompilerParams(dimension_semantics=sem, vmem_limit_bytes=VMEM_LIMIT_BYTES)


def _full(shape):
    n = len(shape)
    return pl.BlockSpec(shape, lambda *_: (0,) * n)


def _layer_norm(x, g, b):
    mu = jnp.mean(x, -1, keepdims=True)
    xc = x - mu
    var = jnp.mean(xc * xc, -1, keepdims=True)
    return xc * lax.rsqrt(var + LN_EPS) * g + b


def _dot(a, b):
    return jnp.dot(a, b, preferred_element_type=F32)


def _ln_kernel(x_ref, g_ref, b_ref, o_ref):
    o_ref[...] = _layer_norm(x_ref[...], g_ref[...], b_ref[...])


def _input_layer_norm(x, g, b):
    T = x.shape[0]
    tm = TOKEN_TILE
    return pl.pallas_call(
        _ln_kernel,
        grid=(T // tm,),
        in_specs=[pl.BlockSpec((tm, D_MODEL), lambda i: (i, 0)), _full((1, D_MODEL)), _full((1, D_MODEL))],
        out_specs=pl.BlockSpec((tm, D_MODEL), lambda i: (i, 0)),
        out_shape=jax.ShapeDtypeStruct((T, D_MODEL), F32),
        compiler_params=_cparams("parallel"),
        name="input_layer_norm",
    )(x, g.reshape(1, -1), b.reshape(1, -1))


def _rope_lanes(t, c, sa, sb):
    half = MLA_ROPE // 2
    return t * c + pltpu.roll(t, LANES - half, 1) * sa + pltpu.roll(t, half, 1) * sb


def _in_proj_kernel(x_ref, w_in_ref, sg_g_ref, sg_b_ref, sg_w_ref, sg_bias_ref, qn_ref, kvn_ref,
                    wuqt_ref, wk_ref, wvt_ref, rc_ref, rsa_ref, rsb_ref, cost_ref, sint_ref,
                    za_ref, sg_ref, qt_ref, k_ref, vt_ref,
                    dq0_ref, dk0_ref, dv0_ref, dq1_ref, dk1_ref, dv1_ref, dq2_ref, dk2_ref, dv2_ref, dil_ref):
    tm = x_ref.shape[0]
    z = _dot(x_ref[...].astype(BF16), w_in_ref[...])

    za_ref[...] = z[:, OFF_A:OFF_A + F_WIDTH]

    zb = jax.nn.gelu(z[:, OFF_B:OFF_B + 2 * SG_WIDTH])
    u = zb[:, :SG_WIDTH]
    vn = _layer_norm(zb[:, SG_WIDTH:], sg_g_ref[...], sg_b_ref[...]).astype(BF16)
    lane = lax.broadcasted_iota(jnp.int32, (SG_CHUNK, LANES), 1)
    low_half = lane < SG_GROUP_DIM
    for ci in range(tm // SG_CHUNK):
        rows = slice(ci * SG_CHUNK, (ci + 1) * SG_CHUNK)
        for j in range(SG_WIDTH // LANES):
            cols = slice(j * LANES, (j + 1) * LANES)
            vblk = vn[rows, cols]
            mixed = jnp.where(low_half, _dot(sg_w_ref[2 * j], vblk), _dot(sg_w_ref[2 * j + 1], vblk))
            sg_ref[rows, cols] = (u[rows, cols] * (mixed + sg_bias_ref[:, cols])).astype(sg_ref.dtype)

    rc, rsa, rsb = rc_ref[...], rsa_ref[...], rsb_ref[...]

    cq = z[:, OFF_CQ:OFF_CQ + MLA_Q_RANK]
    cq = cq * lax.rsqrt(jnp.mean(cq * cq, -1, keepdims=True) + RMS_EPS) * qn_ref[...]
    q_t = _dot(wuqt_ref[...], cq.T.astype(BF16)) * (MLA_QK_DIM ** -0.5 * LOG2_E)
    cos_t, sin_t = cost_ref[...], sint_ref[...]
    half = MLA_ROPE // 2
    pieces = []
    for h in range(MLA_HEADS):
        base = h * MLA_QK_DIM
        x1 = q_t[base + MLA_NOPE:base + MLA_NOPE + half, :]
        x2 = q_t[base + MLA_NOPE + half:base + MLA_QK_DIM, :]
        pieces += [q_t[base:base + MLA_NOPE, :], x1 * cos_t - x2 * sin_t, x1 * sin_t + x2 * cos_t]
    q_t = jnp.concatenate(pieces, axis=0)
    for j in range(tm // ATTN_Q_SUB):
        qt_ref[j] = q_t[:, j * ATTN_Q_SUB:(j + 1) * ATTN_Q_SUB].astype(qt_ref.dtype)

    slab = z[:, OFF_CKV:OFF_CKV + 2 * LANES]
    lane2 = lax.broadcasted_iota(jnp.int32, (tm, 2 * LANES), 1)
    ckv_sq = jnp.where(lane2 < MLA_KV_RANK, slab * slab, 0.0)
    ms = jnp.sum(ckv_sq, -1, keepdims=True) * (1.0 / MLA_KV_RANK)
    ckv = (slab * lax.rsqrt(ms + RMS_EPS) * kvn_ref[...]).astype(BF16)
    k_nope = _dot(ckv, wk_ref[...])
    ckv_t = (slab * lax.rsqrt(ms + RMS_EPS) * kvn_ref[...]).T.astype(BF16)
    v_t = _dot(wvt_ref[...], ckv_t).astype(vt_ref.dtype)
    ones = jnp.ones((ATTN_V_ROWS - MLA_V, ATTN_CHUNK), vt_ref.dtype)
    for j in range(tm // ATTN_CHUNK):
        cols = slice(j * ATTN_CHUNK, (j + 1) * ATTN_CHUNK)
        for h in range(MLA_HEADS):
            vt_ref[j, h * ATTN_V_ROWS:h * ATTN_V_ROWS + MLA_V, :] = v_t[h * MLA_V:(h + 1) * MLA_V, cols]
            vt_ref[j, h * ATTN_V_ROWS + MLA_V:(h + 1) * ATTN_V_ROWS, :] = ones
    kr_slab = _rope_lanes(slab[:, LANES:], rc, rsa, rsb)
    kr_slab = jnp.where(lax.broadcasted_iota(jnp.int32, (tm, LANES), 1) >= MLA_NOPE, kr_slab, 0.0)
    for h in range(MLA_HEADS):
        cols = slice(h * LANES, (h + 1) * LANES)
        k_ref[:, cols] = (k_nope[:, cols] + kr_slab).astype(k_ref.dtype)

    for slab in range(DIL_GROUPS + 2):
        t = z[:, OFF_DQ + slab * LANES:OFF_DQ + (slab + 1) * LANES]
        dil_ref[slab] = t * (DIL_HEAD_DIM ** -0.5 * LOG2_E) if slab < DIL_GROUPS else t
    dq0_ref[...] = dil_ref[0].astype(dq0_ref.dtype)
    dk0_ref[...] = dil_ref[DIL_GROUPS].astype(dk0_ref.dtype)
    dv0_ref[...] = dil_ref[DIL_GROUPS + 1].astype(dv0_ref.dtype)
    for g, (qr, kr, vr) in ((1, (dq1_ref, dk1_ref, dv1_ref)), (2, (dq2_ref, dk2_ref, dv2_ref))):
        d = DIL_PAIRS[g][1]
        for c in range(d):
            rows = pl.ds(c, tm // d, stride=d)
            qr[c] = dil_ref[g, rows, :].astype(qr.dtype)
            kr[c] = dil_ref[DIL_GROUPS, rows, :].astype(kr.dtype)
            vr[c] = dil_ref[DIL_GROUPS + 1, rows, :].astype(vr.dtype)


def _in_proj(x, B, S, lw, rope):
    T = x.shape[0]
    tm = TOKEN_TILE
    nblk = S // tm
    tok = lambda w: pl.BlockSpec((tm, w), lambda i: (i, 0))
    pos = pl.BlockSpec((tm, LANES), lambda i: (i % nblk, 0))
    pos_t = pl.BlockSpec((MLA_ROPE // 2, tm), lambda i: (0, i % nblk))
    out_specs = [tok(F_WIDTH), tok(SG_WIDTH)]
    out_shape = [jax.ShapeDtypeStruct((T, F_WIDTH), F32), jax.ShapeDtypeStruct((T, SG_WIDTH), BF16)]
    out_specs.append(pl.BlockSpec((tm // ATTN_Q_SUB, MLA_HEADS * LANES, ATTN_Q_SUB), lambda i: (i, 0, 0)))
    out_shape.append(jax.ShapeDtypeStruct((T // ATTN_Q_SUB, MLA_HEADS * LANES, ATTN_Q_SUB), BF16))
    out_specs.append(tok(MLA_HEADS * LANES))
    out_shape.append(jax.ShapeDtypeStruct((T, MLA_HEADS * LANES), BF16))
    out_specs.append(pl.BlockSpec((tm // ATTN_CHUNK, MLA_HEADS * ATTN_V_ROWS, ATTN_CHUNK), lambda i: (i, 0, 0)))
    out_shape.append(jax.ShapeDtypeStruct((T // ATTN_CHUNK, MLA_HEADS * ATTN_V_ROWS, ATTN_CHUNK), BF16))
    for _, d in DIL_PAIRS:
        for _ in range(3):
            if d == 1:
                out_specs.append(tok(LANES))
                out_shape.append(jax.ShapeDtypeStruct((T, LANES), BF16))
            else:
                out_specs.append(pl.BlockSpec((None, d, tm // d, LANES), lambda i: (i // nblk, 0, i % nblk, 0)))
                out_shape.append(jax.ShapeDtypeStruct((B, d, S // d, LANES), BF16))
    return pl.pallas_call(
        _in_proj_kernel,
        grid=(T // tm,),
        in_specs=[tok(D_MODEL), _full((D_MODEL, MIX_WIDTH)), _full((1, SG_WIDTH)), _full((1, SG_WIDTH)),
                  _full((SG_GROUPS, SG_CHUNK, SG_CHUNK)), _full((SG_CHUNK, SG_WIDTH)),
                  _full((1, MLA_Q_RANK)), _full((1, 2 * LANES)),
                  _full((MLA_HEADS * LANES, MLA_Q_RANK)), _full((2 * LANES, MLA_HEADS * LANES)),
                  _full((MLA_HEADS * MLA_V, 2 * LANES)), pos, pos, pos, pos_t, pos_t],
        out_specs=out_specs,
        out_shape=out_shape,
        scratch_shapes=[pltpu.VMEM((DIL_GROUPS + 2, tm, LANES), F32)],
        compiler_params=_cparams("parallel"),
        name="in_proj",
    )(x, lw["w_in"], lw["sg_ln_g"], lw["sg_ln_b"], lw["sg_w"], lw["sg_bias"], lw["q_norm"], lw["kv_norm"],
      lw["w_uq"], lw["w_k"], lw["w_v"], *rope)


FOURIER_N1 = 16
FOURIER_ROWS = 16
FOURIER_K1_PER_STEP = 2


def _fourier1_kernel(x_ref, cs_ref, k1_ref, ct_ref, st_ref, gr_ref, gi_ref):
    n1, r, _ = x_ref.shape
    rows = n1 * r
    x = x_ref[...].reshape(rows, F_WIDTH)
    ct, st = ct_ref[...], st_ref[...]
    outs_r, outs_i = [], []
    for g in range(F_GROUPS):
        ab = _dot(x[:, g * LANES:(g + 1) * LANES].astype(BF16), cs_ref[...])
        stacked = jnp.concatenate([ab[:, :LANES], ab[:, LANES:]], axis=0).astype(BF16)
        g2 = _dot(k1_ref[...], stacked)
        gr, gi = g2[:rows], g2[rows:]
        outs_r.append(gr * ct - gi * st)
        outs_i.append(gr * st + gi * ct)
    gr_ref[...] = jnp.concatenate(outs_r, axis=1).reshape(n1, r, F_WIDTH)
    gi_ref[...] = jnp.concatenate(outs_i, axis=1).reshape(n1, r, F_WIDTH)


def _fourier2_kernel(gr_ref, gi_ref, w2_ref, o_ref):
    for j in range(gr_ref.shape[0]):
        stacked = jnp.concatenate([gr_ref[j], gi_ref[j]], axis=0).astype(BF16)
        o_ref[j] = _dot(w2_ref[...], stacked)


def _fourier_mix(za, B, S, consts):
    n1 = FOURIER_N1
    n2 = S // n1
    r = FOURIER_ROWS
    kk = FOURIER_K1_PER_STEP
    cs, k1, ct, st, w2 = consts
    x = za.reshape(B, n1, n2, F_WIDTH)
    blk1 = pl.BlockSpec((None, n1, r, F_WIDTH), lambda b, j: (b, 0, j, 0))
    twid = pl.BlockSpec((None, n1 * r, LANES), lambda b, j: (j, 0, 0))
    gr, gi = pl.pallas_call(
        _fourier1_kernel,
        grid=(B, n2 // r),
        in_specs=[blk1, _full((LANES, 2 * LANES)), _full((2 * n1 * r, 2 * n1 * r)), twid, twid],
        out_specs=[blk1, blk1],
        out_shape=[jax.ShapeDtypeStruct(x.shape, F32)] * 2,
        compiler_params=_cparams("parallel", "parallel"),
        name="fourier_stage1",
    )(x, cs, k1, ct, st)
    blk2 = pl.BlockSpec((None, kk, n2, F_WIDTH), lambda b, j: (b, j, 0, 0))
    return pl.pallas_call(
        _fourier2_kernel,
        grid=(B, n1 // kk),
        in_specs=[blk2, blk2, _full((n2, 2 * n2))],
        out_specs=blk2,
        out_shape=jax.ShapeDtypeStruct((B, n1, n2, F_WIDTH), F32),
        compiler_params=_cparams("parallel", "parallel"),
        name="fourier_stage2",
    )(gr, gi, w2)


def _fourier_consts(S):
    n1 = FOURIER_N1
    n2 = S // n1
    r = FOURIER_ROWS
    c = np.arange(F_GROUP_DIM)
    ang_c = 2.0 * np.pi * np.outer(c, c) / F_GROUP_DIM
    norm = 1.0 / math.sqrt(S * F_GROUP_DIM)
    cs = np.concatenate([np.cos(ang_c), np.sin(ang_c)], axis=1) * norm
    eye = np.eye(r)
    a1 = np.arange(n1)
    ang1 = 2.0 * np.pi * np.outer(a1, a1) / n1
    c1, s1 = np.kron(np.cos(ang1), eye), np.kron(np.sin(ang1), eye)
    k1 = np.block([[c1, -s1], [s1, c1]])
    a2 = np.arange(n2)
    ang_t = 2.0 * np.pi * np.outer(a1, a2) / S

    def twiddle(t):
        t = t.reshape(n1, n2 // r, r).transpose(1, 0, 2).reshape(n2 // r, n1 * r)
        return np.broadcast_to(t[:, :, None], (n2 // r, n1 * r, LANES))

    ang2 = 2.0 * np.pi * np.outer(a2, a2) / n2
    w2 = np.concatenate([np.cos(ang2), -np.sin(ang2)], axis=1)
    return (jnp.asarray(cs, BF16), jnp.asarray(k1, BF16), jnp.asarray(twiddle(np.cos(ang_t)), F32),
            jnp.asarray(twiddle(np.sin(ang_t)), F32), jnp.asarray(w2, BF16))


def _flash_kernel(qt_ref, k_ref, vt_ref, o_ref, m_ref, l_ref, acc_ref):
    ki = pl.program_id(2)
    nsub = qt_ref.shape[0]
    nchunk = k_ref.shape[0] // ATTN_CHUNK

    @pl.when(ki == 0)
    def _():
        m_ref[...] = jnp.full(m_ref.shape, NEG_BIG, F32)
        l_ref[...] = jnp.zeros(l_ref.shape, F32)
        acc_ref[...] = jnp.zeros(acc_ref.shape, F32)

    per_chunk = MLA_HEADS * nsub
    n_tiles = ATTN_CHUNKS_PER_ITER * per_chunk

    def tile_index(it, t):
        cc, rem = divmod(t, per_chunk)
        h, qs = divmod(rem, nsub)
        return it * ATTN_CHUNKS_PER_ITER + cc, h, qs

    def scores(it, t):
        c, h, qs = tile_index(it, t)
        r0 = pl.multiple_of(c * ATTN_CHUNK, ATTN_CHUNK)
        kc = k_ref[pl.ds(r0, ATTN_CHUNK), h * LANES:(h + 1) * LANES]
        return _dot(kc, qt_ref[qs, h * LANES:(h + 1) * LANES, :])

    def accumulate(it, t, s):
        c, h, qs = tile_index(it, t)
        rows = slice(h * MLA_V, (h + 1) * MLA_V)
        m_prev = m_ref[qs, h]
        m_new = jnp.maximum(m_prev, jnp.max(s, axis=0, keepdims=True))
        alpha = jnp.exp2(m_prev - m_new)
        p = jnp.exp2(s - m_new[0:1, :]).astype(BF16)
        m_ref[qs, h] = m_new
        pv = _dot(vt_ref[c, h * ATTN_V_ROWS:(h + 1) * ATTN_V_ROWS, :], p)
        l_ref[qs, h] = alpha * l_ref[qs, h] + pv[MLA_V:MLA_V + SUBLANES, :]
        acc_ref[qs, rows, :] = alpha[0:1, :] * acc_ref[qs, rows, :] + pv[:MLA_V, :]

    def iteration(it, carry):
        pending = {}
        for t in range(n_tiles + ATTN_PIPELINE_DEPTH):
            if t < n_tiles:
                pending[t] = scores(it, t)
            if t >= ATTN_PIPELINE_DEPTH:
                accumulate(it, t - ATTN_PIPELINE_DEPTH, pending.pop(t - ATTN_PIPELINE_DEPTH))
        return carry

    lax.fori_loop(0, nchunk // ATTN_CHUNKS_PER_ITER, iteration, 0)

    @pl.when(ki == pl.num_programs(2) - 1)
    def _():
        for qs in range(nsub):
            out_t = jnp.concatenate(
                [acc_ref[qs, h * MLA_V:(h + 1) * MLA_V, :] * (1.0 / l_ref[qs, h, 0:1, :])
                 for h in range(MLA_HEADS)], axis=0)
            o_ref[qs * ATTN_Q_SUB:(qs + 1) * ATTN_Q_SUB, :] = out_t.T.astype(o_ref.dtype)


def _latent_attention(qt, k, vt, B, S):
    tq = min(ATTN_Q_BLOCK, S)
    tk = min(ATTN_KV_BLOCK, S)
    nsub = tq // ATTN_Q_SUB
    qt = qt.reshape(B, S // ATTN_Q_SUB, MLA_HEADS * LANES, ATTN_Q_SUB)
    k = k.reshape(B, S, MLA_HEADS * LANES)
    vt = vt.reshape(B, S // ATTN_CHUNK, MLA_HEADS * ATTN_V_ROWS, ATTN_CHUNK)
    out = pl.pallas_call(
        _flash_kernel,
        grid=(B, S // tq, S // tk),
        in_specs=[pl.BlockSpec((None, nsub, MLA_HEADS * LANES, ATTN_Q_SUB), lambda b, i, j: (b, i, 0, 0)),
                  pl.BlockSpec((None, tk, MLA_HEADS * LANES), lambda b, i, j: (b, j, 0)),
                  pl.BlockSpec((None, tk // ATTN_CHUNK, MLA_HEADS * ATTN_V_ROWS, ATTN_CHUNK),
                               lambda b, i, j: (b, j, 0, 0))],
        out_specs=pl.BlockSpec((None, tq, MLA_HEADS * MLA_V), lambda b, i, j: (b, i, 0)),
        out_shape=jax.ShapeDtypeStruct((B, S, MLA_HEADS * MLA_V), BF16),
        scratch_shapes=[pltpu.VMEM((nsub, MLA_HEADS, SUBLANES, ATTN_Q_SUB), F32),
                        pltpu.VMEM((nsub, MLA_HEADS, SUBLANES, ATTN_Q_SUB), F32),
                        pltpu.VMEM((nsub, MLA_HEADS * MLA_V, ATTN_Q_SUB), F32)],
        compiler_params=_cparams("parallel", "parallel", "arbitrary"),
        name="latent_attention",
    )(qt, k, vt)
    return out.reshape(B * S, MLA_HEADS * MLA_V)


def _dilated_kernel(q_ref, kp_ref, kc_ref, kn_ref, vp_ref, vc_ref, vn_ref, bias_ref, o_ref, lse_ref, *, rows):
    i = pl.program_id(2)
    qb = q_ref.shape[0]
    k_all = jnp.concatenate([kp_ref[...], kc_ref[...], kn_ref[...]], axis=0)
    v_all = jnp.concatenate([vp_ref[...], vc_ref[...], vn_ref[...]], axis=0)
    v_all_t = v_all.astype(F32).T.astype(BF16)
    row = lax.broadcasted_iota(jnp.int32, (3 * DIL_BLOCK, DIL_BLOCK), 0)
    lane = lax.broadcasted_iota(jnp.int32, (1, LANES), 1)
    dim = lax.broadcasted_iota(jnp.int32, (LANES, 1), 0)
    n_sub = qb // DIL_BLOCK
    scores = {}
    for sub in range(n_sub):
        q = q_ref[sub * DIL_BLOCK:(sub + 1) * DIL_BLOCK, :]
        k3 = k_all[sub * DIL_BLOCK:(sub + 3) * DIL_BLOCK, :]
        for h in range(DIL_HEADS):
            head_lanes = jnp.logical_and(lane >= h * DIL_HEAD_DIM, lane < (h + 1) * DIL_HEAD_DIM)
            qh = jnp.where(head_lanes, q, jnp.zeros_like(q))
            scores[sub, h] = lax.dot_general(k3, qh, (((1,), (1,)), ((), ())), preferred_element_type=F32)
    for sub in range(n_sub):
        inside = 1 <= sub <= n_sub - 2
        key_row = i * qb + (sub - 1) * DIL_BLOCK + row
        in_seq = jnp.logical_and(key_row >= 0, key_row < rows)
        v3_t = v_all_t[:, sub * DIL_BLOCK:(sub + 3) * DIL_BLOCK]
        pvs, lses = [], []
        for h in range(DIL_HEADS):
            s = scores.pop((sub, h)) + bias_ref[h]
            s = s if inside else jnp.where(in_seq, s, NEG_BIG)
            m = jnp.max(s, axis=0, keepdims=True)
            e = jnp.exp2(s - m)
            den = jnp.sum(e, axis=0, keepdims=True)
            pvs.append(_dot(v3_t, e.astype(BF16)) * (1.0 / den))
            lses.append((m + jnp.log2(den)) * (1.0 / LOG2_E))
        out_t = jnp.zeros((LANES, DIL_BLOCK), F32)
        lse_t = jnp.zeros((LANES, DIL_BLOCK), F32)
        for h in range(DIL_HEADS):
            head_rows = jnp.logical_and(dim >= h * DIL_HEAD_DIM, dim < (h + 1) * DIL_HEAD_DIM)
            out_t = jnp.where(head_rows, pvs[h], out_t)
            lse_t = jnp.where(head_rows, lses[h], lse_t)
        o_ref[sub * DIL_BLOCK:(sub + 1) * DIL_BLOCK, :] = out_t.T
        lse_ref[sub * DIL_BLOCK:(sub + 1) * DIL_BLOCK, :] = lse_t.T


def _dilated_group(q, k, v, bias_t, g, d, B, S):
    rows = S // d
    qb = min(DIL_Q_BLOCK, rows)
    per = qb // DIL_BLOCK
    last = rows // DIL_BLOCK - 1
    shape = (B, d, rows, LANES)
    q, k, v = q.reshape(shape), k.reshape(shape), v.reshape(shape)
    edge = lambda f: pl.BlockSpec((None, None, DIL_BLOCK, LANES), f)
    main = pl.BlockSpec((None, None, qb, LANES), lambda b, c, i: (b, c, i, 0))
    prev = edge(lambda b, c, i: (b, c, jnp.maximum(i * per - 1, 0), 0))
    nxt = edge(lambda b, c, i: (b, c, jnp.minimum((i + 1) * per, last), 0))
    return pl.pallas_call(
        functools.partial(_dilated_kernel, rows=rows),
        grid=(B, d, rows // qb),
        in_specs=[main, prev, main, nxt, prev, main, nxt, _full((DIL_HEADS, 3 * DIL_BLOCK, DIL_BLOCK))],
        out_specs=[main, main],
        out_shape=[jax.ShapeDtypeStruct(shape, F32)] * 2,
        compiler_params=_cparams("parallel", "parallel", "parallel"),
        name="dilated_attention_g%d" % g,
    )(q, k, k, k, v, v, v, bias_t)


def _t5_bucket(rel):
    nb = REL_BUCKETS // 2
    max_exact = nb // 2
    ret = jnp.where(rel > 0, nb, 0)
    n = jnp.abs(rel)
    nf = jnp.maximum(n, 1).astype(F32)
    large = max_exact + (jnp.log(nf / max_exact) / math.log(REL_MAX_DIST / max_exact) * (nb - max_exact)).astype(jnp.int32)
    large = jnp.minimum(large, nb - 1)
    return ret + jnp.where(n < max_exact, n, large)


def _dilated_bias_tables(rel_bias):
    shape = (3 * DIL_BLOCK, DIL_BLOCK)
    rel = lax.broadcasted_iota(jnp.int32, shape, 0) - DIL_BLOCK - lax.broadcasted_iota(jnp.int32, shape, 1)
    in_band = jnp.abs(rel) <= DIL_SIDE
    steps = jnp.arange(-DIL_SIDE, DIL_SIDE + 1, dtype=jnp.int32)
    band_onehot = (rel[None] == steps[:, None, None]).astype(F32)
    exact = lax.Precision.HIGHEST
    tables = []
    for g, (_, d) in enumerate(DIL_PAIRS):
        bucket_onehot = (_t5_bucket(d * steps)[:, None] == jnp.arange(REL_BUCKETS)[None, :]).astype(F32)
        b = jnp.dot(bucket_onehot, rel_bias[:, g * DIL_HEADS:(g + 1) * DIL_HEADS].astype(F32), precision=exact)
        table = jnp.einsum("nh,nkq->hkq", b, band_onehot, precision=exact)
        tables.append(jnp.where(in_band[None], table * LOG2_E, NEG_BIG))
    return tables


MERGE_DIL_SLOTS = 4


def _merge_kernel(x_ref, fa_ref, sg_ref, oc_ref, o0_ref, l0_ref, o1_ref, l1_ref, o2_ref, l2_ref,
                  wa_ref, wb_ref, wc_ref, wd_ref, wg_ref, bg_ref, wo_ref, g_ref, b_ref, o_ref, nat_ref):
    x = x_ref[...]
    xb = x.astype(BF16)
    tm = x.shape[0]
    for slot, (src, g) in enumerate(((o1_ref, 1), (l1_ref, 1), (o2_ref, 2), (l2_ref, 2))):
        d = DIL_PAIRS[g][1]
        for c in range(d):
            nat_ref[slot, pl.ds(c, tm // d, stride=d), :] = src[c]
    l0, l1, l2 = l0_ref[...], nat_ref[1], nat_ref[3]
    mx = jnp.maximum(jnp.maximum(l0, l1), l2)
    e0, e1, e2 = jnp.exp(l0 - mx), jnp.exp(l1 - mx), jnp.exp(l2 - mx)
    den = e0 + e1 + e2
    od = o0_ref[...] * (e0 / den) + nat_ref[0] * (e1 / den) + nat_ref[2] * (e2 / den)
    for k1 in range(FOURIER_N1):
        for g in range(F_GROUPS):
            nat_ref[MERGE_DIL_SLOTS + g, pl.ds(k1, tm // FOURIER_N1, stride=FOURIER_N1), :] = (
                fa_ref[k1, :, g * LANES:(g + 1) * LANES])
    fa = jnp.concatenate([nat_ref[MERGE_DIL_SLOTS + g] for g in range(F_GROUPS)], axis=1).astype(BF16)
    branches = ((fa, wa_ref), (sg_ref[...], wb_ref), (oc_ref[...], wc_ref), (od.astype(BF16), wd_ref))
    merged = None
    for i, (act, w_ref) in enumerate(branches):
        cols = slice(i * D_MODEL, (i + 1) * D_MODEL)
        gate = jax.nn.sigmoid(_dot(xb, wg_ref[:, cols]) + bg_ref[:, cols])
        term = gate * _dot(act, w_ref[...])
        merged = term if merged is None else merged + term
    y = DN_ALPHA * x + _dot(merged.astype(BF16), wo_ref[...])
    o_ref[...] = _layer_norm(y, g_ref[...], b_ref[...])


def _merge(x, S, fa, sg, oc, dil, lw):
    T = x.shape[0]
    tm = TOKEN_TILE
    nblk = S // tm
    tok = lambda w: pl.BlockSpec((tm, w), lambda i: (i, 0))
    dil_specs, dil_args = [], []
    for (o, lse), (_, d) in zip(dil, DIL_PAIRS):
        for a in (o, lse):
            if d == 1:
                dil_specs.append(tok(LANES))
                dil_args.append(a.reshape(T, LANES))
            else:
                dil_specs.append(pl.BlockSpec((None, d, tm // d, LANES), lambda i: (i // nblk, 0, i % nblk, 0)))
                dil_args.append(a)
    return pl.pallas_call(
        _merge_kernel,
        grid=(T // tm,),
        in_specs=[tok(D_MODEL),
                  pl.BlockSpec((None, FOURIER_N1, tm // FOURIER_N1, F_WIDTH), lambda i: (i // nblk, 0, i % nblk, 0)),
                  tok(SG_WIDTH), tok(MLA_HEADS * MLA_V)] + dil_specs + [
                  _full((F_WIDTH, D_MODEL)), _full((SG_WIDTH, D_MODEL)), _full((MLA_HEADS * MLA_V, D_MODEL)),
                  _full((DIL_KV_WIDTH, D_MODEL)), _full((D_MODEL, 4 * D_MODEL)), _full((1, 4 * D_MODEL)),
                  _full((D_MODEL, D_MODEL)), _full((1, D_MODEL)), _full((1, D_MODEL))],
        out_specs=tok(D_MODEL),
        out_shape=jax.ShapeDtypeStruct((T, D_MODEL), F32),
        scratch_shapes=[pltpu.VMEM((MERGE_DIL_SLOTS + F_GROUPS, tm, LANES), F32)],
        compiler_params=_cparams("parallel"),
        name="merge",
    )(x, fa, sg, oc, *dil_args, lw["w_a"], lw["w_b"], lw["w_c"], lw["w_d"], lw["w_gate"], lw["b_gate"],
      lw["w_o"], lw["ln1_g"], lw["ln1_b"])


MOE_PAIRS = tuple((a, b) for a in range(MOE_EXPERTS_PER_GROUP) for b in range(a + 1, MOE_EXPERTS_PER_GROUP))
MOE_CLASSES = MOE_GROUPS * len(MOE_PAIRS)
MOE_ROW_TILE = 512
INFO_CLASS, INFO_RANK, INFO_P_LO, INFO_P_HI = 0, 1, 2, 3
MOE_ROW_WIDTH = D_MODEL + LANES


def _route_kernel(x_ref, wr_ref, br_ref, tri_ref, info_ref, counts_ref, run_ref):
    @pl.when(pl.program_id(0) == 0)
    def _():
        run_ref[...] = jnp.zeros(run_ref.shape, F32)

    x = x_ref[...]
    x_hi = x.astype(BF16)
    x_lo = (x - x_hi.astype(F32)).astype(BF16)
    logits = (_dot(x_hi, wr_ref[0]) + (_dot(x_hi, wr_ref[1]) + _dot(x_lo, wr_ref[0]))) + br_ref[...]
    tm = logits.shape[0]
    lane = lax.broadcasted_iota(jnp.int32, (tm, LANES), 1)
    is_g = lane < MOE_GROUPS
    gl = jnp.where(is_g, logits, NEG_BIG)
    gmax = jnp.max(gl, -1, keepdims=True)
    g_top = jnp.min(jnp.where(gl == gmax, lane, LANES), -1, keepdims=True)
    p_group = 1.0 / jnp.sum(jnp.where(is_g, jnp.exp(gl - gmax), 0.0), -1, keepdims=True)
    base = MOE_GROUPS + g_top * MOE_EXPERTS_PER_GROUP
    in_grp = jnp.logical_and(lane >= base, lane < base + MOE_EXPERTS_PER_GROUP)
    el = jnp.where(in_grp, logits, NEG_BIG)
    v1 = jnp.max(el, -1, keepdims=True)
    i1 = jnp.min(jnp.where(el == v1, lane, LANES), -1, keepdims=True)
    el2 = jnp.where(lane == i1, NEG_BIG, el)
    v2 = jnp.max(el2, -1, keepdims=True)
    i2 = jnp.min(jnp.where(el2 == v2, lane, LANES), -1, keepdims=True)
    e2 = jnp.exp(v2 - v1)
    p1 = p_group / (1.0 + e2)
    p2 = p_group * e2 / (1.0 + e2)
    a = jnp.minimum(i1, i2) - base
    b = jnp.maximum(i1, i2) - base
    pair = jnp.where(a == 0, 0, jnp.where(a == 1, 3, 5)) + (b - a - 1)
    cls = g_top * len(MOE_PAIRS) + pair
    first_is_lo = i1 < i2
    p_lo = jnp.where(first_is_lo, p1, p2)
    p_hi = jnp.where(first_is_lo, p2, p1)
    onehot = lane == cls
    before = _dot(tri_ref[...], onehot.astype(BF16)) + run_ref[...]
    rank = jnp.sum(jnp.where(onehot, before, 0.0), -1, keepdims=True)
    run_ref[...] += jnp.sum(onehot.astype(F32), axis=0, keepdims=True)
    counts_ref[...] = run_ref[...]
    info_ref[...] = jnp.where(lane == INFO_CLASS, cls.astype(F32),
                              jnp.where(lane == INFO_RANK, rank,
                                        jnp.where(lane == INFO_P_LO, p_lo,
                                                  jnp.where(lane == INFO_P_HI, p_hi, 0.0))))


ROW_COPY_UNROLL = 8


def _start_row_copies(n, make_copy):
    def start(pair, carry):
        make_copy(2 * pair).start(priority=0)
        make_copy(2 * pair + 1).start(priority=1)
        return carry

    lax.fori_loop(0, n // 2, start, 0, unroll=ROW_COPY_UNROLL // 2)


def _dispatch_kernel(dest_ref, x_ref, info_ref, init_hbm, rows_hbm, row_ref, sem):
    del init_hbm
    i = pl.program_id(0)
    n = pl.num_programs(0)
    tm = x_ref.shape[0]
    slot = i % 2

    def wait_slot(s):
        pltpu.make_async_copy(row_ref.at[s], rows_hbm.at[pl.ds(0, tm)], sem.at[s]).wait()

    @pl.when(i >= 2)
    def _():
        wait_slot(slot)

    row_ref[slot, :, :D_MODEL] = x_ref[...]
    row_ref[slot, :, D_MODEL:] = info_ref[...]
    _start_row_copies(tm, lambda r: pltpu.make_async_copy(
        row_ref.at[slot, pl.ds(r, 1)], rows_hbm.at[pl.ds(dest_ref[0, r], 1)], sem.at[slot]))

    @pl.when(i == n - 1)
    def _():
        wait_slot(slot)

        @pl.when(n > 1)
        def _():
            wait_slot(1 - slot)


def _expert_kernel(ea_ref, eb_ref, nused_ref, rows_ref, wga_ref, wua_ref, wda_ref, wgb_ref, wub_ref, wdb_ref,
                   y_ref):
    del ea_ref, eb_ref
    i = pl.program_id(0)

    @pl.when(i < nused_ref[0])
    def _():
        x = rows_ref[:, :D_MODEL].astype(BF16)
        p_lo = rows_ref[:, D_MODEL + INFO_P_LO:D_MODEL + INFO_P_LO + 1]
        p_hi = rows_ref[:, D_MODEL + INFO_P_HI:D_MODEL + INFO_P_HI + 1]
        ha = jax.nn.silu(_dot(x, wga_ref[...])) * _dot(x, wua_ref[...]) * p_lo
        hb = jax.nn.silu(_dot(x, wgb_ref[...])) * _dot(x, wub_ref[...]) * p_hi
        y_ref[...] = _dot(ha.astype(BF16), wda_ref[...]) + _dot(hb.astype(BF16), wdb_ref[...])

    @pl.when(i >= nused_ref[0])
    def _():
        y_ref[...] = jnp.zeros(y_ref.shape, F32)


def _combine_kernel(dest_ref, next_dest_ref, x_ref, g_ref, b_ref, y_hbm, o_ref, y_ref, sem):
    i = pl.program_id(0)
    n = pl.num_programs(0)
    tm = x_ref.shape[0]
    slot = i % 2

    def fetch(indices_ref, s):
        _start_row_copies(tm, lambda r: pltpu.make_async_copy(
            y_hbm.at[pl.ds(indices_ref[0, r], 1)], y_ref.at[s, pl.ds(r, 1)], sem.at[s]))

    @pl.when(i == 0)
    def _():
        fetch(dest_ref, slot)

    @pl.when(i + 1 < n)
    def _():
        fetch(next_dest_ref, 1 - slot)

    pltpu.make_async_copy(y_hbm.at[pl.ds(0, tm)], y_ref.at[slot], sem.at[slot]).wait()
    o_ref[...] = _layer_norm(DN_ALPHA * x_ref[...] + y_ref[slot], g_ref[...], b_ref[...])


def _moe(x, lw):
    T = x.shape[0]
    tm = TOKEN_TILE
    rt = MOE_ROW_TILE
    n_row_tiles = T // rt + MOE_CLASSES
    n_rows = n_row_tiles * rt
    tok = lambda w: pl.BlockSpec((tm, w), lambda i: (i, 0))

    tri = jnp.asarray(np.tril(np.ones((tm, tm)), -1), BF16)
    info, counts = pl.pallas_call(
        _route_kernel,
        grid=(T // tm,),
        in_specs=[tok(D_MODEL), _full((2, D_MODEL, LANES)), _full((1, LANES)), _full((tm, tm))],
        out_specs=[tok(LANES), _full((1, LANES))],
        out_shape=[jax.ShapeDtypeStruct((T, LANES), F32), jax.ShapeDtypeStruct((1, LANES), F32)],
        scratch_shapes=[pltpu.VMEM((1, LANES), F32)],
        compiler_params=_cparams("arbitrary"),
        name="moe_route",
    )(x, lw["w_router"], lw["b_router"], tri)

    cls = info[:, INFO_CLASS].astype(jnp.int32)
    rank = info[:, INFO_RANK].astype(jnp.int32)
    cnt = counts[0, :MOE_CLASSES].astype(jnp.int32)
    padded = (cnt + rt - 1) // rt * rt
    ends = jnp.cumsum(padded)
    classes = jnp.arange(MOE_CLASSES, dtype=jnp.int32)
    pick = lambda table, idx: jnp.sum(jnp.where(idx[:, None] == classes[None, :], table[None, :], 0), axis=1)
    dest = (pick(ends - padded, cls) + rank).reshape(T // tm, 1, tm)
    tile_start = jnp.arange(n_row_tiles, dtype=jnp.int32) * rt
    tile_cls = jnp.minimum(jnp.sum((tile_start[:, None] >= ends[None, :]).astype(jnp.int32), axis=1),
                           MOE_CLASSES - 1)
    group, pair = np.divmod(np.arange(MOE_CLASSES), len(MOE_PAIRS))
    lo_hi = np.asarray(MOE_PAIRS)[pair]
    ea = pick(jnp.asarray(group * MOE_EXPERTS_PER_GROUP + lo_hi[:, 0], jnp.int32), tile_cls)
    eb = pick(jnp.asarray(group * MOE_EXPERTS_PER_GROUP + lo_hi[:, 1], jnp.int32), tile_cls)
    n_used = (ends[-1:] // rt).astype(jnp.int32)

    dest_spec = pl.BlockSpec((None, 1, tm), lambda i: (i, 0, 0), memory_space=pltpu.SMEM)
    next_dest_spec = pl.BlockSpec((None, 1, tm), lambda i: (jnp.minimum(i + 1, T // tm - 1), 0, 0),
                                  memory_space=pltpu.SMEM)
    hbm = pl.BlockSpec(memory_space=pl.ANY)
    rows = pl.pallas_call(
        _dispatch_kernel,
        grid=(T // tm,),
        in_specs=[dest_spec, tok(D_MODEL), tok(LANES), hbm],
        out_specs=hbm,
        out_shape=jax.ShapeDtypeStruct((n_rows, MOE_ROW_WIDTH), F32),
        scratch_shapes=[pltpu.VMEM((2, tm, MOE_ROW_WIDTH), F32), pltpu.SemaphoreType.DMA((2,))],
        input_output_aliases={3: 0},
        compiler_params=_cparams("arbitrary"),
        name="moe_dispatch",
    )(dest, x, info, jnp.zeros((n_rows, MOE_ROW_WIDTH), F32))

    w_up = lambda sel: pl.BlockSpec((None, D_MODEL, MOE_FF), lambda i, ea, eb, nu: (sel(ea, eb)[i], 0, 0))
    w_dn = lambda sel: pl.BlockSpec((None, MOE_FF, D_MODEL), lambda i, ea, eb, nu: (sel(ea, eb)[i], 0, 0))
    first = lambda ea, eb: ea
    second = lambda ea, eb: eb
    y = pl.pallas_call(
        _expert_kernel,
        grid_spec=pltpu.PrefetchScalarGridSpec(
            num_scalar_prefetch=3,
            grid=(n_row_tiles,),
            in_specs=[pl.BlockSpec((rt, MOE_ROW_WIDTH), lambda i, ea, eb, nu: (i, 0)),
                      w_up(first), w_up(first), w_dn(first), w_up(second), w_up(second), w_dn(second)],
            out_specs=pl.BlockSpec((rt, D_MODEL), lambda i, ea, eb, nu: (i, 0)),
        ),
        out_shape=jax.ShapeDtypeStruct((n_rows, D_MODEL), F32),
        compiler_params=_cparams("arbitrary"),
        name="moe_experts",
    )(ea, eb, n_used, rows, lw["moe_w_gate"], lw["moe_w_up"], lw["moe_w_down"],
      lw["moe_w_gate"], lw["moe_w_up"], lw["moe_w_down"])

    return pl.pallas_call(
        _combine_kernel,
        grid=(T // tm,),
        in_specs=[dest_spec, next_dest_spec, tok(D_MODEL), _full((1, D_MODEL)), _full((1, D_MODEL)), hbm],
        out_specs=tok(D_MODEL),
        out_shape=jax.ShapeDtypeStruct((T, D_MODEL), F32),
        scratch_shapes=[pltpu.VMEM((2, tm, D_MODEL), F32), pltpu.SemaphoreType.DMA((2,))],
        compiler_params=_cparams("arbitrary"),
        name="moe_combine",
    )(dest, dest, x, lw["ln2_g"], lw["ln2_b"], y)


def _rope_tables(S):
    half = MLA_ROPE // 2
    inv = ROPE_BASE ** (-jnp.arange(half, dtype=F32) / half)
    ang = jnp.arange(S, dtype=F32)[:, None] * inv[None, :]
    cos, sin = jnp.cos(ang), jnp.sin(ang)
    one = jnp.ones((S, MLA_NOPE), F32)
    zero = jnp.zeros((S, MLA_NOPE), F32)
    zh = jnp.zeros((S, half), F32)
    rc = jnp.concatenate([one, cos, cos], axis=1)
    rsa = jnp.concatenate([zero, -sin, zh], axis=1)
    rsb = jnp.concatenate([zero, zh, sin], axis=1)
    return rc, rsa, rsb, cos.T, sin.T


def _split_bf16(w):
    hi = w.astype(BF16)
    return jnp.stack([hi, (w - hi.astype(F32)).astype(BF16)])


def _prep_layer(l, p):
    row = lambda a: a.reshape(1, -1).astype(F32)
    w_ukv = p["mla_w_ukv"][l].reshape(MLA_KV_RANK, MLA_HEADS, MLA_NOPE + MLA_V)
    pad_rows = ((0, 2 * LANES - MLA_KV_RANK), (0, 0))
    w_k = jnp.pad(w_ukv[:, :, :MLA_NOPE], ((0, 0), (0, 0), (0, LANES - MLA_NOPE))).reshape(MLA_KV_RANK, -1)
    w_v = w_ukv[:, :, MLA_NOPE:].reshape(MLA_KV_RANK, -1)
    w_router = jnp.concatenate([p["moe_w_rg"][l], p["moe_w_re"][l]], axis=1)
    b_router = jnp.concatenate([p["moe_b_rg"][l], p["moe_b_re"][l]])
    npad = LANES - MOE_GROUPS - MOE_EXPERTS
    return {
        "w_in": p["w_in"][l].astype(BF16),
        "sg_ln_g": row(p["sg_ln_g"][l]), "sg_ln_b": row(p["sg_ln_b"][l]),
        "sg_w": p["sg_w"][l].astype(BF16),
        "sg_bias": jnp.repeat(p["sg_b"][l].T, SG_GROUP_DIM, axis=1).astype(F32),
        "q_norm": row(p["mla_q_norm"][l]),
        "kv_norm": jnp.pad(row(p["mla_kv_norm"][l]), ((0, 0), (0, 2 * LANES - MLA_KV_RANK))),
        "w_uq": p["mla_w_uq"][l].T.astype(BF16),
        "w_k": jnp.pad(w_k, pad_rows).astype(BF16),
        "w_v": jnp.pad(w_v, pad_rows).T.astype(BF16),
        "w_a": p["w_branch_a"][l].astype(BF16), "w_b": p["w_branch_b"][l].astype(BF16),
        "w_c": p["w_branch_c"][l].astype(BF16), "w_d": p["w_branch_d"][l].astype(BF16),
        "w_gate": p["w_gate"][l].astype(BF16), "b_gate": row(p["b_gate"][l]),
        "w_o": p["w_o"][l].astype(BF16),
        "ln1_g": row(p["ln1_g"][l]), "ln1_b": row(p["ln1_b"][l]),
        "w_router": _split_bf16(jnp.pad(w_router, ((0, 0), (0, npad))).astype(F32)),
        "b_router": jnp.pad(b_router, (0, npad)).reshape(1, -1).astype(F32),
        "moe_w_gate": p["moe_w_gate"][l].astype(BF16), "moe_w_up": p["moe_w_up"][l].astype(BF16),
        "moe_w_down": p["moe_w_down"][l].astype(BF16),
        "ln2_g": row(p["ln2_g"][l]), "ln2_b": row(p["ln2_b"][l]),
    }


def _trunk(x, p, layers, bias_tables):
    B, S, _ = x.shape
    rope = _rope_tables(S)
    fconsts = _fourier_consts(S)
    h = _input_layer_norm(x.reshape(B * S, D_MODEL), p["ln_in_g"], p["ln_in_b"])
    for lw in layers:
        za, sg, q, k, vt, *dil_in = _in_proj(h, B, S, lw, rope)
        fa = _fourier_mix(za, B, S, fconsts)
        oc = _latent_attention(q, k, vt, B, S)
        dil = [_dilated_group(*dil_in[3 * g:3 * g + 3], bias_tables[g], g, d, B, S)
               for g, (_, d) in enumerate(DIL_PAIRS)]
        h = _merge(h, S, fa, sg, oc, dil, lw)
        h = _moe(h, lw)
    return h.reshape(B, S, D_MODEL)


def kernel(x_prompt, x_sample, ln_in_g, ln_in_b, rel_bias, w_in, sg_ln_g, sg_ln_b, sg_w, sg_b, mla_q_norm, mla_kv_norm, mla_w_uq, mla_w_ukv, w_branch_a, w_branch_b, w_branch_c, w_branch_d, w_gate, b_gate, w_o, ln1_g, ln1_b, moe_w_rg, moe_b_rg, moe_w_re, moe_b_re, moe_w_gate, moe_w_up, moe_w_down, ln2_g, ln2_b):
    p = dict(ln_in_g=ln_in_g, ln_in_b=ln_in_b, w_in=w_in, sg_ln_g=sg_ln_g, sg_ln_b=sg_ln_b, sg_w=sg_w, sg_b=sg_b,
             mla_q_norm=mla_q_norm, mla_kv_norm=mla_kv_norm, mla_w_uq=mla_w_uq, mla_w_ukv=mla_w_ukv,
             w_branch_a=w_branch_a, w_branch_b=w_branch_b, w_branch_c=w_branch_c, w_branch_d=w_branch_d,
             w_gate=w_gate, b_gate=b_gate, w_o=w_o, ln1_g=ln1_g, ln1_b=ln1_b,
             moe_w_rg=moe_w_rg, moe_b_rg=moe_b_rg, moe_w_re=moe_w_re, moe_b_re=moe_b_re,
             moe_w_gate=moe_w_gate, moe_w_up=moe_w_up, moe_w_down=moe_w_down, ln2_g=ln2_g, ln2_b=ln2_b)
    layers = [_prep_layer(l, p) for l in range(w_in.shape[0])]
    bias_tables = _dilated_bias_tables(rel_bias)
    return _trunk(x_prompt, p, layers, bias_tables), _trunk(x_sample, p, layers, bias_tables)
```
